```python
import jax
import jax.numpy as jnp
from jax import lax
import numpy as np

D_MODEL = 1024
BATCH = 16
SEQ = 2048
DEPTH = 2

GRID_W = 64
CTX_LEN = 256
EPS = 1e-6

M_HEADS = 4
M_HEAD_DIM = 128
M_WIDTH = M_HEADS * M_HEAD_DIM
M_CHUNK = 128
QK_CONV = 3
M_FORGET_BIAS = 3.0
A_HEADS = 4
Q_LORA = 384
KV_LORA = 256
NOPE_DIM = 128
ROPE_DIM = 64
V_DIM = 128
A_WIDTH = A_HEADS * V_DIM
Q_BLOCK = 128
ROPE_BASE = 10000.0
G_GROUPS = 4
G_CH = 128
G_WIDTH = G_GROUPS * G_CH
G_CHUNK = 128
N_BRANCH = 3
BRANCH_WIDTH = 512
D_FF = 2816
FFN_CONV = 3

IN_SIZES = (M_WIDTH, M_WIDTH, M_WIDTH, M_WIDTH, 4 * M_HEADS, Q_LORA, KV_LORA, ROPE_DIM,
            G_WIDTH, G_WIDTH, N_BRANCH * D_MODEL)
IN_DIM = sum(IN_SIZES)
IN_SPLITS = tuple(sum(IN_SIZES[:i + 1]) for i in range(len(IN_SIZES) - 1))

kernel_name = 'hybrid_mlstm_mla_sgu_convffn_dit'


def rms_norm(x, g):
    xf = x.astype(jnp.float32)
    y = xf * lax.rsqrt(jnp.mean(xf * xf, axis=-1, keepdims=True) + EPS)
    return (y * g.astype(jnp.float32)).astype(x.dtype)


def dw_conv(x, w, b):
    k = w.shape[0]
    pad = k // 2
    n = x.shape[1]
    xp = jnp.pad(x, ((0, 0), (pad, pad), (0, 0)))
    out = b + xp[:, 0:n] * w[0]
    for j in range(1, k):
        out = out + xp[:, j:j + n] * w[j]
    return out


def axial_rope(n_tokens):
    rows = n_tokens // GRID_W
    row = jnp.repeat(jnp.arange(rows), GRID_W)
    col = jnp.tile(jnp.arange(GRID_W), rows)
    n_freq = ROPE_DIM // 4
    inv = ROPE_BASE ** (-jnp.arange(n_freq, dtype=jnp.float32) / n_freq)
    ang = jnp.concatenate([row[:, None] * inv, col[:, None] * inv], axis=-1)
    return jnp.cos(ang), jnp.sin(ang)


def apply_rope(x, cos, sin):
    shape = (cos.shape[0],) + (1,) * (x.ndim - 3) + (cos.shape[-1],)
    cos = cos.reshape(shape).astype(x.dtype)
    sin = sin.reshape(shape).astype(x.dtype)
    x1, x2 = x[..., 0::2], x[..., 1::2]
    return jnp.stack([x1 * cos - x2 * sin, x1 * sin + x2 * cos], axis=-1).reshape(x.shape)


def mlstm_zero_state(batch):
    return (jnp.zeros((batch, M_HEADS, M_HEAD_DIM, M_HEAD_DIM), jnp.float32),
            jnp.zeros((batch, M_HEADS, M_HEAD_DIM), jnp.float32),
            jnp.zeros((batch, M_HEADS), jnp.float32))


def mlstm_prepare(q, k, v, gate_pre, w_conv, b_conv, b_gate):
    B, L, _ = q.shape
    qk = jax.nn.silu(dw_conv(jnp.concatenate([q, k], axis=-1), w_conv, b_conv))
    q, k = jnp.split(qk, 2, axis=-1)
    heads = lambda a: a.reshape(B, L, M_HEADS, M_HEAD_DIM).transpose(0, 2, 1, 3).astype(jnp.float32)
    q, k, v = heads(q), heads(k) * (M_HEAD_DIM ** -0.5), heads(v)
    g = (gate_pre + b_gate).astype(jnp.float32).reshape(B, L, 2, 2, M_HEADS).transpose(2, 3, 0, 4, 1)
    log_i = g[:, 0]
    log_f = jax.nn.log_sigmoid(g[:, 1])
    return q, k, v, log_i, log_f


def mlstm_chunked(q, k, v, log_i, log_f, state):
    B, H, L, d = q.shape
    nc = L // M_CHUNK
    to_chunks = lambda a: jnp.moveaxis(a.reshape(B, H, nc, M_CHUNK, *a.shape[3:]), 2, 0)
    xs = (to_chunks(q), to_chunks(k), to_chunks(v), to_chunks(log_i), to_chunks(log_f))
    lower = jnp.tril(jnp.ones((M_CHUNK, M_CHUNK), dtype=bool))

    def step(carry, inp):
        C, n, m = carry
        qc, kc, vc, ic, fc = inp
        b = jnp.cumsum(fc, axis=-1)
        dmat = jnp.where(lower, b[..., :, None] - b[..., None, :] + ic[..., None, :], -jnp.inf)
        inter = b + m[..., None]
        m_t = jnp.maximum(inter, jnp.max(dmat, axis=-1))
        w_intra = jnp.exp(dmat - m_t[..., None])
        w_inter = jnp.exp(inter - m_t)
        s = jnp.einsum('bhtd,bhsd->bhts', qc, kc) * w_intra
        num = w_inter[..., None] * jnp.einsum('bhtd,bhde->bhte', qc, C) + jnp.einsum('bhts,bhse->bhte', s, vc)
        den = w_inter * jnp.einsum('bhtd,bhd->bht', qc, n) + jnp.sum(s, axis=-1)
        h = num / jnp.maximum(jnp.abs(den), jnp.exp(-m_t))[..., None]
        b_last = b[..., -1]
        decay_s = b_last[..., None] - b + ic
        m_new = jnp.maximum(b_last + m, jnp.max(decay_s, axis=-1))
        ws = jnp.exp(decay_s - m_new[..., None])
        wc = jnp.exp(b_last + m - m_new)
        C_new = wc[..., None, None] * C + jnp.einsum('bhs,bhsd,bhse->bhde', ws, kc, vc)
        n_new = wc[..., None] * n + jnp.einsum('bhs,bhsd->bhd', ws, kc)
        return (C_new, n_new, m_new), h

    state, hs = lax.scan(step, state, xs)
    return jnp.moveaxis(hs, 0, 2).reshape(B, H, L, d), state


def mlstm_bidir(q, k, v, log_i, log_f, states):
    h_f, s_f = mlstm_chunked(q, k, v, log_i[0], log_f[0], states[0])
    flip = lambda a: jnp.flip(a, axis=2)
    h_b, s_b = mlstm_chunked(flip(q), flip(k), flip(v), flip(log_i[1]), flip(log_f[1]), states[1])
    return h_f + flip(h_b), (s_f, s_b)


def mlstm_output(h, o, g_head):
    B, H, L, d = h.shape
    h = rms_norm(h.transpose(0, 2, 1, 3), g_head.reshape(H, d)).reshape(B, L, H * d)
    return jax.nn.sigmoid(o) * h.astype(o.dtype)


def mla_q(cq, g_qn, w_uq, rope):
    B, L, _ = cq.shape
    q = (rms_norm(cq, g_qn) @ w_uq).reshape(B, L, A_HEADS, NOPE_DIM + ROPE_DIM)
    if rope is None:
        return q
    return jnp.concatenate([q[..., :NOPE_DIM], apply_rope(q[..., NOPE_DIM:], *rope)], axis=-1)


def mla_kv(ckv, k_rope, g_kvn, w_ukv, rope):
    B, L, _ = ckv.shape
    kv = (rms_norm(ckv, g_kvn) @ w_ukv).reshape(B, L, A_HEADS, NOPE_DIM + V_DIM)
    if rope is not None:
        k_rope = apply_rope(k_rope, *rope)
    k_rope = jnp.broadcast_to(k_rope[:, :, None, :], (B, L, A_HEADS, ROPE_DIM))
    k = jnp.concatenate([kv[..., :NOPE_DIM], k_rope], axis=-1)
    return k, kv[..., NOPE_DIM:]


def attend_blocked(q, k, v):
    B, Lq, H, dk = q.shape
    nb = Lq // Q_BLOCK
    qb = jnp.moveaxis(q.reshape(B, nb, Q_BLOCK, H, dk), 1, 0)
    scale = dk ** -0.5

    def one(q_blk):
        s = jnp.einsum('bqhd,bkhd->bhqk', q_blk, k, preferred_element_type=jnp.float32) * scale
        p = jax.nn.softmax(s, axis=-1).astype(v.dtype)
        return jnp.einsum('bhqk,bkhd->bqhd', p, v)

    o = lax.map(one, qb)
    return jnp.moveaxis(o, 0, 1).reshape(B, Lq, H * v.shape[-1])


def spatial_gating(u, v, g_norm, w_s, b_s):
    B, L, _ = u.shape
    nc = L // G_CHUNK
    u = jax.nn.gelu(u)
    v = jax.nn.gelu(v).reshape(B, nc, G_CHUNK, G_GROUPS, G_CH)
    v = rms_norm(v, g_norm.reshape(G_GROUPS, G_CH))
    mixed = jnp.einsum('gts,bnsgc->bntgc', w_s, v) + b_s.T[:, :, None]
    return u * mixed.reshape(B, L, G_WIDTH)


def merge_branches(ya, yb, yc, gate_pre, w_branch, w_out):
    y = jnp.stack([ya, yb, yc], axis=-2)
    proj = jnp.einsum('blgc,gcd->blgd', y, w_branch)
    gates = jax.nn.sigmoid(gate_pre.reshape(*gate_pre.shape[:-1], N_BRANCH, D_MODEL))
    return jnp.sum(gates * proj, axis=-2) @ w_out


def conv_ffn(h, w_up, w_conv, b_conv, w_down):
    a = dw_conv(h @ w_up, w_conv, b_conv)
    gate, val = jnp.split(a, 2, axis=-1)
    return (jax.nn.silu(gate) * val) @ w_down


def hybrid_layer(x, ctx, cond_x, cond_c, rope, need_ctx_out,
                 w_ada, b_ada, g_mix, w_in, w_qkconv, b_qkconv, b_mgate, g_mhead,
                 g_qnorm, w_uq, g_kvnorm, w_ukv, g_sgu, w_s, b_s, w_branch, w_out,
                 g_ffn, w_up, w_ffconv, b_ffconv, w_down):
    batch = x.shape[0]
    mod_x = jnp.split((cond_x @ w_ada + b_ada)[:, None, :], 6, axis=-1)
    mod_c = jnp.split(cond_c @ w_ada + b_ada, 6, axis=-1)

    hx = rms_norm(x, g_mix) * (1.0 + mod_x[1]) + mod_x[0]
    hc = rms_norm(ctx, g_mix) * (1.0 + mod_c[1]) + mod_c[0]
    (qx, kx, vx, ox, mgx, cqx, ckvx, krx, ux, sx, gtx) = jnp.split(hx @ w_in, IN_SPLITS, axis=-1)
    (qc, kc, vc, oc, mgc, cqc, ckvc, krc, uc, sc, gtc) = jnp.split(hc @ w_in, IN_SPLITS, axis=-1)

    h_c, ctx_states = mlstm_bidir(*mlstm_prepare(qc, kc, vc, mgc, w_qkconv, b_qkconv, b_mgate),
                                  (mlstm_zero_state(batch), mlstm_zero_state(batch)))
    h_x, _ = mlstm_bidir(*mlstm_prepare(qx, kx, vx, mgx, w_qkconv, b_qkconv, b_mgate), ctx_states)
    ya_x = mlstm_output(h_x, ox, g_mhead)

    k_ctx, v_ctx = mla_kv(ckvc, krc, g_kvnorm, w_ukv, None)
    k_lat, v_lat = mla_kv(ckvx, krx, g_kvnorm, w_ukv, rope)
    yb_x = attend_blocked(mla_q(cqx, g_qnorm, w_uq, rope),
                          jnp.concatenate([k_lat, k_ctx], axis=1),
                          jnp.concatenate([v_lat, v_ctx], axis=1))

    yc_x = spatial_gating(ux, sx, g_sgu, w_s, b_s)

    x = x + mod_x[2] * merge_branches(ya_x, yb_x, yc_x, gtx, w_branch, w_out)
    hx2 = rms_norm(x, g_ffn) * (1.0 + mod_x[4]) + mod_x[3]
    x = x + mod_x[5] * conv_ffn(hx2, w_up, w_ffconv, b_ffconv, w_down)

    if need_ctx_out:
        ya_c = mlstm_output(h_c, oc, g_mhead)
        yb_c = attend_blocked(mla_q(cqc, g_qnorm, w_uq, None), k_ctx, v_ctx)
        yc_c = spatial_gating(uc, sc, g_sgu, w_s, b_s)
        ctx = ctx + mod_c[2] * merge_branches(ya_c, yb_c, yc_c, gtc, w_branch, w_out)
        hc2 = rms_norm(ctx, g_ffn) * (1.0 + mod_c[4]) + mod_c[3]
        ctx = ctx + mod_c[5] * conv_ffn(hc2, w_up, w_ffconv, b_ffconv, w_down)
    return x, ctx


def setup_inputs(seed: int = 0) -> dict:
    key = jax.random.key(seed)
    ks = iter(jax.random.split(key, 32))
    nrm = lambda shape, scale: jax.random.normal(next(ks), shape, jnp.float32) * scale
    gain = lambda shape: 1.0 + nrm(shape, 0.02)
    L = DEPTH
    forget_cols = ((jnp.arange(4 * M_HEADS) // M_HEADS) % 2 == 1).astype(jnp.float32)
    return {
        'x': nrm((BATCH, SEQ, D_MODEL), 1.0),
        'c': nrm((BATCH, D_MODEL), 1.0),
        'ctx': nrm((BATCH, CTX_LEN, D_MODEL), 1.0),
        'c_ctx': nrm((D_MODEL,), 1.0),
        'w_ada': nrm((L, D_MODEL, 6 * D_MODEL), 0.5 * D_MODEL ** -0.5),
        'b_ada': nrm((L, 6 * D_MODEL), 0.02),
        'g_mix': gain((L, D_MODEL)),
        'w_in': nrm((L, D_MODEL, IN_DIM), D_MODEL ** -0.5),
        'w_qkconv': nrm((L, QK_CONV, 2 * M_WIDTH), QK_CONV ** -0.5),
        'b_qkconv': nrm((L, 2 * M_WIDTH), 0.02),
        'b_mgate': nrm((L, 4 * M_HEADS), 0.1) + M_FORGET_BIAS * forget_cols,
        'g_mhead': gain((L, M_WIDTH)),
        'g_qnorm': gain((L, Q_LORA)),
        'w_uq': nrm((L, Q_LORA, A_HEADS * (NOPE_DIM + ROPE_DIM)), Q_LORA ** -0.5),
        'g_kvnorm': gain((L, KV_LORA)),
        'w_ukv': nrm((L, KV_LORA, A_HEADS * (NOPE_DIM + V_DIM)), KV_LORA ** -0.5),
        'g_sgu': gain((L, G_WIDTH)),
        'w_s': nrm((L, G_GROUPS, G_CHUNK, G_CHUNK), G_CHUNK ** -0.5),
        'b_s': gain((L, G_GROUPS, G_CHUNK)),
        'w_branch': nrm((L, N_BRANCH, BRANCH_WIDTH, D_MODEL), BRANCH_WIDTH ** -0.5),
        'w_out': nrm((L, D_MODEL, D_MODEL), D_MODEL ** -0.5),
        'g_ffn': gain((L, D_MODEL)),
        'w_up': nrm((L, D_MODEL, 2 * D_FF), D_MODEL ** -0.5),
        'w_ffconv': nrm((L, FFN_CONV, 2 * D_FF), FFN_CONV ** -0.5),
        'b_ffconv': nrm((L, 2 * D_FF), 0.02),
        'w_down': nrm((L, D_FF, D_MODEL), D_FF ** -0.5),
        'g_final': gain((D_MODEL,)),
    }


def reference(x, c, ctx, c_ctx, w_ada, b_ada, g_mix, w_in, w_qkconv, b_qkconv, b_mgate, g_mhead,
              g_qnorm, w_uq, g_kvnorm, w_ukv, g_sgu, w_s, b_s, w_branch, w_out,
              g_ffn, w_up, w_ffconv, b_ffconv, w_down, g_final):
    rope = axial_rope(x.shape[1])
    cond_x = jax.nn.silu(c)
    cond_c = jax.nn.silu(c_ctx)
    for l in range(DEPTH):
        x, ctx = hybrid_layer(x, ctx, cond_x, cond_c, rope, l < DEPTH - 1,
                              w_ada[l], b_ada[l], g_mix[l], w_in[l], w_qkconv[l], b_qkconv[l],
                              b_mgate[l], g_mhead[l], g_qnorm[l], w_uq[l], g_kvnorm[l], w_ukv[l],
                              g_sgu[l], w_s[l], b_s[l], w_branch[l], w_out[l],
                              g_ffn[l], w_up[l], w_ffconv[l], b_ffconv[l], w_down[l])
    return rms_norm(x, g_final)
```

```python
import functools

import jax
import jax.numpy as jnp
import numpy as np
from jax import lax
from jax.experimental import pallas as pl
from jax.experimental.pallas import tpu as pltpu

EPS = 1e-6
GRID_W = 64
ROPE_BASE = 10000.0

HEADS = 4
HEAD_DIM = 128
CHUNK = 128
BRANCH_WIDTH = HEADS * HEAD_DIM
Q_LORA = 384
KV_LORA = 256
ROPE_DIM = 64
N_BRANCH = 3
M_GATES = 4 * HEADS

COL_QKV = 0
COL_O = 3 * BRANCH_WIDTH
COL_US = 4 * BRANCH_WIDTH
COL_GT_BLOCK = 1
KV_SRC = KV_LORA + 2 * ROPE_DIM
GATE_W = 256

ROW_TILE = 256
V7X_VMEM_LIMIT = 56 * 1024 * 1024

_HIGHEST = lax.Precision.HIGHEST
_BF16 = jnp.bfloat16
_F32 = jnp.float32


def _cparams(n_axes, vmem=V7X_VMEM_LIMIT):
    return pltpu.CompilerParams(dimension_semantics=("arbitrary",) * n_axes, vmem_limit_bytes=vmem)


def _sigmoid(x):
    return 1.0 / (1.0 + jnp.exp(-x))


def _silu(x):
    return x * _sigmoid(x)


def _gelu_tanh(x):
    return x * (0.5 * (1.0 + jnp.tanh(0.7978845608028654 * (x + 0.044715 * (x * x * x)))))


def _rms(x):
    return x * lax.rsqrt(jnp.mean(x * x, axis=-1, keepdims=True) + EPS)


def _dot(a, b):
    return jnp.dot(a, b, preferred_element_type=_F32)


def _dot_nt(a, b):
    return lax.dot_general(a, b, (((1,), (1,)), ((), ())), preferred_element_type=_F32)


def _dot_tn(a, b):
    return lax.dot_general(a, b, (((0,), (0,)), ((), ())), preferred_element_type=_F32)


def _ada_kernel(c_ref, w_ref, b_ref, o_ref):
    cond = _silu(c_ref[...])
    o_ref[...] = jnp.dot(cond, w_ref[...], precision=_HIGHEST, preferred_element_type=_F32) + b_ref[...]


def _ada(cond_rows, w_ada, b_ada):
    rows, d = cond_rows.shape
    n = w_ada.shape[1]
    return pl.pallas_call(
        _ada_kernel,
        grid=(n // d,),
        in_specs=[pl.BlockSpec((rows, d), lambda j: (0, 0)),
                  pl.BlockSpec((d, d), lambda j: (0, j)),
                  pl.BlockSpec((1, d), lambda j: (0, j))],
        out_specs=pl.BlockSpec((rows, d), lambda j: (0, j)),
        out_shape=jax.ShapeDtypeStruct((rows, n), _F32),
        compiler_params=_cparams(1),
        name="ada",
    )(cond_rows, w_ada, b_ada.reshape(1, n))


def _proj_kernel(x_ref, mod_ref, g_ref, w_ref, o_ref, *, d, col_tile):
    y = _rms(x_ref[...]) * g_ref[...]
    h = (y * (1.0 + mod_ref[:, d:2 * d]) + mod_ref[:, 0:d]).astype(_BF16)
    for j in range(o_ref.shape[1] // col_tile):
        sl = slice(j * col_tile, (j + 1) * col_tile)
        o_ref[:, sl] = _dot(h, w_ref[:, sl])


def _proj(x, mod, g_mix, w_in, nctx_tiles):
    b, t, d = x.shape
    n = w_in.shape[1]
    mod_row = lambda bi, i: (jnp.where(i < nctx_tiles, b, bi), 0, 0)
    return pl.pallas_call(
        functools.partial(_proj_kernel, d=d, col_tile=1024),
        grid=(b, t // ROW_TILE),
        in_specs=[pl.BlockSpec((None, ROW_TILE, d), lambda bi, i: (bi, i, 0)),
                  pl.BlockSpec((None, 1, mod.shape[2]), mod_row),
                  pl.BlockSpec((1, d), lambda bi, i: (0, 0)),
                  pl.BlockSpec((d, n), lambda bi, i: (0, 0))],
        out_specs=pl.BlockSpec((None, ROW_TILE, n), lambda bi, i: (bi, i, 0)),
        out_shape=jax.ShapeDtypeStruct((b, t, n), _F32),
        compiler_params=_cparams(2),
        name="proj",
    )(x, mod, g_mix.reshape(1, d), w_in)


def _mlstm_kernel(qkv_f, hp_f, hn_f, g_f, qkv_b, hp_b, hn_b, g_b, wc_ref, bc_ref, bg_ref,
                  hf_ref, hb_ref, c_scr, n_scr, m_scr, *, nctx, nchunks):
    j = pl.program_id(1)
    tc = CHUNK
    qk_w = 2 * BRANCH_WIDTH

    @pl.when(j == 0)
    def _():
        c_scr[...] = jnp.zeros_like(c_scr)
        n_scr[...] = jnp.zeros_like(n_scr)
        m_scr[...] = jnp.zeros_like(m_scr)

    chunk_f = j
    chunk_b = jnp.where(j < nctx, nctx - 1 - j, nchunks - 1 - (j - nctx))
    row = lax.broadcasted_iota(jnp.int32, (tc, 1), 0)

    def conv_silu(qkv_ref, hp_ref, hn_ref, c):
        x = qkv_ref[:, 0:qk_w]
        seg_first = jnp.logical_or(c == 0, c == nctx)
        seg_last = jnp.logical_or(c == nctx - 1, c == nchunks - 1)
        prev_row = jnp.where(seg_first, 0.0, hp_ref[7:8, :])
        next_row = jnp.where(seg_last, 0.0, hn_ref[0:1, :])
        x_prev = jnp.where(row == 0, prev_row, pltpu.roll(x, 1, axis=0))
        x_next = jnp.where(row == tc - 1, next_row, pltpu.roll(x, tc - 1, axis=0))
        a = bc_ref[...] + x_prev * wc_ref[0:1, :] + x * wc_ref[1:2, :] + x_next * wc_ref[2:3, :]
        return _silu(a)

    qk_dir = (conv_silu(qkv_f, hp_f, hn_f, chunk_f), conv_silu(qkv_b, hp_b, hn_b, chunk_b))
    v_dir = (qkv_f, qkv_b)
    out_dir = (hf_ref, hb_ref)

    lane = lax.broadcasted_iota(jnp.int32, (1, 128), 1)
    is_fwd = lane < HEADS
    gates_i = jnp.where(is_fwd, g_f[:, 0:128], g_b[:, 0:128]) + bg_ref[:, 0:128]
    gates_f = jnp.where(is_fwd, g_f[:, 128:256], g_b[:, 128:256]) + bg_ref[:, 128:256]
    log_i = gates_i
    log_f = jnp.minimum(gates_f, 0.0) - jnp.log1p(jnp.exp(-jnp.abs(gates_f)))

    r_idx = lax.broadcasted_iota(jnp.int32, (tc, tc), 0)
    c_idx = lax.broadcasted_iota(jnp.int32, (tc, tc), 1)
    causal = (c_idx <= r_idx, c_idx >= r_idx)
    cum_f = jnp.dot(causal[0].astype(_F32), log_f, precision=_HIGHEST, preferred_element_type=_F32)
    cum_b = jnp.dot(causal[1].astype(_F32), log_f, precision=_HIGHEST, preferred_element_type=_F32)
    bcum = jnp.where(is_fwd, cum_f, cum_b)
    b_last = jnp.where(is_fwd, bcum[tc - 1:tc, :], bcum[0:1, :])
    m_old = m_scr[...]
    inter = bcum + m_old
    decay = b_last - bcum + log_i
    m_new = jnp.maximum(b_last + m_old, jnp.max(decay, axis=0, keepdims=True))
    w_src = jnp.exp(decay - m_new)
    w_carry = jnp.exp(b_last + m_old - m_new)
    bcum_t = bcum.T
    log_i_t = log_i.T

    for dr in range(2):
        for h in range(HEADS):
            l = dr * HEADS + h
            hs = slice(h * HEAD_DIM, (h + 1) * HEAD_DIM)
            q = qk_dir[dr][:, hs]
            k = qk_dir[dr][:, BRANCH_WIDTH + h * HEAD_DIM:BRANCH_WIDTH + (h + 1) * HEAD_DIM] * (HEAD_DIM ** -0.5)
            v16 = v_dir[dr][:, qk_w + h * HEAD_DIM:qk_w + (h + 1) * HEAD_DIM].astype(_BF16)
            q16 = q.astype(_BF16)
            dmat = bcum[:, l:l + 1] - bcum_t[l:l + 1, :] + log_i_t[l:l + 1, :]
            dmat = jnp.where(causal[dr], dmat, -jnp.inf)
            inter_l = inter[:, l:l + 1]
            m_t = jnp.maximum(inter_l, jnp.max(dmat, axis=1, keepdims=True))
            w_intra = jnp.exp(dmat - m_t)
            w_inter = jnp.exp(inter_l - m_t)
            s = _dot_nt(q16, k.astype(_BF16)) * w_intra
            c_old = c_scr[l]
            n_old = n_scr[l:l + 1, :]
            num = w_inter * _dot(q16, c_old.astype(_BF16)) + _dot(s.astype(_BF16), v16)
            den = w_inter * jnp.sum(q * n_old, axis=1, keepdims=True) + jnp.sum(s, axis=1, keepdims=True)
            out_dir[dr][:, hs] = num / jnp.maximum(jnp.abs(den), jnp.exp(-m_t))
            kw = k * w_src[:, l:l + 1]
            c_scr[l] = w_carry[:, l:l + 1] * c_old + _dot_tn(kw.astype(_BF16), v16)
            n_scr[l:l + 1, :] = w_carry[:, l:l + 1] * n_old + jnp.sum(kw, axis=0, keepdims=True)
    m_scr[...] = m_new


def _mlstm(p, w_qkconv, b_qkconv, b_gate, nctx):
    b, t, n = p.shape
    nchunks = t // CHUNK
    qk_w = 2 * BRANCH_WIDTH
    halo_blocks = CHUNK // 8
    gate_block = (n - GATE_W) // GATE_W

    def bwd_chunk(j):
        return jnp.where(j < nctx, nctx - 1 - j, nchunks - 1 - (j - nctx))

    def specs(chunk_of):
        return [
            pl.BlockSpec((None, CHUNK, 3 * BRANCH_WIDTH), lambda bi, j: (bi, chunk_of(j), 0)),
            pl.BlockSpec((None, 8, qk_w), lambda bi, j: (bi, jnp.maximum(chunk_of(j) * halo_blocks - 1, 0), 0)),
            pl.BlockSpec((None, 8, qk_w),
                         lambda bi, j: (bi, jnp.minimum((chunk_of(j) + 1) * halo_blocks, t // 8 - 1), 0)),
            pl.BlockSpec((None, CHUNK, GATE_W), lambda bi, j: (bi, chunk_of(j), gate_block)),
        ]

    const = lambda shape: pl.BlockSpec(shape, lambda bi, j: (0,) * len(shape))
    h_shape = jax.ShapeDtypeStruct((b, t, BRANCH_WIDTH), _F32)
    return pl.pallas_call(
        functools.partial(_mlstm_kernel, nctx=nctx, nchunks=nchunks),
        grid=(b, nchunks),
        in_specs=specs(lambda j: j) + specs(bwd_chunk) + [const((3, qk_w)), const((1, qk_w)), const((1, GATE_W))],
        out_specs=[pl.BlockSpec((None, CHUNK, BRANCH_WIDTH), lambda bi, j: (bi, j, 0)),
                   pl.BlockSpec((None, CHUNK, BRANCH_WIDTH), lambda bi, j: (bi, bwd_chunk(j), 0))],
        out_shape=[h_shape, h_shape],
        scratch_shapes=[pltpu.VMEM((2 * HEADS, HEAD_DIM, HEAD_DIM), _F32),
                        pltpu.VMEM((2 * HEADS, HEAD_DIM), _F32),
                        pltpu.VMEM((1, 128), _F32)],
        compiler_params=_cparams(2),
        name="mlstm",
    )(p, p, p, p, p, p, p, p, w_qkconv, b_qkconv.reshape(1, qk_w), b_gate)


def _attn_kernel(cq_ref, kv_ref, ca_ref, sa_ref, caq_ref, saq_ref, gq_ref, wq_ref, gkv_ref, wk_ref, wv_ref,
                 o_ref, k_scr, v_scr, *, ctx_len, need_ctx, scale):
    i = pl.program_id(1)
    t = kv_ref.shape[0]
    tq = cq_ref.shape[0]
    head_w = 2 * HEAD_DIM

    def rope(y, cos_t, sin_t):
        return y * cos_t + pltpu.roll(y, ROPE_DIM, axis=1) * sin_t

    @pl.when(i == 0)
    def _():
        def build(r, carry):
            rows = pl.ds(pl.multiple_of(r * ROW_TILE, ROW_TILE), ROW_TILE)
            ckv = (_rms(kv_ref[rows, 0:KV_LORA]) * gkv_ref[...]).astype(_BF16)
            k_nope = _dot(ckv, wk_ref[...])
            k_rope = rope(kv_ref[rows, KV_LORA:KV_SRC], ca_ref[rows, :], sa_ref[rows, :]).astype(_BF16)
            for h in range(HEADS):
                k_scr[rows, h * head_w:h * head_w + HEAD_DIM] = k_nope[:, h * HEAD_DIM:(h + 1) * HEAD_DIM].astype(_BF16)
                k_scr[rows, h * head_w + HEAD_DIM:(h + 1) * head_w] = k_rope
            v_scr[rows, :] = _dot(ckv, wv_ref[...]).astype(_BF16)
            return carry
        lax.fori_loop(0, t // ROW_TILE, build, 0)

    def attend(n_keys):
        cq = (_rms(cq_ref[...]) * gq_ref[...]).astype(_BF16)
        qa = _dot(cq, wq_ref[...]) * scale
        for h in range(HEADS):
            q_nope = qa[:, h * head_w:h * head_w + HEAD_DIM]
            q_rope = rope(qa[:, h * head_w + HEAD_DIM:(h + 1) * head_w], caq_ref[...], saq_ref[...])
            qh = jnp.concatenate([q_nope, q_rope], axis=1).astype(_BF16)
            s = _dot_nt(qh, k_scr[0:n_keys, h * head_w:(h + 1) * head_w])
            e = jnp.exp(s - jnp.max(s, axis=1, keepdims=True))
            o = _dot(e.astype(_BF16), v_scr[0:n_keys, h * HEAD_DIM:(h + 1) * HEAD_DIM])
            o_ref[:, h * HEAD_DIM:(h + 1) * HEAD_DIM] = (o / jnp.sum(e, axis=1, keepdims=True)).astype(o_ref.dtype)

    @pl.when(i < ctx_len // tq)
    def _():
        if need_ctx:
            attend(ctx_len)
        else:
            o_ref[...] = jnp.zeros_like(o_ref)

    @pl.when(i >= ctx_len // tq)
    def _():
        attend(t)


def _attn(p, cos_t, sin_t, g_qn, w_q, g_kvn, w_k, w_v, ctx_len, need_ctx):
    b, t, n = p.shape
    d3 = p.shape[2]
    gate_cols = GATE_W
    cq_block = (d3 - gate_cols - KV_SRC - Q_LORA) // Q_LORA
    const = lambda shape: pl.BlockSpec(shape, lambda bi, i: (0,) * len(shape))
    scale = (HEAD_DIM + ROPE_DIM) ** -0.5
    return pl.pallas_call(
        functools.partial(_attn_kernel, ctx_len=ctx_len, need_ctx=need_ctx, scale=scale),
        grid=(b, t // ROW_TILE),
        in_specs=[pl.BlockSpec((None, ROW_TILE, Q_LORA), lambda bi, i: (bi, i, cq_block)),
                  pl.BlockSpec((None, t, KV_SRC), lambda bi, i: (bi, 0, cq_block + 1)),
                  const((t, 128)), const((t, 128)),
                  pl.BlockSpec((ROW_TILE, 128), lambda bi, i: (i, 0)),
                  pl.BlockSpec((ROW_TILE, 128), lambda bi, i: (i, 0)),
                  const((1, Q_LORA)), const(w_q.shape), const((1, KV_LORA)), const(w_k.shape), const(w_v.shape)],
        out_specs=pl.BlockSpec((None, ROW_TILE, BRANCH_WIDTH), lambda bi, i: (bi, i, 0)),
        out_shape=jax.ShapeDtypeStruct((b, t, BRANCH_WIDTH), _BF16),
        scratch_shapes=[pltpu.VMEM((t, HEADS * 2 * HEAD_DIM), _BF16), pltpu.VMEM((t, BRANCH_WIDTH), _BF16)],
        compiler_params=_cparams(2),
        name="attn",
    )(p, p, cos_t, sin_t, cos_t, sin_t, g_qn.reshape(1, -1), w_q, g_kvn.reshape(1, -1), w_k, w_v)


def _merge_kernel(hf_ref, hb_ref, o_ref, yb_ref, us_ref, gt_ref, x_ref, mod_ref,
                  gmh_ref, gsgu_ref, ws_ref, bs_ref, wbr_ref, wout_ref, gffn_ref,
                  xo_ref, h2_ref, *, d):
    tm = x_ref.shape[0]
    hsum = hf_ref[...] + hb_ref[...]
    ya = jnp.concatenate(
        [_rms(hsum[:, h * HEAD_DIM:(h + 1) * HEAD_DIM]) for h in range(HEADS)], axis=1) * gmh_ref[...]
    ya = _sigmoid(o_ref[...]) * ya
    u = _gelu_tanh(us_ref[:, 0:BRANCH_WIDTH])
    sv = _gelu_tanh(us_ref[:, BRANCH_WIDTH:2 * BRANCH_WIDTH])
    cols = []
    for g in range(HEADS):
        gs = slice(g * HEAD_DIM, (g + 1) * HEAD_DIM)
        vn = (_rms(sv[:, gs]) * gsgu_ref[:, gs]).astype(_BF16)
        mixed = [_dot(ws_ref[g], vn[n * CHUNK:(n + 1) * CHUNK, :]) + bs_ref[:, g:g + 1] for n in range(tm // CHUNK)]
        cols.append(jnp.concatenate(mixed, axis=0))
    yc = u * jnp.concatenate(cols, axis=1)
    acc = None
    for g, y in enumerate((ya.astype(_BF16), yb_ref[...], yc.astype(_BF16))):
        term = _sigmoid(gt_ref[:, g * d:(g + 1) * d]) * _dot(y, wbr_ref[g])
        acc = term if acc is None else acc + term
    out = _dot(acc.astype(_BF16), wout_ref[...])
    x_new = x_ref[...] + mod_ref[:, 2 * d:3 * d] * out
    xo_ref[...] = x_new
    h2 = _rms(x_new) * gffn_ref[...]
    h2_ref[...] = (h2 * (1.0 + mod_ref[:, 4 * d:5 * d]) + mod_ref[:, 3 * d:4 * d]).astype(h2_ref.dtype)


def _merge(hf, hb, p, yb, x, mod, g_mhead, g_sgu, w_s, b_s_t, w_branch, w_out, g_ffn, nctx_tiles, skip_ctx):
    b, t, d = x.shape
    off = nctx_tiles if skip_ctx else 0
    rows_out = t - off * ROW_TILE
    tile = lambda w, blk: pl.BlockSpec((None, ROW_TILE, w), lambda bi, i: (bi, i + off, blk))
    const = lambda shape: pl.BlockSpec(shape, lambda bi, i: (0,) * len(shape))
    mod_row = lambda bi, i: (jnp.where(i + off < nctx_tiles, b, bi), 0, 0)
    out_tile = lambda: pl.BlockSpec((None, ROW_TILE, d), lambda bi, i: (bi, i, 0))
    return pl.pallas_call(
        functools.partial(_merge_kernel, d=d),
        grid=(b, rows_out // ROW_TILE),
        in_specs=[tile(BRANCH_WIDTH, 0), tile(BRANCH_WIDTH, 0), tile(BRANCH_WIDTH, COL_O // BRANCH_WIDTH),
                  tile(BRANCH_WIDTH, 0), tile(2 * BRANCH_WIDTH, COL_US // (2 * BRANCH_WIDTH)),
                  tile(N_BRANCH * d, COL_GT_BLOCK), tile(d, 0),
                  pl.BlockSpec((None, 1, mod.shape[2]), mod_row),
                  const((1, BRANCH_WIDTH)), const((1, BRANCH_WIDTH)), const(w_s.shape), const(b_s_t.shape),
                  const(w_branch.shape), const(w_out.shape), const((1, d))],
        out_specs=[out_tile(), out_tile()],
        out_shape=[jax.ShapeDtypeStruct((b, rows_out, d), _F32), jax.ShapeDtypeStruct((b, rows_out, d), _BF16)],
        compiler_params=_cparams(2),
        name="merge",
    )(hf, hb, p, yb, p, p, x, mod, g_mhead.reshape(1, -1), g_sgu.reshape(1, -1), w_s, b_s_t,
      w_branch, w_out, g_ffn.reshape(1, d))


FFN_HALO = 16


def _ffn_kernel(h_ref, hp_ref, hn_ref, x_ref, mod_ref, wup_ref, wcv_ref, bcv_ref, wdn_ref, gfin_ref,
                o_ref, ext_scr, acc_scr, *, d, seg_tiles, tiles_total, final_norm):
    i = pl.program_id(1)
    tm = h_ref.shape[0]
    n_chunks, _, two_ck = wup_ref.shape
    ck = two_ck // 2
    seg_first = functools.reduce(jnp.logical_or, [i == s for s in seg_tiles])
    seg_last = functools.reduce(jnp.logical_or, [i == s - 1 for s in seg_tiles[1:] + (tiles_total,)])
    ext_scr[0:FFN_HALO, :] = jnp.where(seg_first, jnp.zeros_like(hp_ref), hp_ref[...])
    ext_scr[FFN_HALO:FFN_HALO + tm, :] = h_ref[...]
    ext_scr[FFN_HALO + tm:, :] = jnp.where(seg_last, jnp.zeros_like(hn_ref), hn_ref[...])
    acc_scr[...] = jnp.zeros_like(acc_scr)
    ext_rows = tm + 2 * FFN_HALO

    def chunk(c, carry):
        a = _dot(ext_scr[...], wup_ref[c])
        w = wcv_ref[c]
        a_prev = pltpu.roll(a, 1, axis=0)[FFN_HALO:FFN_HALO + tm, :]
        a_next = pltpu.roll(a, ext_rows - 1, axis=0)[FFN_HALO:FFN_HALO + tm, :]
        conv = bcv_ref[c] + a_prev * w[0:1, :] + a[FFN_HALO:FFN_HALO + tm, :] * w[1:2, :] + a_next * w[2:3, :]
        act = (_silu(conv[:, 0:ck]) * conv[:, ck:two_ck]).astype(_BF16)
        acc_scr[...] += _dot(act, wdn_ref[c])
        return carry

    lax.fori_loop(0, n_chunks, chunk, 0)
    x_new = x_ref[...] + mod_ref[:, 5 * d:6 * d] * acc_scr[...]
    if final_norm:
        x_new = _rms(x_new) * gfin_ref[...]
    o_ref[...] = x_new


def _ffn(h2, x, mod, w_up, w_cv, b_cv, w_dn, g_final, seg_rows, ctx_tiles, final_norm, row_off):
    b, r, d = x.shape
    off = row_off // ROW_TILE
    rows_out = r - row_off
    tiles = rows_out // ROW_TILE
    seg_tiles = tuple((s - row_off) // ROW_TILE for s in seg_rows)
    hb = ROW_TILE // FFN_HALO
    const = lambda shape: pl.BlockSpec(shape, lambda bi, i: (0,) * len(shape))
    mod_row = lambda bi, i: (jnp.where(i + off < ctx_tiles, b, bi), 0, 0)
    return pl.pallas_call(
        functools.partial(_ffn_kernel, d=d, seg_tiles=seg_tiles, tiles_total=tiles, final_norm=final_norm),
        grid=(b, tiles),
        in_specs=[pl.BlockSpec((None, ROW_TILE, d), lambda bi, i: (bi, i + off, 0)),
                  pl.BlockSpec((None, FFN_HALO, d), lambda bi, i: (bi, jnp.maximum((i + off) * hb - 1, 0), 0)),
                  pl.BlockSpec((None, FFN_HALO, d),
                               lambda bi, i: (bi, jnp.minimum((i + off + 1) * hb, r // FFN_HALO - 1), 0)),
                  pl.BlockSpec((None, ROW_TILE, d), lambda bi, i: (bi, i + off, 0)),
                  pl.BlockSpec((None, 1, mod.shape[2]), mod_row),
                  const(w_up.shape), const(w_cv.shape), const(b_cv.shape), const(w_dn.shape), const((1, d))],
        out_specs=pl.BlockSpec((None, ROW_TILE, d), lambda bi, i: (bi, i, 0)),
        out_shape=jax.ShapeDtypeStruct((b, rows_out, d), _F32),
        scratch_shapes=[pltpu.VMEM((ROW_TILE + 2 * FFN_HALO, d), _BF16), pltpu.VMEM((ROW_TILE, d), _F32)],
        compiler_params=_cparams(2),
        name="ffn",
    )(h2, h2, h2, x, mod, w_up, w_cv, b_cv, w_dn, g_final.reshape(1, d))


def _deinterleave(w):
    return jnp.concatenate([w[..., 0::2], w[..., 1::2]], axis=-1)


def _rotated(w):
    return jnp.concatenate([-w[..., 1::2], w[..., 0::2]], axis=-1)


def _layout_w_in(w, d):
    sizes = (BRANCH_WIDTH,) * 4 + (M_GATES, Q_LORA, KV_LORA, ROPE_DIM, BRANCH_WIDTH, BRANCH_WIDTH, N_BRANCH * d)
    splits = tuple(int(s) for s in np.cumsum(sizes)[:-1])
    q, k, v, o, mg, cq, ckv, kr, u, s, gt = jnp.split(w, splits, axis=1)
    mg = mg.reshape(d, 2, 2, HEADS)
    pad = jnp.zeros((d, 128 - 2 * HEADS), w.dtype)
    cols = [q, k, v, o, u, s, gt, cq, ckv, _deinterleave(kr), _rotated(kr),
            mg[:, :, 0, :].reshape(d, 2 * HEADS), pad, mg[:, :, 1, :].reshape(d, 2 * HEADS), pad]
    return jnp.concatenate(cols, axis=1).astype(_BF16)


def _layout_gate_bias(bg):
    bg = bg.reshape(2, 2, HEADS)
    pad = jnp.zeros((128 - 2 * HEADS,), bg.dtype)
    return jnp.concatenate([bg[:, 0, :].reshape(-1), pad, bg[:, 1, :].reshape(-1), pad]).reshape(1, GATE_W)


def _layout_w_uq(w):
    w = w.reshape(Q_LORA, HEADS, HEAD_DIM + ROPE_DIM)
    nope, rope = w[..., :HEAD_DIM], w[..., HEAD_DIM:]
    return jnp.concatenate([nope, _deinterleave(rope), _rotated(rope)], axis=-1).reshape(Q_LORA, -1).astype(_BF16)


def _layout_w_ukv(w):
    w = w.reshape(KV_LORA, HEADS, 2 * HEAD_DIM)
    return (w[..., :HEAD_DIM].reshape(KV_LORA, -1).astype(_BF16), w[..., HEAD_DIM:].reshape(KV_LORA, -1).astype(_BF16))


def _layout_ffn(w_up, w_cv, b_cv, w_dn, ck):
    d, two_ff = w_up.shape
    ff = two_ff // 2
    nc = ff // ck
    pair = lambda a: jnp.concatenate([a[..., :ff].reshape(a.shape[:-1] + (nc, ck)),
                                      a[..., ff:].reshape(a.shape[:-1] + (nc, ck))], axis=-1)
    w_up_c = jnp.moveaxis(pair(w_up), 1, 0).astype(_BF16)
    w_cv_c = jnp.moveaxis(pair(w_cv), 1, 0)
    b_cv_c = pair(b_cv).reshape(nc, 1, 2 * ck)
    w_dn_c = w_dn.reshape(nc, ck, d).astype(_BF16)
    return w_up_c, w_cv_c, b_cv_c, w_dn_c


def _rope_tables(ctx_len, n_latent):
    rows = n_latent // GRID_W
    row = jnp.repeat(jnp.arange(rows), GRID_W)
    col = jnp.tile(jnp.arange(GRID_W), rows)
    n_freq = ROPE_DIM // 4
    inv = ROPE_BASE ** (-jnp.arange(n_freq, dtype=_F32) / n_freq)
    ang = jnp.concatenate([row[:, None] * inv, col[:, None] * inv], axis=-1)
    zeros = jnp.zeros((n_latent, 128 - ROPE_DIM), _F32)
    cos_l = jnp.concatenate([jnp.cos(ang), jnp.cos(ang), zeros], axis=1)
    sin_l = jnp.concatenate([jnp.sin(ang), jnp.sin(ang), zeros], axis=1)
    cos_c = jnp.concatenate([jnp.ones((ctx_len, ROPE_DIM), _F32), jnp.zeros((ctx_len, 128 - ROPE_DIM), _F32)], axis=1)
    return jnp.concatenate([cos_c, cos_l], axis=0), jnp.concatenate([jnp.zeros_like(cos_c), sin_l], axis=0)


def kernel(x, c, ctx, c_ctx, w_ada, b_ada, g_mix, w_in, w_qkconv, b_qkconv, b_mgate, g_mhead, g_qnorm, w_uq,
           g_kvnorm, w_ukv, g_sgu, w_s, b_s, w_branch, w_out, g_ffn, w_up, w_ffconv, b_ffconv, w_down, g_final):
    b, s, d = x.shape
    ctx_len = ctx.shape[1]
    depth = w_in.shape[0]
    assert ctx_len % ROW_TILE == 0 and s % ROW_TILE == 0 and s % GRID_W == 0
    nctx_tiles = ctx_len // ROW_TILE
    nctx_chunks = ctx_len // CHUNK

    cos_t, sin_t = _rope_tables(ctx_len, s)
    cond_rows = jnp.concatenate([c, c_ctx[None, :], jnp.zeros((7, d), c.dtype)], axis=0)
    stream = jnp.concatenate([ctx, x], axis=1)

    for l in range(depth):
        last = l == depth - 1
        mod = _ada(cond_rows, w_ada[l], b_ada[l]).reshape(cond_rows.shape[0], 1, -1)
        p = _proj(stream, mod, g_mix[l], _layout_w_in(w_in[l], d), nctx_tiles)
        hf, hb = _mlstm(p, w_qkconv[l], b_qkconv[l], _layout_gate_bias(b_mgate[l]), nctx_chunks)
        w_k, w_v = _layout_w_ukv(w_ukv[l])
        yb = _attn(p, cos_t, sin_t, g_qnorm[l], _layout_w_uq(w_uq[l]), g_kvnorm[l], w_k, w_v, ctx_len, not last)
        b_s_t = jnp.pad(b_s[l].T, ((0, 0), (0, 128 - HEADS)))
        x_mid, h2 = _merge(hf, hb, p, yb, stream, mod, g_mhead[l], g_sgu[l], w_s[l].astype(_BF16), b_s_t,
                           w_branch[l].astype(_BF16), w_out[l].astype(_BF16), g_ffn[l], nctx_tiles, last)
        ffn_w = _layout_ffn(w_up[l], w_ffconv[l], b_ffconv[l], w_down[l], 256)
        if last:
            stream = _ffn(h2, x_mid, mod, *ffn_w, g_final, (0,), 0, True, 0)
        else:
            stream = _ffn(h2, x_mid, mod, *ffn_w, g_final, (0, ctx_len), nctx_tiles, False, 0)
    return stream
```

```python
import functools

import jax
import jax.numpy as jnp
import numpy as np
from jax import lax
from jax.experimental import pallas as pl
from jax.experimental.pallas import tpu as pltpu

EPS = 1e-6
GRID_W = 64
ROPE_BASE = 10000.0

HEADS = 4
HEAD_DIM = 128
CHUNK = 128
BRANCH_WIDTH = HEADS * HEAD_DIM
Q_LORA = 384
KV_LORA = 256
ROPE_DIM = 64
N_BRANCH = 3
M_GATES = 4 * HEADS

COL_QKV = 0
COL_O = 3 * BRANCH_WIDTH
COL_US = 4 * BRANCH_WIDTH
COL_GT_BLOCK = 1
KV_SRC = KV_LORA + 2 * ROPE_DIM
GATE_W = 256

ROW_TILE = 256
V7X_VMEM_LIMIT = 56 * 1024 * 1024

_HIGHEST = lax.Precision.HIGHEST
_BF16 = jnp.bfloat16
_F32 = jnp.float32


def _cparams(n_axes, vmem=V7X_VMEM_LIMIT):
    return pltpu.CompilerParams(dimension_semantics=("arbitrary",) * n_axes, vmem_limit_bytes=vmem)


def _sigmoid(x):
    return 1.0 / (1.0 + jnp.exp(-x))


def _silu(x):
    return x * _sigmoid(x)


def _gelu_tanh(x):
    return x * (0.5 * (1.0 + jnp.tanh(0.7978845608028654 * (x + 0.044715 * (x * x * x)))))


def _rms(x):
    return x * lax.rsqrt(jnp.mean(x * x, axis=-1, keepdims=True) + EPS)


def _dot(a, b):
    return jnp.dot(a, b, preferred_element_type=_F32)


def _dot_nt(a, b):
    return lax.dot_general(a, b, (((1,), (1,)), ((), ())), preferred_element_type=_F32)


def _dot_tn(a, b):
    return lax.dot_general(a, b, (((0,), (0,)), ((), ())), preferred_element_type=_F32)


def _ada_kernel(c_ref, w_ref, b_ref, o_ref):
    cond = _silu(c_ref[...])
    o_ref[...] = jnp.dot(cond, w_ref[...], precision=_HIGHEST, preferred_element_type=_F32) + b_ref[...]


def _ada(cond_rows, w_ada, b_ada):
    rows, d = cond_rows.shape
    n = w_ada.shape[1]
    return pl.pallas_call(
        _ada_kernel,
        grid=(n // d,),
        in_specs=[pl.BlockSpec((rows, d), lambda j: (0, 0)),
                  pl.BlockSpec((d, d), lambda j: (0, j)),
                  pl.BlockSpec((1, d), lambda j: (0, j))],
        out_specs=pl.BlockSpec((rows, d), lambda j: (0, j)),
        out_shape=jax.ShapeDtypeStruct((rows, n), _F32),
        compiler_params=_cparams(1),
        name="ada",
    )(cond_rows, w_ada, b_ada.reshape(1, n))


def _proj_kernel(x_ref, mod_ref, g_ref, w_ref, o_ref, *, d, col_tile):
    y = _rms(x_ref[...]) * g_ref[...]
    h = (y * (1.0 + mod_ref[:, d:2 * d]) + mod_ref[:, 0:d]).astype(_BF16)
    for j in range(o_ref.shape[1] // col_tile):
        sl = slice(j * col_tile, (j + 1) * col_tile)
        o_ref[:, sl] = _dot(h, w_ref[:, sl])


def _proj(x, mod, g_mix, w_in, nctx_tiles):
    b, t, d = x.shape
    n = w_in.shape[1]
    mod_row = lambda bi, i: (jnp.where(i < nctx_tiles, b, bi), 0, 0)
    return pl.pallas_call(
        functools.partial(_proj_kernel, d=d, col_tile=1024),
        grid=(b, t // ROW_TILE),
        in_specs=[pl.BlockSpec((None, ROW_TILE, d), lambda bi, i: (bi, i, 0)),
                  pl.BlockSpec((None, 1, mod.shape[2]), mod_row),
                  pl.BlockSpec((1, d), lambda bi, i: (0, 0)),
                  pl.BlockSpec((d, n), lambda bi, i: (0, 0))],
        out_specs=pl.BlockSpec((None, ROW_TILE, n), lambda bi, i: (bi, i, 0)),
        out_shape=jax.ShapeDtypeStruct((b, t, n), _F32),
        compiler_params=_cparams(2),
        name="proj",
    )(x, mod, g_mix.reshape(1, d), w_in)


def _mlstm_kernel_v1(qkv_f, hp_f, hn_f, g_f, qkv_b, hp_b, hn_b, g_b, wc_ref, bc_ref, bg_ref,
                     hf_ref, hb_ref, c_scr, n_scr, m_scr, *, nctx, nchunks):
    j = pl.program_id(1)
    tc = CHUNK
    qk_w = 2 * BRANCH_WIDTH

    @pl.when(j == 0)
    def _():
        c_scr[...] = jnp.zeros_like(c_scr)
        n_scr[...] = jnp.zeros_like(n_scr)
        m_scr[...] = jnp.zeros_like(m_scr)

    chunk_f = j
    chunk_b = jnp.where(j < nctx, nctx - 1 - j, nchunks - 1 - (j - nctx))
    row = lax.broadcasted_iota(jnp.int32, (tc, 1), 0)

    def conv_silu(qkv_ref, hp_ref, hn_ref, c):
        x = qkv_ref[:, 0:qk_w]
        seg_first = jnp.logical_or(c == 0, c == nctx)
        seg_last = jnp.logical_or(c == nctx - 1, c == nchunks - 1)
        prev_row = jnp.where(seg_first, 0.0, hp_ref[7:8, :])
        next_row = jnp.where(seg_last, 0.0, hn_ref[0:1, :])
        x_prev = jnp.where(row == 0, prev_row, pltpu.roll(x, 1, axis=0))
        x_next = jnp.where(row == tc - 1, next_row, pltpu.roll(x, tc - 1, axis=0))
        a = bc_ref[...] + x_prev * wc_ref[0:1, :] + x * wc_ref[1:2, :] + x_next * wc_ref[2:3, :]
        return _silu(a)

    qk_dir = (conv_silu(qkv_f, hp_f, hn_f, chunk_f), conv_silu(qkv_b, hp_b, hn_b, chunk_b))
    v_dir = (qkv_f, qkv_b)
    out_dir = (hf_ref, hb_ref)

    lane = lax.broadcasted_iota(jnp.int32, (1, 128), 1)
    is_fwd = lane < HEADS
    gates_i = jnp.where(is_fwd, g_f[:, 0:128], g_b[:, 0:128]) + bg_ref[:, 0:128]
    gates_f = jnp.where(is_fwd, g_f[:, 128:256], g_b[:, 128:256]) + bg_ref[:, 128:256]
    log_i = gates_i
    log_f = jnp.minimum(gates_f, 0.0) - jnp.log1p(jnp.exp(-jnp.abs(gates_f)))

    r_idx = lax.broadcasted_iota(jnp.int32, (tc, tc), 0)
    c_idx = lax.broadcasted_iota(jnp.int32, (tc, tc), 1)
    causal = (c_idx <= r_idx, c_idx >= r_idx)
    cum_f = jnp.dot(causal[0].astype(_F32), log_f, precision=_HIGHEST, preferred_element_type=_F32)
    cum_b = jnp.dot(causal[1].astype(_F32), log_f, precision=_HIGHEST, preferred_element_type=_F32)
    bcum = jnp.where(is_fwd, cum_f, cum_b)
    b_last = jnp.where(is_fwd, bcum[tc - 1:tc, :], bcum[0:1, :])
    m_old = m_scr[...]
    inter = bcum + m_old
    decay = b_last - bcum + log_i
    m_new = jnp.maximum(b_last + m_old, jnp.max(decay, axis=0, keepdims=True))
    w_src = jnp.exp(decay - m_new)
    w_carry = jnp.exp(b_last + m_old - m_new)
    bcum_t = bcum.T
    log_i_t = log_i.T

    for dr in range(2):
        for h in range(HEADS):
            l = dr * HEADS + h
            hs = slice(h * HEAD_DIM, (h + 1) * HEAD_DIM)
            q = qk_dir[dr][:, hs]
            k = qk_dir[dr][:, BRANCH_WIDTH + h * HEAD_DIM:BRANCH_WIDTH + (h + 1) * HEAD_DIM] * (HEAD_DIM ** -0.5)
            v16 = v_dir[dr][:, qk_w + h * HEAD_DIM:qk_w + (h + 1) * HEAD_DIM].astype(_BF16)
            q16 = q.astype(_BF16)
            dmat = bcum[:, l:l + 1] - bcum_t[l:l + 1, :] + log_i_t[l:l + 1, :]
            dmat = jnp.where(causal[dr], dmat, -jnp.inf)
            inter_l = inter[:, l:l + 1]
            m_t = jnp.maximum(inter_l, jnp.max(dmat, axis=1, keepdims=True))
            w_intra = jnp.exp(dmat - m_t)
            w_inter = jnp.exp(inter_l - m_t)
            s = _dot_nt(q16, k.astype(_BF16)) * w_intra
            c_old = c_scr[l]
            n_old = n_scr[l:l + 1, :]
            num = w_inter * _dot(q16, c_old.astype(_BF16)) + _dot(s.astype(_BF16), v16)
            den = w_inter * jnp.sum(q * n_old, axis=1, keepdims=True) + jnp.sum(s, axis=1, keepdims=True)
            out_dir[dr][:, hs] = num / jnp.maximum(jnp.abs(den), jnp.exp(-m_t))
            kw = k * w_src[:, l:l + 1]
            c_scr[l] = w_carry[:, l:l + 1] * c_old + _dot_tn(kw.astype(_BF16), v16)
            n_scr[l:l + 1, :] = w_carry[:, l:l + 1] * n_old + jnp.sum(kw, axis=0, keepdims=True)
    m_scr[...] = m_new


def _mlstm_v1(p, w_qkconv, b_qkconv, b_gate, nctx):
    b, t, n = p.shape
    nchunks = t // CHUNK
    qk_w = 2 * BRANCH_WIDTH
    halo_blocks = CHUNK // 8
    gate_block = (n - GATE_W) // GATE_W

    def bwd_chunk(j):
        return jnp.where(j < nctx, nctx - 1 - j, nchunks - 1 - (j - nctx))

    def specs(chunk_of):
        return [
            pl.BlockSpec((None, CHUNK, 3 * BRANCH_WIDTH), lambda bi, j: (bi, chunk_of(j), 0)),
            pl.BlockSpec((None, 8, qk_w), lambda bi, j: (bi, jnp.maximum(chunk_of(j) * halo_blocks - 1, 0), 0)),
            pl.BlockSpec((None, 8, qk_w),
                         lambda bi, j: (bi, jnp.minimum((chunk_of(j) + 1) * halo_blocks, t // 8 - 1), 0)),
            pl.BlockSpec((None, CHUNK, GATE_W), lambda bi, j: (bi, chunk_of(j), gate_block)),
        ]

    const = lambda shape: pl.BlockSpec(shape, lambda bi, j: (0,) * len(shape))
    h_shape = jax.ShapeDtypeStruct((b, t, BRANCH_WIDTH), _F32)
    return pl.pallas_call(
        functools.partial(_mlstm_kernel_v1, nctx=nctx, nchunks=nchunks),
        grid=(b, nchunks),
        in_specs=specs(lambda j: j) + specs(bwd_chunk) + [const((3, qk_w)), const((1, qk_w)), const((1, GATE_W))],
        out_specs=[pl.BlockSpec((None, CHUNK, BRANCH_WIDTH), lambda bi, j: (bi, j, 0)),
                   pl.BlockSpec((None, CHUNK, BRANCH_WIDTH), lambda bi, j: (bi, bwd_chunk(j), 0))],
        out_shape=[h_shape, h_shape],
        scratch_shapes=[pltpu.VMEM((2 * HEADS, HEAD_DIM, HEAD_DIM), _F32),
                        pltpu.VMEM((2 * HEADS, HEAD_DIM), _F32),
                        pltpu.VMEM((1, 128), _F32)],
        compiler_params=_cparams(2),
        name="mlstm",
    )(p, p, p, p, p, p, p, p, w_qkconv, b_qkconv.reshape(1, qk_w), b_gate)


def _segment_edges(i, seg_tiles, tiles_total):
    first = functools.reduce(jnp.logical_or, [i == s for s in seg_tiles])
    last = functools.reduce(jnp.logical_or, [i == s - 1 for s in seg_tiles[1:] + (tiles_total,)])
    return first, last


def _qkconv_kernel(x_ref, hp_ref, hn_ref, w_ref, b_ref, q_ref, kt_ref, *, seg_tiles, tiles_total):
    tm = x_ref.shape[0]
    seg_first, seg_last = _segment_edges(pl.program_id(1), seg_tiles, tiles_total)
    row = lax.broadcasted_iota(jnp.int32, (tm, 1), 0)
    x = x_ref[...]
    prev_row = jnp.where(seg_first, 0.0, hp_ref[7:8, :])
    next_row = jnp.where(seg_last, 0.0, hn_ref[0:1, :])
    x_prev = jnp.where(row == 0, prev_row, pltpu.roll(x, 1, axis=0))
    x_next = jnp.where(row == tm - 1, next_row, pltpu.roll(x, tm - 1, axis=0))
    a = _silu(b_ref[...] + x_prev * w_ref[0:1, :] + x * w_ref[1:2, :] + x_next * w_ref[2:3, :])
    q_ref[...] = a[:, 0:BRANCH_WIDTH].astype(q_ref.dtype)
    for h in range(HEADS):
        k = a[:, BRANCH_WIDTH + h * HEAD_DIM:BRANCH_WIDTH + (h + 1) * HEAD_DIM] * (HEAD_DIM ** -0.5)
        kt_ref[h] = k.T.astype(kt_ref.dtype)


def _qkconv(p, w_qkconv, b_qkconv, seg_rows):
    b, t, _ = p.shape
    qk_w = 2 * BRANCH_WIDTH
    tiles = t // ROW_TILE
    hb = ROW_TILE // 8
    const = lambda shape: pl.BlockSpec(shape, lambda bi, i: (0,) * len(shape))
    return pl.pallas_call(
        functools.partial(_qkconv_kernel, seg_tiles=tuple(s // ROW_TILE for s in seg_rows), tiles_total=tiles),
        grid=(b, tiles),
        in_specs=[pl.BlockSpec((None, ROW_TILE, qk_w), lambda bi, i: (bi, i, 0)),
                  pl.BlockSpec((None, 8, qk_w), lambda bi, i: (bi, jnp.maximum(i * hb - 1, 0), 0)),
                  pl.BlockSpec((None, 8, qk_w), lambda bi, i: (bi, jnp.minimum((i + 1) * hb, t // 8 - 1), 0)),
                  const((3, qk_w)), const((1, qk_w))],
        out_specs=[pl.BlockSpec((None, ROW_TILE, BRANCH_WIDTH), lambda bi, i: (bi, i, 0)),
                   pl.BlockSpec((None, HEADS, HEAD_DIM, ROW_TILE), lambda bi, i: (bi, 0, 0, i))],
        out_shape=[jax.ShapeDtypeStruct((b, t, BRANCH_WIDTH), _BF16),
                   jax.ShapeDtypeStruct((b, HEADS, HEAD_DIM, t), _BF16)],
        compiler_params=_cparams(2),
        name="qkconv",
    )(p, p, p, w_qkconv, b_qkconv.reshape(1, qk_w))


def _mlstm_kernel(q_f, kt_f, v_f, g_f, q_b, kt_b, v_b, g_b, bg_ref, hf_ref, hb_ref, c_scr, m_scr, *, nctx, nchunks):
    j = pl.program_id(1)
    tc = CHUNK

    @pl.when(j == 0)
    def _():
        c_scr[...] = jnp.zeros_like(c_scr)
        m_scr[...] = jnp.zeros_like(m_scr)

    lane = lax.broadcasted_iota(jnp.int32, (1, 128), 1)
    row = lax.broadcasted_iota(jnp.int32, (tc, 1), 0)
    is_fwd = lane < HEADS
    log_i = jnp.where(is_fwd, g_f[:, 0:128], g_b[:, 0:128]) + bg_ref[:, 0:128]
    gates_f = jnp.where(is_fwd, g_f[:, 128:256], g_b[:, 128:256]) + bg_ref[:, 128:256]
    log_f = jnp.minimum(gates_f, 0.0) - jnp.log(1.0 + jnp.exp(-jnp.abs(gates_f)))

    r_idx = lax.broadcasted_iota(jnp.int32, (tc, tc), 0)
    c_idx = lax.broadcasted_iota(jnp.int32, (tc, tc), 1)
    causal = (c_idx <= r_idx, c_idx >= r_idx)
    part_hi = log_f.astype(_BF16)
    rest = log_f - part_hi.astype(_F32)
    part_mid = rest.astype(_BF16)
    part_lo = (rest - part_mid.astype(_F32)).astype(_BF16)
    cum3 = _dot(causal[0].astype(_BF16), jnp.concatenate([part_hi, part_mid, part_lo], axis=1))
    cum_f = cum3[:, 0:128] + cum3[:, 128:256] + cum3[:, 256:384]
    b_last = cum_f[tc - 1:tc, :]
    bcum = jnp.where(is_fwd, cum_f, b_last - cum_f + log_f)
    r = log_i - bcum

    run_f, run_b, step = r, r, 1
    while step < tc:
        run_f = jnp.maximum(run_f, jnp.where(row >= step, pltpu.roll(run_f, step, axis=0), -jnp.inf))
        run_b = jnp.maximum(run_b, jnp.where(row < tc - step, pltpu.roll(run_b, tc - step, axis=0), -jnp.inf))
        step *= 2
    m_old = m_scr[...]
    g = jnp.maximum(m_old, jnp.where(is_fwd, run_f, run_b))
    decay = b_last + r
    m_new = jnp.maximum(b_last + m_old, jnp.max(decay, axis=0, keepdims=True))
    w_src_t = jnp.exp(decay - m_new).T
    w_carry = jnp.exp(b_last + m_old - m_new)
    r_t = r.T
    ones = jnp.ones((tc, HEAD_DIM), _BF16)

    for dr, (q_ref, kt_ref, v_ref, out_ref) in enumerate(((q_f, kt_f, v_f, hf_ref), (q_b, kt_b, v_b, hb_ref))):
        for h in range(HEADS):
            l = dr * HEADS + h
            hs = slice(h * HEAD_DIM, (h + 1) * HEAD_DIM)
            q16 = q_ref[:, hs]
            kt = kt_ref[h]
            v_ext = jnp.concatenate([v_ref[:, hs].astype(_BF16), ones], axis=1)
            g_l = jnp.broadcast_to(g[:, l:l + 1], (tc, tc))
            b_l = jnp.broadcast_to(bcum[:, l:l + 1], (tc, tc))
            w_intra = jnp.exp(jnp.where(causal[dr], r_t[l:l + 1, :] - g_l, -jnp.inf))
            w_inter = jnp.exp(m_old[:, l:l + 1] - g_l)
            s16 = (_dot(q16, kt) * w_intra).astype(_BF16)
            c_old = c_scr[l]
            qc = _dot(q16, c_old.astype(_BF16))
            sv = _dot(s16, v_ext)
            num = w_inter * qc[:, 0:HEAD_DIM] + sv[:, 0:HEAD_DIM]
            den = w_inter * qc[:, HEAD_DIM:] + sv[:, HEAD_DIM:]
            out_ref[:, hs] = num / jnp.maximum(jnp.abs(den), jnp.exp(-(b_l + g_l)))
            kw_t = (kt.astype(_F32) * w_src_t[l:l + 1, :]).astype(_BF16)
            c_scr[l] = w_carry[:, l:l + 1] * c_old + _dot(kw_t, v_ext)
    m_scr[...] = m_new


def _mlstm(p, q, kt, b_gate, nctx):
    b, t, n = p.shape
    nchunks = t // CHUNK
    gate_block = (n - GATE_W) // GATE_W
    v_block = 2 * BRANCH_WIDTH // BRANCH_WIDTH

    def bwd_chunk(j):
        return jnp.where(j < nctx, nctx - 1 - j, nchunks - 1 - (j - nctx))

    def specs(chunk_of):
        return [
            pl.BlockSpec((None, CHUNK, BRANCH_WIDTH), lambda bi, j: (bi, chunk_of(j), 0)),
            pl.BlockSpec((None, HEADS, HEAD_DIM, CHUNK), lambda bi, j: (bi, 0, 0, chunk_of(j))),
            pl.BlockSpec((None, CHUNK, BRANCH_WIDTH), lambda bi, j: (bi, chunk_of(j), v_block)),
            pl.BlockSpec((None, CHUNK, GATE_W), lambda bi, j: (bi, chunk_of(j), gate_block)),
        ]

    h_shape = jax.ShapeDtypeStruct((b, t, BRANCH_WIDTH), _F32)
    return pl.pallas_call(
        functools.partial(_mlstm_kernel, nctx=nctx, nchunks=nchunks),
        grid=(b, nchunks),
        in_specs=specs(lambda j: j) + specs(bwd_chunk) + [pl.BlockSpec((1, GATE_W), lambda bi, j: (0, 0))],
        out_specs=[pl.BlockSpec((None, CHUNK, BRANCH_WIDTH), lambda bi, j: (bi, j, 0)),
                   pl.BlockSpec((None, CHUNK, BRANCH_WIDTH), lambda bi, j: (bi, bwd_chunk(j), 0))],
        out_shape=[h_shape, h_shape],
        scratch_shapes=[pltpu.VMEM((2 * HEADS, HEAD_DIM, 2 * HEAD_DIM), _F32), pltpu.VMEM((1, 128), _F32)],
        compiler_params=_cparams(2),
        name="mlstm",
    )(q, kt, p, p, q, kt, p, p, b_gate)


def _attn_kernel(cq_ref, kv_ref, ca_ref, sa_ref, caq_ref, saq_ref, gq_ref, wq_ref, gkv_ref, wk_ref, wv_ref,
                 o_ref, k_scr, v_scr, *, ctx_len, need_ctx, scale):
    i = pl.program_id(1)
    t = kv_ref.shape[0]
    tq = cq_ref.shape[0]
    head_w = 2 * HEAD_DIM

    def rope(y, cos_t, sin_t):
        return y * cos_t + pltpu.roll(y, ROPE_DIM, axis=1) * sin_t

    @pl.when(i == 0)
    def _():
        def build(r, carry):
            rows = pl.ds(pl.multiple_of(r * ROW_TILE, ROW_TILE), ROW_TILE)
            ckv = (_rms(kv_ref[rows, 0:KV_LORA]) * gkv_ref[...]).astype(_BF16)
            k_nope = _dot(ckv, wk_ref[...])
            k_rope = rope(kv_ref[rows, KV_LORA:KV_SRC], ca_ref[rows, :], sa_ref[rows, :]).astype(_BF16)
            for h in range(HEADS):
                k_scr[rows, h * head_w:h * head_w + HEAD_DIM] = k_nope[:, h * HEAD_DIM:(h + 1) * HEAD_DIM].astype(_BF16)
                k_scr[rows, h * head_w + HEAD_DIM:(h + 1) * head_w] = k_rope
            v_scr[rows, :] = _dot(ckv, wv_ref[...]).astype(_BF16)
            return carry
        lax.fori_loop(0, t // ROW_TILE, build, 0)

    def attend(n_keys):
        cq = (_rms(cq_ref[...]) * gq_ref[...]).astype(_BF16)
        qa = _dot(cq, wq_ref[...]) * scale
        for h in range(HEADS):
            q_nope = qa[:, h * head_w:h * head_w + HEAD_DIM]
            q_rope = rope(qa[:, h * head_w + HEAD_DIM:(h + 1) * head_w], caq_ref[...], saq_ref[...])
            qh = jnp.concatenate([q_nope, q_rope], axis=1).astype(_BF16)
            s = _dot_nt(qh, k_scr[0:n_keys, h * head_w:(h + 1) * head_w])
            e = jnp.exp(s - jnp.max(s, axis=1, keepdims=True))
            o = _dot(e.astype(_BF16), v_scr[0:n_keys, h * HEAD_DIM:(h + 1) * HEAD_DIM])
            o_ref[:, h * HEAD_DIM:(h + 1) * HEAD_DIM] = (o / jnp.sum(e, axis=1, keepdims=True)).astype(o_ref.dtype)

    @pl.when(i < ctx_len // tq)
    def _():
        if need_ctx:
            attend(ctx_len)
        else:
            o_ref[...] = jnp.zeros_like(o_ref)

    @pl.when(i >= ctx_len // tq)
    def _():
        attend(t)


def _attn(p, cos_t, sin_t, g_qn, w_q, g_kvn, w_k, w_v, ctx_len, need_ctx):
    b, t, n = p.shape
    d3 = p.shape[2]
    gate_cols = GATE_W
    cq_block = (d3 - gate_cols - KV_SRC - Q_LORA) // Q_LORA
    const = lambda shape: pl.BlockSpec(shape, lambda bi, i: (0,) * len(shape))
    scale = (HEAD_DIM + ROPE_DIM) ** -0.5
    return pl.pallas_call(
        functools.partial(_attn_kernel, ctx_len=ctx_len, need_ctx=need_ctx, scale=scale),
        grid=(b, t // ROW_TILE),
        in_specs=[pl.BlockSpec((None, ROW_TILE, Q_LORA), lambda bi, i: (bi, i, cq_block)),
                  pl.BlockSpec((None, t, KV_SRC), lambda bi, i: (bi, 0, cq_block + 1)),
                  const((t, 128)), const((t, 128)),
                  pl.BlockSpec((ROW_TILE, 128), lambda bi, i: (i, 0)),
                  pl.BlockSpec((ROW_TILE, 128), lambda bi, i: (i, 0)),
                  const((1, Q_LORA)), const(w_q.shape), const((1, KV_LORA)), const(w_k.shape), const(w_v.shape)],
        out_specs=pl.BlockSpec((None, ROW_TILE, BRANCH_WIDTH), lambda bi, i: (bi, i, 0)),
        out_shape=jax.ShapeDtypeStruct((b, t, BRANCH_WIDTH), _BF16),
        scratch_shapes=[pltpu.VMEM((t, HEADS * 2 * HEAD_DIM), _BF16), pltpu.VMEM((t, BRANCH_WIDTH), _BF16)],
        compiler_params=_cparams(2),
        name="attn",
    )(p, p, cos_t, sin_t, cos_t, sin_t, g_qn.reshape(1, -1), w_q, g_kvn.reshape(1, -1), w_k, w_v)


def _merge_kernel(hf_ref, hb_ref, o_ref, yb_ref, us_ref, gt_ref, x_ref, mod_ref,
                  gmh_ref, gsgu_ref, ws_ref, bs_ref, wbr_ref, wout_ref, gffn_ref,
                  xo_ref, h2_ref, *, d):
    tm = x_ref.shape[0]
    hsum = hf_ref[...] + hb_ref[...]
    ya = jnp.concatenate(
        [_rms(hsum[:, h * HEAD_DIM:(h + 1) * HEAD_DIM]) for h in range(HEADS)], axis=1) * gmh_ref[...]
    ya = _sigmoid(o_ref[...]) * ya
    u = _gelu_tanh(us_ref[:, 0:BRANCH_WIDTH])
    sv = _gelu_tanh(us_ref[:, BRANCH_WIDTH:2 * BRANCH_WIDTH])
    cols = []
    for g in range(HEADS):
        gs = slice(g * HEAD_DIM, (g + 1) * HEAD_DIM)
        vn = (_rms(sv[:, gs]) * gsgu_ref[:, gs]).astype(_BF16)
        mixed = [_dot(ws_ref[g], vn[n * CHUNK:(n + 1) * CHUNK, :]) + bs_ref[:, g:g + 1] for n in range(tm // CHUNK)]
        cols.append(jnp.concatenate(mixed, axis=0))
    yc = u * jnp.concatenate(cols, axis=1)
    acc = None
    for g, y in enumerate((ya.astype(_BF16), yb_ref[...], yc.astype(_BF16))):
        term = _sigmoid(gt_ref[:, g * d:(g + 1) * d]) * _dot(y, wbr_ref[g])
        acc = term if acc is None else acc + term
    out = _dot(acc.astype(_BF16), wout_ref[...])
    x_new = x_ref[...] + mod_ref[:, 2 * d:3 * d] * out
    xo_ref[...] = x_new
    h2 = _rms(x_new) * gffn_ref[...]
    h2_ref[...] = (h2 * (1.0 + mod_ref[:, 4 * d:5 * d]) + mod_ref[:, 3 * d:4 * d]).astype(h2_ref.dtype)


def _merge(hf, hb, p, yb, x, mod, g_mhead, g_sgu, w_s, b_s_t, w_branch, w_out, g_ffn, nctx_tiles, skip_ctx):
    b, t, d = x.shape
    off = nctx_tiles if skip_ctx else 0
    rows_out = t - off * ROW_TILE
    tile = lambda w, blk: pl.BlockSpec((None, ROW_TILE, w), lambda bi, i: (bi, i + off, blk))
    const = lambda shape: pl.BlockSpec(shape, lambda bi, i: (0,) * len(shape))
    mod_row = lambda bi, i: (jnp.where(i + off < nctx_tiles, b, bi), 0, 0)
    out_tile = lambda: pl.BlockSpec((None, ROW_TILE, d), lambda bi, i: (bi, i, 0))
    return pl.pallas_call(
        functools.partial(_merge_kernel, d=d),
        grid=(b, rows_out // ROW_TILE),
        in_specs=[tile(BRANCH_WIDTH, 0), tile(BRANCH_WIDTH, 0), tile(BRANCH_WIDTH, COL_O // BRANCH_WIDTH),
                  tile(BRANCH_WIDTH, 0), tile(2 * BRANCH_WIDTH, COL_US // (2 * BRANCH_WIDTH)),
                  tile(N_BRANCH * d, COL_GT_BLOCK), tile(d, 0),
                  pl.BlockSpec((None, 1, mod.shape[2]), mod_row),
                  const((1, BRANCH_WIDTH)), const((1, BRANCH_WIDTH)), const(w_s.shape), const(b_s_t.shape),
                  const(w_branch.shape), const(w_out.shape), const((1, d))],
        out_specs=[out_tile(), out_tile()],
        out_shape=[jax.ShapeDtypeStruct((b, rows_out, d), _F32), jax.ShapeDtypeStruct((b, rows_out, d), _BF16)],
        compiler_params=_cparams(2),
        name="merge",
    )(hf, hb, p, yb, p, p, x, mod, g_mhead.reshape(1, -1), g_sgu.reshape(1, -1), w_s, b_s_t,
      w_branch, w_out, g_ffn.reshape(1, d))


FFN_HALO = 16


def _ffn_kernel(h_ref, hp_ref, hn_ref, x_ref, mod_ref, wup_ref, wcv_ref, bcv_ref, wdn_ref, gfin_ref,
                o_ref, ext_scr, act_scr, *, d, seg_tiles, tiles_total, final_norm):
    i = pl.program_id(1)
    tm = h_ref.shape[0]
    n_chunks, _, two_ck = wup_ref.shape
    ck = two_ck // 2
    seg_first = functools.reduce(jnp.logical_or, [i == s for s in seg_tiles])
    seg_last = functools.reduce(jnp.logical_or, [i == s - 1 for s in seg_tiles[1:] + (tiles_total,)])
    ext_scr[0:FFN_HALO, :] = jnp.where(seg_first, jnp.zeros_like(hp_ref), hp_ref[...])
    ext_scr[FFN_HALO:FFN_HALO + tm, :] = h_ref[...]
    ext_scr[FFN_HALO + tm:, :] = jnp.where(seg_last, jnp.zeros_like(hn_ref), hn_ref[...])
    ext_rows = tm + 2 * FFN_HALO

    for c in range(n_chunks):
        a = _dot(ext_scr[...], wup_ref[c])
        w = wcv_ref[c]
        a_prev = pltpu.roll(a, 1, axis=0)[FFN_HALO:FFN_HALO + tm, :]
        a_next = pltpu.roll(a, ext_rows - 1, axis=0)[FFN_HALO:FFN_HALO + tm, :]
        conv = bcv_ref[c] + a_prev * w[0:1, :] + a[FFN_HALO:FFN_HALO + tm, :] * w[1:2, :] + a_next * w[2:3, :]
        act_scr[:, c * ck:(c + 1) * ck] = (_silu(conv[:, 0:ck]) * conv[:, ck:two_ck]).astype(_BF16)

    x_new = x_ref[...] + mod_ref[:, 5 * d:6 * d] * _dot(act_scr[...], wdn_ref[...])
    if final_norm:
        x_new = _rms(x_new) * gfin_ref[...]
    o_ref[...] = x_new


def _ffn(h2, x, mod, w_up, w_cv, b_cv, w_dn, g_final, seg_rows, ctx_tiles, final_norm, row_off):
    b, r, d = x.shape
    off = row_off // ROW_TILE
    rows_out = r - row_off
    tiles = rows_out // ROW_TILE
    seg_tiles = tuple((s - row_off) // ROW_TILE for s in seg_rows)
    hb = ROW_TILE // FFN_HALO
    const = lambda shape: pl.BlockSpec(shape, lambda bi, i: (0,) * len(shape))
    mod_row = lambda bi, i: (jnp.where(i + off < ctx_tiles, b, bi), 0, 0)
    return pl.pallas_call(
        functools.partial(_ffn_kernel, d=d, seg_tiles=seg_tiles, tiles_total=tiles, final_norm=final_norm),
        grid=(b, tiles),
        in_specs=[pl.BlockSpec((None, ROW_TILE, d), lambda bi, i: (bi, i + off, 0)),
                  pl.BlockSpec((None, FFN_HALO, d), lambda bi, i: (bi, jnp.maximum((i + off) * hb - 1, 0), 0)),
                  pl.BlockSpec((None, FFN_HALO, d),
                               lambda bi, i: (bi, jnp.minimum((i + off + 1) * hb, r // FFN_HALO - 1), 0)),
                  pl.BlockSpec((None, ROW_TILE, d), lambda bi, i: (bi, i + off, 0)),
                  pl.BlockSpec((None, 1, mod.shape[2]), mod_row),
                  const(w_up.shape), const(w_cv.shape), const(b_cv.shape), const(w_dn.shape), const((1, d))],
        out_specs=pl.BlockSpec((None, ROW_TILE, d), lambda bi, i: (bi, i, 0)),
        out_shape=jax.ShapeDtypeStruct((b, rows_out, d), _F32),
        scratch_shapes=[pltpu.VMEM((ROW_TILE + 2 * FFN_HALO, d), _BF16),
                        pltpu.VMEM((ROW_TILE, w_dn.shape[0]), _BF16)],
        compiler_params=_cparams(2),
        name="ffn",
    )(h2, h2, h2, x, mod, w_up, w_cv, b_cv, w_dn, g_final.reshape(1, d))


def _deinterleave(w):
    return jnp.concatenate([w[..., 0::2], w[..., 1::2]], axis=-1)


def _rotated(w):
    return jnp.concatenate([-w[..., 1::2], w[..., 0::2]], axis=-1)


def _layout_w_in(w, d):
    sizes = (BRANCH_WIDTH,) * 4 + (M_GATES, Q_LORA, KV_LORA, ROPE_DIM, BRANCH_WIDTH, BRANCH_WIDTH, N_BRANCH * d)
    splits = tuple(int(s) for s in np.cumsum(sizes)[:-1])
    q, k, v, o, mg, cq, ckv, kr, u, s, gt = jnp.split(w, splits, axis=1)
    mg = mg.reshape(d, 2, 2, HEADS)
    pad = jnp.zeros((d, 128 - 2 * HEADS), w.dtype)
    cols = [q, k, v, o, u, s, gt, cq, ckv, _deinterleave(kr), _rotated(kr),
            mg[:, :, 0, :].reshape(d, 2 * HEADS), pad, mg[:, :, 1, :].reshape(d, 2 * HEADS), pad]
    return jnp.concatenate(cols, axis=1).astype(_BF16)


def _layout_gate_bias(bg):
    bg = bg.reshape(2, 2, HEADS)
    pad = jnp.zeros((128 - 2 * HEADS,), bg.dtype)
    return jnp.concatenate([bg[:, 0, :].reshape(-1), pad, bg[:, 1, :].reshape(-1), pad]).reshape(1, GATE_W)


def _layout_w_uq(w):
    w = w.reshape(Q_LORA, HEADS, HEAD_DIM + ROPE_DIM)
    nope, rope = w[..., :HEAD_DIM], w[..., HEAD_DIM:]
    return jnp.concatenate([nope, _deinterleave(rope), _rotated(rope)], axis=-1).reshape(Q_LORA, -1).astype(_BF16)


def _layout_w_ukv(w):
    w = w.reshape(KV_LORA, HEADS, 2 * HEAD_DIM)
    return (w[..., :HEAD_DIM].reshape(KV_LORA, -1).astype(_BF16), w[..., HEAD_DIM:].reshape(KV_LORA, -1).astype(_BF16))


def _layout_ffn(w_up, w_cv, b_cv, w_dn, ck):
    d, two_ff = w_up.shape
    ff = two_ff // 2
    nc = ff // ck
    pair = lambda a: jnp.concatenate([a[..., :ff].reshape(a.shape[:-1] + (nc, ck)),
                                      a[..., ff:].reshape(a.shape[:-1] + (nc, ck))], axis=-1)
    w_up_c = jnp.moveaxis(pair(w_up), 1, 0).astype(_BF16)
    w_cv_c = jnp.moveaxis(pair(w_cv), 1, 0)
    b_cv_c = pair(b_cv).reshape(nc, 1, 2 * ck)
    return w_up_c, w_cv_c, b_cv_c, w_dn.astype(_BF16)


def _rope_tables(ctx_len, n_latent):
    rows = n_latent // GRID_W
    row = jnp.repeat(jnp.arange(rows), GRID_W)
    col = jnp.tile(jnp.arange(GRID_W), rows)
    n_freq = ROPE_DIM // 4
    inv = ROPE_BASE ** (-jnp.arange(n_freq, dtype=_F32) / n_freq)
    ang = jnp.concatenate([row[:, None] * inv, col[:, None] * inv], axis=-1)
    zeros = jnp.zeros((n_latent, 128 - ROPE_DIM), _F32)
    cos_l = jnp.concatenate([jnp.cos(ang), jnp.cos(ang), zeros], axis=1)
    sin_l = jnp.concatenate([jnp.sin(ang), jnp.sin(ang), zeros], axis=1)
    cos_c = jnp.concatenate([jnp.ones((ctx_len, ROPE_DIM), _F32), jnp.zeros((ctx_len, 128 - ROPE_DIM), _F32)], axis=1)
    return jnp.concatenate([cos_c, cos_l], axis=0), jnp.concatenate([jnp.zeros_like(cos_c), sin_l], axis=0)


def kernel(x, c, ctx, c_ctx, w_ada, b_ada, g_mix, w_in, w_qkconv, b_qkconv, b_mgate, g_mhead, g_qnorm, w_uq,
           g_kvnorm, w_ukv, g_sgu, w_s, b_s, w_branch, w_out, g_ffn, w_up, w_ffconv, b_ffconv, w_down, g_final):
    b, s, d = x.shape
    ctx_len = ctx.shape[1]
    depth = w_in.shape[0]
    assert ctx_len % ROW_TILE == 0 and s % ROW_TILE == 0 and s % GRID_W == 0
    nctx_tiles = ctx_len // ROW_TILE
    nctx_chunks = ctx_len // CHUNK

    cos_t, sin_t = _rope_tables(ctx_len, s)
    cond_rows = jnp.concatenate([c, c_ctx[None, :], jnp.zeros((7, d), c.dtype)], axis=0)
    stream = jnp.concatenate([ctx, x], axis=1)

    for l in range(depth):
        last = l == depth - 1
        mod = _ada(cond_rows, w_ada[l], b_ada[l]).reshape(cond_rows.shape[0], 1, -1)
        p = _proj(stream, mod, g_mix[l], _layout_w_in(w_in[l], d), nctx_tiles)
        q_act, kt_act = _qkconv(p, w_qkconv[l], b_qkconv[l], (0, ctx_len))
        hf, hb = _mlstm(p, q_act, kt_act, _layout_gate_bias(b_mgate[l]), nctx_chunks)
        w_k, w_v = _layout_w_ukv(w_ukv[l])
        yb = _attn(p, cos_t, sin_t, g_qnorm[l], _layout_w_uq(w_uq[l]), g_kvnorm[l], w_k, w_v, ctx_len, not last)
        b_s_t = jnp.pad(b_s[l].T, ((0, 0), (0, 128 - HEADS)))
        x_mid, h2 = _merge(hf, hb, p, yb, stream, mod, g_mhead[l], g_sgu[l], w_s[l].astype(_BF16), b_s_t,
                           w_branch[l].astype(_BF16), w_out[l].astype(_BF16), g_ffn[l], nctx_tiles, last)
        ffn_w = _layout_ffn(w_up[l], w_ffconv[l], b_ffconv[l], w_down[l], 256)
        if last:
            stream = _ffn(h2, x_mid, mod, *ffn_w, g_final, (0,), 0, True, 0)
        else:
            stream = _ffn(h2, x_mid, mod, *ffn_w, g_final, (0, ctx_len), nctx_tiles, False, 0)
    return stream
```

```python
import functools

import jax
import jax.numpy as jnp
import numpy as np
from jax import lax
from jax.experimental import pallas as pl
from jax.experimental.pallas import tpu as pltpu

EPS = 1e-6
GRID_W = 64
ROPE_BASE = 10000.0
LOG2_E = 1.4426950408889634

HEADS = 4
HEAD_DIM = 128
CHUNK = 128
BRANCH_WIDTH = HEADS * HEAD_DIM
Q_LORA = 384
KV_LORA = 256
ROPE_DIM = 64
N_BRANCH = 3
M_GATES = 4 * HEADS
KV_SRC = KV_LORA + 2 * ROPE_DIM
ATTN_W = Q_LORA + KV_SRC
GATE_W = 256

ROW_TILE = 256
FFN_HALO = 16
FFN_CHUNK = 256
V7X_VMEM_LIMIT = 56 * 1024 * 1024

_BF16 = jnp.bfloat16
_F32 = jnp.float32


def _cparams(n_axes):
    return pltpu.CompilerParams(dimension_semantics=("arbitrary",) * n_axes, vmem_limit_bytes=V7X_VMEM_LIMIT)


def _const_spec(shape):
    return pl.BlockSpec(tuple(shape), lambda *_: (0,) * len(shape))


def _layer_spec(stacked_shape, layer):
    shape = tuple(stacked_shape[1:])
    return pl.BlockSpec((None,) + shape, lambda *_: (layer,) + (0,) * len(shape))


def _mod_spec(mod, layer, nctx_tiles, off):
    batch = mod.shape[1] - 8
    return pl.BlockSpec((None, None, 1, mod.shape[3]),
                        lambda bi, i: (layer, jnp.where(i + off < nctx_tiles, batch, bi), 0, 0))


def _row_specs(width, nctx_tiles, lat_off, off=0):
    ctx = pl.BlockSpec((None, ROW_TILE, width), lambda bi, i: (bi, jnp.minimum(i + off, nctx_tiles - 1), 0))
    lat = pl.BlockSpec((None, ROW_TILE, width), lambda bi, i: (bi, jnp.maximum(i + off - nctx_tiles, 0) + lat_off, 0))
    return [ctx, lat]


def _sigmoid(x):
    return 1.0 / (1.0 + jnp.exp(-x))


def _silu(x):
    return x * _sigmoid(x)


def _gelu_tanh(x):
    return x * (0.5 * (1.0 + jnp.tanh(0.7978845608028654 * (x + 0.044715 * (x * x * x)))))


def _rms(x):
    return x * lax.rsqrt(jnp.mean(x * x, axis=-1, keepdims=True) + EPS)


def _dot(a, b):
    return jnp.dot(a, b, preferred_element_type=_F32)


def _dot_nt(a, b):
    return lax.dot_general(a, b, (((1,), (1,)), ((), ())), preferred_element_type=_F32)


def _segment_edges(i, seg_tiles, tiles_total):
    first = functools.reduce(jnp.logical_or, [i == s for s in seg_tiles])
    last = functools.reduce(jnp.logical_or, [i == s - 1 for s in seg_tiles[1:] + (tiles_total,)])
    return first, last


def _ada_kernel(c_ref, w_ref, b_ref, o_ref):
    cond = _silu(c_ref[...])
    o_ref[...] = jnp.dot(cond, w_ref[...], precision=lax.Precision.HIGHEST, preferred_element_type=_F32) + b_ref[...]


def _ada(cond_rows, w_ada, b_ada):
    rows, d = cond_rows.shape
    depth, _, n = w_ada.shape
    out = pl.pallas_call(
        _ada_kernel,
        grid=(depth, n // d),
        in_specs=[pl.BlockSpec((rows, d), lambda l, j: (0, 0)),
                  pl.BlockSpec((None, d, d), lambda l, j: (l, 0, j)),
                  pl.BlockSpec((None, 1, d), lambda l, j: (l, 0, j))],
        out_specs=pl.BlockSpec((None, rows, d), lambda l, j: (l, 0, j)),
        out_shape=jax.ShapeDtypeStruct((depth, rows, n), _F32),
        compiler_params=_cparams(2),
        name="ada",
    )(cond_rows, w_ada, b_ada.reshape(depth, 1, n))
    return out.reshape(depth, rows, 1, n)


def _proj_kernel(xc_ref, xl_ref, mod_ref, g_ref, w_ref, qk_ref, vo_ref, us_ref, gt_ref, at_ref, gate_ref,
                 *, d, nctx_tiles, ctx_needs_all):
    i = pl.program_id(1)
    bw = BRANCH_WIDTH
    x = jnp.where(i < nctx_tiles, xc_ref[...], xl_ref[...])
    h = (_rms(x) * g_ref[...] * (1.0 + mod_ref[:, d:2 * d]) + mod_ref[:, 0:d]).astype(_BF16)
    col_at = 6 * bw + N_BRANCH * d
    qk_ref[...] = _dot(h, w_ref[:, 0:2 * bw])
    vo = _dot(h, w_ref[:, 2 * bw:4 * bw])
    vo_ref[:, 0:bw] = vo[:, 0:bw].astype(_BF16)
    at_ref[...] = _dot(h, w_ref[:, col_at:col_at + ATTN_W]).astype(_BF16)
    gate_ref[...] = _dot(h, w_ref[:, col_at + ATTN_W:col_at + ATTN_W + GATE_W])

    def mixer_inputs():
        vo_ref[:, bw:2 * bw] = _sigmoid(vo[:, bw:2 * bw]).astype(_BF16)
        us_ref[...] = _gelu_tanh(_dot(h, w_ref[:, 4 * bw:6 * bw])).astype(_BF16)
        for g in range(N_BRANCH):
            cols = slice(6 * bw + g * d, 6 * bw + (g + 1) * d)
            gt_ref[:, g * d:(g + 1) * d] = _sigmoid(_dot(h, w_ref[:, cols])).astype(_BF16)

    if ctx_needs_all:
        mixer_inputs()
    else:
        pl.when(i >= nctx_tiles)(mixer_inputs)

        @pl.when(i < nctx_tiles)
        def _():
            vo_ref[:, bw:2 * bw] = jnp.zeros((ROW_TILE, bw), _BF16)
            us_ref[...] = jnp.zeros_like(us_ref)
            gt_ref[...] = jnp.zeros_like(gt_ref)


def _proj(x_ctx, x_lat, lat_off, mod, g_mix, w_in, layer, nctx_tiles, ctx_needs_all):
    b, _, d = x_ctx.shape
    t = nctx_tiles * ROW_TILE + (x_lat.shape[1] - lat_off * ROW_TILE)
    bw = BRANCH_WIDTH
    widths = (2 * bw, 2 * bw, 2 * bw, N_BRANCH * d, ATTN_W, GATE_W)
    dtypes = (_F32, _BF16, _BF16, _BF16, _BF16, _F32)
    assert sum(widths) == w_in.shape[2]
    return pl.pallas_call(
        functools.partial(_proj_kernel, d=d, nctx_tiles=nctx_tiles, ctx_needs_all=ctx_needs_all),
        grid=(b, t // ROW_TILE),
        in_specs=_row_specs(d, nctx_tiles, lat_off) + [_mod_spec(mod, layer, nctx_tiles, 0),
                                                      _layer_spec(g_mix.shape, layer), _layer_spec(w_in.shape, layer)],
        out_specs=[pl.BlockSpec((None, ROW_TILE, w), lambda bi, i: (bi, i, 0)) for w in widths],
        out_shape=[jax.ShapeDtypeStruct((b, t, w), dt) for w, dt in zip(widths, dtypes)],
        compiler_params=_cparams(2),
        name="proj",
    )(x_ctx, x_lat, mod, g_mix, w_in)


def _qkconv_kernel(x_ref, hp_ref, hn_ref, w_ref, b_ref, q_ref, kt_ref, *, seg_tiles, tiles_total):
    tm = x_ref.shape[0]
    seg_first, seg_last = _segment_edges(pl.program_id(1), seg_tiles, tiles_total)
    row = lax.broadcasted_iota(jnp.int32, (tm, 1), 0)
    x = x_ref[...]
    prev_row = jnp.where(seg_first, 0.0, hp_ref[7:8, :])
    next_row = jnp.where(seg_last, 0.0, hn_ref[0:1, :])
    x_prev = jnp.where(row == 0, prev_row, pltpu.roll(x, 1, axis=0))
    x_next = jnp.where(row == tm - 1, next_row, pltpu.roll(x, tm - 1, axis=0))
    a = _silu(b_ref[...] + x_prev * w_ref[0:1, :] + x * w_ref[1:2, :] + x_next * w_ref[2:3, :])
    q_ref[...] = a[:, 0:BRANCH_WIDTH].astype(q_ref.dtype)
    for h in range(HEADS):
        k = a[:, BRANCH_WIDTH + h * HEAD_DIM:BRANCH_WIDTH + (h + 1) * HEAD_DIM] * (HEAD_DIM ** -0.5)
        kt_ref[h] = k.T.astype(kt_ref.dtype)


def _qkconv(qk, w_qkconv, b_qkconv, layer, seg_rows):
    b, t, qk_w = qk.shape
    tiles = t // ROW_TILE
    hb = ROW_TILE // 8
    return pl.pallas_call(
        functools.partial(_qkconv_kernel, seg_tiles=tuple(s // ROW_TILE for s in seg_rows), tiles_total=tiles),
        grid=(b, tiles),
        in_specs=[pl.BlockSpec((None, ROW_TILE, qk_w), lambda bi, i: (bi, i, 0)),
                  pl.BlockSpec((None, 8, qk_w), lambda bi, i: (bi, jnp.maximum(i * hb - 1, 0), 0)),
                  pl.BlockSpec((None, 8, qk_w), lambda bi, i: (bi, jnp.minimum((i + 1) * hb, t // 8 - 1), 0)),
                  _layer_spec(w_qkconv.shape, layer), _layer_spec(b_qkconv.shape, layer)],
        out_specs=[pl.BlockSpec((None, ROW_TILE, BRANCH_WIDTH), lambda bi, i: (bi, i, 0)),
                   pl.BlockSpec((None, HEADS, HEAD_DIM, ROW_TILE), lambda bi, i: (bi, 0, 0, i))],
        out_shape=[jax.ShapeDtypeStruct((b, t, BRANCH_WIDTH), _BF16),
                   jax.ShapeDtypeStruct((b, HEADS, HEAD_DIM, t), _BF16)],
        compiler_params=_cparams(2),
        name="qkconv",
    )(qk, qk, qk, w_qkconv, b_qkconv)


def _mlstm_kernel(q_f, kt_f, v_f, g_f, q_b, kt_b, v_b, g_b, bg_ref, hf_ref, hb_ref, c_scr, m_scr):
    tc = CHUNK

    @pl.when(pl.program_id(1) == 0)
    def _():
        c_scr[...] = jnp.zeros_like(c_scr)
        m_scr[...] = jnp.zeros_like(m_scr)

    lane = lax.broadcasted_iota(jnp.int32, (1, 128), 1)
    row = lax.broadcasted_iota(jnp.int32, (tc, 1), 0)
    is_fwd = lane < HEADS
    log_i = jnp.where(is_fwd, g_f[:, 0:128], g_b[:, 0:128]) + bg_ref[:, 0:128]
    gates_f = jnp.where(is_fwd, g_f[:, 128:256], g_b[:, 128:256]) + bg_ref[:, 128:256]
    log_f = jnp.minimum(gates_f, 0.0) - jnp.log(1.0 + jnp.exp(-jnp.abs(gates_f)))

    r_idx = lax.broadcasted_iota(jnp.int32, (tc, tc), 0)
    c_idx = lax.broadcasted_iota(jnp.int32, (tc, tc), 1)
    causal = (c_idx <= r_idx, c_idx >= r_idx)
    part_hi = log_f.astype(_BF16)
    rest = log_f - part_hi.astype(_F32)
    part_mid = rest.astype(_BF16)
    part_lo = (rest - part_mid.astype(_F32)).astype(_BF16)
    cum3 = _dot(causal[0].astype(_BF16), jnp.concatenate([part_hi, part_mid, part_lo], axis=1))
    cum_f = cum3[:, 0:128] + cum3[:, 128:256] + cum3[:, 256:384]
    b_last = cum_f[tc - 1:tc, :]
    bcum = jnp.where(is_fwd, cum_f, b_last - cum_f + log_f)
    r = log_i - bcum

    run_f, run_b, step = r, r, 1
    while step < tc:
        run_f = jnp.maximum(run_f, jnp.where(row >= step, pltpu.roll(run_f, step, axis=0), -jnp.inf))
        run_b = jnp.maximum(run_b, jnp.where(row < tc - step, pltpu.roll(run_b, tc - step, axis=0), -jnp.inf))
        step *= 2
    m_old = m_scr[...]
    g = jnp.maximum(m_old, jnp.where(is_fwd, run_f, run_b))
    decay = b_last + r
    m_new = jnp.maximum(b_last + m_old, jnp.max(decay, axis=0, keepdims=True))
    w_src_t = jnp.exp(decay - m_new).T
    w_carry = jnp.exp(b_last + m_old - m_new)
    r_t = r.T
    ones = jnp.ones((tc, HEAD_DIM), _BF16)

    for dr, (q_ref, kt_ref, v_ref, out_ref) in enumerate(((q_f, kt_f, v_f, hf_ref), (q_b, kt_b, v_b, hb_ref))):
        for h in range(HEADS):
            l = dr * HEADS + h
            hs = slice(h * HEAD_DIM, (h + 1) * HEAD_DIM)
            q16 = q_ref[:, hs]
            kt = kt_ref[h]
            v_ext = jnp.concatenate([v_ref[:, hs], ones], axis=1)
            g_l = jnp.broadcast_to(g[:, l:l + 1], (tc, tc))
            b_l = jnp.broadcast_to(bcum[:, l:l + 1], (tc, tc))
            w_intra = jnp.exp(jnp.where(causal[dr], r_t[l:l + 1, :] - g_l, -jnp.inf))
            w_inter = jnp.exp(m_old[:, l:l + 1] - g_l)
            s16 = (_dot(q16, kt) * w_intra).astype(_BF16)
            c_old = c_scr[l]
            qc = _dot(q16, c_old.astype(_BF16))
            sv = _dot(s16, v_ext)
            num = w_inter * qc[:, 0:HEAD_DIM] + sv[:, 0:HEAD_DIM]
            den = w_inter * qc[:, HEAD_DIM:] + sv[:, HEAD_DIM:]
            out_ref[:, hs] = num / jnp.maximum(jnp.abs(den), jnp.exp(-(b_l + g_l)))
            kw_t = (kt.astype(_F32) * w_src_t[l:l + 1, :]).astype(_BF16)
            c_scr[l] = w_carry[:, l:l + 1] * c_old + _dot(kw_t, v_ext)
    m_scr[...] = m_new


def _mlstm(q, kt, vo, gates, b_gate, layer, nctx):
    b, t, _ = q.shape
    nchunks = t // CHUNK

    def bwd_chunk(j):
        return jnp.where(j < nctx, nctx - 1 - j, nchunks - 1 - (j - nctx))

    def specs(chunk_of):
        return [
            pl.BlockSpec((None, CHUNK, BRANCH_WIDTH), lambda bi, j: (bi, chunk_of(j), 0)),
            pl.BlockSpec((None, HEADS, HEAD_DIM, CHUNK), lambda bi, j: (bi, 0, 0, chunk_of(j))),
            pl.BlockSpec((None, CHUNK, BRANCH_WIDTH), lambda bi, j: (bi, chunk_of(j), 0)),
            pl.BlockSpec((None, CHUNK, GATE_W), lambda bi, j: (bi, chunk_of(j), 0)),
        ]

    h_shape = jax.ShapeDtypeStruct((b, t, BRANCH_WIDTH), _F32)
    return pl.pallas_call(
        _mlstm_kernel,
        grid=(b, nchunks),
        in_specs=specs(lambda j: j) + specs(bwd_chunk) + [_layer_spec(b_gate.shape, layer)],
        out_specs=[pl.BlockSpec((None, CHUNK, BRANCH_WIDTH), lambda bi, j: (bi, j, 0)),
                   pl.BlockSpec((None, CHUNK, BRANCH_WIDTH), lambda bi, j: (bi, bwd_chunk(j), 0))],
        out_shape=[h_shape, h_shape],
        scratch_shapes=[pltpu.VMEM((2 * HEADS, HEAD_DIM, 2 * HEAD_DIM), _F32), pltpu.VMEM((1, 128), _F32)],
        compiler_params=_cparams(2),
        name="mlstm",
    )(q, kt, vo, gates, q, kt, vo, gates, b_gate)


def _attn_kernel(cq_ref, kv_ref, ca_ref, sa_ref, caq_ref, saq_ref, gq_ref, wq_ref, gkv_ref, wk_ref, wv_ref,
                 o_ref, k_scr, v_scr, *, ctx_len, need_ctx, scale):
    i = pl.program_id(1)
    t = kv_ref.shape[0]
    tq = cq_ref.shape[0]
    head_w = 2 * HEAD_DIM

    def rope(y, cos_t, sin_t):
        return y * cos_t + pltpu.roll(y, ROPE_DIM, axis=1) * sin_t

    @pl.when(i == 0)
    def _():
        for r in range(t // ROW_TILE):
            rows = slice(r * ROW_TILE, (r + 1) * ROW_TILE)
            ckv = (_rms(kv_ref[rows, 0:KV_LORA].astype(_F32)) * gkv_ref[...]).astype(_BF16)
            k_nope = _dot(ckv, wk_ref[...])
            k_rope = rope(kv_ref[rows, KV_LORA:KV_SRC].astype(_F32), ca_ref[rows, :], sa_ref[rows, :]).astype(_BF16)
            for h in range(HEADS):
                k_scr[rows, h * head_w:h * head_w + HEAD_DIM] = k_nope[:, h * HEAD_DIM:(h + 1) * HEAD_DIM].astype(_BF16)
                k_scr[rows, h * head_w + HEAD_DIM:(h + 1) * head_w] = k_rope
            v_scr[rows, :] = _dot(ckv, wv_ref[...]).astype(_BF16)

    def attend(n_keys):
        cq = (_rms(cq_ref[...].astype(_F32)) * gq_ref[...]).astype(_BF16)
        qa = _dot(cq, wq_ref[...]) * (scale * LOG2_E)
        for h in range(HEADS):
            q_nope = qa[:, h * head_w:h * head_w + HEAD_DIM]
            q_rope = rope(qa[:, h * head_w + HEAD_DIM:(h + 1) * head_w], caq_ref[...], saq_ref[...])
            qh = jnp.concatenate([q_nope, q_rope], axis=1).astype(_BF16)
            s = _dot_nt(qh, k_scr[0:n_keys, h * head_w:(h + 1) * head_w])
            e = jnp.exp2(s - jnp.max(s, axis=1, keepdims=True))
            o = _dot(e.astype(_BF16), v_scr[0:n_keys, h * HEAD_DIM:(h + 1) * HEAD_DIM])
            o_ref[:, h * HEAD_DIM:(h + 1) * HEAD_DIM] = (o / jnp.sum(e, axis=1, keepdims=True)).astype(o_ref.dtype)

    @pl.when(i < ctx_len // tq)
    def _():
        if need_ctx:
            attend(ctx_len)
        else:
            o_ref[...] = jnp.zeros_like(o_ref)

    @pl.when(i >= ctx_len // tq)
    def _():
        attend(t)


def _attn(at, cos_t, sin_t, g_qn, w_q, g_kvn, w_k, w_v, layer, ctx_len, need_ctx):
    b, t, _ = at.shape
    scale = (HEAD_DIM + ROPE_DIM) ** -0.5
    return pl.pallas_call(
        functools.partial(_attn_kernel, ctx_len=ctx_len, need_ctx=need_ctx, scale=scale),
        grid=(b, t // ROW_TILE),
        in_specs=[pl.BlockSpec((None, ROW_TILE, Q_LORA), lambda bi, i: (bi, i, 0)),
                  pl.BlockSpec((None, t, KV_SRC), lambda bi, i: (bi, 0, 1)),
                  _const_spec((t, 128)), _const_spec((t, 128)),
                  pl.BlockSpec((ROW_TILE, 128), lambda bi, i: (i, 0)),
                  pl.BlockSpec((ROW_TILE, 128), lambda bi, i: (i, 0)),
                  _layer_spec(g_qn.shape, layer), _layer_spec(w_q.shape, layer), _layer_spec(g_kvn.shape, layer),
                  _layer_spec(w_k.shape, layer), _layer_spec(w_v.shape, layer)],
        out_specs=pl.BlockSpec((None, ROW_TILE, BRANCH_WIDTH), lambda bi, i: (bi, i, 0)),
        out_shape=jax.ShapeDtypeStruct((b, t, BRANCH_WIDTH), _BF16),
        scratch_shapes=[pltpu.VMEM((t, HEADS * 2 * HEAD_DIM), _BF16), pltpu.VMEM((t, BRANCH_WIDTH), _BF16)],
        compiler_params=_cparams(2),
        name="attn",
    )(at, at, cos_t, sin_t, cos_t, sin_t, g_qn, w_q, g_kvn, w_k, w_v)


def _merge_kernel(hf_ref, hb_ref, so_ref, yb_ref, us_ref, gt_ref, xc_ref, xl_ref, mod_ref,
                  gmh_ref, gsgu_ref, ws_ref, bs_ref, wbr_ref, wout_ref, gffn_ref,
                  xo_ref, h2_ref, *, d, nctx_tiles, off):
    tm = xo_ref.shape[0]
    x = jnp.where(pl.program_id(1) + off < nctx_tiles, xc_ref[...], xl_ref[...])
    hsum = hf_ref[...] + hb_ref[...]
    ya = jnp.concatenate(
        [_rms(hsum[:, h * HEAD_DIM:(h + 1) * HEAD_DIM]) for h in range(HEADS)], axis=1) * gmh_ref[...]
    ya = so_ref[...].astype(_F32) * ya
    cols = []
    for g in range(HEADS):
        gs = slice(BRANCH_WIDTH + g * HEAD_DIM, BRANCH_WIDTH + (g + 1) * HEAD_DIM)
        vn = (_rms(us_ref[:, gs].astype(_F32)) * gsgu_ref[:, g * HEAD_DIM:(g + 1) * HEAD_DIM]).astype(_BF16)
        mixed = [_dot(ws_ref[g], vn[n * CHUNK:(n + 1) * CHUNK, :]) + bs_ref[:, g:g + 1] for n in range(tm // CHUNK)]
        cols.append(jnp.concatenate(mixed, axis=0))
    yc = us_ref[:, 0:BRANCH_WIDTH].astype(_F32) * jnp.concatenate(cols, axis=1)
    acc = None
    for g, y in enumerate((ya.astype(_BF16), yb_ref[...], yc.astype(_BF16))):
        term = gt_ref[:, g * d:(g + 1) * d].astype(_F32) * _dot(y, wbr_ref[g])
        acc = term if acc is None else acc + term
    out = _dot(acc.astype(_BF16), wout_ref[...])
    x_new = x + mod_ref[:, 2 * d:3 * d] * out
    xo_ref[...] = x_new
    h2 = _rms(x_new) * gffn_ref[...]
    h2_ref[...] = (h2 * (1.0 + mod_ref[:, 4 * d:5 * d]) + mod_ref[:, 3 * d:4 * d]).astype(h2_ref.dtype)


def _merge(hf, hb, vo, yb, us, gt, x_ctx, x_lat, lat_off, mod, g_mhead, g_sgu, w_s, b_s_t, w_branch, w_out, g_ffn,
           layer, nctx_tiles, skip_ctx):
    b, t, _ = hf.shape
    d = x_ctx.shape[2]
    off = nctx_tiles if skip_ctx else 0
    rows_out = t - off * ROW_TILE
    tile = lambda w, blk: pl.BlockSpec((None, ROW_TILE, w), lambda bi, i: (bi, i + off, blk))
    out_tile = lambda: pl.BlockSpec((None, ROW_TILE, d), lambda bi, i: (bi, i, 0))
    return pl.pallas_call(
        functools.partial(_merge_kernel, d=d, nctx_tiles=nctx_tiles, off=off),
        grid=(b, rows_out // ROW_TILE),
        in_specs=[tile(BRANCH_WIDTH, 0), tile(BRANCH_WIDTH, 0), tile(BRANCH_WIDTH, 1), tile(BRANCH_WIDTH, 0),
                  tile(2 * BRANCH_WIDTH, 0), tile(N_BRANCH * d, 0)]
        + _row_specs(d, nctx_tiles, lat_off, off) + [_mod_spec(mod, layer, nctx_tiles, off)]
        + [_layer_spec(a.shape, layer) for a in (g_mhead, g_sgu, w_s, b_s_t, w_branch, w_out, g_ffn)],
        out_specs=[out_tile(), out_tile()],
        out_shape=[jax.ShapeDtypeStruct((b, rows_out, d), _F32), jax.ShapeDtypeStruct((b, rows_out, d), _BF16)],
        compiler_params=_cparams(2),
        name="merge",
    )(hf, hb, vo, yb, us, gt, x_ctx, x_lat, mod, g_mhead, g_sgu, w_s, b_s_t, w_branch, w_out, g_ffn)


def _ffn_kernel(h_ref, hp_ref, hn_ref, x_ref, mod_ref, wup_ref, wcv_ref, bcv_ref, wdn_ref, gfin_ref,
                o_ref, ext_scr, act_scr, *, d, seg_tiles, tiles_total, final_norm):
    tm = h_ref.shape[0]
    ff = wdn_ref.shape[0]
    ck = FFN_CHUNK
    seg_first, seg_last = _segment_edges(pl.program_id(1), seg_tiles, tiles_total)
    ext_scr[0:FFN_HALO, :] = jnp.where(seg_first, jnp.zeros_like(hp_ref), hp_ref[...])
    ext_scr[FFN_HALO:FFN_HALO + tm, :] = h_ref[...]
    ext_scr[FFN_HALO + tm:, :] = jnp.where(seg_last, jnp.zeros_like(hn_ref), hn_ref[...])
    ext_rows = tm + 2 * FFN_HALO
    inner = slice(FFN_HALO, FFN_HALO + tm)

    def conv(cols):
        a = _dot(ext_scr[...], wup_ref[:, cols])
        a_prev = pltpu.roll(a, 1, axis=0)[inner, :]
        a_next = pltpu.roll(a, ext_rows - 1, axis=0)[inner, :]
        return (bcv_ref[:, cols] + a_prev * wcv_ref[0:1, cols] + a[inner, :] * wcv_ref[1:2, cols]
                + a_next * wcv_ref[2:3, cols])

    for c in range(ff // ck):
        gate = conv(slice(c * ck, (c + 1) * ck))
        val = conv(slice(ff + c * ck, ff + (c + 1) * ck))
        act_scr[:, c * ck:(c + 1) * ck] = (_silu(gate) * val).astype(_BF16)

    x_new = x_ref[...] + mod_ref[:, 5 * d:6 * d] * _dot(act_scr[...], wdn_ref[...])
    if final_norm:
        x_new = _rms(x_new) * gfin_ref[...]
    o_ref[...] = x_new


def _ffn(h2, x, mod, w_up, w_cv, b_cv, w_dn, g_final, layer, seg_rows, nctx_tiles, final_norm):
    b, r, d = x.shape
    tiles = r // ROW_TILE
    hb = ROW_TILE // FFN_HALO
    return pl.pallas_call(
        functools.partial(_ffn_kernel, d=d, seg_tiles=tuple(s // ROW_TILE for s in seg_rows), tiles_total=tiles,
                          final_norm=final_norm),
        grid=(b, tiles),
        in_specs=[pl.BlockSpec((None, ROW_TILE, d), lambda bi, i: (bi, i, 0)),
                  pl.BlockSpec((None, FFN_HALO, d), lambda bi, i: (bi, jnp.maximum(i * hb - 1, 0), 0)),
                  pl.BlockSpec((None, FFN_HALO, d), lambda bi, i: (bi, jnp.minimum((i + 1) * hb, r // FFN_HALO - 1), 0)),
                  pl.BlockSpec((None, ROW_TILE, d), lambda bi, i: (bi, i, 0)),
                  _mod_spec(mod, layer, nctx_tiles, 0)]
        + [_layer_spec(a.shape, layer) for a in (w_up, w_cv, b_cv, w_dn)] + [_const_spec((1, d))],
        out_specs=pl.BlockSpec((None, ROW_TILE, d), lambda bi, i: (bi, i, 0)),
        out_shape=jax.ShapeDtypeStruct((b, r, d), _F32),
        scratch_shapes=[pltpu.VMEM((ROW_TILE + 2 * FFN_HALO, d), _BF16),
                        pltpu.VMEM((ROW_TILE, w_dn.shape[1]), _BF16)],
        compiler_params=_cparams(2),
        name="ffn",
    )(h2, h2, h2, x, mod, w_up, w_cv, b_cv, w_dn, g_final.reshape(1, d))


def _deinterleave(w):
    return jnp.concatenate([w[..., 0::2], w[..., 1::2]], axis=-1)


def _rotated(w):
    return jnp.concatenate([-w[..., 1::2], w[..., 0::2]], axis=-1)


def _layout_w_in(w):
    depth, d, _ = w.shape
    w = w.astype(_BF16)
    sizes = (BRANCH_WIDTH,) * 4 + (M_GATES, Q_LORA, KV_LORA, ROPE_DIM, BRANCH_WIDTH, BRANCH_WIDTH, N_BRANCH * d)
    splits = tuple(int(s) for s in np.cumsum(sizes)[:-1])
    q, k, v, o, mg, cq, ckv, kr, u, s, gt = jnp.split(w, splits, axis=2)
    mg = mg.reshape(depth, d, 2, 2, HEADS)
    pad = jnp.zeros((depth, d, 128 - 2 * HEADS), w.dtype)
    cols = [q, k, v, o, u, s, gt, cq, ckv, _deinterleave(kr), _rotated(kr),
            mg[:, :, :, 0, :].reshape(depth, d, 2 * HEADS), pad, mg[:, :, :, 1, :].reshape(depth, d, 2 * HEADS), pad]
    return jnp.concatenate(cols, axis=2)


def _layout_gate_bias(bg):
    depth = bg.shape[0]
    bg = bg.reshape(depth, 2, 2, HEADS)
    pad = jnp.zeros((depth, 128 - 2 * HEADS), bg.dtype)
    return jnp.concatenate([bg[:, :, 0, :].reshape(depth, -1), pad, bg[:, :, 1, :].reshape(depth, -1), pad],
                           axis=1).reshape(depth, 1, GATE_W)


def _layout_w_uq(w):
    depth = w.shape[0]
    w = w.reshape(depth, Q_LORA, HEADS, HEAD_DIM + ROPE_DIM)
    nope, rope = w[..., :HEAD_DIM], w[..., HEAD_DIM:]
    return jnp.concatenate([nope, _deinterleave(rope), _rotated(rope)], axis=-1).reshape(depth, Q_LORA, -1).astype(_BF16)


def _layout_w_ukv(w):
    depth = w.shape[0]
    w = w.reshape(depth, KV_LORA, HEADS, 2 * HEAD_DIM).astype(_BF16)
    return w[..., :HEAD_DIM].reshape(depth, KV_LORA, -1), w[..., HEAD_DIM:].reshape(depth, KV_LORA, -1)


def _rope_tables(ctx_len, n_latent):
    rows = n_latent // GRID_W
    row = jnp.repeat(jnp.arange(rows), GRID_W)
    col = jnp.tile(jnp.arange(GRID_W), rows)
    n_freq = ROPE_DIM // 4
    inv = ROPE_BASE ** (-jnp.arange(n_freq, dtype=_F32) / n_freq)
    ang = jnp.concatenate([row[:, None] * inv, col[:, None] * inv], axis=-1)
    zeros = jnp.zeros((n_latent, 128 - ROPE_DIM), _F32)
    cos_l = jnp.concatenate([jnp.cos(ang), jnp.cos(ang), zeros], axis=1)
    sin_l = jnp.concatenate([jnp.sin(ang), jnp.sin(ang), zeros], axis=1)
    cos_c = jnp.concatenate([jnp.ones((ctx_len, ROPE_DIM), _F32), jnp.zeros((ctx_len, 128 - ROPE_DIM), _F32)], axis=1)
    return jnp.concatenate([cos_c, cos_l], axis=0), jnp.concatenate([jnp.zeros_like(cos_c), sin_l], axis=0)


def kernel(x, c, ctx, c_ctx, w_ada, b_ada, g_mix, w_in, w_qkconv, b_qkconv, b_mgate, g_mhead, g_qnorm, w_uq,
           g_kvnorm, w_ukv, g_sgu, w_s, b_s, w_branch, w_out, g_ffn, w_up, w_ffconv, b_ffconv, w_down, g_final):
    b, s, d = x.shape
    ctx_len = ctx.shape[1]
    depth = w_in.shape[0]
    assert ctx_len % ROW_TILE == 0 and s % ROW_TILE == 0 and s % GRID_W == 0
    nctx_tiles = ctx_len // ROW_TILE
    row_param = lambda a: a.reshape(depth, 1, a.shape[-1])

    cos_t, sin_t = _rope_tables(ctx_len, s)
    cond_rows = jnp.concatenate([c, c_ctx[None, :], jnp.zeros((7, d), c.dtype)], axis=0)
    mod = _ada(cond_rows, w_ada, b_ada)
    w_in_l = _layout_w_in(w_in)
    b_gate_l = _layout_gate_bias(b_mgate)
    w_uq_l = _layout_w_uq(w_uq)
    w_k_l, w_v_l = _layout_w_ukv(w_ukv)
    b_s_t = jnp.pad(jnp.swapaxes(b_s, 1, 2), ((0, 0), (0, 0), (0, 128 - HEADS)))
    w_s16, w_branch16, w_out16 = w_s.astype(_BF16), w_branch.astype(_BF16), w_out.astype(_BF16)
    w_up16, w_down16 = w_up.astype(_BF16), w_down.astype(_BF16)
    g_mix_r, g_mhead_r, g_sgu_r, g_ffn_r = row_param(g_mix), row_param(g_mhead), row_param(g_sgu), row_param(g_ffn)
    g_qn_r, g_kvn_r, b_qkconv_r, b_ffconv_r = row_param(g_qnorm), row_param(g_kvnorm), row_param(b_qkconv), row_param(b_ffconv)

    x_ctx, x_lat, lat_off = ctx, x, 0
    for l in range(depth):
        last = l == depth - 1
        qk, vo, us, gt, at, gates = _proj(x_ctx, x_lat, lat_off, mod, g_mix_r, w_in_l, l, nctx_tiles, not last)
        q_act, kt_act = _qkconv(qk, w_qkconv, b_qkconv_r, l, (0, ctx_len))
        hf, hb = _mlstm(q_act, kt_act, vo, gates, b_gate_l, l, ctx_len // CHUNK)
        yb = _attn(at, cos_t, sin_t, g_qn_r, w_uq_l, g_kvn_r, w_k_l, w_v_l, l, ctx_len, not last)
        x_mid, h2 = _merge(hf, hb, vo, yb, us, gt, x_ctx, x_lat, lat_off, mod, g_mhead_r, g_sgu_r, w_s16, b_s_t,
                           w_branch16, w_out16, g_ffn_r, l, nctx_tiles, last)
        if last:
            return _ffn(h2, x_mid, mod, w_up16, w_ffconv, b_ffconv_r, w_down16, g_final, l, (0,), 0, True)
        stream = _ffn(h2, x_mid, mod, w_up16, w_ffconv, b_ffconv_r, w_down16, g_final, l, (0, ctx_len), nctx_tiles,
                      False)
        x_ctx, x_lat, lat_off = stream, stream, nctx_tiles
```

```python
import functools

import jax
import jax.numpy as jnp
import numpy as np
from jax import lax
from jax.experimental import pallas as pl
from jax.experimental.pallas import tpu as pltpu

EPS = 1e-6
GRID_W = 64
ROPE_BASE = 10000.0
LOG2_E = 1.4426950408889634

HEADS = 4
HEAD_DIM = 128
CHUNK = 128
BRANCH_WIDTH = HEADS * HEAD_DIM
Q_LORA = 384
KV_LORA = 256
ROPE_DIM = 64
N_BRANCH = 3
M_GATES = 4 * HEADS
KV_SRC = KV_LORA + 2 * ROPE_DIM
ATTN_W = Q_LORA + KV_SRC
GATE_W = 256

ROW_TILE = 256
FFN_HALO = 16
FFN_CHUNK = 256
V7X_VMEM_LIMIT = 56 * 1024 * 1024

_BF16 = jnp.bfloat16
_F32 = jnp.float32


def _cparams(n_axes, flags=None):
    return pltpu.CompilerParams(dimension_semantics=("arbitrary",) * n_axes, vmem_limit_bytes=V7X_VMEM_LIMIT,
                                flags=flags)


def _const_spec(shape):
    return pl.BlockSpec(tuple(shape), lambda *_: (0,) * len(shape))


def _layer_spec(stacked_shape, layer):
    shape = tuple(stacked_shape[1:])
    return pl.BlockSpec((None,) + shape, lambda *_: (layer,) + (0,) * len(shape))


def _mod_spec(mod, layer, nctx_tiles, off):
    batch = mod.shape[1] - 8
    return pl.BlockSpec((None, None, 1, mod.shape[3]),
                        lambda bi, i: (layer, jnp.where(i + off < nctx_tiles, batch, bi), 0, 0))


def _row_specs(width, nctx_tiles, lat_off, off=0):
    ctx = pl.BlockSpec((None, ROW_TILE, width), lambda bi, i: (bi, jnp.minimum(i + off, nctx_tiles - 1), 0))
    lat = pl.BlockSpec((None, ROW_TILE, width), lambda bi, i: (bi, jnp.maximum(i + off - nctx_tiles, 0) + lat_off, 0))
    return [ctx, lat]


def _sigmoid(x):
    return 1.0 / (1.0 + jnp.exp(-x))


def _silu(x):
    return x * _sigmoid(x)


def _gelu_tanh(x):
    return x * (0.5 * (1.0 + jnp.tanh(0.7978845608028654 * (x + 0.044715 * (x * x * x)))))


def _rms(x):
    return x * lax.rsqrt(jnp.mean(x * x, axis=-1, keepdims=True) + EPS)


def _dot(a, b):
    return jnp.dot(a, b, preferred_element_type=_F32)


def _dot_nt(a, b):
    return lax.dot_general(a, b, (((1,), (1,)), ((), ())), preferred_element_type=_F32)


def _segment_edges(i, seg_tiles, tiles_total):
    first = functools.reduce(jnp.logical_or, [i == s for s in seg_tiles])
    last = functools.reduce(jnp.logical_or, [i == s - 1 for s in seg_tiles[1:] + (tiles_total,)])
    return first, last


def _ada_kernel(c_ref, w_ref, b_ref, o_ref):
    cond = _silu(c_ref[...])
    o_ref[...] = jnp.dot(cond, w_ref[...], precision=lax.Precision.HIGHEST, preferred_element_type=_F32) + b_ref[...]


def _ada(cond_rows, w_ada, b_ada):
    rows, d = cond_rows.shape
    depth, _, n = w_ada.shape
    out = pl.pallas_call(
        _ada_kernel,
        grid=(depth, n // d),
        in_specs=[pl.BlockSpec((rows, d), lambda l, j: (0, 0)),
                  pl.BlockSpec((None, d, d), lambda l, j: (l, 0, j)),
                  pl.BlockSpec((None, 1, d), lambda l, j: (l, 0, j))],
        out_specs=pl.BlockSpec((None, rows, d), lambda l, j: (l, 0, j)),
        out_shape=jax.ShapeDtypeStruct((depth, rows, n), _F32),
        compiler_params=_cparams(2),
        name="ada",
    )(cond_rows, w_ada, b_ada.reshape(depth, 1, n))
    return out.reshape(depth, rows, 1, n)


def _proj_kernel(xc_ref, xl_ref, mod_ref, g_ref, w_ref, qk_ref, vo_ref, us_ref, gt_ref, at_ref, gate_ref,
                 *, d, nctx_tiles, ctx_needs_all):
    i = pl.program_id(1)
    bw = BRANCH_WIDTH
    x = jnp.where(i < nctx_tiles, xc_ref[...], xl_ref[...])
    h = (_rms(x) * g_ref[...] * (1.0 + mod_ref[:, d:2 * d]) + mod_ref[:, 0:d]).astype(_BF16)
    col_at = 6 * bw + N_BRANCH * d
    qk_ref[...] = _dot(h, w_ref[:, 0:2 * bw])
    vo = _dot(h, w_ref[:, 2 * bw:4 * bw])
    vo_ref[:, 0:bw] = vo[:, 0:bw].astype(_BF16)
    at_ref[...] = _dot(h, w_ref[:, col_at:col_at + ATTN_W]).astype(_BF16)
    gate_ref[...] = _dot(h, w_ref[:, col_at + ATTN_W:col_at + ATTN_W + GATE_W])

    def mixer_inputs():
        vo_ref[:, bw:2 * bw] = _sigmoid(vo[:, bw:2 * bw]).astype(_BF16)
        us_ref[...] = _gelu_tanh(_dot(h, w_ref[:, 4 * bw:6 * bw])).astype(_BF16)
        for g in range(N_BRANCH):
            cols = slice(6 * bw + g * d, 6 * bw + (g + 1) * d)
            gt_ref[:, g * d:(g + 1) * d] = _sigmoid(_dot(h, w_ref[:, cols])).astype(_BF16)

    if ctx_needs_all:
        mixer_inputs()
    else:
        pl.when(i >= nctx_tiles)(mixer_inputs)

        @pl.when(i < nctx_tiles)
        def _():
            vo_ref[:, bw:2 * bw] = jnp.zeros((ROW_TILE, bw), _BF16)
            us_ref[...] = jnp.zeros_like(us_ref)
            gt_ref[...] = jnp.zeros_like(gt_ref)


def _proj(x_ctx, x_lat, lat_off, mod, g_mix, w_in, layer, nctx_tiles, ctx_needs_all):
    b, _, d = x_ctx.shape
    t = nctx_tiles * ROW_TILE + (x_lat.shape[1] - lat_off * ROW_TILE)
    bw = BRANCH_WIDTH
    widths = (2 * bw, 2 * bw, 2 * bw, N_BRANCH * d, ATTN_W, GATE_W)
    dtypes = (_F32, _BF16, _BF16, _BF16, _BF16, _F32)
    assert sum(widths) == w_in.shape[2]
    return pl.pallas_call(
        functools.partial(_proj_kernel, d=d, nctx_tiles=nctx_tiles, ctx_needs_all=ctx_needs_all),
        grid=(b, t // ROW_TILE),
        in_specs=_row_specs(d, nctx_tiles, lat_off) + [_mod_spec(mod, layer, nctx_tiles, 0),
                                                      _layer_spec(g_mix.shape, layer), _layer_spec(w_in.shape, layer)],
        out_specs=[pl.BlockSpec((None, ROW_TILE, w), lambda bi, i: (bi, i, 0)) for w in widths],
        out_shape=[jax.ShapeDtypeStruct((b, t, w), dt) for w, dt in zip(widths, dtypes)],
        compiler_params=_cparams(2),
        name="proj",
    )(x_ctx, x_lat, mod, g_mix, w_in)


def _qkconv_kernel(x_ref, hp_ref, hn_ref, w_ref, b_ref, q_ref, kt_ref, *, seg_tiles, tiles_total):
    tm = x_ref.shape[0]
    seg_first, seg_last = _segment_edges(pl.program_id(1), seg_tiles, tiles_total)
    row = lax.broadcasted_iota(jnp.int32, (tm, 1), 0)
    x = x_ref[...]
    prev_row = jnp.where(seg_first, 0.0, hp_ref[7:8, :])
    next_row = jnp.where(seg_last, 0.0, hn_ref[0:1, :])
    x_prev = jnp.where(row == 0, prev_row, pltpu.roll(x, 1, axis=0))
    x_next = jnp.where(row == tm - 1, next_row, pltpu.roll(x, tm - 1, axis=0))
    a = _silu(b_ref[...] + x_prev * w_ref[0:1, :] + x * w_ref[1:2, :] + x_next * w_ref[2:3, :])
    q_ref[...] = a[:, 0:BRANCH_WIDTH].astype(q_ref.dtype)
    for h in range(HEADS):
        k = a[:, BRANCH_WIDTH + h * HEAD_DIM:BRANCH_WIDTH + (h + 1) * HEAD_DIM] * (HEAD_DIM ** -0.5)
        kt_ref[h] = k.T.astype(kt_ref.dtype)


def _qkconv(qk, w_qkconv, b_qkconv, layer, seg_rows):
    b, t, qk_w = qk.shape
    tiles = t // ROW_TILE
    hb = ROW_TILE // 8
    return pl.pallas_call(
        functools.partial(_qkconv_kernel, seg_tiles=tuple(s // ROW_TILE for s in seg_rows), tiles_total=tiles),
        grid=(b, tiles),
        in_specs=[pl.BlockSpec((None, ROW_TILE, qk_w), lambda bi, i: (bi, i, 0)),
                  pl.BlockSpec((None, 8, qk_w), lambda bi, i: (bi, jnp.maximum(i * hb - 1, 0), 0)),
                  pl.BlockSpec((None, 8, qk_w), lambda bi, i: (bi, jnp.minimum((i + 1) * hb, t // 8 - 1), 0)),
                  _layer_spec(w_qkconv.shape, layer), _layer_spec(b_qkconv.shape, layer)],
        out_specs=[pl.BlockSpec((None, ROW_TILE, BRANCH_WIDTH), lambda bi, i: (bi, i, 0)),
                   pl.BlockSpec((None, HEADS, HEAD_DIM, ROW_TILE), lambda bi, i: (bi, 0, 0, i))],
        out_shape=[jax.ShapeDtypeStruct((b, t, BRANCH_WIDTH), _BF16),
                   jax.ShapeDtypeStruct((b, HEADS, HEAD_DIM, t), _BF16)],
        compiler_params=_cparams(2),
        name="qkconv",
    )(qk, qk, qk, w_qkconv, b_qkconv)


def _mlstm_kernel(q_f, kt_f, v_f, gf_cur, gf_nxt, q_b, kt_b, v_b, gb_cur, gb_nxt, bg_ref, hf_ref, hb_ref,
                  c_scr, m_row, m_col, *pre):
    tc = CHUNK
    j = pl.program_id(1)

    def scan_step(slot):
        _mlstm_scan_step(slot, is_fwd, causal, (q_f, kt_f, v_f, hf_ref), (q_b, kt_b, v_b, hb_ref), c_scr, m_row, m_col)

    pre_a, pre_b = pre[:len(pre) // 2], pre[len(pre) // 2:]
    lane = lax.broadcasted_iota(jnp.int32, (1, 128), 1)
    row = lax.broadcasted_iota(jnp.int32, (tc, 1), 0)
    is_fwd = lane < HEADS
    r_idx = lax.broadcasted_iota(jnp.int32, (tc, tc), 0)
    c_idx = lax.broadcasted_iota(jnp.int32, (tc, tc), 1)
    causal = (c_idx <= r_idx, c_idx >= r_idx)

    def gate_prologue(g_f, g_b, slot):
        bcum_ref, run_ref, rt_ref, dt_ref, rows_ref, blc_ref, dmc_ref = slot
        log_i = jnp.where(is_fwd, g_f[:, 0:128], g_b[:, 0:128]) + bg_ref[:, 0:128]
        gates_f = jnp.where(is_fwd, g_f[:, 128:256], g_b[:, 128:256]) + bg_ref[:, 128:256]
        log_f = jnp.minimum(gates_f, 0.0) - jnp.log(1.0 + jnp.exp(-jnp.abs(gates_f)))
        cum_f, step = log_f, 1
        while step < tc:
            cum_f = cum_f + jnp.where(row >= step, pltpu.roll(cum_f, step, axis=0), 0.0)
            step *= 2
        b_last = cum_f[tc - 1:tc, :]
        bcum = jnp.where(is_fwd, cum_f, b_last - cum_f + log_f)
        r = log_i - bcum
        run_f, run_b, step = r, r, 1
        while step < tc:
            run_f = jnp.maximum(run_f, jnp.where(row >= step, pltpu.roll(run_f, step, axis=0), -jnp.inf))
            run_b = jnp.maximum(run_b, jnp.where(row < tc - step, pltpu.roll(run_b, tc - step, axis=0), -jnp.inf))
            step *= 2
        decay = b_last + r
        r_t = r.T[0:8, :]
        decay_t = decay.T[0:8, :]
        bcum_ref[...] = bcum
        run_ref[...] = jnp.where(is_fwd, run_f, run_b)
        rt_ref[...] = r_t
        dt_ref[...] = decay_t
        rows_ref[0:1, :] = b_last
        rows_ref[1:2, :] = jnp.max(decay, axis=0, keepdims=True)
        blc_ref[...] = decay_t - r_t
        dmc_ref[...] = jnp.broadcast_to(jnp.max(decay_t, axis=1, keepdims=True), (8, tc))

    @pl.when(j == 0)
    def _():
        c_scr[...] = jnp.zeros_like(c_scr)
        m_row[...] = jnp.zeros_like(m_row)
        m_col[...] = jnp.zeros_like(m_col)
        gate_prologue(gf_cur, gb_cur, pre_a)

    @pl.when(j % 2 == 0)
    def _():
        gate_prologue(gf_nxt, gb_nxt, pre_b)
        scan_step(pre_a)

    @pl.when(j % 2 == 1)
    def _():
        gate_prologue(gf_nxt, gb_nxt, pre_a)
        scan_step(pre_b)


def _mlstm_scan_step(slot, is_fwd, causal, refs_f, refs_b, c_scr, m_row, m_col):
    tc = CHUNK
    bcum_ref, run_ref, rt_ref, dt_ref, rows_ref, blc_ref, dmc_ref = slot
    bcum = bcum_ref[...]
    b_last, decay_max = rows_ref[0:1, :], rows_ref[1:2, :]
    m_old = m_row[...]
    g = jnp.maximum(m_old, run_ref[...])
    m_new = jnp.maximum(b_last + m_old, decay_max)
    w_carry = jnp.exp(b_last + m_old - m_new)
    m_col_new = jnp.maximum(blc_ref[...] + m_col[...], dmc_ref[...])
    w_src_t = jnp.exp(dt_ref[...] - m_col_new)
    r_t = rt_ref[...]
    ones = jnp.ones((tc, HEAD_DIM), _BF16)

    scans = [(dr * HEADS + h, refs, h, slice(h * HEAD_DIM, (h + 1) * HEAD_DIM), causal[dr])
             for dr, refs in enumerate((refs_f, refs_b)) for h in range(HEADS)]
    v_ext = [jnp.concatenate([refs[2][:, hs], ones], axis=1) for _, refs, _, hs, _ in scans]
    s16, g_b = [], []
    for l, (q_ref, kt_ref, _, _), h, hs, mask in scans:
        g_l = jnp.broadcast_to(g[:, l:l + 1], (tc, tc))
        w_intra = jnp.exp(jnp.where(mask, r_t[l:l + 1, :] - g_l, -jnp.inf))
        s16.append((_dot(q_ref[:, hs], kt_ref[h]) * w_intra).astype(_BF16))
        g_b.append(g_l)
    qc = []
    for l, (q_ref, kt_ref, _, _), h, hs, _ in scans:
        c_old = c_scr[l]
        qc.append(_dot(q_ref[:, hs], c_old.astype(_BF16)))
        kw_t = (kt_ref[h].astype(_F32) * w_src_t[l:l + 1, :]).astype(_BF16)
        c_scr[l] = w_carry[:, l:l + 1] * c_old + _dot(kw_t, v_ext[l])
    for l, (_, _, _, out_ref), h, hs, _ in scans:
        w_inter = jnp.exp(m_old[:, l:l + 1] - g_b[l])
        b_l = jnp.broadcast_to(bcum[:, l:l + 1], (tc, tc))
        sv = _dot(s16[l], v_ext[l])
        num = w_inter * qc[l][:, 0:HEAD_DIM] + sv[:, 0:HEAD_DIM]
        den = w_inter * qc[l][:, HEAD_DIM:] + sv[:, HEAD_DIM:]
        out_ref[:, hs] = num / jnp.maximum(jnp.abs(den), jnp.exp(-(b_l + g_b[l])))
    m_row[...] = m_new
    m_col[...] = m_col_new


def _mlstm(q, kt, vo, gates, b_gate, layer, nctx):
    b, t, _ = q.shape
    nchunks = t // CHUNK
    assert nchunks % 2 == 0

    def bwd_chunk(j):
        return jnp.where(j < nctx, nctx - 1 - j, nchunks - 1 - (j - nctx))

    def specs(chunk_of):
        nxt = lambda j: chunk_of(jnp.minimum(j + 1, nchunks - 1))
        return [
            pl.BlockSpec((None, CHUNK, BRANCH_WIDTH), lambda bi, j: (bi, chunk_of(j), 0)),
            pl.BlockSpec((None, HEADS, HEAD_DIM, CHUNK), lambda bi, j: (bi, 0, 0, chunk_of(j))),
            pl.BlockSpec((None, CHUNK, BRANCH_WIDTH), lambda bi, j: (bi, chunk_of(j), 0)),
            pl.BlockSpec((None, CHUNK, GATE_W), lambda bi, j: (bi, chunk_of(j), 0)),
            pl.BlockSpec((None, CHUNK, GATE_W), lambda bi, j: (bi, nxt(j), 0)),
        ]

    gate_slot = [pltpu.VMEM((CHUNK, 128), _F32), pltpu.VMEM((CHUNK, 128), _F32), pltpu.VMEM((8, CHUNK), _F32),
                 pltpu.VMEM((8, CHUNK), _F32), pltpu.VMEM((8, 128), _F32), pltpu.VMEM((8, CHUNK), _F32),
                 pltpu.VMEM((8, CHUNK), _F32)]
    h_shape = jax.ShapeDtypeStruct((b, t, BRANCH_WIDTH), _F32)
    return pl.pallas_call(
        _mlstm_kernel,
        grid=(b, nchunks),
        in_specs=specs(lambda j: j) + specs(bwd_chunk) + [_layer_spec(b_gate.shape, layer)],
        out_specs=[pl.BlockSpec((None, CHUNK, BRANCH_WIDTH), lambda bi, j: (bi, j, 0)),
                   pl.BlockSpec((None, CHUNK, BRANCH_WIDTH), lambda bi, j: (bi, bwd_chunk(j), 0))],
        out_shape=[h_shape, h_shape],
        scratch_shapes=[pltpu.VMEM((2 * HEADS, HEAD_DIM, 2 * HEAD_DIM), _F32), pltpu.VMEM((1, 128), _F32),
                        pltpu.VMEM((8, CHUNK), _F32)] + gate_slot + gate_slot,
        compiler_params=_cparams(2),
        name="mlstm",
    )(q, kt, vo, gates, gates, q, kt, vo, gates, gates, b_gate)


def _attn_kernel(cq_ref, kv_ref, ca_ref, sa_ref, caq_ref, saq_ref, gq_ref, wq_ref, gkv_ref, wk_ref, wv_ref,
                 o_ref, k_scr, v_scr, *, ctx_len, need_ctx, scale):
    i = pl.program_id(1)
    t = kv_ref.shape[0]
    tq = cq_ref.shape[0]
    head_w = 2 * HEAD_DIM

    def rope(y, cos_t, sin_t):
        return y * cos_t + pltpu.roll(y, ROPE_DIM, axis=1) * sin_t

    @pl.when(i == 0)
    def _():
        for r in range(t // ROW_TILE):
            rows = slice(r * ROW_TILE, (r + 1) * ROW_TILE)
            ckv = (_rms(kv_ref[rows, 0:KV_LORA].astype(_F32)) * gkv_ref[...]).astype(_BF16)
            k_nope = _dot(ckv, wk_ref[...])
            k_rope = rope(kv_ref[rows, KV_LORA:KV_SRC].astype(_F32), ca_ref[rows, :], sa_ref[rows, :]).astype(_BF16)
            for h in range(HEADS):
                k_scr[rows, h * head_w:h * head_w + HEAD_DIM] = k_nope[:, h * HEAD_DIM:(h + 1) * HEAD_DIM].astype(_BF16)
                k_scr[rows, h * head_w + HEAD_DIM:(h + 1) * head_w] = k_rope
            v_scr[rows, :] = _dot(ckv, wv_ref[...]).astype(_BF16)

    def attend(n_keys):
        cq = (_rms(cq_ref[...].astype(_F32)) * gq_ref[...]).astype(_BF16)
        qa = _dot(cq, wq_ref[...]) * (scale * LOG2_E)
        def scores(h):
            q_nope = qa[:, h * head_w:h * head_w + HEAD_DIM]
            q_rope = rope(qa[:, h * head_w + HEAD_DIM:(h + 1) * head_w], caq_ref[...], saq_ref[...])
            qh = jnp.concatenate([q_nope, q_rope], axis=1).astype(_BF16)
            return _dot_nt(qh, k_scr[0:n_keys, h * head_w:(h + 1) * head_w])

        s_next = scores(0)
        for h in range(HEADS):
            s = s_next
            if h + 1 < HEADS:
                s_next = scores(h + 1)
            e = jnp.exp2(s - jnp.max(s, axis=1, keepdims=True))
            o = _dot(e.astype(_BF16), v_scr[0:n_keys, h * HEAD_DIM:(h + 1) * HEAD_DIM])
            o_ref[:, h * HEAD_DIM:(h + 1) * HEAD_DIM] = (o / jnp.sum(e, axis=1, keepdims=True)).astype(o_ref.dtype)

    @pl.when(i < ctx_len // tq)
    def _():
        if need_ctx:
            attend(ctx_len)
        else:
            o_ref[...] = jnp.zeros_like(o_ref)

    @pl.when(i >= ctx_len // tq)
    def _():
        attend(t)


def _attn(at, cos_t, sin_t, g_qn, w_q, g_kvn, w_k, w_v, layer, ctx_len, need_ctx):
    b, t, _ = at.shape
    scale = (HEAD_DIM + ROPE_DIM) ** -0.5
    return pl.pallas_call(
        functools.partial(_attn_kernel, ctx_len=ctx_len, need_ctx=need_ctx, scale=scale),
        grid=(b, t // ROW_TILE),
        in_specs=[pl.BlockSpec((None, ROW_TILE, Q_LORA), lambda bi, i: (bi, i, 0)),
                  pl.BlockSpec((None, t, KV_SRC), lambda bi, i: (bi, 0, 1)),
                  _const_spec((t, 128)), _const_spec((t, 128)),
                  pl.BlockSpec((ROW_TILE, 128), lambda bi, i: (i, 0)),
                  pl.BlockSpec((ROW_TILE, 128), lambda bi, i: (i, 0)),
                  _layer_spec(g_qn.shape, layer), _layer_spec(w_q.shape, layer), _layer_spec(g_kvn.shape, layer),
                  _layer_spec(w_k.shape, layer), _layer_spec(w_v.shape, layer)],
        out_specs=pl.BlockSpec((None, ROW_TILE, BRANCH_WIDTH), lambda bi, i: (bi, i, 0)),
        out_shape=jax.ShapeDtypeStruct((b, t, BRANCH_WIDTH), _BF16),
        scratch_shapes=[pltpu.VMEM((t, HEADS * 2 * HEAD_DIM), _BF16), pltpu.VMEM((t, BRANCH_WIDTH), _BF16)],
        compiler_params=_cparams(2),
        name="attn",
    )(at, at, cos_t, sin_t, cos_t, sin_t, g_qn, w_q, g_kvn, w_k, w_v)


def _merge_kernel(hf_ref, hb_ref, so_ref, yb_ref, us_ref, gt_ref, xc_ref, xl_ref, mod_ref,
                  gmh_ref, gsgu_ref, ws_ref, bs_ref, wbr_ref, wout_ref, gffn_ref,
                  xo_ref, h2_ref, *, d, nctx_tiles, off):
    tm = xo_ref.shape[0]
    x = jnp.where(pl.program_id(1) + off < nctx_tiles, xc_ref[...], xl_ref[...])
    hsum = hf_ref[...] + hb_ref[...]
    ya = jnp.concatenate(
        [_rms(hsum[:, h * HEAD_DIM:(h + 1) * HEAD_DIM]) for h in range(HEADS)], axis=1) * gmh_ref[...]
    ya = so_ref[...].astype(_F32) * ya
    cols = []
    for g in range(HEADS):
        gs = slice(BRANCH_WIDTH + g * HEAD_DIM, BRANCH_WIDTH + (g + 1) * HEAD_DIM)
        vn = (_rms(us_ref[:, gs].astype(_F32)) * gsgu_ref[:, g * HEAD_DIM:(g + 1) * HEAD_DIM]).astype(_BF16)
        mixed = [_dot(ws_ref[g], vn[n * CHUNK:(n + 1) * CHUNK, :]) + bs_ref[:, g:g + 1] for n in range(tm // CHUNK)]
        cols.append(jnp.concatenate(mixed, axis=0))
    yc = us_ref[:, 0:BRANCH_WIDTH].astype(_F32) * jnp.concatenate(cols, axis=1)
    acc = None
    for g, y in enumerate((ya.astype(_BF16), yb_ref[...], yc.astype(_BF16))):
        term = gt_ref[:, g * d:(g + 1) * d].astype(_F32) * _dot(y, wbr_ref[g])
        acc = term if acc is None else acc + term
    out = _dot(acc.astype(_BF16), wout_ref[...])
    x_new = x + mod_ref[:, 2 * d:3 * d] * out
    xo_ref[...] = x_new
    h2 = _rms(x_new) * gffn_ref[...]
    h2_ref[...] = (h2 * (1.0 + mod_ref[:, 4 * d:5 * d]) + mod_ref[:, 3 * d:4 * d]).astype(h2_ref.dtype)


def _merge(hf, hb, vo, yb, us, gt, x_ctx, x_lat, lat_off, mod, g_mhead, g_sgu, w_s, b_s_t, w_branch, w_out, g_ffn,
           layer, nctx_tiles, skip_ctx):
    b, t, _ = hf.shape
    d = x_ctx.shape[2]
    off = nctx_tiles if skip_ctx else 0
    rows_out = t - off * ROW_TILE
    tile = lambda w, blk: pl.BlockSpec((None, ROW_TILE, w), lambda bi, i: (bi, i + off, blk))
    out_tile = lambda: pl.BlockSpec((None, ROW_TILE, d), lambda bi, i: (bi, i, 0))
    return pl.pallas_call(
        functools.partial(_merge_kernel, d=d, nctx_tiles=nctx_tiles, off=off),
        grid=(b, rows_out // ROW_TILE),
        in_specs=[tile(BRANCH_WIDTH, 0), tile(BRANCH_WIDTH, 0), tile(BRANCH_WIDTH, 1), tile(BRANCH_WIDTH, 0),
                  tile(2 * BRANCH_WIDTH, 0), tile(N_BRANCH * d, 0)]
        + _row_specs(d, nctx_tiles, lat_off, off) + [_mod_spec(mod, layer, nctx_tiles, off)]
        + [_layer_spec(a.shape, layer) for a in (g_mhead, g_sgu, w_s, b_s_t, w_branch, w_out, g_ffn)],
        out_specs=[out_tile(), out_tile()],
        out_shape=[jax.ShapeDtypeStruct((b, rows_out, d), _F32), jax.ShapeDtypeStruct((b, rows_out, d), _BF16)],
        compiler_params=_cparams(2),
        name="merge",
    )(hf, hb, vo, yb, us, gt, x_ctx, x_lat, mod, g_mhead, g_sgu, w_s, b_s_t, w_branch, w_out, g_ffn)


def _ffn_kernel(h_ref, hp_ref, hn_ref, x_ref, mod_ref, wup_ref, wcv_ref, bcv_ref, wdn_ref, gfin_ref,
                o_ref, ext_scr, act_scr, *, d, seg_tiles, tiles_total, final_norm):
    tm = h_ref.shape[0]
    ff = wdn_ref.shape[0]
    ck = FFN_CHUNK
    seg_first, seg_last = _segment_edges(pl.program_id(1), seg_tiles, tiles_total)
    ext_scr[0:FFN_HALO, :] = jnp.where(seg_first, jnp.zeros_like(hp_ref), hp_ref[...])
    ext_scr[FFN_HALO:FFN_HALO + tm, :] = h_ref[...]
    ext_scr[FFN_HALO + tm:, :] = jnp.where(seg_last, jnp.zeros_like(hn_ref), hn_ref[...])
    ext_rows = tm + 2 * FFN_HALO
    inner = slice(FFN_HALO, FFN_HALO + tm)

    def conv(cols):
        a = _dot(ext_scr[...], wup_ref[:, cols])
        a_prev = pltpu.roll(a, 1, axis=0)[inner, :]
        a_next = pltpu.roll(a, ext_rows - 1, axis=0)[inner, :]
        return (bcv_ref[:, cols] + a_prev * wcv_ref[0:1, cols] + a[inner, :] * wcv_ref[1:2, cols]
                + a_next * wcv_ref[2:3, cols])

    for c in range(ff // ck):
        gate = conv(slice(c * ck, (c + 1) * ck))
        val = conv(slice(ff + c * ck, ff + (c + 1) * ck))
        act_scr[:, c * ck:(c + 1) * ck] = (_silu(gate) * val).astype(_BF16)

    x_new = x_ref[...] + mod_ref[:, 5 * d:6 * d] * _dot(act_scr[...], wdn_ref[...])
    if final_norm:
        x_new = _rms(x_new) * gfin_ref[...]
    o_ref[...] = x_new


def _ffn(h2, x, mod, w_up, w_cv, b_cv, w_dn, g_final, layer, seg_rows, nctx_tiles, final_norm):
    b, r, d = x.shape
    tiles = r // ROW_TILE
    hb = ROW_TILE // FFN_HALO
    return pl.pallas_call(
        functools.partial(_ffn_kernel, d=d, seg_tiles=tuple(s // ROW_TILE for s in seg_rows), tiles_total=tiles,
                          final_norm=final_norm),
        grid=(b, tiles),
        in_specs=[pl.BlockSpec((None, ROW_TILE, d), lambda bi, i: (bi, i, 0)),
                  pl.BlockSpec((None, FFN_HALO, d), lambda bi, i: (bi, jnp.maximum(i * hb - 1, 0), 0)),
                  pl.BlockSpec((None, FFN_HALO, d), lambda bi, i: (bi, jnp.minimum((i + 1) * hb, r // FFN_HALO - 1), 0)),
                  pl.BlockSpec((None, ROW_TILE, d), lambda bi, i: (bi, i, 0)),
                  _mod_spec(mod, layer, nctx_tiles, 0)]
        + [_layer_spec(a.shape, layer) for a in (w_up, w_cv, b_cv, w_dn)] + [_const_spec((1, d))],
        out_specs=pl.BlockSpec((None, ROW_TILE, d), lambda bi, i: (bi, i, 0)),
        out_shape=jax.ShapeDtypeStruct((b, r, d), _F32),
        scratch_shapes=[pltpu.VMEM((ROW_TILE + 2 * FFN_HALO, d), _BF16),
                        pltpu.VMEM((ROW_TILE, w_dn.shape[1]), _BF16)],
        compiler_params=_cparams(2),
        name="ffn",
    )(h2, h2, h2, x, mod, w_up, w_cv, b_cv, w_dn, g_final.reshape(1, d))


def _deinterleave(w):
    return jnp.concatenate([w[..., 0::2], w[..., 1::2]], axis=-1)


def _rotated(w):
    return jnp.concatenate([-w[..., 1::2], w[..., 0::2]], axis=-1)


def _layout_w_in(w):
    depth, d, _ = w.shape
    w = w.astype(_BF16)
    sizes = (BRANCH_WIDTH,) * 4 + (M_GATES, Q_LORA, KV_LORA, ROPE_DIM, BRANCH_WIDTH, BRANCH_WIDTH, N_BRANCH * d)
    splits = tuple(int(s) for s in np.cumsum(sizes)[:-1])
    q, k, v, o, mg, cq, ckv, kr, u, s, gt = jnp.split(w, splits, axis=2)
    mg = mg.reshape(depth, d, 2, 2, HEADS)
    pad = jnp.zeros((depth, d, 128 - 2 * HEADS), w.dtype)
    cols = [q, k, v, o, u, s, gt, cq, ckv, _deinterleave(kr), _rotated(kr),
            mg[:, :, :, 0, :].reshape(depth, d, 2 * HEADS), pad, mg[:, :, :, 1, :].reshape(depth, d, 2 * HEADS), pad]
    return jnp.concatenate(cols, axis=2)


def _layout_gate_bias(bg):
    depth = bg.shape[0]
    bg = bg.reshape(depth, 2, 2, HEADS)
    pad = jnp.zeros((depth, 128 - 2 * HEADS), bg.dtype)
    return jnp.concatenate([bg[:, :, 0, :].reshape(depth, -1), pad, bg[:, :, 1, :].reshape(depth, -1), pad],
                           axis=1).reshape(depth, 1, GATE_W)


def _layout_w_uq(w):
    depth = w.shape[0]
    w = w.reshape(depth, Q_LORA, HEADS, HEAD_DIM + ROPE_DIM)
    nope, rope = w[..., :HEAD_DIM], w[..., HEAD_DIM:]
    return jnp.concatenate([nope, _deinterleave(rope), _rotated(rope)], axis=-1).reshape(depth, Q_LORA, -1).astype(_BF16)


def _layout_w_ukv(w):
    depth = w.shape[0]
    w = w.reshape(depth, KV_LORA, HEADS, 2 * HEAD_DIM).astype(_BF16)
    return w[..., :HEAD_DIM].reshape(depth, KV_LORA, -1), w[..., HEAD_DIM:].reshape(depth, KV_LORA, -1)


def _rope_tables(ctx_len, n_latent):
    rows = n_latent // GRID_W
    row = jnp.repeat(jnp.arange(rows), GRID_W)
    col = jnp.tile(jnp.arange(GRID_W), rows)
    n_freq = ROPE_DIM // 4
    inv = ROPE_BASE ** (-jnp.arange(n_freq, dtype=_F32) / n_freq)
    ang = jnp.concatenate([row[:, None] * inv, col[:, None] * inv], axis=-1)
    zeros = jnp.zeros((n_latent, 128 - ROPE_DIM), _F32)
    cos_l = jnp.concatenate([jnp.cos(ang), jnp.cos(ang), zeros], axis=1)
    sin_l = jnp.concatenate([jnp.sin(ang), jnp.sin(ang), zeros], axis=1)
    cos_c = jnp.concatenate([jnp.ones((ctx_len, ROPE_DIM), _F32), jnp.zeros((ctx_len, 128 - ROPE_DIM), _F32)], axis=1)
    return jnp.concatenate([cos_c, cos_l], axis=0), jnp.concatenate([jnp.zeros_like(cos_c), sin_l], axis=0)


def kernel(x, c, ctx, c_ctx, w_ada, b_ada, g_mix, w_in, w_qkconv, b_qkconv, b_mgate, g_mhead, g_qnorm, w_uq,
           g_kvnorm, w_ukv, g_sgu, w_s, b_s, w_branch, w_out, g_ffn, w_up, w_ffconv, b_ffconv, w_down, g_final):
    b, s, d = x.shape
    ctx_len = ctx.shape[1]
    depth = w_in.shape[0]
    assert ctx_len % ROW_TILE == 0 and s % ROW_TILE == 0 and s % GRID_W == 0
    nctx_tiles = ctx_len // ROW_TILE
    row_param = lambda a: a.reshape(depth, 1, a.shape[-1])

    cos_t, sin_t = _rope_tables(ctx_len, s)
    cond_rows = jnp.concatenate([c, c_ctx[None, :], jnp.zeros((7, d), c.dtype)], axis=0)
    mod = _ada(cond_rows, w_ada, b_ada)
    w_in_l = _layout_w_in(w_in)
    b_gate_l = _layout_gate_bias(b_mgate)
    w_uq_l = _layout_w_uq(w_uq)
    w_k_l, w_v_l = _layout_w_ukv(w_ukv)
    b_s_t = jnp.pad(jnp.swapaxes(b_s, 1, 2), ((0, 0), (0, 0), (0, 128 - HEADS)))
    w_s16, w_branch16, w_out16 = w_s.astype(_BF16), w_branch.astype(_BF16), w_out.astype(_BF16)
    w_up16, w_down16 = w_up.astype(_BF16), w_down.astype(_BF16)
    g_mix_r, g_mhead_r, g_sgu_r, g_ffn_r = row_param(g_mix), row_param(g_mhead), row_param(g_sgu), row_param(g_ffn)
    g_qn_r, g_kvn_r, b_qkconv_r, b_ffconv_r = row_param(g_qnorm), row_param(g_kvnorm), row_param(b_qkconv), row_param(b_ffconv)

    x_ctx, x_lat, lat_off = ctx, x, 0
    for l in range(depth):
        last = l == depth - 1
        qk, vo, us, gt, at, gates = _proj(x_ctx, x_lat, lat_off, mod, g_mix_r, w_in_l, l, nctx_tiles, not last)
        q_act, kt_act = _qkconv(qk, w_qkconv, b_qkconv_r, l, (0, ctx_len))
        hf, hb = _mlstm(q_act, kt_act, vo, gates, b_gate_l, l, ctx_len // CHUNK)
        yb = _attn(at, cos_t, sin_t, g_qn_r, w_uq_l, g_kvn_r, w_k_l, w_v_l, l, ctx_len, not last)
        x_mid, h2 = _merge(hf, hb, vo, yb, us, gt, x_ctx, x_lat, lat_off, mod, g_mhead_r, g_sgu_r, w_s16, b_s_t,
                           w_branch16, w_out16, g_ffn_r, l, nctx_tiles, last)
        if last:
            return _ffn(h2, x_mid, mod, w_up16, w_ffconv, b_ffconv_r, w_down16, g_final, l, (0,), 0, True)
        stream = _ffn(h2, x_mid, mod, w_up16, w_ffconv, b_ffconv_r, w_down16, g_final, l, (0, ctx_len), nctx_tiles,
                      False)
        x_ctx, x_lat, lat_off = stream, stream, nctx_tiles
```

```python
import functools

import jax
import jax.numpy as jnp
import numpy as np
from jax import lax
from jax.experimental import pallas as pl
from jax.experimental.pallas import tpu as pltpu

EPS = 1e-6
GRID_W = 64
ROPE_BASE = 10000.0
LOG2_E = 1.4426950408889634

HEADS = 4
HEAD_DIM = 128
CHUNK = 128
BRANCH_WIDTH = HEADS * HEAD_DIM
Q_LORA = 384
KV_LORA = 256
ROPE_DIM = 64
N_BRANCH = 3
M_GATES = 4 * HEADS
KV_SRC = KV_LORA + 2 * ROPE_DIM
ATTN_W = Q_LORA + KV_SRC
GATE_W = 256

ROW_TILE = 256
FFN_HALO = 16
FFN_CHUNK = 256
V7X_VMEM_LIMIT = 56 * 1024 * 1024

_BF16 = jnp.bfloat16
_F32 = jnp.float32


def _cparams(n_axes, flags=None):
    return pltpu.CompilerParams(dimension_semantics=("arbitrary",) * n_axes, vmem_limit_bytes=V7X_VMEM_LIMIT,
                                flags=flags)


def _const_spec(shape):
    return pl.BlockSpec(tuple(shape), lambda *_: (0,) * len(shape))


def _layer_spec(stacked_shape, layer):
    shape = tuple(stacked_shape[1:])
    return pl.BlockSpec((None,) + shape, lambda *_: (layer,) + (0,) * len(shape))


def _mod_spec(mod, layer, nctx_tiles, off):
    batch = mod.shape[1] - 8
    return pl.BlockSpec((None, None, 1, mod.shape[3]),
                        lambda bi, i: (layer, jnp.where(i + off < nctx_tiles, batch, bi), 0, 0))


def _row_specs(width, nctx_tiles, lat_off, off=0):
    ctx = pl.BlockSpec((None, ROW_TILE, width), lambda bi, i: (bi, jnp.minimum(i + off, nctx_tiles - 1), 0))
    lat = pl.BlockSpec((None, ROW_TILE, width), lambda bi, i: (bi, jnp.maximum(i + off - nctx_tiles, 0) + lat_off, 0))
    return [ctx, lat]


def _sigmoid(x):
    return 1.0 / (1.0 + jnp.exp(-x))


def _silu(x):
    return x * _sigmoid(x)


def _gelu_tanh(x):
    return x * (0.5 * (1.0 + jnp.tanh(0.7978845608028654 * (x + 0.044715 * (x * x * x)))))


def _rms(x):
    return x * lax.rsqrt(jnp.mean(x * x, axis=-1, keepdims=True) + EPS)


def _dot(a, b):
    return jnp.dot(a, b, preferred_element_type=_F32)


def _dot_nt(a, b):
    return lax.dot_general(a, b, (((1,), (1,)), ((), ())), preferred_element_type=_F32)


def _segment_edges(i, seg_tiles, tiles_total):
    first = functools.reduce(jnp.logical_or, [i == s for s in seg_tiles])
    last = functools.reduce(jnp.logical_or, [i == s - 1 for s in seg_tiles[1:] + (tiles_total,)])
    return first, last


def _ada_kernel(c_ref, w_ref, b_ref, o_ref):
    cond = _silu(c_ref[...])
    o_ref[...] = jnp.dot(cond, w_ref[...], precision=lax.Precision.HIGHEST, preferred_element_type=_F32) + b_ref[...]


def _ada(cond_rows, w_ada, b_ada):
    rows, d = cond_rows.shape
    depth, _, n = w_ada.shape
    out = pl.pallas_call(
        _ada_kernel,
        grid=(depth, n // d),
        in_specs=[pl.BlockSpec((rows, d), lambda l, j: (0, 0)),
                  pl.BlockSpec((None, d, d), lambda l, j: (l, 0, j)),
                  pl.BlockSpec((None, 1, d), lambda l, j: (l, 0, j))],
        out_specs=pl.BlockSpec((None, rows, d), lambda l, j: (l, 0, j)),
        out_shape=jax.ShapeDtypeStruct((depth, rows, n), _F32),
        compiler_params=_cparams(2),
        name="ada",
    )(cond_rows, w_ada, b_ada.reshape(depth, 1, n))
    return out.reshape(depth, rows, 1, n)


def _proj_kernel(xc0_ref, xl0_ref, mod0_ref, xc1_ref, xl1_ref, mod1_ref, g_ref, w_ref,
                 qk_ref, vo_ref, us_ref, gt_ref, at_ref, gate_ref, *, d, nctx_tiles, tiles):
    bw = BRANCH_WIDTH
    col_at = 6 * bw + N_BRANCH * d
    normed = []
    for h, (xc_ref, xl_ref, mod_ref) in enumerate(((xc0_ref, xl0_ref, mod0_ref), (xc1_ref, xl1_ref, mod1_ref))):
        x = jnp.where((2 * pl.program_id(0) + h) % tiles < nctx_tiles, xc_ref[...], xl_ref[...])
        normed.append((_rms(x) * g_ref[...] * (1.0 + mod_ref[:, d:2 * d]) + mod_ref[:, 0:d]).astype(_BF16))
    for h, hn in enumerate(normed):
        rows = slice(h * ROW_TILE, (h + 1) * ROW_TILE)
        us_ref[rows, :] = _gelu_tanh(_dot(hn, w_ref[:, 4 * bw:6 * bw])).astype(_BF16)
        for g in range(N_BRANCH):
            cols = slice(6 * bw + g * d, 6 * bw + (g + 1) * d)
            gt_ref[rows, g * d:(g + 1) * d] = _sigmoid(_dot(hn, w_ref[:, cols])).astype(_BF16)
        vo_ref[rows, bw:2 * bw] = _sigmoid(_dot(hn, w_ref[:, 3 * bw:4 * bw])).astype(_BF16)
        vo_ref[rows, 0:bw] = _dot(hn, w_ref[:, 2 * bw:3 * bw]).astype(_BF16)
        qk_ref[rows, :] = _dot(hn, w_ref[:, 0:2 * bw])
        at_ref[rows, :] = _dot(hn, w_ref[:, col_at:col_at + ATTN_W]).astype(_BF16)
        gate_ref[rows, :] = _dot(hn, w_ref[:, col_at + ATTN_W:col_at + ATTN_W + GATE_W])


def _pair_tile(h, tiles):
    def index(n):
        k = 2 * n + h
        return k // tiles, k % tiles
    return index


def _stream_tile_specs(at, off, d, nctx_tiles, lat_off, mod, layer):
    batch_row = mod.shape[1] - 8
    return [pl.BlockSpec((None, ROW_TILE, d), lambda n: (at(n)[0], jnp.minimum(at(n)[1] + off, nctx_tiles - 1), 0)),
            pl.BlockSpec((None, ROW_TILE, d),
                         lambda n: (at(n)[0], jnp.maximum(at(n)[1] + off - nctx_tiles, 0) + lat_off, 0)),
            pl.BlockSpec((None, None, 1, mod.shape[3]),
                         lambda n: (layer, jnp.where(at(n)[1] + off < nctx_tiles, batch_row, at(n)[0]), 0, 0))]


def _proj(x_ctx, x_lat, lat_off, mod, g_mix, w_in, layer, nctx_tiles):
    b, _, d = x_ctx.shape
    tiles = nctx_tiles + x_lat.shape[1] // ROW_TILE - lat_off
    bw = BRANCH_WIDTH
    widths = (2 * bw, 2 * bw, 2 * bw, N_BRANCH * d, ATTN_W, GATE_W)
    dtypes = (_F32, _BF16, _BF16, _BF16, _BF16, _F32)
    assert sum(widths) == w_in.shape[2] and (b * tiles) % 2 == 0
    rows = b * tiles * ROW_TILE
    tile_args = (x_ctx, x_lat, mod)
    outs = pl.pallas_call(
        functools.partial(_proj_kernel, d=d, nctx_tiles=nctx_tiles, tiles=tiles),
        grid=(b * tiles // 2,),
        in_specs=_stream_tile_specs(_pair_tile(0, tiles), 0, d, nctx_tiles, lat_off, mod, layer)
        + _stream_tile_specs(_pair_tile(1, tiles), 0, d, nctx_tiles, lat_off, mod, layer)
        + [_layer_spec(g_mix.shape, layer), _layer_spec(w_in.shape, layer)],
        out_specs=[pl.BlockSpec((2 * ROW_TILE, w), lambda n: (n, 0)) for w in widths],
        out_shape=[jax.ShapeDtypeStruct((rows, w), dt) for w, dt in zip(widths, dtypes)],
        compiler_params=_cparams(1),
        name="proj",
    )(*tile_args, *tile_args, g_mix, w_in)
    return [o.reshape(b, tiles * ROW_TILE, o.shape[1]) for o in outs]


def _qkconv_kernel(x_ref, hp_ref, hn_ref, w_ref, b_ref, q_ref, kt_ref, *, seg_tiles, tiles_total):
    tm = x_ref.shape[0]
    seg_first, seg_last = _segment_edges(pl.program_id(1), seg_tiles, tiles_total)
    row = lax.broadcasted_iota(jnp.int32, (tm, 1), 0)
    x = x_ref[...]
    prev_row = jnp.where(seg_first, 0.0, hp_ref[7:8, :])
    next_row = jnp.where(seg_last, 0.0, hn_ref[0:1, :])
    x_prev = jnp.where(row == 0, prev_row, pltpu.roll(x, 1, axis=0))
    x_next = jnp.where(row == tm - 1, next_row, pltpu.roll(x, tm - 1, axis=0))
    a = _silu(b_ref[...] + x_prev * w_ref[0:1, :] + x * w_ref[1:2, :] + x_next * w_ref[2:3, :])
    q_ref[...] = a[:, 0:BRANCH_WIDTH].astype(q_ref.dtype)
    for h in range(HEADS):
        k = a[:, BRANCH_WIDTH + h * HEAD_DIM:BRANCH_WIDTH + (h + 1) * HEAD_DIM] * (HEAD_DIM ** -0.5)
        kt_ref[h] = k.T.astype(kt_ref.dtype)


def _qkconv(qk, w_qkconv, b_qkconv, layer, seg_rows):
    b, t, qk_w = qk.shape
    tiles = t // ROW_TILE
    hb = ROW_TILE // 8
    return pl.pallas_call(
        functools.partial(_qkconv_kernel, seg_tiles=tuple(s // ROW_TILE for s in seg_rows), tiles_total=tiles),
        grid=(b, tiles),
        in_specs=[pl.BlockSpec((None, ROW_TILE, qk_w), lambda bi, i: (bi, i, 0)),
                  pl.BlockSpec((None, 8, qk_w), lambda bi, i: (bi, jnp.maximum(i * hb - 1, 0), 0)),
                  pl.BlockSpec((None, 8, qk_w), lambda bi, i: (bi, jnp.minimum((i + 1) * hb, t // 8 - 1), 0)),
                  _layer_spec(w_qkconv.shape, layer), _layer_spec(b_qkconv.shape, layer)],
        out_specs=[pl.BlockSpec((None, ROW_TILE, BRANCH_WIDTH), lambda bi, i: (bi, i, 0)),
                   pl.BlockSpec((None, HEADS, HEAD_DIM, ROW_TILE), lambda bi, i: (bi, 0, 0, i))],
        out_shape=[jax.ShapeDtypeStruct((b, t, BRANCH_WIDTH), _BF16),
                   jax.ShapeDtypeStruct((b, HEADS, HEAD_DIM, t), _BF16)],
        compiler_params=_cparams(2),
        name="qkconv",
    )(qk, qk, qk, w_qkconv, b_qkconv)


def _mlstm_kernel(q_f, kt_f, v_f, gf_cur, gf_nxt, q_b, kt_b, v_b, gb_cur, gb_nxt, bg_ref, hf_ref, hb_ref,
                  c_scr, m_row, m_col, *pre):
    tc = CHUNK
    j = pl.program_id(1)

    def scan_step(slot):
        _mlstm_scan_step(slot, is_fwd, causal, (q_f, kt_f, v_f, hf_ref), (q_b, kt_b, v_b, hb_ref), c_scr, m_row, m_col)

    pre_a, pre_b = pre[:len(pre) // 2], pre[len(pre) // 2:]
    lane = lax.broadcasted_iota(jnp.int32, (1, 128), 1)
    row = lax.broadcasted_iota(jnp.int32, (tc, 1), 0)
    is_fwd = lane < HEADS
    r_idx = lax.broadcasted_iota(jnp.int32, (tc, tc), 0)
    c_idx = lax.broadcasted_iota(jnp.int32, (tc, tc), 1)
    causal = (c_idx <= r_idx, c_idx >= r_idx)

    def gate_prologue(g_f, g_b, slot):
        bcum_ref, run_ref, rt_ref, dt_ref, rows_ref, blc_ref, dmc_ref = slot
        log_i = jnp.where(is_fwd, g_f[:, 0:128], g_b[:, 0:128]) + bg_ref[:, 0:128]
        gates_f = jnp.where(is_fwd, g_f[:, 128:256], g_b[:, 128:256]) + bg_ref[:, 128:256]
        log_f = jnp.minimum(gates_f, 0.0) - jnp.log(1.0 + jnp.exp(-jnp.abs(gates_f)))
        cum_f, step = log_f, 1
        while step < tc:
            cum_f = cum_f + jnp.where(row >= step, pltpu.roll(cum_f, step, axis=0), 0.0)
            step *= 2
        b_last = cum_f[tc - 1:tc, :]
        bcum = jnp.where(is_fwd, cum_f, b_last - cum_f + log_f)
        r = log_i - bcum
        run_f, run_b, step = r, r, 1
        while step < tc:
            run_f = jnp.maximum(run_f, jnp.where(row >= step, pltpu.roll(run_f, step, axis=0), -jnp.inf))
            run_b = jnp.maximum(run_b, jnp.where(row < tc - step, pltpu.roll(run_b, tc - step, axis=0), -jnp.inf))
            step *= 2
        decay = b_last + r
        r_t = r.T[0:8, :]
        decay_t = decay.T[0:8, :]
        bcum_ref[...] = bcum
        run_ref[...] = jnp.where(is_fwd, run_f, run_b)
        rt_ref[...] = r_t
        dt_ref[...] = decay_t
        rows_ref[0:1, :] = b_last
        rows_ref[1:2, :] = jnp.max(decay, axis=0, keepdims=True)
        blc_ref[...] = decay_t - r_t
        dmc_ref[...] = jnp.broadcast_to(jnp.max(decay_t, axis=1, keepdims=True), (8, tc))

    @pl.when(j == 0)
    def _():
        c_scr[...] = jnp.zeros_like(c_scr)
        m_row[...] = jnp.zeros_like(m_row)
        m_col[...] = jnp.zeros_like(m_col)
        gate_prologue(gf_cur, gb_cur, pre_a)

    @pl.when(j % 2 == 0)
    def _():
        gate_prologue(gf_nxt, gb_nxt, pre_b)
        scan_step(pre_a)

    @pl.when(j % 2 == 1)
    def _():
        gate_prologue(gf_nxt, gb_nxt, pre_a)
        scan_step(pre_b)


def _mlstm_scan_step(slot, is_fwd, causal, refs_f, refs_b, c_scr, m_row, m_col):
    tc = CHUNK
    bcum_ref, run_ref, rt_ref, dt_ref, rows_ref, blc_ref, dmc_ref = slot
    bcum = bcum_ref[...]
    b_last, decay_max = rows_ref[0:1, :], rows_ref[1:2, :]
    m_old = m_row[...]
    g = jnp.maximum(m_old, run_ref[...])
    m_new = jnp.maximum(b_last + m_old, decay_max)
    w_carry = jnp.exp(b_last + m_old - m_new)
    m_col_new = jnp.maximum(blc_ref[...] + m_col[...], dmc_ref[...])
    w_src_t = jnp.exp(dt_ref[...] - m_col_new)
    r_t = rt_ref[...]
    ones = jnp.ones((tc, HEAD_DIM), _BF16)

    scans = [(dr * HEADS + h, refs, h, slice(h * HEAD_DIM, (h + 1) * HEAD_DIM), causal[dr])
             for dr, refs in enumerate((refs_f, refs_b)) for h in range(HEADS)]
    v_ext = [jnp.concatenate([refs[2][:, hs], ones], axis=1) for _, refs, _, hs, _ in scans]
    s16, g_b = [], []
    for l, (q_ref, kt_ref, _, _), h, hs, mask in scans:
        g_l = jnp.broadcast_to(g[:, l:l + 1], (tc, tc))
        w_intra = jnp.exp(jnp.where(mask, r_t[l:l + 1, :] - g_l, -jnp.inf))
        s16.append((_dot(q_ref[:, hs], kt_ref[h]) * w_intra).astype(_BF16))
        g_b.append(g_l)
    qc = []
    for l, (q_ref, kt_ref, _, _), h, hs, _ in scans:
        c_old = c_scr[l]
        qc.append(_dot(q_ref[:, hs], c_old.astype(_BF16)))
        kw_t = (kt_ref[h].astype(_F32) * w_src_t[l:l + 1, :]).astype(_BF16)
        c_scr[l] = w_carry[:, l:l + 1] * c_old + _dot(kw_t, v_ext[l])
    for l, (_, _, _, out_ref), h, hs, _ in scans:
        w_inter = jnp.exp(m_old[:, l:l + 1] - g_b[l])
        b_l = jnp.broadcast_to(bcum[:, l:l + 1], (tc, tc))
        sv = _dot(s16[l], v_ext[l])
        num = w_inter * qc[l][:, 0:HEAD_DIM] + sv[:, 0:HEAD_DIM]
        den = w_inter * qc[l][:, HEAD_DIM:] + sv[:, HEAD_DIM:]
        out_ref[:, hs] = num / jnp.maximum(jnp.abs(den), jnp.exp(-(b_l + g_b[l])))
    m_row[...] = m_new
    m_col[...] = m_col_new


def _mlstm(q, kt, vo, gates, b_gate, layer, nctx):
    b, t, _ = q.shape
    nchunks = t // CHUNK
    assert nchunks % 2 == 0

    def bwd_chunk(j):
        return jnp.where(j < nctx, nctx - 1 - j, nchunks - 1 - (j - nctx))

    def specs(chunk_of):
        nxt = lambda j: chunk_of(jnp.minimum(j + 1, nchunks - 1))
        return [
            pl.BlockSpec((None, CHUNK, BRANCH_WIDTH), lambda bi, j: (bi, chunk_of(j), 0)),
            pl.BlockSpec((None, HEADS, HEAD_DIM, CHUNK), lambda bi, j: (bi, 0, 0, chunk_of(j))),
            pl.BlockSpec((None, CHUNK, BRANCH_WIDTH), lambda bi, j: (bi, chunk_of(j), 0)),
            pl.BlockSpec((None, CHUNK, GATE_W), lambda bi, j: (bi, chunk_of(j), 0)),
            pl.BlockSpec((None, CHUNK, GATE_W), lambda bi, j: (bi, nxt(j), 0)),
        ]

    gate_slot = [pltpu.VMEM((CHUNK, 128), _F32), pltpu.VMEM((CHUNK, 128), _F32), pltpu.VMEM((8, CHUNK), _F32),
                 pltpu.VMEM((8, CHUNK), _F32), pltpu.VMEM((8, 128), _F32), pltpu.VMEM((8, CHUNK), _F32),
                 pltpu.VMEM((8, CHUNK), _F32)]
    h_shape = jax.ShapeDtypeStruct((b, t, BRANCH_WIDTH), _F32)
    return pl.pallas_call(
        _mlstm_kernel,
        grid=(b, nchunks),
        in_specs=specs(lambda j: j) + specs(bwd_chunk) + [_layer_spec(b_gate.shape, layer)],
        out_specs=[pl.BlockSpec((None, CHUNK, BRANCH_WIDTH), lambda bi, j: (bi, j, 0)),
                   pl.BlockSpec((None, CHUNK, BRANCH_WIDTH), lambda bi, j: (bi, bwd_chunk(j), 0))],
        out_shape=[h_shape, h_shape],
        scratch_shapes=[pltpu.VMEM((2 * HEADS, HEAD_DIM, 2 * HEAD_DIM), _F32), pltpu.VMEM((1, 128), _F32),
                        pltpu.VMEM((8, CHUNK), _F32)] + gate_slot + gate_slot,
        compiler_params=_cparams(2),
        name="mlstm",
    )(q, kt, vo, gates, gates, q, kt, vo, gates, gates, b_gate)


def _attn_kernel(cq_ref, kv_ref, ca_ref, sa_ref, caq_ref, saq_ref, gq_ref, wq_ref, gkv_ref, wk_ref, wv_ref,
                 o_ref, k_scr, v_scr, *, ctx_len, need_ctx, scale):
    i = pl.program_id(1)
    t = kv_ref.shape[0]
    tq = cq_ref.shape[0]
    head_w = 2 * HEAD_DIM

    def rope(y, cos_t, sin_t):
        return y * cos_t + pltpu.roll(y, ROPE_DIM, axis=1) * sin_t

    @pl.when(i == 0)
    def _():
        for r in range(t // ROW_TILE):
            rows = slice(r * ROW_TILE, (r + 1) * ROW_TILE)
            ckv = (_rms(kv_ref[rows, 0:KV_LORA].astype(_F32)) * gkv_ref[...]).astype(_BF16)
            k_nope = _dot(ckv, wk_ref[...])
            k_rope = rope(kv_ref[rows, KV_LORA:KV_SRC].astype(_F32), ca_ref[rows, :], sa_ref[rows, :]).astype(_BF16)
            for h in range(HEADS):
                k_scr[rows, h * head_w:h * head_w + HEAD_DIM] = k_nope[:, h * HEAD_DIM:(h + 1) * HEAD_DIM].astype(_BF16)
                k_scr[rows, h * head_w + HEAD_DIM:(h + 1) * head_w] = k_rope
            v_scr[rows, :] = _dot(ckv, wv_ref[...]).astype(_BF16)

    def attend(n_keys):
        cq = (_rms(cq_ref[...].astype(_F32)) * gq_ref[...]).astype(_BF16)
        qa = _dot(cq, wq_ref[...]) * (scale * LOG2_E)
        def scores(h):
            q_nope = qa[:, h * head_w:h * head_w + HEAD_DIM]
            q_rope = rope(qa[:, h * head_w + HEAD_DIM:(h + 1) * head_w], caq_ref[...], saq_ref[...])
            qh = jnp.concatenate([q_nope, q_rope], axis=1).astype(_BF16)
            return _dot_nt(qh, k_scr[0:n_keys, h * head_w:(h + 1) * head_w])

        s_next = scores(0)
        for h in range(HEADS):
            s = s_next
            if h + 1 < HEADS:
                s_next = scores(h + 1)
            e = jnp.exp2(s - jnp.max(s, axis=1, keepdims=True))
            o = _dot(e.astype(_BF16), v_scr[0:n_keys, h * HEAD_DIM:(h + 1) * HEAD_DIM])
            o_ref[:, h * HEAD_DIM:(h + 1) * HEAD_DIM] = (o / jnp.sum(e, axis=1, keepdims=True)).astype(o_ref.dtype)

    @pl.when(i < ctx_len // tq)
    def _():
        if need_ctx:
            attend(ctx_len)
        else:
            o_ref[...] = jnp.zeros_like(o_ref)

    @pl.when(i >= ctx_len // tq)
    def _():
        attend(t)


def _attn(at, cos_t, sin_t, g_qn, w_q, g_kvn, w_k, w_v, layer, ctx_len, need_ctx):
    b, t, _ = at.shape
    scale = (HEAD_DIM + ROPE_DIM) ** -0.5
    return pl.pallas_call(
        functools.partial(_attn_kernel, ctx_len=ctx_len, need_ctx=need_ctx, scale=scale),
        grid=(b, t // ROW_TILE),
        in_specs=[pl.BlockSpec((None, ROW_TILE, Q_LORA), lambda bi, i: (bi, i, 0)),
                  pl.BlockSpec((None, t, KV_SRC), lambda bi, i: (bi, 0, 1)),
                  _const_spec((t, 128)), _const_spec((t, 128)),
                  pl.BlockSpec((ROW_TILE, 128), lambda bi, i: (i, 0)),
                  pl.BlockSpec((ROW_TILE, 128), lambda bi, i: (i, 0)),
                  _layer_spec(g_qn.shape, layer), _layer_spec(w_q.shape, layer), _layer_spec(g_kvn.shape, layer),
                  _layer_spec(w_k.shape, layer), _layer_spec(w_v.shape, layer)],
        out_specs=pl.BlockSpec((None, ROW_TILE, BRANCH_WIDTH), lambda bi, i: (bi, i, 0)),
        out_shape=jax.ShapeDtypeStruct((b, t, BRANCH_WIDTH), _BF16),
        scratch_shapes=[pltpu.VMEM((t, HEADS * 2 * HEAD_DIM), _BF16), pltpu.VMEM((t, BRANCH_WIDTH), _BF16)],
        compiler_params=_cparams(2),
        name="attn",
    )(at, at, cos_t, sin_t, cos_t, sin_t, g_qn, w_q, g_kvn, w_k, w_v)


def _merge_mix(hf_ref, hb_ref, so_ref, yb_ref, us_ref, gt_ref, gmh_ref, gsgu_ref, ws_ref, bs_ref, wbr_ref, wout_ref, d):
    tm = hf_ref.shape[0]
    half = d // 2
    pb = [_dot(yb_ref[...], wbr_ref[1, :, c * half:(c + 1) * half]) for c in range(2)]
    hsum = hf_ref[...] + hb_ref[...]
    ya = jnp.concatenate(
        [_rms(hsum[:, h * HEAD_DIM:(h + 1) * HEAD_DIM]) for h in range(HEADS)], axis=1) * gmh_ref[...]
    ya = so_ref[...].astype(_F32) * ya
    cols = []
    for g in range(HEADS):
        gs = slice(BRANCH_WIDTH + g * HEAD_DIM, BRANCH_WIDTH + (g + 1) * HEAD_DIM)
        vn = (_rms(us_ref[:, gs].astype(_F32)) * gsgu_ref[:, g * HEAD_DIM:(g + 1) * HEAD_DIM]).astype(_BF16)
        mixed = [_dot(ws_ref[g], vn[n * CHUNK:(n + 1) * CHUNK, :]) + bs_ref[:, g:g + 1] for n in range(tm // CHUNK)]
        cols.append(jnp.concatenate(mixed, axis=0))
    yc = us_ref[:, 0:BRANCH_WIDTH].astype(_F32) * jnp.concatenate(cols, axis=1)
    ya16, yc16 = ya.astype(_BF16), yc.astype(_BF16)
    pa = [_dot(ya16, wbr_ref[0, :, c * half:(c + 1) * half]) for c in range(2)]
    pc = [_dot(yc16, wbr_ref[2, :, c * half:(c + 1) * half]) for c in range(2)]
    out = None
    for c in range(2):
        gate = lambda g: gt_ref[:, g * d + c * half:g * d + (c + 1) * half].astype(_F32)
        merged = (gate(0) * pa[c] + gate(1) * pb[c] + gate(2) * pc[c]).astype(_BF16)
        part = _dot(merged, wout_ref[c * half:(c + 1) * half, :])
        out = part if out is None else out + part
    return out


MERGE_TILE_INPUTS = 9


def _merge_kernel(*refs, d, nctx_tiles, off, tiles):
    per_tile = (refs[0:MERGE_TILE_INPUTS], refs[MERGE_TILE_INPUTS:2 * MERGE_TILE_INPUTS])
    gmh_ref, gsgu_ref, ws_ref, bs_ref, wbr_ref, wout_ref, gffn_ref, xo_ref, h2_ref = refs[2 * MERGE_TILE_INPUTS:]
    outs = [_merge_mix(*tile_refs[0:6], gmh_ref, gsgu_ref, ws_ref, bs_ref, wbr_ref, wout_ref, d)
            for tile_refs in per_tile]
    for h, tile_refs in enumerate(per_tile):
        xc_ref, xl_ref, mod_ref = tile_refs[6:9]
        rows = slice(h * ROW_TILE, (h + 1) * ROW_TILE)
        tile_in_batch = (2 * pl.program_id(0) + h) % tiles + off
        x = jnp.where(tile_in_batch < nctx_tiles, xc_ref[...], xl_ref[...])
        x_new = x + mod_ref[:, 2 * d:3 * d] * outs[h]
        xo_ref[rows, :] = x_new
        h2 = _rms(x_new) * gffn_ref[...]
        h2_ref[rows, :] = (h2 * (1.0 + mod_ref[:, 4 * d:5 * d]) + mod_ref[:, 3 * d:4 * d]).astype(h2_ref.dtype)


def _merge(hf, hb, vo, yb, us, gt, x_ctx, x_lat, lat_off, mod, g_mhead, g_sgu, w_s, b_s_t, w_branch, w_out, g_ffn,
           layer, nctx_tiles, skip_ctx):
    b, t, _ = hf.shape
    d = x_ctx.shape[2]
    off = nctx_tiles if skip_ctx else 0
    tiles = t // ROW_TILE - off
    assert (b * tiles) % 2 == 0

    def tile_specs(h):
        at = _pair_tile(h, tiles)
        tile = lambda w, blk: pl.BlockSpec((None, ROW_TILE, w), lambda n: (at(n)[0], at(n)[1] + off, blk))
        return [tile(BRANCH_WIDTH, 0), tile(BRANCH_WIDTH, 0), tile(BRANCH_WIDTH, 1), tile(BRANCH_WIDTH, 0),
                tile(2 * BRANCH_WIDTH, 0), tile(N_BRANCH * d, 0)] + _stream_tile_specs(at, off, d, nctx_tiles, lat_off,
                                                                                       mod, layer)

    tile_args = (hf, hb, vo, yb, us, gt, x_ctx, x_lat, mod)
    params = (g_mhead, g_sgu, w_s, b_s_t, w_branch, w_out, g_ffn)
    out_tile = lambda: pl.BlockSpec((2 * ROW_TILE, d), lambda n: (n, 0))
    rows_out = b * tiles * ROW_TILE
    x_mid, h2 = pl.pallas_call(
        functools.partial(_merge_kernel, d=d, nctx_tiles=nctx_tiles, off=off, tiles=tiles),
        grid=(b * tiles // 2,),
        in_specs=tile_specs(0) + tile_specs(1) + [_layer_spec(a.shape, layer) for a in params],
        out_specs=[out_tile(), out_tile()],
        out_shape=[jax.ShapeDtypeStruct((rows_out, d), _F32), jax.ShapeDtypeStruct((rows_out, d), _BF16)],
        compiler_params=_cparams(1),
        name="merge",
    )(*tile_args, *tile_args, *params)
    return x_mid.reshape(b, tiles * ROW_TILE, d), h2.reshape(b, tiles * ROW_TILE, d)


def _ffn_kernel(h_ref, hp_ref, hn_ref, x_ref, mod_ref, wup_ref, wcv_ref, bcv_ref, wdn_ref, gfin_ref,
                o_ref, ext_scr, act_scr, *, d, seg_tiles, tiles_total, final_norm):
    tm = h_ref.shape[0]
    ff = wdn_ref.shape[0]
    ck = FFN_CHUNK
    seg_first, seg_last = _segment_edges(pl.program_id(1), seg_tiles, tiles_total)
    ext_scr[0:FFN_HALO, :] = jnp.where(seg_first, jnp.zeros_like(hp_ref), hp_ref[...])
    ext_scr[FFN_HALO:FFN_HALO + tm, :] = h_ref[...]
    ext_scr[FFN_HALO + tm:, :] = jnp.where(seg_last, jnp.zeros_like(hn_ref), hn_ref[...])
    ext_rows = tm + 2 * FFN_HALO
    inner = slice(FFN_HALO, FFN_HALO + tm)

    def conv(cols):
        a = _dot(ext_scr[...], wup_ref[:, cols])
        a_prev = pltpu.roll(a, 1, axis=0)[inner, :]
        a_next = pltpu.roll(a, ext_rows - 1, axis=0)[inner, :]
        return (bcv_ref[:, cols] + a_prev * wcv_ref[0:1, cols] + a[inner, :] * wcv_ref[1:2, cols]
                + a_next * wcv_ref[2:3, cols])

    for c in range(ff // ck):
        gate = conv(slice(c * ck, (c + 1) * ck))
        val = conv(slice(ff + c * ck, ff + (c + 1) * ck))
        act_scr[:, c * ck:(c + 1) * ck] = (_silu(gate) * val).astype(_BF16)

    x_new = x_ref[...] + mod_ref[:, 5 * d:6 * d] * _dot(act_scr[...], wdn_ref[...])
    if final_norm:
        x_new = _rms(x_new) * gfin_ref[...]
    o_ref[...] = x_new


def _ffn(h2, x, mod, w_up, w_cv, b_cv, w_dn, g_final, layer, seg_rows, nctx_tiles, final_norm):
    b, r, d = x.shape
    tiles = r // ROW_TILE
    hb = ROW_TILE // FFN_HALO
    return pl.pallas_call(
        functools.partial(_ffn_kernel, d=d, seg_tiles=tuple(s // ROW_TILE for s in seg_rows), tiles_total=tiles,
                          final_norm=final_norm),
        grid=(b, tiles),
        in_specs=[pl.BlockSpec((None, ROW_TILE, d), lambda bi, i: (bi, i, 0)),
                  pl.BlockSpec((None, FFN_HALO, d), lambda bi, i: (bi, jnp.maximum(i * hb - 1, 0), 0)),
                  pl.BlockSpec((None, FFN_HALO, d), lambda bi, i: (bi, jnp.minimum((i + 1) * hb, r // FFN_HALO - 1), 0)),
                  pl.BlockSpec((None, ROW_TILE, d), lambda bi, i: (bi, i, 0)),
                  _mod_spec(mod, layer, nctx_tiles, 0)]
        + [_layer_spec(a.shape, layer) for a in (w_up, w_cv, b_cv, w_dn)] + [_const_spec((1, d))],
        out_specs=pl.BlockSpec((None, ROW_TILE, d), lambda bi, i: (bi, i, 0)),
        out_shape=jax.ShapeDtypeStruct((b, r, d), _F32),
        scratch_shapes=[pltpu.VMEM((ROW_TILE + 2 * FFN_HALO, d), _BF16),
                        pltpu.VMEM((ROW_TILE, w_dn.shape[1]), _BF16)],
        compiler_params=_cparams(2),
        name="ffn",
    )(h2, h2, h2, x, mod, w_up, w_cv, b_cv, w_dn, g_final.reshape(1, d))


def _deinterleave(w):
    return jnp.concatenate([w[..., 0::2], w[..., 1::2]], axis=-1)


def _rotated(w):
    return jnp.concatenate([-w[..., 1::2], w[..., 0::2]], axis=-1)


def _layout_w_in(w):
    depth, d, _ = w.shape
    w = w.astype(_BF16)
    sizes = (BRANCH_WIDTH,) * 4 + (M_GATES, Q_LORA, KV_LORA, ROPE_DIM, BRANCH_WIDTH, BRANCH_WIDTH, N_BRANCH * d)
    splits = tuple(int(s) for s in np.cumsum(sizes)[:-1])
    q, k, v, o, mg, cq, ckv, kr, u, s, gt = jnp.split(w, splits, axis=2)
    mg = mg.reshape(depth, d, 2, 2, HEADS)
    pad = jnp.zeros((depth, d, 128 - 2 * HEADS), w.dtype)
    cols = [q, k, v, o, u, s, gt, cq, ckv, _deinterleave(kr), _rotated(kr),
            mg[:, :, :, 0, :].reshape(depth, d, 2 * HEADS), pad, mg[:, :, :, 1, :].reshape(depth, d, 2 * HEADS), pad]
    return jnp.concatenate(cols, axis=2)


def _layout_gate_bias(bg):
    depth = bg.shape[0]
    bg = bg.reshape(depth, 2, 2, HEADS)
    pad = jnp.zeros((depth, 128 - 2 * HEADS), bg.dtype)
    return jnp.concatenate([bg[:, :, 0, :].reshape(depth, -1), pad, bg[:, :, 1, :].reshape(depth, -1), pad],
                           axis=1).reshape(depth, 1, GATE_W)


def _layout_w_uq(w):
    depth = w.shape[0]
    w = w.reshape(depth, Q_LORA, HEADS, HEAD_DIM + ROPE_DIM)
    nope, rope = w[..., :HEAD_DIM], w[..., HEAD_DIM:]
    return jnp.concatenate([nope, _deinterleave(rope), _rotated(rope)], axis=-1).reshape(depth, Q_LORA, -1).astype(_BF16)


def _layout_w_ukv(w):
    depth = w.shape[0]
    w = w.reshape(depth, KV_LORA, HEADS, 2 * HEAD_DIM).astype(_BF16)
    return w[..., :HEAD_DIM].reshape(depth, KV_LORA, -1), w[..., HEAD_DIM:].reshape(depth, KV_LORA, -1)


def _rope_tables(ctx_len, n_latent):
    rows = n_latent // GRID_W
    row = jnp.repeat(jnp.arange(rows), GRID_W)
    col = jnp.tile(jnp.arange(GRID_W), rows)
    n_freq = ROPE_DIM // 4
    inv = ROPE_BASE ** (-jnp.arange(n_freq, dtype=_F32) / n_freq)
    ang = jnp.concatenate([row[:, None] * inv, col[:, None] * inv], axis=-1)
    zeros = jnp.zeros((n_latent, 128 - ROPE_DIM), _F32)
    cos_l = jnp.concatenate([jnp.cos(ang), jnp.cos(ang), zeros], axis=1)
    sin_l = jnp.concatenate([jnp.sin(ang), jnp.sin(ang), zeros], axis=1)
    cos_c = jnp.concatenate([jnp.ones((ctx_len, ROPE_DIM), _F32), jnp.zeros((ctx_len, 128 - ROPE_DIM), _F32)], axis=1)
    return jnp.concatenate([cos_c, cos_l], axis=0), jnp.concatenate([jnp.zeros_like(cos_c), sin_l], axis=0)


def kernel(x, c, ctx, c_ctx, w_ada, b_ada, g_mix, w_in, w_qkconv, b_qkconv, b_mgate, g_mhead, g_qnorm, w_uq,
           g_kvnorm, w_ukv, g_sgu, w_s, b_s, w_branch, w_out, g_ffn, w_up, w_ffconv, b_ffconv, w_down, g_final):
    b, s, d = x.shape
    ctx_len = ctx.shape[1]
    depth = w_in.shape[0]
    assert ctx_len % ROW_TILE == 0 and s % ROW_TILE == 0 and s % GRID_W == 0
    nctx_tiles = ctx_len // ROW_TILE
    row_param = lambda a: a.reshape(depth, 1, a.shape[-1])

    cos_t, sin_t = _rope_tables(ctx_len, s)
    cond_rows = jnp.concatenate([c, c_ctx[None, :], jnp.zeros((7, d), c.dtype)], axis=0)
    mod = _ada(cond_rows, w_ada, b_ada)
    w_in_l = _layout_w_in(w_in)
    b_gate_l = _layout_gate_bias(b_mgate)
    w_uq_l = _layout_w_uq(w_uq)
    w_k_l, w_v_l = _layout_w_ukv(w_ukv)
    b_s_t = jnp.pad(jnp.swapaxes(b_s, 1, 2), ((0, 0), (0, 0), (0, 128 - HEADS)))
    w_s16, w_branch16, w_out16 = w_s.astype(_BF16), w_branch.astype(_BF16), w_out.astype(_BF16)
    w_up16, w_down16 = w_up.astype(_BF16), w_down.astype(_BF16)
    g_mix_r, g_mhead_r, g_sgu_r, g_ffn_r = row_param(g_mix), row_param(g_mhead), row_param(g_sgu), row_param(g_ffn)
    g_qn_r, g_kvn_r, b_qkconv_r, b_ffconv_r = row_param(g_qnorm), row_param(g_kvnorm), row_param(b_qkconv), row_param(b_ffconv)

    x_ctx, x_lat, lat_off = ctx, x, 0
    for l in range(depth):
        last = l == depth - 1
        qk, vo, us, gt, at, gates = _proj(x_ctx, x_lat, lat_off, mod, g_mix_r, w_in_l, l, nctx_tiles)
        q_act, kt_act = _qkconv(qk, w_qkconv, b_qkconv_r, l, (0, ctx_len))
        hf, hb = _mlstm(q_act, kt_act, vo, gates, b_gate_l, l, ctx_len // CHUNK)
        yb = _attn(at, cos_t, sin_t, g_qn_r, w_uq_l, g_kvn_r, w_k_l, w_v_l, l, ctx_len, not last)
        x_mid, h2 = _merge(hf, hb, vo, yb, us, gt, x_ctx, x_lat, lat_off, mod, g_mhead_r, g_sgu_r, w_s16, b_s_t,
                           w_branch16, w_out16, g_ffn_r, l, nctx_tiles, last)
        if last:
            return _ffn(h2, x_mid, mod, w_up16, w_ffconv, b_ffconv_r, w_down16, g_final, l, (0,), 0, True)
        stream = _ffn(h2, x_mid, mod, w_up16, w_ffconv, b_ffconv_r, w_down16, g_final, l, (0, ctx_len), nctx_tiles,
                      False)
        x_ctx, x_lat, lat_off = stream, stream, nctx_tiles
```

```python
import functools

import jax
import jax.numpy as jnp
import numpy as np
from jax import lax
from jax.experimental import pallas as pl
from jax.experimental.pallas import tpu as pltpu

EPS = 1e-6
GRID_W = 64
ROPE_BASE = 10000.0
LOG2_E = 1.4426950408889634

HEADS = 4
HEAD_DIM = 128
CHUNK = 128
BRANCH_WIDTH = HEADS * HEAD_DIM
Q_LORA = 384
KV_LORA = 256
ROPE_DIM = 64
N_BRANCH = 3
M_GATES = 4 * HEADS
KV_SRC = KV_LORA + 2 * ROPE_DIM
ATTN_W = Q_LORA + KV_SRC
GATE_W = 256

ROW_TILE = 256
FFN_HALO = 16
FFN_CHUNK = 256
V7X_VMEM_LIMIT = 56 * 1024 * 1024

_BF16 = jnp.bfloat16
_F32 = jnp.float32


def _cparams(n_axes, flags=None):
    return pltpu.CompilerParams(dimension_semantics=("arbitrary",) * n_axes, vmem_limit_bytes=V7X_VMEM_LIMIT,
                                flags=flags)


def _const_spec(shape):
    return pl.BlockSpec(tuple(shape), lambda *_: (0,) * len(shape))


def _layer_spec(stacked_shape, layer):
    shape = tuple(stacked_shape[1:])
    return pl.BlockSpec((None,) + shape, lambda *_: (layer,) + (0,) * len(shape))


def _mod_spec(mod, layer, nctx_tiles, off):
    batch = mod.shape[1] - 8
    return pl.BlockSpec((None, None, 1, mod.shape[3]),
                        lambda bi, i: (layer, jnp.where(i + off < nctx_tiles, batch, bi), 0, 0))


def _row_specs(width, nctx_tiles, lat_off, off=0):
    ctx = pl.BlockSpec((None, ROW_TILE, width), lambda bi, i: (bi, jnp.minimum(i + off, nctx_tiles - 1), 0))
    lat = pl.BlockSpec((None, ROW_TILE, width), lambda bi, i: (bi, jnp.maximum(i + off - nctx_tiles, 0) + lat_off, 0))
    return [ctx, lat]


def _sigmoid(x):
    return 1.0 / (1.0 + jnp.exp(-x))


def _silu(x):
    return x * _sigmoid(x)


def _gelu_tanh(x):
    return x * (0.5 * (1.0 + jnp.tanh(0.7978845608028654 * (x + 0.044715 * (x * x * x)))))


def _rms(x):
    return x * lax.rsqrt(jnp.mean(x * x, axis=-1, keepdims=True) + EPS)


def _dot(a, b):
    return jnp.dot(a, b, preferred_element_type=_F32)


def _dot_nt(a, b):
    return lax.dot_general(a, b, (((1,), (1,)), ((), ())), preferred_element_type=_F32)


def _segment_edges(i, seg_tiles, tiles_total):
    first = functools.reduce(jnp.logical_or, [i == s for s in seg_tiles])
    last = functools.reduce(jnp.logical_or, [i == s - 1 for s in seg_tiles[1:] + (tiles_total,)])
    return first, last


def _ada_kernel(c_ref, w_ref, b_ref, o_ref):
    cond = _silu(c_ref[...])
    o_ref[...] = jnp.dot(cond, w_ref[...], precision=lax.Precision.HIGHEST, preferred_element_type=_F32) + b_ref[...]


def _ada(cond_rows, w_ada, b_ada):
    rows, d = cond_rows.shape
    depth, _, n = w_ada.shape
    out = pl.pallas_call(
        _ada_kernel,
        grid=(depth, n // d),
        in_specs=[pl.BlockSpec((rows, d), lambda l, j: (0, 0)),
                  pl.BlockSpec((None, d, d), lambda l, j: (l, 0, j)),
                  pl.BlockSpec((None, 1, d), lambda l, j: (l, 0, j))],
        out_specs=pl.BlockSpec((None, rows, d), lambda l, j: (l, 0, j)),
        out_shape=jax.ShapeDtypeStruct((depth, rows, n), _F32),
        compiler_params=_cparams(2),
        name="ada",
    )(cond_rows, w_ada, b_ada.reshape(depth, 1, n))
    return out.reshape(depth, rows, 1, n)


def _proj_kernel(xc0_ref, xl0_ref, mod0_ref, xc1_ref, xl1_ref, mod1_ref, g_ref, w_ref,
                 qk_ref, vo_ref, us_ref, gt_ref, at_ref, gate_ref, *, d, nctx_tiles, tiles):
    bw = BRANCH_WIDTH
    col_at = 6 * bw + N_BRANCH * d
    normed = []
    for h, (xc_ref, xl_ref, mod_ref) in enumerate(((xc0_ref, xl0_ref, mod0_ref), (xc1_ref, xl1_ref, mod1_ref))):
        x = jnp.where((2 * pl.program_id(0) + h) % tiles < nctx_tiles, xc_ref[...], xl_ref[...])
        normed.append((_rms(x) * g_ref[...] * (1.0 + mod_ref[:, d:2 * d]) + mod_ref[:, 0:d]).astype(_BF16))
    for h, hn in enumerate(normed):
        rows = slice(h * ROW_TILE, (h + 1) * ROW_TILE)
        us_ref[rows, :] = _gelu_tanh(_dot(hn, w_ref[:, 4 * bw:6 * bw])).astype(_BF16)
        for g in range(N_BRANCH):
            cols = slice(6 * bw + g * d, 6 * bw + (g + 1) * d)
            gt_ref[rows, g * d:(g + 1) * d] = _sigmoid(_dot(hn, w_ref[:, cols])).astype(_BF16)
        vo_ref[rows, bw:2 * bw] = _sigmoid(_dot(hn, w_ref[:, 3 * bw:4 * bw])).astype(_BF16)
        vo_ref[rows, 0:bw] = _dot(hn, w_ref[:, 2 * bw:3 * bw]).astype(_BF16)
        qk_ref[rows, :] = _dot(hn, w_ref[:, 0:2 * bw])
        at_ref[rows, :] = _dot(hn, w_ref[:, col_at:col_at + ATTN_W]).astype(_BF16)
        gate_ref[rows, :] = _dot(hn, w_ref[:, col_at + ATTN_W:col_at + ATTN_W + GATE_W])


def _pair_tile(h, tiles):
    def index(n):
        k = 2 * n + h
        return k // tiles, k % tiles
    return index


def _stream_tile_specs(at, off, d, nctx_tiles, lat_off, mod, layer):
    batch_row = mod.shape[1] - 8
    return [pl.BlockSpec((None, ROW_TILE, d), lambda n: (at(n)[0], jnp.minimum(at(n)[1] + off, nctx_tiles - 1), 0)),
            pl.BlockSpec((None, ROW_TILE, d),
                         lambda n: (at(n)[0], jnp.maximum(at(n)[1] + off - nctx_tiles, 0) + lat_off, 0)),
            pl.BlockSpec((None, None, 1, mod.shape[3]),
                         lambda n: (layer, jnp.where(at(n)[1] + off < nctx_tiles, batch_row, at(n)[0]), 0, 0))]


def _proj(x_ctx, x_lat, lat_off, mod, g_mix, w_in, layer, nctx_tiles):
    b, _, d = x_ctx.shape
    tiles = nctx_tiles + x_lat.shape[1] // ROW_TILE - lat_off
    bw = BRANCH_WIDTH
    widths = (2 * bw, 2 * bw, 2 * bw, N_BRANCH * d, ATTN_W, GATE_W)
    dtypes = (_F32, _BF16, _BF16, _BF16, _BF16, _F32)
    assert sum(widths) == w_in.shape[2] and (b * tiles) % 2 == 0
    rows = b * tiles * ROW_TILE
    tile_args = (x_ctx, x_lat, mod)
    outs = pl.pallas_call(
        functools.partial(_proj_kernel, d=d, nctx_tiles=nctx_tiles, tiles=tiles),
        grid=(b * tiles // 2,),
        in_specs=_stream_tile_specs(_pair_tile(0, tiles), 0, d, nctx_tiles, lat_off, mod, layer)
        + _stream_tile_specs(_pair_tile(1, tiles), 0, d, nctx_tiles, lat_off, mod, layer)
        + [_layer_spec(g_mix.shape, layer), _layer_spec(w_in.shape, layer)],
        out_specs=[pl.BlockSpec((2 * ROW_TILE, w), lambda n: (n, 0)) for w in widths],
        out_shape=[jax.ShapeDtypeStruct((rows, w), dt) for w, dt in zip(widths, dtypes)],
        compiler_params=_cparams(1),
        name="proj",
    )(*tile_args, *tile_args, g_mix, w_in)
    return [o.reshape(b, tiles * ROW_TILE, o.shape[1]) for o in outs]


def _qkconv_kernel(x_ref, hp_ref, hn_ref, w_ref, b_ref, q_ref, kt_ref, *, seg_tiles, tiles_total):
    tm = x_ref.shape[0]
    seg_first, seg_last = _segment_edges(pl.program_id(1), seg_tiles, tiles_total)
    row = lax.broadcasted_iota(jnp.int32, (tm, 1), 0)
    x = x_ref[...]
    prev_row = jnp.where(seg_first, 0.0, hp_ref[7:8, :])
    next_row = jnp.where(seg_last, 0.0, hn_ref[0:1, :])
    x_prev = jnp.where(row == 0, prev_row, pltpu.roll(x, 1, axis=0))
    x_next = jnp.where(row == tm - 1, next_row, pltpu.roll(x, tm - 1, axis=0))
    a = _silu(b_ref[...] + x_prev * w_ref[0:1, :] + x * w_ref[1:2, :] + x_next * w_ref[2:3, :])
    q_ref[...] = a[:, 0:BRANCH_WIDTH].astype(q_ref.dtype)
    for h in range(HEADS):
        k = a[:, BRANCH_WIDTH + h * HEAD_DIM:BRANCH_WIDTH + (h + 1) * HEAD_DIM] * (HEAD_DIM ** -0.5)
        kt_ref[h] = k.T.astype(kt_ref.dtype)


def _qkconv(qk, w_qkconv, b_qkconv, layer, seg_rows):
    b, t, qk_w = qk.shape
    tiles = t // ROW_TILE
    hb = ROW_TILE // 8
    return pl.pallas_call(
        functools.partial(_qkconv_kernel, seg_tiles=tuple(s // ROW_TILE for s in seg_rows), tiles_total=tiles),
        grid=(b, tiles),
        in_specs=[pl.BlockSpec((None, ROW_TILE, qk_w), lambda bi, i: (bi, i, 0)),
                  pl.BlockSpec((None, 8, qk_w), lambda bi, i: (bi, jnp.maximum(i * hb - 1, 0), 0)),
                  pl.BlockSpec((None, 8, qk_w), lambda bi, i: (bi, jnp.minimum((i + 1) * hb, t // 8 - 1), 0)),
                  _layer_spec(w_qkconv.shape, layer), _layer_spec(b_qkconv.shape, layer)],
        out_specs=[pl.BlockSpec((None, ROW_TILE, BRANCH_WIDTH), lambda bi, i: (bi, i, 0)),
                   pl.BlockSpec((None, HEADS, HEAD_DIM, ROW_TILE), lambda bi, i: (bi, 0, 0, i))],
        out_shape=[jax.ShapeDtypeStruct((b, t, BRANCH_WIDTH), _BF16),
                   jax.ShapeDtypeStruct((b, HEADS, HEAD_DIM, t), _BF16)],
        compiler_params=_cparams(2),
        name="qkconv",
    )(qk, qk, qk, w_qkconv, b_qkconv)


def _mlstm_kernel(q_f, kt_f, v_f, gf_cur, gf_nxt, q_b, kt_b, v_b, gb_cur, gb_nxt, bg_ref, hf_ref, hb_ref,
                  c_scr, m_row, m_col, *pre):
    tc = CHUNK
    low, high = pl.ds(0, tc), pl.ds(tc, tc)

    def scan_step(slot, fwd_rows, bwd_rows):
        views = [(q.at[rows, :], kt.at[:, :, rows], v.at[rows, :], out.at[rows, :])
                 for rows, (q, kt, v, out) in ((fwd_rows, (q_f, kt_f, v_f, hf_ref)), (bwd_rows, (q_b, kt_b, v_b, hb_ref)))]
        _mlstm_scan_step(slot, is_fwd, causal, views[0], views[1], c_scr, m_row, m_col)

    pre_a, pre_b = pre[:len(pre) // 2], pre[len(pre) // 2:]
    lane = lax.broadcasted_iota(jnp.int32, (1, 128), 1)
    row = lax.broadcasted_iota(jnp.int32, (tc, 1), 0)
    is_fwd = lane < HEADS
    r_idx = lax.broadcasted_iota(jnp.int32, (tc, tc), 0)
    c_idx = lax.broadcasted_iota(jnp.int32, (tc, tc), 1)
    causal = (c_idx <= r_idx, c_idx >= r_idx)

    def gate_prologue(g_f, g_b, slot):
        bcum_ref, run_ref, rt_ref, dt_ref, rows_ref, blc_ref, dmc_ref = slot
        log_i = jnp.where(is_fwd, g_f[:, 0:128], g_b[:, 0:128]) + bg_ref[:, 0:128]
        gates_f = jnp.where(is_fwd, g_f[:, 128:256], g_b[:, 128:256]) + bg_ref[:, 128:256]
        log_f = jnp.minimum(gates_f, 0.0) - jnp.log(1.0 + jnp.exp(-jnp.abs(gates_f)))
        cum_f, step = log_f, 1
        while step < tc:
            cum_f = cum_f + jnp.where(row >= step, pltpu.roll(cum_f, step, axis=0), 0.0)
            step *= 2
        b_last = cum_f[tc - 1:tc, :]
        bcum = jnp.where(is_fwd, cum_f, b_last - cum_f + log_f)
        r = log_i - bcum
        run_f, run_b, step = r, r, 1
        while step < tc:
            run_f = jnp.maximum(run_f, jnp.where(row >= step, pltpu.roll(run_f, step, axis=0), -jnp.inf))
            run_b = jnp.maximum(run_b, jnp.where(row < tc - step, pltpu.roll(run_b, tc - step, axis=0), -jnp.inf))
            step *= 2
        decay = b_last + r
        r_t = r.T[0:8, :]
        decay_t = decay.T[0:8, :]
        bcum_ref[...] = bcum
        run_ref[...] = jnp.where(is_fwd, run_f, run_b)
        rt_ref[...] = r_t
        dt_ref[...] = decay_t
        rows_ref[0:1, :] = b_last
        rows_ref[1:2, :] = jnp.max(decay, axis=0, keepdims=True)
        blc_ref[...] = decay_t - r_t
        dmc_ref[...] = jnp.broadcast_to(jnp.max(decay_t, axis=1, keepdims=True), (8, tc))

    @pl.when(pl.program_id(1) == 0)
    def _():
        c_scr[...] = jnp.zeros_like(c_scr)
        m_row[...] = jnp.zeros_like(m_row)
        m_col[...] = jnp.zeros_like(m_col)
        gate_prologue(gf_cur.at[low, :], gb_cur.at[high, :], pre_a)

    gate_prologue(gf_cur.at[high, :], gb_cur.at[low, :], pre_b)
    scan_step(pre_a, low, high)
    gate_prologue(gf_nxt.at[low, :], gb_nxt.at[high, :], pre_a)
    scan_step(pre_b, high, low)


def _mlstm_scan_step(slot, is_fwd, causal, refs_f, refs_b, c_scr, m_row, m_col):
    tc = CHUNK
    bcum_ref, run_ref, rt_ref, dt_ref, rows_ref, blc_ref, dmc_ref = slot
    bcum = bcum_ref[...]
    b_last, decay_max = rows_ref[0:1, :], rows_ref[1:2, :]
    m_old = m_row[...]
    g = jnp.maximum(m_old, run_ref[...])
    m_new = jnp.maximum(b_last + m_old, decay_max)
    w_carry = jnp.exp(b_last + m_old - m_new)
    m_col_new = jnp.maximum(blc_ref[...] + m_col[...], dmc_ref[...])
    w_src_t = jnp.exp(dt_ref[...] - m_col_new)
    r_t = rt_ref[...]
    ones = jnp.ones((tc, HEAD_DIM), _BF16)

    scans = [(dr * HEADS + h, refs, h, slice(h * HEAD_DIM, (h + 1) * HEAD_DIM), causal[dr])
             for dr, refs in enumerate((refs_f, refs_b)) for h in range(HEADS)]
    v_ext = [jnp.concatenate([refs[2][:, hs], ones], axis=1) for _, refs, _, hs, _ in scans]
    s16, g_b = [], []
    for l, (q_ref, kt_ref, _, _), h, hs, mask in scans:
        g_l = jnp.broadcast_to(g[:, l:l + 1], (tc, tc))
        w_intra = jnp.exp(jnp.where(mask, r_t[l:l + 1, :] - g_l, -jnp.inf))
        s16.append((_dot(q_ref[:, hs], kt_ref[h]) * w_intra).astype(_BF16))
        g_b.append(g_l)
    qc = []
    for l, (q_ref, kt_ref, _, _), h, hs, _ in scans:
        c_old = c_scr[l]
        qc.append(_dot(q_ref[:, hs], c_old.astype(_BF16)))
        kw_t = (kt_ref[h].astype(_F32) * w_src_t[l:l + 1, :]).astype(_BF16)
        c_scr[l] = w_carry[:, l:l + 1] * c_old + _dot(kw_t, v_ext[l])
    for l, (_, _, _, out_ref), h, hs, _ in scans:
        w_inter = jnp.exp(m_old[:, l:l + 1] - g_b[l])
        b_l = jnp.broadcast_to(bcum[:, l:l + 1], (tc, tc))
        sv = _dot(s16[l], v_ext[l])
        num = w_inter * qc[l][:, 0:HEAD_DIM] + sv[:, 0:HEAD_DIM]
        den = w_inter * qc[l][:, HEAD_DIM:] + sv[:, HEAD_DIM:]
        out_ref[:, hs] = num / jnp.maximum(jnp.abs(den), jnp.exp(-(b_l + g_b[l])))
    m_row[...] = m_new
    m_col[...] = m_col_new


def _mlstm(q, kt, vo, gates, b_gate, layer, nctx):
    b, t, _ = q.shape
    blk = 2 * CHUNK
    nblocks, nctx_blocks = t // blk, nctx // 2
    assert t % blk == 0 and nctx % 2 == 0

    def bwd_block(j):
        return jnp.where(j < nctx_blocks, nctx_blocks - 1 - j, nblocks - 1 - (j - nctx_blocks))

    def specs(block_of):
        nxt = lambda j: block_of(jnp.minimum(j + 1, nblocks - 1))
        return [
            pl.BlockSpec((None, blk, BRANCH_WIDTH), lambda bi, j: (bi, block_of(j), 0)),
            pl.BlockSpec((None, HEADS, HEAD_DIM, blk), lambda bi, j: (bi, 0, 0, block_of(j))),
            pl.BlockSpec((None, blk, BRANCH_WIDTH), lambda bi, j: (bi, block_of(j), 0)),
            pl.BlockSpec((None, blk, GATE_W), lambda bi, j: (bi, block_of(j), 0)),
            pl.BlockSpec((None, blk, GATE_W), lambda bi, j: (bi, nxt(j), 0)),
        ]

    gate_slot = [pltpu.VMEM((CHUNK, 128), _F32), pltpu.VMEM((CHUNK, 128), _F32), pltpu.VMEM((8, CHUNK), _F32),
                 pltpu.VMEM((8, CHUNK), _F32), pltpu.VMEM((8, 128), _F32), pltpu.VMEM((8, CHUNK), _F32),
                 pltpu.VMEM((8, CHUNK), _F32)]
    h_shape = jax.ShapeDtypeStruct((b, t, BRANCH_WIDTH), _F32)
    return pl.pallas_call(
        _mlstm_kernel,
        grid=(b, nblocks),
        in_specs=specs(lambda j: j) + specs(bwd_block) + [_layer_spec(b_gate.shape, layer)],
        out_specs=[pl.BlockSpec((None, blk, BRANCH_WIDTH), lambda bi, j: (bi, j, 0)),
                   pl.BlockSpec((None, blk, BRANCH_WIDTH), lambda bi, j: (bi, bwd_block(j), 0))],
        out_shape=[h_shape, h_shape],
        scratch_shapes=[pltpu.VMEM((2 * HEADS, HEAD_DIM, 2 * HEAD_DIM), _F32), pltpu.VMEM((1, 128), _F32),
                        pltpu.VMEM((8, CHUNK), _F32)] + gate_slot + gate_slot,
        compiler_params=_cparams(2),
        name="mlstm",
    )(q, kt, vo, gates, gates, q, kt, vo, gates, gates, b_gate)


def _attn_kernel(cq_ref, kv_ref, ca_ref, sa_ref, caq_ref, saq_ref, gq_ref, wq_ref, gkv_ref, wk_ref, wv_ref,
                 o_ref, k_scr, v_scr, *, ctx_len, need_ctx, scale):
    i = pl.program_id(1)
    t = kv_ref.shape[0]
    tq = cq_ref.shape[0]
    head_w = 2 * HEAD_DIM

    def rope(y, cos_t, sin_t):
        return y * cos_t + pltpu.roll(y, ROPE_DIM, axis=1) * sin_t

    @pl.when(i == 0)
    def _():
        for r in range(t // ROW_TILE):
            rows = slice(r * ROW_TILE, (r + 1) * ROW_TILE)
            ckv = (_rms(kv_ref[rows, 0:KV_LORA].astype(_F32)) * gkv_ref[...]).astype(_BF16)
            k_nope = _dot(ckv, wk_ref[...])
            k_rope = rope(kv_ref[rows, KV_LORA:KV_SRC].astype(_F32), ca_ref[rows, :], sa_ref[rows, :]).astype(_BF16)
            for h in range(HEADS):
                k_scr[rows, h * head_w:h * head_w + HEAD_DIM] = k_nope[:, h * HEAD_DIM:(h + 1) * HEAD_DIM].astype(_BF16)
                k_scr[rows, h * head_w + HEAD_DIM:(h + 1) * head_w] = k_rope
            v_scr[rows, :] = _dot(ckv, wv_ref[...]).astype(_BF16)

    def attend(n_keys):
        cq = (_rms(cq_ref[...].astype(_F32)) * gq_ref[...]).astype(_BF16)
        qa = _dot(cq, wq_ref[...]) * (scale * LOG2_E)
        def scores(h):
            q_nope = qa[:, h * head_w:h * head_w + HEAD_DIM]
            q_rope = rope(qa[:, h * head_w + HEAD_DIM:(h + 1) * head_w], caq_ref[...], saq_ref[...])
            qh = jnp.concatenate([q_nope, q_rope], axis=1).astype(_BF16)
            return _dot_nt(qh, k_scr[0:n_keys, h * head_w:(h + 1) * head_w])

        s_next = scores(0)
        for h in range(HEADS):
            s = s_next
            if h + 1 < HEADS:
                s_next = scores(h + 1)
            e = jnp.exp2(s - jnp.max(s, axis=1, keepdims=True))
            o = _dot(e.astype(_BF16), v_scr[0:n_keys, h * HEAD_DIM:(h + 1) * HEAD_DIM])
            o_ref[:, h * HEAD_DIM:(h + 1) * HEAD_DIM] = (o / jnp.sum(e, axis=1, keepdims=True)).astype(o_ref.dtype)

    @pl.when(i < ctx_len // tq)
    def _():
        if need_ctx:
            attend(ctx_len)
        else:
            o_ref[...] = jnp.zeros_like(o_ref)

    @pl.when(i >= ctx_len // tq)
    def _():
        attend(t)


def _attn(at, cos_t, sin_t, g_qn, w_q, g_kvn, w_k, w_v, layer, ctx_len, need_ctx):
    b, t, _ = at.shape
    scale = (HEAD_DIM + ROPE_DIM) ** -0.5
    return pl.pallas_call(
        functools.partial(_attn_kernel, ctx_len=ctx_len, need_ctx=need_ctx, scale=scale),
        grid=(b, t // ROW_TILE),
        in_specs=[pl.BlockSpec((None, ROW_TILE, Q_LORA), lambda bi, i: (bi, i, 0)),
                  pl.BlockSpec((None, t, KV_SRC), lambda bi, i: (bi, 0, 1)),
                  _const_spec((t, 128)), _const_spec((t, 128)),
                  pl.BlockSpec((ROW_TILE, 128), lambda bi, i: (i, 0)),
                  pl.BlockSpec((ROW_TILE, 128), lambda bi, i: (i, 0)),
                  _layer_spec(g_qn.shape, layer), _layer_spec(w_q.shape, layer), _layer_spec(g_kvn.shape, layer),
                  _layer_spec(w_k.shape, layer), _layer_spec(w_v.shape, layer)],
        out_specs=pl.BlockSpec((None, ROW_TILE, BRANCH_WIDTH), lambda bi, i: (bi, i, 0)),
        out_shape=jax.ShapeDtypeStruct((b, t, BRANCH_WIDTH), _BF16),
        scratch_shapes=[pltpu.VMEM((t, HEADS * 2 * HEAD_DIM), _BF16), pltpu.VMEM((t, BRANCH_WIDTH), _BF16)],
        compiler_params=_cparams(2),
        name="attn",
    )(at, at, cos_t, sin_t, cos_t, sin_t, g_qn, w_q, g_kvn, w_k, w_v)


def _merge_mix(hf_ref, hb_ref, so_ref, yb_ref, us_ref, gt_ref, gmh_ref, gsgu_ref, ws_ref, bs_ref, wbr_ref, wout_ref, d):
    tm = hf_ref.shape[0]
    half = d // 2
    pb = [_dot(yb_ref[...], wbr_ref[1, :, c * half:(c + 1) * half]) for c in range(2)]
    hsum = hf_ref[...] + hb_ref[...]
    ya = jnp.concatenate(
        [_rms(hsum[:, h * HEAD_DIM:(h + 1) * HEAD_DIM]) for h in range(HEADS)], axis=1) * gmh_ref[...]
    ya = so_ref[...].astype(_F32) * ya
    cols = []
    for g in range(HEADS):
        gs = slice(BRANCH_WIDTH + g * HEAD_DIM, BRANCH_WIDTH + (g + 1) * HEAD_DIM)
        vn = (_rms(us_ref[:, gs].astype(_F32)) * gsgu_ref[:, g * HEAD_DIM:(g + 1) * HEAD_DIM]).astype(_BF16)
        mixed = [_dot(ws_ref[g], vn[n * CHUNK:(n + 1) * CHUNK, :]) + bs_ref[:, g:g + 1] for n in range(tm // CHUNK)]
        cols.append(jnp.concatenate(mixed, axis=0))
    yc = us_ref[:, 0:BRANCH_WIDTH].astype(_F32) * jnp.concatenate(cols, axis=1)
    ya16, yc16 = ya.astype(_BF16), yc.astype(_BF16)
    pa = [_dot(ya16, wbr_ref[0, :, c * half:(c + 1) * half]) for c in range(2)]
    pc = [_dot(yc16, wbr_ref[2, :, c * half:(c + 1) * half]) for c in range(2)]
    out = None
    for c in range(2):
        gate = lambda g: gt_ref[:, g * d + c * half:g * d + (c + 1) * half].astype(_F32)
        merged = (gate(0) * pa[c] + gate(1) * pb[c] + gate(2) * pc[c]).astype(_BF16)
        part = _dot(merged, wout_ref[c * half:(c + 1) * half, :])
        out = part if out is None else out + part
    return out


MERGE_TILE_INPUTS = 9


def _merge_kernel(*refs, d, nctx_tiles, off, tiles):
    per_tile = (refs[0:MERGE_TILE_INPUTS], refs[MERGE_TILE_INPUTS:2 * MERGE_TILE_INPUTS])
    gmh_ref, gsgu_ref, ws_ref, bs_ref, wbr_ref, wout_ref, gffn_ref, xo_ref, h2_ref = refs[2 * MERGE_TILE_INPUTS:]
    outs = [_merge_mix(*tile_refs[0:6], gmh_ref, gsgu_ref, ws_ref, bs_ref, wbr_ref, wout_ref, d)
            for tile_refs in per_tile]
    for h, tile_refs in enumerate(per_tile):
        xc_ref, xl_ref, mod_ref = tile_refs[6:9]
        rows = slice(h * ROW_TILE, (h + 1) * ROW_TILE)
        tile_in_batch = (2 * pl.program_id(0) + h) % tiles + off
        x = jnp.where(tile_in_batch < nctx_tiles, xc_ref[...], xl_ref[...])
        x_new = x + mod_ref[:, 2 * d:3 * d] * outs[h]
        xo_ref[rows, :] = x_new
        h2 = _rms(x_new) * gffn_ref[...]
        h2_ref[rows, :] = (h2 * (1.0 + mod_ref[:, 4 * d:5 * d]) + mod_ref[:, 3 * d:4 * d]).astype(h2_ref.dtype)


def _merge(hf, hb, vo, yb, us, gt, x_ctx, x_lat, lat_off, mod, g_mhead, g_sgu, w_s, b_s_t, w_branch, w_out, g_ffn,
           layer, nctx_tiles, skip_ctx):
    b, t, _ = hf.shape
    d = x_ctx.shape[2]
    off = nctx_tiles if skip_ctx else 0
    tiles = t // ROW_TILE - off
    assert (b * tiles) % 2 == 0

    def tile_specs(h):
        at = _pair_tile(h, tiles)
        tile = lambda w, blk: pl.BlockSpec((None, ROW_TILE, w), lambda n: (at(n)[0], at(n)[1] + off, blk))
        return [tile(BRANCH_WIDTH, 0), tile(BRANCH_WIDTH, 0), tile(BRANCH_WIDTH, 1), tile(BRANCH_WIDTH, 0),
                tile(2 * BRANCH_WIDTH, 0), tile(N_BRANCH * d, 0)] + _stream_tile_specs(at, off, d, nctx_tiles, lat_off,
                                                                                       mod, layer)

    tile_args = (hf, hb, vo, yb, us, gt, x_ctx, x_lat, mod)
    params = (g_mhead, g_sgu, w_s, b_s_t, w_branch, w_out, g_ffn)
    out_tile = lambda: pl.BlockSpec((2 * ROW_TILE, d), lambda n: (n, 0))
    rows_out = b * tiles * ROW_TILE
    x_mid, h2 = pl.pallas_call(
        functools.partial(_merge_kernel, d=d, nctx_tiles=nctx_tiles, off=off, tiles=tiles),
        grid=(b * tiles // 2,),
        in_specs=tile_specs(0) + tile_specs(1) + [_layer_spec(a.shape, layer) for a in params],
        out_specs=[out_tile(), out_tile()],
        out_shape=[jax.ShapeDtypeStruct((rows_out, d), _F32), jax.ShapeDtypeStruct((rows_out, d), _BF16)],
        compiler_params=_cparams(1),
        name="merge",
    )(*tile_args, *tile_args, *params)
    return x_mid.reshape(b, tiles * ROW_TILE, d), h2.reshape(b, tiles * ROW_TILE, d)


def _ffn_kernel(*refs, d, n_sub, groups, final_norm):
    h_refs, (hp_ref, hn_ref), x_refs = refs[0:n_sub], refs[n_sub:n_sub + 2], refs[n_sub + 2:2 * n_sub + 2]
    mod_ref, wup_ref, wcv_ref, bcv_ref, wdn_ref, gfin_ref, o_ref, ext_scr, act_scr = refs[2 * n_sub + 2:]
    tm = n_sub * ROW_TILE
    ff = wdn_ref.shape[0]
    ck = FFN_CHUNK
    group = pl.program_id(0) % groups
    ext_scr[0:FFN_HALO, :] = jnp.where(group == 0, jnp.zeros_like(hp_ref), hp_ref[...])
    for s, h_ref in enumerate(h_refs):
        ext_scr[FFN_HALO + s * ROW_TILE:FFN_HALO + (s + 1) * ROW_TILE, :] = h_ref[...]
    ext_scr[FFN_HALO + tm:, :] = jnp.where(group == groups - 1, jnp.zeros_like(hn_ref), hn_ref[...])
    ext_rows = tm + 2 * FFN_HALO
    inner = slice(FFN_HALO, FFN_HALO + tm)

    def conv(cols):
        a = _dot(ext_scr[...], wup_ref[:, cols])
        a_prev = pltpu.roll(a, 1, axis=0)[inner, :]
        a_next = pltpu.roll(a, ext_rows - 1, axis=0)[inner, :]
        return (bcv_ref[:, cols] + a_prev * wcv_ref[0:1, cols] + a[inner, :] * wcv_ref[1:2, cols]
                + a_next * wcv_ref[2:3, cols])

    for c in range(ff // ck):
        gate = conv(slice(c * ck, (c + 1) * ck))
        val = conv(slice(ff + c * ck, ff + (c + 1) * ck))
        act_scr[:, c * ck:(c + 1) * ck] = (_silu(gate) * val).astype(_BF16)

    down = _dot(act_scr[...], wdn_ref[...])
    for s, x_ref in enumerate(x_refs):
        rows = slice(s * ROW_TILE, (s + 1) * ROW_TILE)
        x_new = x_ref[...] + mod_ref[:, 5 * d:6 * d] * down[rows, :]
        if final_norm:
            x_new = _rms(x_new) * gfin_ref[...]
        o_ref[rows, :] = x_new


def _ffn(h2, x, mod, w_up, w_cv, b_cv, w_dn, g_final, layer, first_tile, seg_tiles, n_sub, is_ctx, final_norm):
    b, r, d = x.shape
    assert seg_tiles % n_sub == 0
    groups = seg_tiles // n_sub
    hb = ROW_TILE // FFN_HALO
    batch_row = mod.shape[1] - 8
    tile0 = lambda n: first_tile + (n % groups) * n_sub
    sub_specs = [pl.BlockSpec((None, ROW_TILE, d), lambda n, s=s: (n // groups, tile0(n) + s, 0)) for s in range(n_sub)]
    return pl.pallas_call(
        functools.partial(_ffn_kernel, d=d, n_sub=n_sub, groups=groups, final_norm=final_norm),
        grid=(b * groups,),
        in_specs=sub_specs
        + [pl.BlockSpec((None, FFN_HALO, d), lambda n: (n // groups, jnp.maximum(tile0(n) * hb - 1, 0), 0)),
           pl.BlockSpec((None, FFN_HALO, d),
                        lambda n: (n // groups, jnp.minimum((tile0(n) + n_sub) * hb, r // FFN_HALO - 1), 0))]
        + sub_specs
        + [pl.BlockSpec((None, None, 1, mod.shape[3]), lambda n: (layer, batch_row if is_ctx else n // groups, 0, 0))]
        + [_layer_spec(a.shape, layer) for a in (w_up, w_cv, b_cv, w_dn)] + [_const_spec((1, d))],
        out_specs=pl.BlockSpec((None, n_sub * ROW_TILE, d), lambda n: (n // groups, n % groups, 0)),
        out_shape=jax.ShapeDtypeStruct((b, seg_tiles * ROW_TILE, d), _F32),
        scratch_shapes=[pltpu.VMEM((n_sub * ROW_TILE + 2 * FFN_HALO, d), _BF16),
                        pltpu.VMEM((n_sub * ROW_TILE, w_dn.shape[1]), _BF16)],
        compiler_params=_cparams(1),
        name="ffn",
    )(*([h2] * (n_sub + 2)), *([x] * n_sub), mod, w_up, w_cv, b_cv, w_dn, g_final.reshape(1, d))


def _deinterleave(w):
    return jnp.concatenate([w[..., 0::2], w[..., 1::2]], axis=-1)


def _rotated(w):
    return jnp.concatenate([-w[..., 1::2], w[..., 0::2]], axis=-1)


def _layout_w_in(w):
    depth, d, _ = w.shape
    w = w.astype(_BF16)
    sizes = (BRANCH_WIDTH,) * 4 + (M_GATES, Q_LORA, KV_LORA, ROPE_DIM, BRANCH_WIDTH, BRANCH_WIDTH, N_BRANCH * d)
    splits = tuple(int(s) for s in np.cumsum(sizes)[:-1])
    q, k, v, o, mg, cq, ckv, kr, u, s, gt = jnp.split(w, splits, axis=2)
    mg = mg.reshape(depth, d, 2, 2, HEADS)
    pad = jnp.zeros((depth, d, 128 - 2 * HEADS), w.dtype)
    cols = [q, k, v, o, u, s, gt, cq, ckv, _deinterleave(kr), _rotated(kr),
            mg[:, :, :, 0, :].reshape(depth, d, 2 * HEADS), pad, mg[:, :, :, 1, :].reshape(depth, d, 2 * HEADS), pad]
    return jnp.concatenate(cols, axis=2)


def _layout_gate_bias(bg):
    depth = bg.shape[0]
    bg = bg.reshape(depth, 2, 2, HEADS)
    pad = jnp.zeros((depth, 128 - 2 * HEADS), bg.dtype)
    return jnp.concatenate([bg[:, :, 0, :].reshape(depth, -1), pad, bg[:, :, 1, :].reshape(depth, -1), pad],
                           axis=1).reshape(depth, 1, GATE_W)


def _layout_w_uq(w):
    depth = w.shape[0]
    w = w.reshape(depth, Q_LORA, HEADS, HEAD_DIM + ROPE_DIM)
    nope, rope = w[..., :HEAD_DIM], w[..., HEAD_DIM:]
    return jnp.concatenate([nope, _deinterleave(rope), _rotated(rope)], axis=-1).reshape(depth, Q_LORA, -1).astype(_BF16)


def _layout_w_ukv(w):
    depth = w.shape[0]
    w = w.reshape(depth, KV_LORA, HEADS, 2 * HEAD_DIM).astype(_BF16)
    return w[..., :HEAD_DIM].reshape(depth, KV_LORA, -1), w[..., HEAD_DIM:].reshape(depth, KV_LORA, -1)


def _rope_tables(ctx_len, n_latent):
    rows = n_latent // GRID_W
    row = jnp.repeat(jnp.arange(rows), GRID_W)
    col = jnp.tile(jnp.arange(GRID_W), rows)
    n_freq = ROPE_DIM // 4
    inv = ROPE_BASE ** (-jnp.arange(n_freq, dtype=_F32) / n_freq)
    ang = jnp.concatenate([row[:, None] * inv, col[:, None] * inv], axis=-1)
    zeros = jnp.zeros((n_latent, 128 - ROPE_DIM), _F32)
    cos_l = jnp.concatenate([jnp.cos(ang), jnp.cos(ang), zeros], axis=1)
    sin_l = jnp.concatenate([jnp.sin(ang), jnp.sin(ang), zeros], axis=1)
    cos_c = jnp.concatenate([jnp.ones((ctx_len, ROPE_DIM), _F32), jnp.zeros((ctx_len, 128 - ROPE_DIM), _F32)], axis=1)
    return jnp.concatenate([cos_c, cos_l], axis=0), jnp.concatenate([jnp.zeros_like(cos_c), sin_l], axis=0)


def kernel(x, c, ctx, c_ctx, w_ada, b_ada, g_mix, w_in, w_qkconv, b_qkconv, b_mgate, g_mhead, g_qnorm, w_uq,
           g_kvnorm, w_ukv, g_sgu, w_s, b_s, w_branch, w_out, g_ffn, w_up, w_ffconv, b_ffconv, w_down, g_final):
    b, s, d = x.shape
    ctx_len = ctx.shape[1]
    depth = w_in.shape[0]
    assert ctx_len % ROW_TILE == 0 and s % ROW_TILE == 0 and s % GRID_W == 0
    nctx_tiles = ctx_len // ROW_TILE
    row_param = lambda a: a.reshape(depth, 1, a.shape[-1])

    cos_t, sin_t = _rope_tables(ctx_len, s)
    cond_rows = jnp.concatenate([c, c_ctx[None, :], jnp.zeros((7, d), c.dtype)], axis=0)
    mod = _ada(cond_rows, w_ada, b_ada)
    w_in_l = _layout_w_in(w_in)
    b_gate_l = _layout_gate_bias(b_mgate)
    w_uq_l = _layout_w_uq(w_uq)
    w_k_l, w_v_l = _layout_w_ukv(w_ukv)
    b_s_t = jnp.pad(jnp.swapaxes(b_s, 1, 2), ((0, 0), (0, 0), (0, 128 - HEADS)))
    w_s16, w_branch16, w_out16 = w_s.astype(_BF16), w_branch.astype(_BF16), w_out.astype(_BF16)
    w_up16, w_down16 = w_up.astype(_BF16), w_down.astype(_BF16)
    g_mix_r, g_mhead_r, g_sgu_r, g_ffn_r = row_param(g_mix), row_param(g_mhead), row_param(g_sgu), row_param(g_ffn)
    g_qn_r, g_kvn_r, b_qkconv_r, b_ffconv_r = row_param(g_qnorm), row_param(g_kvnorm), row_param(b_qkconv), row_param(b_ffconv)

    x_ctx, x_lat, lat_off = ctx, x, 0
    for l in range(depth):
        last = l == depth - 1
        qk, vo, us, gt, at, gates = _proj(x_ctx, x_lat, lat_off, mod, g_mix_r, w_in_l, l, nctx_tiles)
        q_act, kt_act = _qkconv(qk, w_qkconv, b_qkconv_r, l, (0, ctx_len))
        hf, hb = _mlstm(q_act, kt_act, vo, gates, b_gate_l, l, ctx_len // CHUNK)
        yb = _attn(at, cos_t, sin_t, g_qn_r, w_uq_l, g_kvn_r, w_k_l, w_v_l, l, ctx_len, not last)
        x_mid, h2 = _merge(hf, hb, vo, yb, us, gt, x_ctx, x_lat, lat_off, mod, g_mhead_r, g_sgu_r, w_s16, b_s_t,
                           w_branch16, w_out16, g_ffn_r, l, nctx_tiles, last)
        ffn = functools.partial(_ffn, h2, x_mid, mod, w_up16, w_ffconv, b_ffconv_r, w_down16, g_final, l)
        if last:
            return ffn(0, s // ROW_TILE, 2, False, True)
        x_lat = ffn(nctx_tiles, s // ROW_TILE, 2, False, False)
        x_ctx = ffn(0, nctx_tiles, 1, True, False)
```

```python
import functools

import jax
import jax.numpy as jnp
import numpy as np
from jax import lax
from jax.experimental import pallas as pl
from jax.experimental.pallas import tpu as pltpu

EPS = 1e-6
GRID_W = 64
ROPE_BASE = 10000.0
LOG2_E = 1.4426950408889634

HEADS = 4
HEAD_DIM = 128
CHUNK = 128
BRANCH_WIDTH = HEADS * HEAD_DIM
Q_LORA = 384
KV_LORA = 256
ROPE_DIM = 64
N_BRANCH = 3
M_GATES = 4 * HEADS
KV_SRC = KV_LORA + 2 * ROPE_DIM
ATTN_W = Q_LORA + KV_SRC
GATE_W = 256

ROW_TILE = 256
FFN_HALO = 16
FFN_CHUNK = 256
V7X_VMEM_LIMIT = 56 * 1024 * 1024

_BF16 = jnp.bfloat16
_F32 = jnp.float32


def _cparams(n_axes, flags=None):
    return pltpu.CompilerParams(dimension_semantics=("arbitrary",) * n_axes, vmem_limit_bytes=V7X_VMEM_LIMIT,
                                flags=flags)


def _const_spec(shape):
    return pl.BlockSpec(tuple(shape), lambda *_: (0,) * len(shape))


def _layer_spec(stacked_shape, layer):
    shape = tuple(stacked_shape[1:])
    return pl.BlockSpec((None,) + shape, lambda *_: (layer,) + (0,) * len(shape))


def _mod_spec(mod, layer, nctx_tiles, off):
    batch = mod.shape[1] - 8
    return pl.BlockSpec((None, None, 1, mod.shape[3]),
                        lambda bi, i: (layer, jnp.where(i + off < nctx_tiles, batch, bi), 0, 0))


def _row_specs(width, nctx_tiles, lat_off, off=0):
    ctx = pl.BlockSpec((None, ROW_TILE, width), lambda bi, i: (bi, jnp.minimum(i + off, nctx_tiles - 1), 0))
    lat = pl.BlockSpec((None, ROW_TILE, width), lambda bi, i: (bi, jnp.maximum(i + off - nctx_tiles, 0) + lat_off, 0))
    return [ctx, lat]


def _sigmoid(x):
    return 1.0 / (1.0 + jnp.exp(-x))


def _silu(x):
    return x * _sigmoid(x)


def _gelu_tanh(x):
    return x * (0.5 * (1.0 + jnp.tanh(0.7978845608028654 * (x + 0.044715 * (x * x * x)))))


def _rms(x):
    return x * lax.rsqrt(jnp.mean(x * x, axis=-1, keepdims=True) + EPS)


def _dot(a, b):
    return jnp.dot(a, b, preferred_element_type=_F32)


def _dot_nt(a, b):
    return lax.dot_general(a, b, (((1,), (1,)), ((), ())), preferred_element_type=_F32)


def _segment_edges(i, seg_tiles, tiles_total):
    first = functools.reduce(jnp.logical_or, [i == s for s in seg_tiles])
    last = functools.reduce(jnp.logical_or, [i == s - 1 for s in seg_tiles[1:] + (tiles_total,)])
    return first, last


def _ada_kernel(c_ref, w_ref, b_ref, o_ref):
    cond = _silu(c_ref[...])
    o_ref[...] = jnp.dot(cond, w_ref[...], precision=lax.Precision.HIGHEST, preferred_element_type=_F32) + b_ref[...]


def _ada(cond_rows, w_ada, b_ada):
    rows, d = cond_rows.shape
    depth, _, n = w_ada.shape
    out = pl.pallas_call(
        _ada_kernel,
        grid=(depth, n // d),
        in_specs=[pl.BlockSpec((rows, d), lambda l, j: (0, 0)),
                  pl.BlockSpec((None, d, d), lambda l, j: (l, 0, j)),
                  pl.BlockSpec((None, 1, d), lambda l, j: (l, 0, j))],
        out_specs=pl.BlockSpec((None, rows, d), lambda l, j: (l, 0, j)),
        out_shape=jax.ShapeDtypeStruct((depth, rows, n), _F32),
        compiler_params=_cparams(2),
        name="ada",
    )(cond_rows, w_ada, b_ada.reshape(depth, 1, n))
    return out.reshape(depth, rows, 1, n)


def _proj_kernel(xc0_ref, xl0_ref, mod0_ref, xc1_ref, xl1_ref, mod1_ref, g_ref, w_ref,
                 qk_ref, vo_ref, us_ref, gt_ref, at_ref, gate_ref, *, d, nctx_tiles, tiles):
    bw = BRANCH_WIDTH
    col_at = 6 * bw + N_BRANCH * d
    normed = []
    for h, (xc_ref, xl_ref, mod_ref) in enumerate(((xc0_ref, xl0_ref, mod0_ref), (xc1_ref, xl1_ref, mod1_ref))):
        x = jnp.where((2 * pl.program_id(0) + h) % tiles < nctx_tiles, xc_ref[...], xl_ref[...])
        normed.append((_rms(x) * g_ref[...] * (1.0 + mod_ref[:, d:2 * d]) + mod_ref[:, 0:d]).astype(_BF16))
    for h, hn in enumerate(normed):
        rows = slice(h * ROW_TILE, (h + 1) * ROW_TILE)
        us_ref[rows, :] = _gelu_tanh(_dot(hn, w_ref[:, 4 * bw:6 * bw])).astype(_BF16)
        for g in range(N_BRANCH):
            cols = slice(6 * bw + g * d, 6 * bw + (g + 1) * d)
            gt_ref[rows, g * d:(g + 1) * d] = _sigmoid(_dot(hn, w_ref[:, cols])).astype(_BF16)
        vo_ref[rows, bw:2 * bw] = _sigmoid(_dot(hn, w_ref[:, 3 * bw:4 * bw])).astype(_BF16)
        vo_ref[rows, 0:bw] = _dot(hn, w_ref[:, 2 * bw:3 * bw]).astype(_BF16)
        qk_ref[rows, :] = _dot(hn, w_ref[:, 0:2 * bw])
        at_ref[rows, :] = _dot(hn, w_ref[:, col_at:col_at + ATTN_W]).astype(_BF16)
        gate_ref[rows, :] = _dot(hn, w_ref[:, col_at + ATTN_W:col_at + ATTN_W + GATE_W])


def _pair_tile(h, tiles):
    def index(n):
        k = 2 * n + h
        return k // tiles, k % tiles
    return index


def _two_source_specs(at, off, width, nctx_tiles):
    return [pl.BlockSpec((None, ROW_TILE, width), lambda n: (at(n)[0], jnp.minimum(at(n)[1] + off, nctx_tiles - 1), 0)),
            pl.BlockSpec((None, ROW_TILE, width), lambda n: (at(n)[0], jnp.maximum(at(n)[1] + off - nctx_tiles, 0), 0))]


def _stream_tile_specs(at, off, d, nctx_tiles, mod, layer):
    batch_row = mod.shape[1] - 8
    return _two_source_specs(at, off, d, nctx_tiles) + [
        pl.BlockSpec((None, None, 1, mod.shape[3]),
                     lambda n: (layer, jnp.where(at(n)[1] + off < nctx_tiles, batch_row, at(n)[0]), 0, 0))]


def _proj(x_ctx, x_lat, mod, g_mix, w_in, layer, nctx_tiles):
    b, _, d = x_ctx.shape
    tiles = nctx_tiles + x_lat.shape[1] // ROW_TILE
    bw = BRANCH_WIDTH
    widths = (2 * bw, 2 * bw, 2 * bw, N_BRANCH * d, ATTN_W, GATE_W)
    dtypes = (_F32, _BF16, _BF16, _BF16, _BF16, _F32)
    assert sum(widths) == w_in.shape[2] and (b * tiles) % 2 == 0
    rows = b * tiles * ROW_TILE
    tile_args = (x_ctx, x_lat, mod)
    outs = pl.pallas_call(
        functools.partial(_proj_kernel, d=d, nctx_tiles=nctx_tiles, tiles=tiles),
        grid=(b * tiles // 2,),
        in_specs=_stream_tile_specs(_pair_tile(0, tiles), 0, d, nctx_tiles, mod, layer)
        + _stream_tile_specs(_pair_tile(1, tiles), 0, d, nctx_tiles, mod, layer)
        + [_layer_spec(g_mix.shape, layer), _layer_spec(w_in.shape, layer)],
        out_specs=[pl.BlockSpec((2 * ROW_TILE, w), lambda n: (n, 0)) for w in widths],
        out_shape=[jax.ShapeDtypeStruct((rows, w), dt) for w, dt in zip(widths, dtypes)],
        compiler_params=_cparams(1),
        name="proj",
    )(*tile_args, *tile_args, g_mix, w_in)
    return [o.reshape(b, tiles * ROW_TILE, o.shape[1]) for o in outs]


def _qkconv_kernel(x_ref, hp_ref, hn_ref, w_ref, b_ref, q_ref, kt_ref, *, seg_tiles, tiles_total):
    tm = x_ref.shape[0]
    seg_first, seg_last = _segment_edges(pl.program_id(1), seg_tiles, tiles_total)
    row = lax.broadcasted_iota(jnp.int32, (tm, 1), 0)
    x = x_ref[...]
    prev_row = jnp.where(seg_first, 0.0, hp_ref[7:8, :])
    next_row = jnp.where(seg_last, 0.0, hn_ref[0:1, :])
    x_prev = jnp.where(row == 0, prev_row, pltpu.roll(x, 1, axis=0))
    x_next = jnp.where(row == tm - 1, next_row, pltpu.roll(x, tm - 1, axis=0))
    a = _silu(b_ref[...] + x_prev * w_ref[0:1, :] + x * w_ref[1:2, :] + x_next * w_ref[2:3, :])
    q_ref[...] = a[:, 0:BRANCH_WIDTH].astype(q_ref.dtype)
    for h in range(HEADS):
        k = a[:, BRANCH_WIDTH + h * HEAD_DIM:BRANCH_WIDTH + (h + 1) * HEAD_DIM] * (HEAD_DIM ** -0.5)
        kt_ref[h] = k.T.astype(kt_ref.dtype)


def _qkconv(qk, w_qkconv, b_qkconv, layer, seg_rows):
    b, t, qk_w = qk.shape
    tiles = t // ROW_TILE
    hb = ROW_TILE // 8
    return pl.pallas_call(
        functools.partial(_qkconv_kernel, seg_tiles=tuple(s // ROW_TILE for s in seg_rows), tiles_total=tiles),
        grid=(b, tiles),
        in_specs=[pl.BlockSpec((None, ROW_TILE, qk_w), lambda bi, i: (bi, i, 0)),
                  pl.BlockSpec((None, 8, qk_w), lambda bi, i: (bi, jnp.maximum(i * hb - 1, 0), 0)),
                  pl.BlockSpec((None, 8, qk_w), lambda bi, i: (bi, jnp.minimum((i + 1) * hb, t // 8 - 1), 0)),
                  _layer_spec(w_qkconv.shape, layer), _layer_spec(b_qkconv.shape, layer)],
        out_specs=[pl.BlockSpec((None, ROW_TILE, BRANCH_WIDTH), lambda bi, i: (bi, i, 0)),
                   pl.BlockSpec((None, HEADS, HEAD_DIM, ROW_TILE), lambda bi, i: (bi, 0, 0, i))],
        out_shape=[jax.ShapeDtypeStruct((b, t, BRANCH_WIDTH), _BF16),
                   jax.ShapeDtypeStruct((b, HEADS, HEAD_DIM, t), _BF16)],
        compiler_params=_cparams(2),
        name="qkconv",
    )(qk, qk, qk, w_qkconv, b_qkconv)


def _mlstm_kernel(q_f, kt_f, v_f, gf_cur, gf_nxt, q_b, kt_b, v_b, gb_cur, gb_nxt, bg_ref, hf_ref, hb_ref,
                  c_scr, m_row, m_col, *pre):
    tc = CHUNK
    low, high = pl.ds(0, tc), pl.ds(tc, tc)

    def scan_step(slot, fwd_rows, bwd_rows):
        views = [(q.at[rows, :], kt.at[:, :, rows], v.at[rows, :], out.at[rows, :])
                 for rows, (q, kt, v, out) in ((fwd_rows, (q_f, kt_f, v_f, hf_ref)), (bwd_rows, (q_b, kt_b, v_b, hb_ref)))]
        _mlstm_scan_step(slot, is_fwd, causal, views[0], views[1], c_scr, m_row, m_col)

    pre_a, pre_b = pre[:len(pre) // 2], pre[len(pre) // 2:]
    lane = lax.broadcasted_iota(jnp.int32, (1, 128), 1)
    row = lax.broadcasted_iota(jnp.int32, (tc, 1), 0)
    is_fwd = lane < HEADS
    r_idx = lax.broadcasted_iota(jnp.int32, (tc, tc), 0)
    c_idx = lax.broadcasted_iota(jnp.int32, (tc, tc), 1)
    causal = (c_idx <= r_idx, c_idx >= r_idx)

    def gate_prologue(g_f, g_b, slot):
        bcum_ref, run_ref, rt_ref, dt_ref, rows_ref, blc_ref, dmc_ref = slot
        log_i = jnp.where(is_fwd, g_f[:, 0:128], g_b[:, 0:128]) + bg_ref[:, 0:128]
        gates_f = jnp.where(is_fwd, g_f[:, 128:256], g_b[:, 128:256]) + bg_ref[:, 128:256]
        log_f = jnp.minimum(gates_f, 0.0) - jnp.log(1.0 + jnp.exp(-jnp.abs(gates_f)))
        cum_f, step = log_f, 1
        while step < tc:
            cum_f = cum_f + jnp.where(row >= step, pltpu.roll(cum_f, step, axis=0), 0.0)
            step *= 2
        b_last = cum_f[tc - 1:tc, :]
        bcum = jnp.where(is_fwd, cum_f, b_last - cum_f + log_f)
        r = log_i - bcum
        run_f, run_b, step = r, r, 1
        while step < tc:
            run_f = jnp.maximum(run_f, jnp.where(row >= step, pltpu.roll(run_f, step, axis=0), -jnp.inf))
            run_b = jnp.maximum(run_b, jnp.where(row < tc - step, pltpu.roll(run_b, tc - step, axis=0), -jnp.inf))
            step *= 2
        decay = b_last + r
        r_t = r.T[0:8, :]
        decay_t = decay.T[0:8, :]
        bcum_ref[...] = bcum
        run_ref[...] = jnp.where(is_fwd, run_f, run_b)
        rt_ref[...] = r_t
        dt_ref[...] = decay_t
        rows_ref[0:1, :] = b_last
        rows_ref[1:2, :] = jnp.max(decay, axis=0, keepdims=True)
        blc_ref[...] = decay_t - r_t
        dmc_ref[...] = jnp.broadcast_to(jnp.max(decay_t, axis=1, keepdims=True), (8, tc))

    @pl.when(pl.program_id(1) == 0)
    def _():
        c_scr[...] = jnp.zeros_like(c_scr)
        m_row[...] = jnp.zeros_like(m_row)
        m_col[...] = jnp.zeros_like(m_col)
        gate_prologue(gf_cur.at[low, :], gb_cur.at[high, :], pre_a)

    gate_prologue(gf_cur.at[high, :], gb_cur.at[low, :], pre_b)
    scan_step(pre_a, low, high)
    gate_prologue(gf_nxt.at[low, :], gb_nxt.at[high, :], pre_a)
    scan_step(pre_b, high, low)


def _mlstm_scan_step(slot, is_fwd, causal, refs_f, refs_b, c_scr, m_row, m_col):
    tc = CHUNK
    bcum_ref, run_ref, rt_ref, dt_ref, rows_ref, blc_ref, dmc_ref = slot
    bcum = bcum_ref[...]
    b_last, decay_max = rows_ref[0:1, :], rows_ref[1:2, :]
    m_old = m_row[...]
    g = jnp.maximum(m_old, run_ref[...])
    m_new = jnp.maximum(b_last + m_old, decay_max)
    w_carry = jnp.exp(b_last + m_old - m_new)
    m_col_new = jnp.maximum(blc_ref[...] + m_col[...], dmc_ref[...])
    w_src_t = jnp.exp(dt_ref[...] - m_col_new)
    r_t = rt_ref[...]
    ones = jnp.ones((tc, HEAD_DIM), _BF16)

    scans = [(dr * HEADS + h, refs, h, slice(h * HEAD_DIM, (h + 1) * HEAD_DIM), causal[dr])
             for dr, refs in enumerate((refs_f, refs_b)) for h in range(HEADS)]
    v_ext = [jnp.concatenate([refs[2][:, hs], ones], axis=1) for _, refs, _, hs, _ in scans]
    s16, g_b = [], []
    for l, (q_ref, kt_ref, _, _), h, hs, mask in scans:
        g_l = jnp.broadcast_to(g[:, l:l + 1], (tc, tc))
        w_intra = jnp.exp(jnp.where(mask, r_t[l:l + 1, :] - g_l, -jnp.inf))
        s16.append((_dot(q_ref[:, hs], kt_ref[h]) * w_intra).astype(_BF16))
        g_b.append(g_l)
    qc = []
    for l, (q_ref, kt_ref, _, _), h, hs, _ in scans:
        c_old = c_scr[l]
        qc.append(_dot(q_ref[:, hs], c_old.astype(_BF16)))
        kw_t = (kt_ref[h].astype(_F32) * w_src_t[l:l + 1, :]).astype(_BF16)
        c_scr[l] = w_carry[:, l:l + 1] * c_old + _dot(kw_t, v_ext[l])
    for l, (_, _, _, out_ref), h, hs, _ in scans:
        w_inter = jnp.exp(m_old[:, l:l + 1] - g_b[l])
        b_l = jnp.broadcast_to(bcum[:, l:l + 1], (tc, tc))
        sv = _dot(s16[l], v_ext[l])
        num = w_inter * qc[l][:, 0:HEAD_DIM] + sv[:, 0:HEAD_DIM]
        den = w_inter * qc[l][:, HEAD_DIM:] + sv[:, HEAD_DIM:]
        out_ref[:, hs] = num / jnp.maximum(jnp.abs(den), jnp.exp(-(b_l + g_b[l])))
    m_row[...] = m_new
    m_col[...] = m_col_new


def _mlstm(q, kt, vo, gates, b_gate, layer, nctx):
    b, t, _ = q.shape
    blk = 2 * CHUNK
    nblocks, nctx_blocks = t // blk, nctx // 2
    assert t % blk == 0 and nctx % 2 == 0

    def bwd_block(j):
        return jnp.where(j < nctx_blocks, nctx_blocks - 1 - j, nblocks - 1 - (j - nctx_blocks))

    def specs(block_of):
        nxt = lambda j: block_of(jnp.minimum(j + 1, nblocks - 1))
        return [
            pl.BlockSpec((None, blk, BRANCH_WIDTH), lambda bi, j: (bi, block_of(j), 0)),
            pl.BlockSpec((None, HEADS, HEAD_DIM, blk), lambda bi, j: (bi, 0, 0, block_of(j))),
            pl.BlockSpec((None, blk, BRANCH_WIDTH), lambda bi, j: (bi, block_of(j), 0)),
            pl.BlockSpec((None, blk, GATE_W), lambda bi, j: (bi, block_of(j), 0)),
            pl.BlockSpec((None, blk, GATE_W), lambda bi, j: (bi, nxt(j), 0)),
        ]

    gate_slot = [pltpu.VMEM((CHUNK, 128), _F32), pltpu.VMEM((CHUNK, 128), _F32), pltpu.VMEM((8, CHUNK), _F32),
                 pltpu.VMEM((8, CHUNK), _F32), pltpu.VMEM((8, 128), _F32), pltpu.VMEM((8, CHUNK), _F32),
                 pltpu.VMEM((8, CHUNK), _F32)]
    h_shape = jax.ShapeDtypeStruct((b, t, BRANCH_WIDTH), _F32)
    return pl.pallas_call(
        _mlstm_kernel,
        grid=(b, nblocks),
        in_specs=specs(lambda j: j) + specs(bwd_block) + [_layer_spec(b_gate.shape, layer)],
        out_specs=[pl.BlockSpec((None, blk, BRANCH_WIDTH), lambda bi, j: (bi, j, 0)),
                   pl.BlockSpec((None, blk, BRANCH_WIDTH), lambda bi, j: (bi, bwd_block(j), 0))],
        out_shape=[h_shape, h_shape],
        scratch_shapes=[pltpu.VMEM((2 * HEADS, HEAD_DIM, 2 * HEAD_DIM), _F32), pltpu.VMEM((1, 128), _F32),
                        pltpu.VMEM((8, CHUNK), _F32)] + gate_slot + gate_slot,
        compiler_params=_cparams(2),
        name="mlstm",
    )(q, kt, vo, gates, gates, q, kt, vo, gates, gates, b_gate)


ATTN_SUB = 2
VT_ONES = 16


def _attn_kernel(cq_ctx_ref, cq_a_ref, cq_b_ref, kv_ref, ca_ref, sa_ref, gq_ref, wq_ref, gkv_ref, wk_ref, wv_ref,
                 o_ctx_ref, o_lat_ref, k_scr, vt_scr, *, ctx_len, need_ctx, scale):
    i = pl.program_id(1)
    t = kv_ref.shape[0]
    head_w = 2 * HEAD_DIM

    def rope(y, cos_t, sin_t):
        return y * cos_t + pltpu.roll(y, ROPE_DIM, axis=1) * sin_t

    def attend(cq_rows, row0, n_keys, o_ref):
        table_rows = pl.ds(pl.multiple_of(row0, ROW_TILE), cq_rows.shape[0])
        cos_q, sin_q = ca_ref[table_rows, :], sa_ref[table_rows, :]
        cq = (_rms(cq_rows.astype(_F32)) * gq_ref[...]).astype(_BF16)
        qa = _dot(cq, wq_ref[...]) * (scale * LOG2_E)

        def scores(h):
            q_nope = qa[:, h * head_w:h * head_w + HEAD_DIM]
            q_rope = rope(qa[:, h * head_w + HEAD_DIM:(h + 1) * head_w], cos_q, sin_q)
            qh = jnp.concatenate([q_nope, q_rope], axis=1).astype(_BF16)
            return _dot_nt(qh, k_scr[0:n_keys, h * head_w:(h + 1) * head_w])

        s_next = scores(0)
        for h in range(HEADS):
            s = s_next
            if h + 1 < HEADS:
                s_next = scores(h + 1)
            e = jnp.exp2(s - jnp.max(s, axis=1, keepdims=True)).astype(_BF16)
            ot = _dot_nt(vt_scr[h, :, 0:n_keys], e)
            ot = ot[0:HEAD_DIM, :] / ot[HEAD_DIM:HEAD_DIM + 1, :]
            o_ref[:, h * HEAD_DIM:(h + 1) * HEAD_DIM] = ot.T.astype(o_ref.dtype)

    @pl.when(i == 0)
    def _():
        for r in range(t // ROW_TILE):
            rows = slice(r * ROW_TILE, (r + 1) * ROW_TILE)
            ckv = (_rms(kv_ref[rows, 0:KV_LORA].astype(_F32)) * gkv_ref[...]).astype(_BF16)
            k_nope = _dot(ckv, wk_ref[...])
            k_rope = rope(kv_ref[rows, KV_LORA:KV_SRC].astype(_F32), ca_ref[rows, :], sa_ref[rows, :]).astype(_BF16)
            for h in range(HEADS):
                k_scr[rows, h * head_w:h * head_w + HEAD_DIM] = k_nope[:, h * HEAD_DIM:(h + 1) * HEAD_DIM].astype(_BF16)
                k_scr[rows, h * head_w + HEAD_DIM:(h + 1) * head_w] = k_rope
            vv = _dot(ckv, wv_ref[...])
            for h in range(HEADS):
                vt_scr[h, 0:HEAD_DIM, rows] = vv[:, h * HEAD_DIM:(h + 1) * HEAD_DIM].T.astype(_BF16)
                vt_scr[h, HEAD_DIM:, rows] = jnp.ones((VT_ONES, ROW_TILE), _BF16)
        if need_ctx:
            attend(cq_ctx_ref[...], 0, ctx_len, o_ctx_ref)
        else:
            o_ctx_ref[...] = jnp.zeros_like(o_ctx_ref)

    @pl.when(i > 0)
    def _():
        cq_rows = jnp.concatenate([cq_a_ref[...], cq_b_ref[...]], axis=0)
        attend(cq_rows, ctx_len + (i - 1) * (ATTN_SUB * ROW_TILE), t, o_lat_ref)


def _attn(at, cos_t, sin_t, g_qn, w_q, g_kvn, w_k, w_v, layer, ctx_len, need_ctx):
    b, t, _ = at.shape
    assert ctx_len == ROW_TILE and (t - ctx_len) % (ATTN_SUB * ROW_TILE) == 0
    lat_steps = (t - ctx_len) // (ATTN_SUB * ROW_TILE)
    scale = (HEAD_DIM + ROPE_DIM) ** -0.5
    lat_tile = lambda s: pl.BlockSpec((None, ROW_TILE, Q_LORA),
                                      lambda bi, i: (bi, 1 + ATTN_SUB * jnp.maximum(i - 1, 0) + s, 0))
    return pl.pallas_call(
        functools.partial(_attn_kernel, ctx_len=ctx_len, need_ctx=need_ctx, scale=scale),
        grid=(b, 1 + lat_steps),
        in_specs=[pl.BlockSpec((None, ROW_TILE, Q_LORA), lambda bi, i: (bi, 0, 0)), lat_tile(0), lat_tile(1),
                  pl.BlockSpec((None, t, KV_SRC), lambda bi, i: (bi, 0, 1)),
                  _const_spec((t, 128)), _const_spec((t, 128)),
                  _layer_spec(g_qn.shape, layer), _layer_spec(w_q.shape, layer), _layer_spec(g_kvn.shape, layer),
                  _layer_spec(w_k.shape, layer), _layer_spec(w_v.shape, layer)],
        out_specs=[pl.BlockSpec((None, ROW_TILE, BRANCH_WIDTH), lambda bi, i: (bi, 0, 0)),
                   pl.BlockSpec((None, ATTN_SUB * ROW_TILE, BRANCH_WIDTH), lambda bi, i: (bi, jnp.maximum(i - 1, 0), 0))],
        out_shape=[jax.ShapeDtypeStruct((b, ctx_len, BRANCH_WIDTH), _BF16),
                   jax.ShapeDtypeStruct((b, t - ctx_len, BRANCH_WIDTH), _BF16)],
        scratch_shapes=[pltpu.VMEM((t, HEADS * 2 * HEAD_DIM), _BF16),
                        pltpu.VMEM((HEADS, HEAD_DIM + VT_ONES, t), _BF16)],
        compiler_params=_cparams(2),
        name="attn",
    )(at, at, at, at, cos_t, sin_t, g_qn, w_q, g_kvn, w_k, w_v)


def _merge_mix(hf_ref, hb_ref, so_ref, yb, us_ref, gt_ref, gmh_ref, gsgu_ref, ws_ref, bs_ref, wbr_ref, wout_ref, d):
    tm = hf_ref.shape[0]
    half = d // 2
    pb = [_dot(yb, wbr_ref[1, :, c * half:(c + 1) * half]) for c in range(2)]
    hsum = hf_ref[...] + hb_ref[...]
    ya = jnp.concatenate(
        [_rms(hsum[:, h * HEAD_DIM:(h + 1) * HEAD_DIM]) for h in range(HEADS)], axis=1) * gmh_ref[...]
    ya = so_ref[...].astype(_F32) * ya
    cols = []
    for g in range(HEADS):
        gs = slice(BRANCH_WIDTH + g * HEAD_DIM, BRANCH_WIDTH + (g + 1) * HEAD_DIM)
        vn = (_rms(us_ref[:, gs].astype(_F32)) * gsgu_ref[:, g * HEAD_DIM:(g + 1) * HEAD_DIM]).astype(_BF16)
        mixed = [_dot(ws_ref[g], vn[n * CHUNK:(n + 1) * CHUNK, :]) + bs_ref[:, g:g + 1] for n in range(tm // CHUNK)]
        cols.append(jnp.concatenate(mixed, axis=0))
    yc = us_ref[:, 0:BRANCH_WIDTH].astype(_F32) * jnp.concatenate(cols, axis=1)
    ya16, yc16 = ya.astype(_BF16), yc.astype(_BF16)
    pa = [_dot(ya16, wbr_ref[0, :, c * half:(c + 1) * half]) for c in range(2)]
    pc = [_dot(yc16, wbr_ref[2, :, c * half:(c + 1) * half]) for c in range(2)]
    out = None
    for c in range(2):
        gate = lambda g: gt_ref[:, g * d + c * half:g * d + (c + 1) * half].astype(_F32)
        merged = (gate(0) * pa[c] + gate(1) * pb[c] + gate(2) * pc[c]).astype(_BF16)
        part = _dot(merged, wout_ref[c * half:(c + 1) * half, :])
        out = part if out is None else out + part
    return out


MERGE_TILE_INPUTS = 10


def _merge_kernel(*refs, d, nctx_tiles, off, tiles):
    per_tile = (refs[0:MERGE_TILE_INPUTS], refs[MERGE_TILE_INPUTS:2 * MERGE_TILE_INPUTS])
    gmh_ref, gsgu_ref, ws_ref, bs_ref, wbr_ref, wout_ref, gffn_ref, xo_ref, h2_ref = refs[2 * MERGE_TILE_INPUTS:]
    is_ctx = [(2 * pl.program_id(0) + h) % tiles + off < nctx_tiles for h in range(2)]
    outs = []
    for h, (hf_ref, hb_ref, so_ref, us_ref, gt_ref, ybc_ref, ybl_ref) in enumerate(t[0:7] for t in per_tile):
        yb = jnp.where(is_ctx[h], ybc_ref[...], ybl_ref[...])
        outs.append(_merge_mix(hf_ref, hb_ref, so_ref, yb, us_ref, gt_ref, gmh_ref, gsgu_ref, ws_ref, bs_ref, wbr_ref,
                               wout_ref, d))
    for h, tile_refs in enumerate(per_tile):
        xc_ref, xl_ref, mod_ref = tile_refs[7:10]
        rows = slice(h * ROW_TILE, (h + 1) * ROW_TILE)
        x = jnp.where(is_ctx[h], xc_ref[...], xl_ref[...])
        x_new = x + mod_ref[:, 2 * d:3 * d] * outs[h]
        xo_ref[rows, :] = x_new
        h2 = _rms(x_new) * gffn_ref[...]
        h2_ref[rows, :] = (h2 * (1.0 + mod_ref[:, 4 * d:5 * d]) + mod_ref[:, 3 * d:4 * d]).astype(h2_ref.dtype)


def _merge(hf, hb, vo, yb_ctx, yb_lat, us, gt, x_ctx, x_lat, mod, g_mhead, g_sgu, w_s, b_s_t, w_branch, w_out, g_ffn,
           layer, nctx_tiles, skip_ctx):
    b, t, _ = hf.shape
    d = x_ctx.shape[2]
    off = nctx_tiles if skip_ctx else 0
    tiles = t // ROW_TILE - off
    assert (b * tiles) % 2 == 0

    def tile_specs(h):
        at = _pair_tile(h, tiles)
        tile = lambda w, blk: pl.BlockSpec((None, ROW_TILE, w), lambda n: (at(n)[0], at(n)[1] + off, blk))
        return ([tile(BRANCH_WIDTH, 0), tile(BRANCH_WIDTH, 0), tile(BRANCH_WIDTH, 1), tile(2 * BRANCH_WIDTH, 0),
                 tile(N_BRANCH * d, 0)] + _two_source_specs(at, off, BRANCH_WIDTH, nctx_tiles)
                + _stream_tile_specs(at, off, d, nctx_tiles, mod, layer))

    tile_args = (hf, hb, vo, us, gt, yb_ctx, yb_lat, x_ctx, x_lat, mod)
    params = (g_mhead, g_sgu, w_s, b_s_t, w_branch, w_out, g_ffn)
    out_tile = lambda: pl.BlockSpec((2 * ROW_TILE, d), lambda n: (n, 0))
    rows_out = b * tiles * ROW_TILE
    x_mid, h2 = pl.pallas_call(
        functools.partial(_merge_kernel, d=d, nctx_tiles=nctx_tiles, off=off, tiles=tiles),
        grid=(b * tiles // 2,),
        in_specs=tile_specs(0) + tile_specs(1) + [_layer_spec(a.shape, layer) for a in params],
        out_specs=[out_tile(), out_tile()],
        out_shape=[jax.ShapeDtypeStruct((rows_out, d), _F32), jax.ShapeDtypeStruct((rows_out, d), _BF16)],
        compiler_params=_cparams(1),
        name="merge",
    )(*tile_args, *tile_args, *params)
    return x_mid.reshape(b, tiles * ROW_TILE, d), h2.reshape(b, tiles * ROW_TILE, d)


def _ffn_kernel(*refs, d, n_sub, groups, final_norm):
    h_refs, (hp_ref, hn_ref), x_refs = refs[0:n_sub], refs[n_sub:n_sub + 2], refs[n_sub + 2:2 * n_sub + 2]
    mod_ref, wup_ref, wcv_ref, bcv_ref, wdn_ref, gfin_ref, o_ref, ext_scr, act_scr = refs[2 * n_sub + 2:]
    tm = n_sub * ROW_TILE
    ff = wdn_ref.shape[0]
    ck = FFN_CHUNK
    group = pl.program_id(0) % groups
    ext_scr[0:FFN_HALO, :] = jnp.where(group == 0, jnp.zeros_like(hp_ref), hp_ref[...])
    for s, h_ref in enumerate(h_refs):
        ext_scr[FFN_HALO + s * ROW_TILE:FFN_HALO + (s + 1) * ROW_TILE, :] = h_ref[...]
    ext_scr[FFN_HALO + tm:, :] = jnp.where(group == groups - 1, jnp.zeros_like(hn_ref), hn_ref[...])
    ext_rows = tm + 2 * FFN_HALO
    inner = slice(FFN_HALO, FFN_HALO + tm)

    def conv(cols):
        a = _dot(ext_scr[...], wup_ref[:, cols])
        a_prev = pltpu.roll(a, 1, axis=0)[inner, :]
        a_next = pltpu.roll(a, ext_rows - 1, axis=0)[inner, :]
        return (bcv_ref[:, cols] + a_prev * wcv_ref[0:1, cols] + a[inner, :] * wcv_ref[1:2, cols]
                + a_next * wcv_ref[2:3, cols])

    for c in range(ff // ck):
        gate = conv(slice(c * ck, (c + 1) * ck))
        val = conv(slice(ff + c * ck, ff + (c + 1) * ck))
        act_scr[:, c * ck:(c + 1) * ck] = (_silu(gate) * val).astype(_BF16)

    down = _dot(act_scr[...], wdn_ref[...])
    for s, x_ref in enumerate(x_refs):
        rows = slice(s * ROW_TILE, (s + 1) * ROW_TILE)
        x_new = x_ref[...] + mod_ref[:, 5 * d:6 * d] * down[rows, :]
        if final_norm:
            x_new = _rms(x_new) * gfin_ref[...]
        o_ref[rows, :] = x_new


def _ffn(h2, x, mod, w_up, w_cv, b_cv, w_dn, g_final, layer, first_tile, seg_tiles, n_sub, is_ctx, final_norm):
    b, r, d = x.shape
    assert seg_tiles % n_sub == 0
    groups = seg_tiles // n_sub
    hb = ROW_TILE // FFN_HALO
    batch_row = mod.shape[1] - 8
    tile0 = lambda n: first_tile + (n % groups) * n_sub
    sub_specs = [pl.BlockSpec((None, ROW_TILE, d), lambda n, s=s: (n // groups, tile0(n) + s, 0)) for s in range(n_sub)]
    return pl.pallas_call(
        functools.partial(_ffn_kernel, d=d, n_sub=n_sub, groups=groups, final_norm=final_norm),
        grid=(b * groups,),
        in_specs=sub_specs
        + [pl.BlockSpec((None, FFN_HALO, d), lambda n: (n // groups, jnp.maximum(tile0(n) * hb - 1, 0), 0)),
           pl.BlockSpec((None, FFN_HALO, d),
                        lambda n: (n // groups, jnp.minimum((tile0(n) + n_sub) * hb, r // FFN_HALO - 1), 0))]
        + sub_specs
        + [pl.BlockSpec((None, None, 1, mod.shape[3]), lambda n: (layer, batch_row if is_ctx else n // groups, 0, 0))]
        + [_layer_spec(a.shape, layer) for a in (w_up, w_cv, b_cv, w_dn)] + [_const_spec((1, d))],
        out_specs=pl.BlockSpec((None, n_sub * ROW_TILE, d), lambda n: (n // groups, n % groups, 0)),
        out_shape=jax.ShapeDtypeStruct((b, seg_tiles * ROW_TILE, d), _F32),
        scratch_shapes=[pltpu.VMEM((n_sub * ROW_TILE + 2 * FFN_HALO, d), _BF16),
                        pltpu.VMEM((n_sub * ROW_TILE, w_dn.shape[1]), _BF16)],
        compiler_params=_cparams(1),
        name="ffn",
    )(*([h2] * (n_sub + 2)), *([x] * n_sub), mod, w_up, w_cv, b_cv, w_dn, g_final.reshape(1, d))


def _deinterleave(w):
    return jnp.concatenate([w[..., 0::2], w[..., 1::2]], axis=-1)


def _rotated(w):
    return jnp.concatenate([-w[..., 1::2], w[..., 0::2]], axis=-1)


def _layout_w_in(w):
    depth, d, _ = w.shape
    w = w.astype(_BF16)
    sizes = (BRANCH_WIDTH,) * 4 + (M_GATES, Q_LORA, KV_LORA, ROPE_DIM, BRANCH_WIDTH, BRANCH_WIDTH, N_BRANCH * d)
    splits = tuple(int(s) for s in np.cumsum(sizes)[:-1])
    q, k, v, o, mg, cq, ckv, kr, u, s, gt = jnp.split(w, splits, axis=2)
    mg = mg.reshape(depth, d, 2, 2, HEADS)
    pad = jnp.zeros((depth, d, 128 - 2 * HEADS), w.dtype)
    cols = [q, k, v, o, u, s, gt, cq, ckv, _deinterleave(kr), _rotated(kr),
            mg[:, :, :, 0, :].reshape(depth, d, 2 * HEADS), pad, mg[:, :, :, 1, :].reshape(depth, d, 2 * HEADS), pad]
    return jnp.concatenate(cols, axis=2)


def _layout_gate_bias(bg):
    depth = bg.shape[0]
    bg = bg.reshape(depth, 2, 2, HEADS)
    pad = jnp.zeros((depth, 128 - 2 * HEADS), bg.dtype)
    return jnp.concatenate([bg[:, :, 0, :].reshape(depth, -1), pad, bg[:, :, 1, :].reshape(depth, -1), pad],
                           axis=1).reshape(depth, 1, GATE_W)


def _layout_w_uq(w):
    depth = w.shape[0]
    w = w.reshape(depth, Q_LORA, HEADS, HEAD_DIM + ROPE_DIM)
    nope, rope = w[..., :HEAD_DIM], w[..., HEAD_DIM:]
    return jnp.concatenate([nope, _deinterleave(rope), _rotated(rope)], axis=-1).reshape(depth, Q_LORA, -1).astype(_BF16)


def _layout_w_ukv(w):
    depth = w.shape[0]
    w = w.reshape(depth, KV_LORA, HEADS, 2 * HEAD_DIM).astype(_BF16)
    return w[..., :HEAD_DIM].reshape(depth, KV_LORA, -1), w[..., HEAD_DIM:].reshape(depth, KV_LORA, -1)


def _rope_tables(ctx_len, n_latent):
    rows = n_latent // GRID_W
    row = jnp.repeat(jnp.arange(rows), GRID_W)
    col = jnp.tile(jnp.arange(GRID_W), rows)
    n_freq = ROPE_DIM // 4
    inv = ROPE_BASE ** (-jnp.arange(n_freq, dtype=_F32) / n_freq)
    ang = jnp.concatenate([row[:, None] * inv, col[:, None] * inv], axis=-1)
    zeros = jnp.zeros((n_latent, 128 - ROPE_DIM), _F32)
    cos_l = jnp.concatenate([jnp.cos(ang), jnp.cos(ang), zeros], axis=1)
    sin_l = jnp.concatenate([jnp.sin(ang), jnp.sin(ang), zeros], axis=1)
    cos_c = jnp.concatenate([jnp.ones((ctx_len, ROPE_DIM), _F32), jnp.zeros((ctx_len, 128 - ROPE_DIM), _F32)], axis=1)
    return jnp.concatenate([cos_c, cos_l], axis=0), jnp.concatenate([jnp.zeros_like(cos_c), sin_l], axis=0)


def kernel(x, c, ctx, c_ctx, w_ada, b_ada, g_mix, w_in, w_qkconv, b_qkconv, b_mgate, g_mhead, g_qnorm, w_uq,
           g_kvnorm, w_ukv, g_sgu, w_s, b_s, w_branch, w_out, g_ffn, w_up, w_ffconv, b_ffconv, w_down, g_final):
    b, s, d = x.shape
    ctx_len = ctx.shape[1]
    depth = w_in.shape[0]
    assert ctx_len % ROW_TILE == 0 and s % ROW_TILE == 0 and s % GRID_W == 0
    nctx_tiles = ctx_len // ROW_TILE
    row_param = lambda a: a.reshape(depth, 1, a.shape[-1])

    cos_t, sin_t = _rope_tables(ctx_len, s)
    cond_rows = jnp.concatenate([c, c_ctx[None, :], jnp.zeros((7, d), c.dtype)], axis=0)
    mod = _ada(cond_rows, w_ada, b_ada)
    w_in_l = _layout_w_in(w_in)
    b_gate_l = _layout_gate_bias(b_mgate)
    w_uq_l = _layout_w_uq(w_uq)
    w_k_l, w_v_l = _layout_w_ukv(w_ukv)
    b_s_t = jnp.pad(jnp.swapaxes(b_s, 1, 2), ((0, 0), (0, 0), (0, 128 - HEADS)))
    w_s16, w_branch16, w_out16 = w_s.astype(_BF16), w_branch.astype(_BF16), w_out.astype(_BF16)
    w_up16, w_down16 = w_up.astype(_BF16), w_down.astype(_BF16)
    g_mix_r, g_mhead_r, g_sgu_r, g_ffn_r = row_param(g_mix), row_param(g_mhead), row_param(g_sgu), row_param(g_ffn)
    g_qn_r, g_kvn_r, b_qkconv_r, b_ffconv_r = row_param(g_qnorm), row_param(g_kvnorm), row_param(b_qkconv), row_param(b_ffconv)

    x_ctx, x_lat = ctx, x
    for l in range(depth):
        last = l == depth - 1
        qk, vo, us, gt, at, gates = _proj(x_ctx, x_lat, mod, g_mix_r, w_in_l, l, nctx_tiles)
        q_act, kt_act = _qkconv(qk, w_qkconv, b_qkconv_r, l, (0, ctx_len))
        hf, hb = _mlstm(q_act, kt_act, vo, gates, b_gate_l, l, ctx_len // CHUNK)
        yb_ctx, yb_lat = _attn(at, cos_t, sin_t, g_qn_r, w_uq_l, g_kvn_r, w_k_l, w_v_l, l, ctx_len, not last)
        x_mid, h2 = _merge(hf, hb, vo, yb_ctx, yb_lat, us, gt, x_ctx, x_lat, mod, g_mhead_r, g_sgu_r, w_s16, b_s_t,
                           w_branch16, w_out16, g_ffn_r, l, nctx_tiles, last)
        ffn = functools.partial(_ffn, h2, x_mid, mod, w_up16, w_ffconv, b_ffconv_r, w_down16, g_final, l)
        if last:
            return ffn(0, s // ROW_TILE, 2, False, True)
        x_lat = ffn(nctx_tiles, s // ROW_TILE, 2, False, False)
        x_ctx = ffn(0, nctx_tiles, 1, True, False)
```

```python
import functools

import jax
import jax.numpy as jnp
import numpy as np
from jax import lax
from jax.experimental import pallas as pl
from jax.experimental.pallas import tpu as pltpu

EPS = 1e-6
GRID_W = 64
ROPE_BASE = 10000.0
LOG2_E = 1.4426950408889634

HEADS = 4
HEAD_DIM = 128
CHUNK = 128
BRANCH_WIDTH = HEADS * HEAD_DIM
Q_LORA = 384
KV_LORA = 256
ROPE_DIM = 64
N_BRANCH = 3
M_GATES = 4 * HEADS
KV_SRC = KV_LORA + 2 * ROPE_DIM
ATTN_W = Q_LORA + KV_SRC
GATE_W = 256

ROW_TILE = 256
FFN_HALO = 16
FFN_CHUNK = 256
V7X_VMEM_LIMIT = 56 * 1024 * 1024

_BF16 = jnp.bfloat16
_F32 = jnp.float32


def _cparams(n_axes):
    return pltpu.CompilerParams(dimension_semantics=("arbitrary",) * n_axes, vmem_limit_bytes=V7X_VMEM_LIMIT)


def _const_spec(shape):
    return pl.BlockSpec(tuple(shape), lambda *_: (0,) * len(shape))


def _layer_spec(stacked_shape, layer):
    shape = tuple(stacked_shape[1:])
    return pl.BlockSpec((None,) + shape, lambda *_: (layer,) + (0,) * len(shape))


def _sigmoid(x):
    return 1.0 / (1.0 + jnp.exp(-x))


def _silu(x):
    return x * _sigmoid(x)


def _gelu_tanh(x):
    return x * (0.5 * (1.0 + jnp.tanh(0.7978845608028654 * (x + 0.044715 * (x * x * x)))))


def _rms(x):
    return x * lax.rsqrt(jnp.mean(x * x, axis=-1, keepdims=True) + EPS)


def _dot(a, b):
    return jnp.dot(a, b, preferred_element_type=_F32)


def _dot_nt(a, b):
    return lax.dot_general(a, b, (((1,), (1,)), ((), ())), preferred_element_type=_F32)


def _ada_kernel(c_ref, w_ref, b_ref, o_ref):
    cond = _silu(c_ref[...])
    o_ref[...] = jnp.dot(cond, w_ref[...], precision=lax.Precision.HIGHEST, preferred_element_type=_F32) + b_ref[...]


def _ada(cond_rows, w_ada, b_ada):
    rows, d = cond_rows.shape
    depth, _, n = w_ada.shape
    out = pl.pallas_call(
        _ada_kernel,
        grid=(depth, n // d),
        in_specs=[pl.BlockSpec((rows, d), lambda l, j: (0, 0)),
                  pl.BlockSpec((None, d, d), lambda l, j: (l, 0, j)),
                  pl.BlockSpec((None, 1, d), lambda l, j: (l, 0, j))],
        out_specs=pl.BlockSpec((None, rows, d), lambda l, j: (l, 0, j)),
        out_shape=jax.ShapeDtypeStruct((depth, rows, n), _F32),
        compiler_params=_cparams(2),
        name="ada",
    )(cond_rows, w_ada, b_ada.reshape(depth, 1, n))
    return out.reshape(depth, rows, 1, n)


PROJ_TILE_INPUTS = 5
QK_HALO = 16


def _proj_kernel(*refs, d, nctx_tiles, tiles):
    per_tile = (refs[0:PROJ_TILE_INPUTS], refs[PROJ_TILE_INPUTS:2 * PROJ_TILE_INPUTS])
    g_ref, w_ref, wcv_ref, bcv_ref, q_ref, kt_ref, vo_ref, us_ref, gt_ref, at_ref, gate_ref = refs[2 * PROJ_TILE_INPUTS:]
    bw = BRANCH_WIDTH
    col_at = 6 * bw + N_BRANCH * d
    ext_rows = ROW_TILE + 2 * QK_HALO
    inner = slice(QK_HALO, QK_HALO + ROW_TILE)
    normed = []
    for h, (xc_ref, xl_ref, mod_ref, xp_ref, xn_ref) in enumerate(per_tile):
        tile = (2 * pl.program_id(0) + h) % tiles
        norm = lambda x: (_rms(x) * g_ref[...] * (1.0 + mod_ref[:, d:2 * d]) + mod_ref[:, 0:d]).astype(_BF16)
        hn = norm(jnp.where(tile < nctx_tiles, xc_ref[...], xl_ref[...]))
        zeros = jnp.zeros((QK_HALO, d), _BF16)
        h_prev = jnp.where(tile > nctx_tiles, norm(xp_ref[...]), zeros)
        h_next = jnp.where(jnp.logical_and(tile >= nctx_tiles, tile < tiles - 1), norm(xn_ref[...]), zeros)
        normed.append((hn, jnp.concatenate([h_prev, hn, h_next], axis=0)))
    for h, (hn, h_ext) in enumerate(normed):
        rows = slice(h * ROW_TILE, (h + 1) * ROW_TILE)
        qk = _dot(h_ext, w_ref[:, 0:2 * bw])
        qk_prev = pltpu.roll(qk, 1, axis=0)[inner, :]
        qk_next = pltpu.roll(qk, ext_rows - 1, axis=0)[inner, :]
        a = _silu(bcv_ref[...] + qk_prev * wcv_ref[0:1, :] + qk[inner, :] * wcv_ref[1:2, :] + qk_next * wcv_ref[2:3, :])
        q_ref[rows, :] = a[:, 0:bw].astype(_BF16)
        for head in range(HEADS):
            k = a[:, bw + head * HEAD_DIM:bw + (head + 1) * HEAD_DIM] * (HEAD_DIM ** -0.5)
            kt_ref[h, head] = k.T.astype(_BF16)
        us_ref[rows, :] = _gelu_tanh(_dot(hn, w_ref[:, 4 * bw:6 * bw])).astype(_BF16)
        for g in range(N_BRANCH):
            cols = slice(6 * bw + g * d, 6 * bw + (g + 1) * d)
            gt_ref[rows, g * d:(g + 1) * d] = _sigmoid(_dot(hn, w_ref[:, cols])).astype(_BF16)
        vo_ref[rows, bw:2 * bw] = _sigmoid(_dot(hn, w_ref[:, 3 * bw:4 * bw])).astype(_BF16)
        vo_ref[rows, 0:bw] = _dot(hn, w_ref[:, 2 * bw:3 * bw]).astype(_BF16)
        at_ref[rows, :] = _dot(hn, w_ref[:, col_at:col_at + ATTN_W]).astype(_BF16)
        gate_ref[rows, :] = _dot(hn, w_ref[:, col_at + ATTN_W:col_at + ATTN_W + GATE_W])


def _pair_tile(h, tiles):
    def index(n):
        k = 2 * n + h
        return k // tiles, k % tiles
    return index


def _two_source_specs(at, off, width, nctx_tiles):
    return [pl.BlockSpec((None, ROW_TILE, width), lambda n: (at(n)[0], jnp.minimum(at(n)[1] + off, nctx_tiles - 1), 0)),
            pl.BlockSpec((None, ROW_TILE, width), lambda n: (at(n)[0], jnp.maximum(at(n)[1] + off - nctx_tiles, 0), 0))]


def _stream_tile_specs(at, off, d, nctx_tiles, mod, layer):
    batch_row = mod.shape[1] - 8
    return _two_source_specs(at, off, d, nctx_tiles) + [
        pl.BlockSpec((None, None, 1, mod.shape[3]),
                     lambda n: (layer, jnp.where(at(n)[1] + off < nctx_tiles, batch_row, at(n)[0]), 0, 0))]


def _proj(x_ctx, x_lat, mod, g_mix, w_in, w_qkconv, b_qkconv, layer, nctx_tiles):
    b, _, d = x_ctx.shape
    lat_rows = x_lat.shape[1]
    tiles = nctx_tiles + lat_rows // ROW_TILE
    bw = BRANCH_WIDTH
    widths = (bw, 2 * bw, 2 * bw, N_BRANCH * d, ATTN_W, GATE_W)
    dtypes = (_BF16, _BF16, _BF16, _BF16, _BF16, _F32)
    assert nctx_tiles == 1 and sum(widths) + bw == w_in.shape[2] and (b * tiles) % 2 == 0
    rows = b * tiles * ROW_TILE
    hb = ROW_TILE // QK_HALO

    def tile_specs(h):
        at = _pair_tile(h, tiles)
        lat_tile = lambda n: jnp.maximum(at(n)[1] - nctx_tiles, 0)
        halo = lambda block_of: pl.BlockSpec((None, QK_HALO, d), lambda n: (at(n)[0], block_of(n), 0))
        return _stream_tile_specs(at, 0, d, nctx_tiles, mod, layer) + [
            halo(lambda n: jnp.maximum(lat_tile(n) * hb - 1, 0)),
            halo(lambda n: jnp.minimum((lat_tile(n) + 1) * hb, lat_rows // QK_HALO - 1))]

    flat = lambda w: pl.BlockSpec((2 * ROW_TILE, w), lambda n: (n, 0))
    tile_args = (x_ctx, x_lat, mod, x_lat, x_lat)
    params = (g_mix, w_in, w_qkconv, b_qkconv)
    outs = pl.pallas_call(
        functools.partial(_proj_kernel, d=d, nctx_tiles=nctx_tiles, tiles=tiles),
        grid=(b * tiles // 2,),
        in_specs=tile_specs(0) + tile_specs(1) + [_layer_spec(a.shape, layer) for a in params],
        out_specs=[flat(widths[0]), pl.BlockSpec((2, HEADS, HEAD_DIM, ROW_TILE), lambda n: (n, 0, 0, 0))]
        + [flat(w) for w in widths[1:]],
        out_shape=[jax.ShapeDtypeStruct((rows, widths[0]), dtypes[0]),
                   jax.ShapeDtypeStruct((b * tiles, HEADS, HEAD_DIM, ROW_TILE), _BF16)]
        + [jax.ShapeDtypeStruct((rows, w), dt) for w, dt in zip(widths[1:], dtypes[1:])],
        compiler_params=_cparams(1),
        name="proj",
    )(*tile_args, *tile_args, *params)
    q, kt = outs[0].reshape(b, tiles * ROW_TILE, bw), outs[1]
    return [q, kt] + [o.reshape(b, tiles * ROW_TILE, o.shape[1]) for o in outs[2:]]


def _mlstm_kernel(q_f, kt_f, v_f, gf_cur, gf_nxt, q_b, kt_b, v_b, gb_cur, gb_nxt, bg_ref, hf_ref, hb_ref,
                  c_scr, m_row, m_col, *pre):
    tc = CHUNK
    low, high = pl.ds(0, tc), pl.ds(tc, tc)

    def scan_step(slot, fwd_rows, bwd_rows):
        views = [(q.at[rows, :], kt.at[:, :, rows], v.at[rows, :], out.at[rows, :])
                 for rows, (q, kt, v, out) in ((fwd_rows, (q_f, kt_f, v_f, hf_ref)), (bwd_rows, (q_b, kt_b, v_b, hb_ref)))]
        _mlstm_scan_step(slot, is_fwd, causal, views[0], views[1], c_scr, m_row, m_col)

    pre_a, pre_b = pre[:len(pre) // 2], pre[len(pre) // 2:]
    lane = lax.broadcasted_iota(jnp.int32, (1, 128), 1)
    row = lax.broadcasted_iota(jnp.int32, (tc, 1), 0)
    is_fwd = lane < HEADS
    r_idx = lax.broadcasted_iota(jnp.int32, (tc, tc), 0)
    c_idx = lax.broadcasted_iota(jnp.int32, (tc, tc), 1)
    causal = (c_idx <= r_idx, c_idx >= r_idx)

    def gate_prologue(g_f, g_b, slot):
        bcum_ref, run_ref, rt_ref, dt_ref, rows_ref, blc_ref, dmc_ref = slot
        log_i = jnp.where(is_fwd, g_f[:, 0:128], g_b[:, 0:128]) + bg_ref[:, 0:128]
        gates_f = jnp.where(is_fwd, g_f[:, 128:256], g_b[:, 128:256]) + bg_ref[:, 128:256]
        log_f = jnp.minimum(gates_f, 0.0) - jnp.log(1.0 + jnp.exp(-jnp.abs(gates_f)))
        cum_f, step = log_f, 1
        while step < tc:
            cum_f = cum_f + jnp.where(row >= step, pltpu.roll(cum_f, step, axis=0), 0.0)
            step *= 2
        b_last = cum_f[tc - 1:tc, :]
        bcum = jnp.where(is_fwd, cum_f, b_last - cum_f + log_f)
        r = log_i - bcum
        run_f, run_b, step = r, r, 1
        while step < tc:
            run_f = jnp.maximum(run_f, jnp.where(row >= step, pltpu.roll(run_f, step, axis=0), -jnp.inf))
            run_b = jnp.maximum(run_b, jnp.where(row < tc - step, pltpu.roll(run_b, tc - step, axis=0), -jnp.inf))
            step *= 2
        decay = b_last + r
        r_t = r.T[0:8, :]
        decay_t = decay.T[0:8, :]
        bcum_ref[...] = bcum
        run_ref[...] = jnp.where(is_fwd, run_f, run_b)
        rt_ref[...] = r_t
        dt_ref[...] = decay_t
        rows_ref[0:1, :] = b_last
        rows_ref[1:2, :] = jnp.max(decay, axis=0, keepdims=True)
        blc_ref[...] = decay_t - r_t
        dmc_ref[...] = jnp.broadcast_to(jnp.max(decay_t, axis=1, keepdims=True), (8, tc))

    @pl.when(pl.program_id(1) == 0)
    def _():
        c_scr[...] = jnp.zeros_like(c_scr)
        m_row[...] = jnp.zeros_like(m_row)
        m_col[...] = jnp.zeros_like(m_col)
        gate_prologue(gf_cur.at[low, :], gb_cur.at[high, :], pre_a)

    gate_prologue(gf_cur.at[high, :], gb_cur.at[low, :], pre_b)
    scan_step(pre_a, low, high)
    gate_prologue(gf_nxt.at[low, :], gb_nxt.at[high, :], pre_a)
    scan_step(pre_b, high, low)


def _mlstm_scan_step(slot, is_fwd, causal, refs_f, refs_b, c_scr, m_row, m_col):
    tc = CHUNK
    bcum_ref, run_ref, rt_ref, dt_ref, rows_ref, blc_ref, dmc_ref = slot
    bcum = bcum_ref[...]
    b_last, decay_max = rows_ref[0:1, :], rows_ref[1:2, :]
    m_old = m_row[...]
    g = jnp.maximum(m_old, run_ref[...])
    m_new = jnp.maximum(b_last + m_old, decay_max)
    w_carry = jnp.exp(b_last + m_old - m_new)
    m_col_new = jnp.maximum(blc_ref[...] + m_col[...], dmc_ref[...])
    w_src_t = jnp.exp(dt_ref[...] - m_col_new)
    r_t = rt_ref[...]
    ones = jnp.ones((tc, HEAD_DIM), _BF16)

    scans = [(dr * HEADS + h, refs, h, slice(h * HEAD_DIM, (h + 1) * HEAD_DIM), causal[dr])
             for dr, refs in enumerate((refs_f, refs_b)) for h in range(HEADS)]
    v_ext = [jnp.concatenate([refs[2][:, hs], ones], axis=1) for _, refs, _, hs, _ in scans]
    s16, g_b = [], []
    for l, (q_ref, kt_ref, _, _), h, hs, mask in scans:
        g_l = jnp.broadcast_to(g[:, l:l + 1], (tc, tc))
        w_intra = jnp.exp(jnp.where(mask, r_t[l:l + 1, :] - g_l, -jnp.inf))
        s16.append((_dot(q_ref[:, hs], kt_ref[h]) * w_intra).astype(_BF16))
        g_b.append(g_l)
    qc = []
    for l, (q_ref, kt_ref, _, _), h, hs, _ in scans:
        c_old = c_scr[l]
        qc.append(_dot(q_ref[:, hs], c_old.astype(_BF16)))
        kw_t = (kt_ref[h].astype(_F32) * w_src_t[l:l + 1, :]).astype(_BF16)
        c_scr[l] = w_carry[:, l:l + 1] * c_old + _dot(kw_t, v_ext[l])
    for l, (_, _, _, out_ref), h, hs, _ in scans:
        w_inter = jnp.exp(m_old[:, l:l + 1] - g_b[l])
        b_l = jnp.broadcast_to(bcum[:, l:l + 1], (tc, tc))
        sv = _dot(s16[l], v_ext[l])
        num = w_inter * qc[l][:, 0:HEAD_DIM] + sv[:, 0:HEAD_DIM]
        den = w_inter * qc[l][:, HEAD_DIM:] + sv[:, HEAD_DIM:]
        out_ref[:, hs] = num / jnp.maximum(jnp.abs(den), jnp.exp(-(b_l + g_b[l])))
    m_row[...] = m_new
    m_col[...] = m_col_new


def _mlstm(q, kt, vo, gates, b_gate, layer, nctx):
    b, t, _ = q.shape
    blk = 2 * CHUNK
    nblocks, nctx_blocks = t // blk, nctx // 2
    assert t % blk == 0 and nctx % 2 == 0

    def bwd_block(j):
        return jnp.where(j < nctx_blocks, nctx_blocks - 1 - j, nblocks - 1 - (j - nctx_blocks))

    def specs(block_of):
        nxt = lambda j: block_of(jnp.minimum(j + 1, nblocks - 1))
        return [
            pl.BlockSpec((None, blk, BRANCH_WIDTH), lambda bi, j: (bi, block_of(j), 0)),
            pl.BlockSpec((None, HEADS, HEAD_DIM, blk), lambda bi, j: (bi * nblocks + block_of(j), 0, 0, 0)),
            pl.BlockSpec((None, blk, BRANCH_WIDTH), lambda bi, j: (bi, block_of(j), 0)),
            pl.BlockSpec((None, blk, GATE_W), lambda bi, j: (bi, block_of(j), 0)),
            pl.BlockSpec((None, blk, GATE_W), lambda bi, j: (bi, nxt(j), 0)),
        ]

    gate_slot = [pltpu.VMEM((CHUNK, 128), _F32), pltpu.VMEM((CHUNK, 128), _F32), pltpu.VMEM((8, CHUNK), _F32),
                 pltpu.VMEM((8, CHUNK), _F32), pltpu.VMEM((8, 128), _F32), pltpu.VMEM((8, CHUNK), _F32),
                 pltpu.VMEM((8, CHUNK), _F32)]
    h_shape = jax.ShapeDtypeStruct((b, t, BRANCH_WIDTH), _F32)
    return pl.pallas_call(
        _mlstm_kernel,
        grid=(b, nblocks),
        in_specs=specs(lambda j: j) + specs(bwd_block) + [_layer_spec(b_gate.shape, layer)],
        out_specs=[pl.BlockSpec((None, blk, BRANCH_WIDTH), lambda bi, j: (bi, j, 0)),
                   pl.BlockSpec((None, blk, BRANCH_WIDTH), lambda bi, j: (bi, bwd_block(j), 0))],
        out_shape=[h_shape, h_shape],
        scratch_shapes=[pltpu.VMEM((2 * HEADS, HEAD_DIM, 2 * HEAD_DIM), _F32), pltpu.VMEM((1, 128), _F32),
                        pltpu.VMEM((8, CHUNK), _F32)] + gate_slot + gate_slot,
        compiler_params=_cparams(2),
        name="mlstm",
    )(q, kt, vo, gates, gates, q, kt, vo, gates, gates, b_gate)


ATTN_SUB = 2
VT_ONES = 16


def _attn_kernel(cq_ctx_ref, cq_a_ref, cq_b_ref, kv_ref, ca_ref, sa_ref, gq_ref, wq_ref, gkv_ref, wk_ref, wv_ref,
                 o_ctx_ref, o_lat_ref, k_scr, vt_scr, *, ctx_len, need_ctx, scale):
    i = pl.program_id(1)
    t = kv_ref.shape[0]
    head_w = 2 * HEAD_DIM

    def rope(y, cos_t, sin_t):
        return y * cos_t + pltpu.roll(y, ROPE_DIM, axis=1) * sin_t

    def attend(cq_rows, row0, n_keys, o_ref):
        table_rows = pl.ds(pl.multiple_of(row0, ROW_TILE), cq_rows.shape[0])
        cos_q, sin_q = ca_ref[table_rows, :], sa_ref[table_rows, :]
        cq = (_rms(cq_rows.astype(_F32)) * gq_ref[...]).astype(_BF16)
        qa = _dot(cq, wq_ref[...]) * (scale * LOG2_E)

        def scores(h):
            q_nope = qa[:, h * head_w:h * head_w + HEAD_DIM]
            q_rope = rope(qa[:, h * head_w + HEAD_DIM:(h + 1) * head_w], cos_q, sin_q)
            qh = jnp.concatenate([q_nope, q_rope], axis=1).astype(_BF16)
            return _dot_nt(qh, k_scr[0:n_keys, h * head_w:(h + 1) * head_w])

        s_next = scores(0)
        for h in range(HEADS):
            s = s_next
            if h + 1 < HEADS:
                s_next = scores(h + 1)
            e = jnp.exp2(s - jnp.max(s, axis=1, keepdims=True)).astype(_BF16)
            ot = _dot_nt(vt_scr[h, :, 0:n_keys], e)
            ot = ot[0:HEAD_DIM, :] / ot[HEAD_DIM:HEAD_DIM + 1, :]
            o_ref[:, h * HEAD_DIM:(h + 1) * HEAD_DIM] = ot.T.astype(o_ref.dtype)

    @pl.when(i == 0)
    def _():
        for r in range(t // ROW_TILE):
            rows = slice(r * ROW_TILE, (r + 1) * ROW_TILE)
            ckv = (_rms(kv_ref[rows, 0:KV_LORA].astype(_F32)) * gkv_ref[...]).astype(_BF16)
            k_nope = _dot(ckv, wk_ref[...])
            k_rope = rope(kv_ref[rows, KV_LORA:KV_SRC].astype(_F32), ca_ref[rows, :], sa_ref[rows, :]).astype(_BF16)
            for h in range(HEADS):
                k_scr[rows, h * head_w:h * head_w + HEAD_DIM] = k_nope[:, h * HEAD_DIM:(h + 1) * HEAD_DIM].astype(_BF16)
                k_scr[rows, h * head_w + HEAD_DIM:(h + 1) * head_w] = k_rope
            vv = _dot(ckv, wv_ref[...])
            for h in range(HEADS):
                vt_scr[h, 0:HEAD_DIM, rows] = vv[:, h * HEAD_DIM:(h + 1) * HEAD_DIM].T.astype(_BF16)
                vt_scr[h, HEAD_DIM:, rows] = jnp.ones((VT_ONES, ROW_TILE), _BF16)
        if need_ctx:
            attend(cq_ctx_ref[...], 0, ctx_len, o_ctx_ref)
        else:
            o_ctx_ref[...] = jnp.zeros_like(o_ctx_ref)

    @pl.when(i > 0)
    def _():
        cq_rows = jnp.concatenate([cq_a_ref[...], cq_b_ref[...]], axis=0)
        attend(cq_rows, ctx_len + (i - 1) * (ATTN_SUB * ROW_TILE), t, o_lat_ref)


def _attn(at, cos_t, sin_t, g_qn, w_q, g_kvn, w_k, w_v, layer, ctx_len, need_ctx):
    b, t, _ = at.shape
    assert ctx_len == ROW_TILE and (t - ctx_len) % (ATTN_SUB * ROW_TILE) == 0
    lat_steps = (t - ctx_len) // (ATTN_SUB * ROW_TILE)
    scale = (HEAD_DIM + ROPE_DIM) ** -0.5
    lat_tile = lambda s: pl.BlockSpec((None, ROW_TILE, Q_LORA),
                                      lambda bi, i: (bi, 1 + ATTN_SUB * jnp.maximum(i - 1, 0) + s, 0))
    return pl.pallas_call(
        functools.partial(_attn_kernel, ctx_len=ctx_len, need_ctx=need_ctx, scale=scale),
        grid=(b, 1 + lat_steps),
        in_specs=[pl.BlockSpec((None, ROW_TILE, Q_LORA), lambda bi, i: (bi, 0, 0)), lat_tile(0), lat_tile(1),
                  pl.BlockSpec((None, t, KV_SRC), lambda bi, i: (bi, 0, 1)),
                  _const_spec((t, 128)), _const_spec((t, 128)),
                  _layer_spec(g_qn.shape, layer), _layer_spec(w_q.shape, layer), _layer_spec(g_kvn.shape, layer),
                  _layer_spec(w_k.shape, layer), _layer_spec(w_v.shape, layer)],
        out_specs=[pl.BlockSpec((None, ROW_TILE, BRANCH_WIDTH), lambda bi, i: (bi, 0, 0)),
                   pl.BlockSpec((None, ATTN_SUB * ROW_TILE, BRANCH_WIDTH), lambda bi, i: (bi, jnp.maximum(i - 1, 0), 0))],
        out_shape=[jax.ShapeDtypeStruct((b, ctx_len, BRANCH_WIDTH), _BF16),
                   jax.ShapeDtypeStruct((b, t - ctx_len, BRANCH_WIDTH), _BF16)],
        scratch_shapes=[pltpu.VMEM((t, HEADS * 2 * HEAD_DIM), _BF16),
                        pltpu.VMEM((HEADS, HEAD_DIM + VT_ONES, t), _BF16)],
        compiler_params=_cparams(2),
        name="attn",
    )(at, at, at, at, cos_t, sin_t, g_qn, w_q, g_kvn, w_k, w_v)


def _merge_mix(hf_ref, hb_ref, so_ref, yb, us_ref, gt_ref, gmh_ref, gsgu_ref, ws_ref, bs_ref, wbr_ref, wout_ref, d):
    tm = hf_ref.shape[0]
    half = d // 2
    pb = [_dot(yb, wbr_ref[1, :, c * half:(c + 1) * half]) for c in range(2)]
    hsum = hf_ref[...] + hb_ref[...]
    ya = jnp.concatenate(
        [_rms(hsum[:, h * HEAD_DIM:(h + 1) * HEAD_DIM]) for h in range(HEADS)], axis=1) * gmh_ref[...]
    ya = so_ref[...].astype(_F32) * ya
    cols = []
    for g in range(HEADS):
        gs = slice(BRANCH_WIDTH + g * HEAD_DIM, BRANCH_WIDTH + (g + 1) * HEAD_DIM)
        vn = (_rms(us_ref[:, gs].astype(_F32)) * gsgu_ref[:, g * HEAD_DIM:(g + 1) * HEAD_DIM]).astype(_BF16)
        mixed = [_dot(ws_ref[g], vn[n * CHUNK:(n + 1) * CHUNK, :]) + bs_ref[:, g:g + 1] for n in range(tm // CHUNK)]
        cols.append(jnp.concatenate(mixed, axis=0))
    yc = us_ref[:, 0:BRANCH_WIDTH].astype(_F32) * jnp.concatenate(cols, axis=1)
    ya16, yc16 = ya.astype(_BF16), yc.astype(_BF16)
    pa = [_dot(ya16, wbr_ref[0, :, c * half:(c + 1) * half]) for c in range(2)]
    pc = [_dot(yc16, wbr_ref[2, :, c * half:(c + 1) * half]) for c in range(2)]
    out = None
    for c in range(2):
        gate = lambda g: gt_ref[:, g * d + c * half:g * d + (c + 1) * half].astype(_F32)
        merged = (gate(0) * pa[c] + gate(1) * pb[c] + gate(2) * pc[c]).astype(_BF16)
        part = _dot(merged, wout_ref[c * half:(c + 1) * half, :])
        out = part if out is None else out + part
    return out


MERGE_TILE_INPUTS = 10


def _merge_kernel(*refs, d, nctx_tiles, off, tiles):
    per_tile = (refs[0:MERGE_TILE_INPUTS], refs[MERGE_TILE_INPUTS:2 * MERGE_TILE_INPUTS])
    gmh_ref, gsgu_ref, ws_ref, bs_ref, wbr_ref, wout_ref, gffn_ref, xo_ref, h2_ref = refs[2 * MERGE_TILE_INPUTS:]
    is_ctx = [(2 * pl.program_id(0) + h) % tiles + off < nctx_tiles for h in range(2)]
    outs = []
    for h, (hf_ref, hb_ref, so_ref, us_ref, gt_ref, ybc_ref, ybl_ref) in enumerate(t[0:7] for t in per_tile):
        yb = jnp.where(is_ctx[h], ybc_ref[...], ybl_ref[...])
        outs.append(_merge_mix(hf_ref, hb_ref, so_ref, yb, us_ref, gt_ref, gmh_ref, gsgu_ref, ws_ref, bs_ref, wbr_ref,
                               wout_ref, d))
    for h, tile_refs in enumerate(per_tile):
        xc_ref, xl_ref, mod_ref = tile_refs[7:10]
        rows = slice(h * ROW_TILE, (h + 1) * ROW_TILE)
        x = jnp.where(is_ctx[h], xc_ref[...], xl_ref[...])
        x_new = x + mod_ref[:, 2 * d:3 * d] * outs[h]
        xo_ref[rows, :] = x_new
        h2 = _rms(x_new) * gffn_ref[...]
        h2_ref[rows, :] = (h2 * (1.0 + mod_ref[:, 4 * d:5 * d]) + mod_ref[:, 3 * d:4 * d]).astype(h2_ref.dtype)


def _merge(hf, hb, vo, yb_ctx, yb_lat, us, gt, x_ctx, x_lat, mod, g_mhead, g_sgu, w_s, b_s_t, w_branch, w_out, g_ffn,
           layer, nctx_tiles, skip_ctx):
    b, t, _ = hf.shape
    d = x_ctx.shape[2]
    off = nctx_tiles if skip_ctx else 0
    tiles = t // ROW_TILE - off
    assert (b * tiles) % 2 == 0

    def tile_specs(h):
        at = _pair_tile(h, tiles)
        tile = lambda w, blk: pl.BlockSpec((None, ROW_TILE, w), lambda n: (at(n)[0], at(n)[1] + off, blk))
        return ([tile(BRANCH_WIDTH, 0), tile(BRANCH_WIDTH, 0), tile(BRANCH_WIDTH, 1), tile(2 * BRANCH_WIDTH, 0),
                 tile(N_BRANCH * d, 0)] + _two_source_specs(at, off, BRANCH_WIDTH, nctx_tiles)
                + _stream_tile_specs(at, off, d, nctx_tiles, mod, layer))

    tile_args = (hf, hb, vo, us, gt, yb_ctx, yb_lat, x_ctx, x_lat, mod)
    params = (g_mhead, g_sgu, w_s, b_s_t, w_branch, w_out, g_ffn)
    out_tile = lambda: pl.BlockSpec((2 * ROW_TILE, d), lambda n: (n, 0))
    rows_out = b * tiles * ROW_TILE
    x_mid, h2 = pl.pallas_call(
        functools.partial(_merge_kernel, d=d, nctx_tiles=nctx_tiles, off=off, tiles=tiles),
        grid=(b * tiles // 2,),
        in_specs=tile_specs(0) + tile_specs(1) + [_layer_spec(a.shape, layer) for a in params],
        out_specs=[out_tile(), out_tile()],
        out_shape=[jax.ShapeDtypeStruct((rows_out, d), _F32), jax.ShapeDtypeStruct((rows_out, d), _BF16)],
        compiler_params=_cparams(1),
        name="merge",
    )(*tile_args, *tile_args, *params)
    return x_mid.reshape(b, tiles * ROW_TILE, d), h2.reshape(b, tiles * ROW_TILE, d)


def _ffn_kernel(*refs, d, n_sub, groups, final_norm):
    h_refs, (hp_ref, hn_ref), x_refs = refs[0:n_sub], refs[n_sub:n_sub + 2], refs[n_sub + 2:2 * n_sub + 2]
    mod_ref, wup_ref, wcv_ref, bcv_ref, wdn_ref, gfin_ref, o_ref, ext_scr, act_scr = refs[2 * n_sub + 2:]
    tm = n_sub * ROW_TILE
    ff = wdn_ref.shape[0]
    ck = FFN_CHUNK
    group = pl.program_id(0) % groups
    ext_scr[0:FFN_HALO, :] = jnp.where(group == 0, jnp.zeros_like(hp_ref), hp_ref[...])
    for s, h_ref in enumerate(h_refs):
        ext_scr[FFN_HALO + s * ROW_TILE:FFN_HALO + (s + 1) * ROW_TILE, :] = h_ref[...]
    ext_scr[FFN_HALO + tm:, :] = jnp.where(group == groups - 1, jnp.zeros_like(hn_ref), hn_ref[...])
    ext_rows = tm + 2 * FFN_HALO
    inner = slice(FFN_HALO, FFN_HALO + tm)

    def conv(cols):
        a = _dot(ext_scr[...], wup_ref[:, cols])
        a_prev = pltpu.roll(a, 1, axis=0)[inner, :]
        a_next = pltpu.roll(a, ext_rows - 1, axis=0)[inner, :]
        return (bcv_ref[:, cols] + a_prev * wcv_ref[0:1, cols] + a[inner, :] * wcv_ref[1:2, cols]
                + a_next * wcv_ref[2:3, cols])

    for c in range(ff // ck):
        gate = conv(slice(c * ck, (c + 1) * ck))
        val = conv(slice(ff + c * ck, ff + (c + 1) * ck))
        act_scr[:, c * ck:(c + 1) * ck] = (_silu(gate) * val).astype(_BF16)

    down = _dot(act_scr[...], wdn_ref[...])
    for s, x_ref in enumerate(x_refs):
        rows = slice(s * ROW_TILE, (s + 1) * ROW_TILE)
        x_new = x_ref[...] + mod_ref[:, 5 * d:6 * d] * down[rows, :]
        if final_norm:
            x_new = _rms(x_new) * gfin_ref[...]
        o_ref[rows, :] = x_new


def _ffn(h2, x, mod, w_up, w_cv, b_cv, w_dn, g_final, layer, first_tile, seg_tiles, n_sub, is_ctx, final_norm):
    b, r, d = x.shape
    assert seg_tiles % n_sub == 0
    groups = seg_tiles // n_sub
    hb = ROW_TILE // FFN_HALO
    batch_row = mod.shape[1] - 8
    tile0 = lambda n: first_tile + (n % groups) * n_sub
    sub_specs = [pl.BlockSpec((None, ROW_TILE, d), lambda n, s=s: (n // groups, tile0(n) + s, 0)) for s in range(n_sub)]
    return pl.pallas_call(
        functools.partial(_ffn_kernel, d=d, n_sub=n_sub, groups=groups, final_norm=final_norm),
        grid=(b * groups,),
        in_specs=sub_specs
        + [pl.BlockSpec((None, FFN_HALO, d), lambda n: (n // groups, jnp.maximum(tile0(n) * hb - 1, 0), 0)),
           pl.BlockSpec((None, FFN_HALO, d),
                        lambda n: (n // groups, jnp.minimum((tile0(n) + n_sub) * hb, r // FFN_HALO - 1), 0))]
        + sub_specs
        + [pl.BlockSpec((None, None, 1, mod.shape[3]), lambda n: (layer, batch_row if is_ctx else n // groups, 0, 0))]
        + [_layer_spec(a.shape, layer) for a in (w_up, w_cv, b_cv, w_dn)] + [_const_spec((1, d))],
        out_specs=pl.BlockSpec((None, n_sub * ROW_TILE, d), lambda n: (n // groups, n % groups, 0)),
        out_shape=jax.ShapeDtypeStruct((b, seg_tiles * ROW_TILE, d), _F32),
        scratch_shapes=[pltpu.VMEM((n_sub * ROW_TILE + 2 * FFN_HALO, d), _BF16),
                        pltpu.VMEM((n_sub * ROW_TILE, w_dn.shape[1]), _BF16)],
        compiler_params=_cparams(1),
        name="ffn",
    )(*([h2] * (n_sub + 2)), *([x] * n_sub), mod, w_up, w_cv, b_cv, w_dn, g_final.reshape(1, d))


def _deinterleave(w):
    return jnp.concatenate([w[..., 0::2], w[..., 1::2]], axis=-1)


def _rotated(w):
    return jnp.concatenate([-w[..., 1::2], w[..., 0::2]], axis=-1)


def _layout_w_in(w):
    depth, d, _ = w.shape
    w = w.astype(_BF16)
    sizes = (BRANCH_WIDTH,) * 4 + (M_GATES, Q_LORA, KV_LORA, ROPE_DIM, BRANCH_WIDTH, BRANCH_WIDTH, N_BRANCH * d)
    splits = tuple(int(s) for s in np.cumsum(sizes)[:-1])
    q, k, v, o, mg, cq, ckv, kr, u, s, gt = jnp.split(w, splits, axis=2)
    mg = mg.reshape(depth, d, 2, 2, HEADS)
    pad = jnp.zeros((depth, d, 128 - 2 * HEADS), w.dtype)
    cols = [q, k, v, o, u, s, gt, cq, ckv, _deinterleave(kr), _rotated(kr),
            mg[:, :, :, 0, :].reshape(depth, d, 2 * HEADS), pad, mg[:, :, :, 1, :].reshape(depth, d, 2 * HEADS), pad]
    return jnp.concatenate(cols, axis=2)


def _layout_gate_bias(bg):
    depth = bg.shape[0]
    bg = bg.reshape(depth, 2, 2, HEADS)
    pad = jnp.zeros((depth, 128 - 2 * HEADS), bg.dtype)
    return jnp.concatenate([bg[:, :, 0, :].reshape(depth, -1), pad, bg[:, :, 1, :].reshape(depth, -1), pad],
                           axis=1).reshape(depth, 1, GATE_W)


def _layout_w_uq(w):
    depth = w.shape[0]
    w = w.reshape(depth, Q_LORA, HEADS, HEAD_DIM + ROPE_DIM)
    nope, rope = w[..., :HEAD_DIM], w[..., HEAD_DIM:]
    return jnp.concatenate([nope, _deinterleave(rope), _rotated(rope)], axis=-1).reshape(depth, Q_LORA, -1).astype(_BF16)


def _layout_w_ukv(w):
    depth = w.shape[0]
    w = w.reshape(depth, KV_LORA, HEADS, 2 * HEAD_DIM).astype(_BF16)
    return w[..., :HEAD_DIM].reshape(depth, KV_LORA, -1), w[..., HEAD_DIM:].reshape(depth, KV_LORA, -1)


def _rope_tables(ctx_len, n_latent):
    rows = n_latent // GRID_W
    row = jnp.repeat(jnp.arange(rows), GRID_W)
    col = jnp.tile(jnp.arange(GRID_W), rows)
    n_freq = ROPE_DIM // 4
    inv = ROPE_BASE ** (-jnp.arange(n_freq, dtype=_F32) / n_freq)
    ang = jnp.concatenate([row[:, None] * inv, col[:, None] * inv], axis=-1)
    zeros = jnp.zeros((n_latent, 128 - ROPE_DIM), _F32)
    cos_l = jnp.concatenate([jnp.cos(ang), jnp.cos(ang), zeros], axis=1)
    sin_l = jnp.concatenate([jnp.sin(ang), jnp.sin(ang), zeros], axis=1)
    cos_c = jnp.concatenate([jnp.ones((ctx_len, ROPE_DIM), _F32), jnp.zeros((ctx_len, 128 - ROPE_DIM), _F32)], axis=1)
    return jnp.concatenate([cos_c, cos_l], axis=0), jnp.concatenate([jnp.zeros_like(cos_c), sin_l], axis=0)


def kernel(x, c, ctx, c_ctx, w_ada, b_ada, g_mix, w_in, w_qkconv, b_qkconv, b_mgate, g_mhead, g_qnorm, w_uq,
           g_kvnorm, w_ukv, g_sgu, w_s, b_s, w_branch, w_out, g_ffn, w_up, w_ffconv, b_ffconv, w_down, g_final):
    b, s, d = x.shape
    ctx_len = ctx.shape[1]
    depth = w_in.shape[0]
    assert ctx_len % ROW_TILE == 0 and s % ROW_TILE == 0 and s % GRID_W == 0
    nctx_tiles = ctx_len // ROW_TILE
    row_param = lambda a: a.reshape(depth, 1, a.shape[-1])

    cos_t, sin_t = _rope_tables(ctx_len, s)
    cond_rows = jnp.concatenate([c, c_ctx[None, :], jnp.zeros((7, d), c.dtype)], axis=0)
    mod = _ada(cond_rows, w_ada, b_ada)
    w_in_l = _layout_w_in(w_in)
    b_gate_l = _layout_gate_bias(b_mgate)
    w_uq_l = _layout_w_uq(w_uq)
    w_k_l, w_v_l = _layout_w_ukv(w_ukv)
    b_s_t = jnp.pad(jnp.swapaxes(b_s, 1, 2), ((0, 0), (0, 0), (0, 128 - HEADS)))
    w_s16, w_branch16, w_out16 = w_s.astype(_BF16), w_branch.astype(_BF16), w_out.astype(_BF16)
    w_up16, w_down16 = w_up.astype(_BF16), w_down.astype(_BF16)
    g_mix_r, g_mhead_r, g_sgu_r, g_ffn_r = row_param(g_mix), row_param(g_mhead), row_param(g_sgu), row_param(g_ffn)
    g_qn_r, g_kvn_r, b_qkconv_r, b_ffconv_r = row_param(g_qnorm), row_param(g_kvnorm), row_param(b_qkconv), row_param(b_ffconv)

    x_ctx, x_lat = ctx, x
    for l in range(depth):
        last = l == depth - 1
        q_act, kt_act, vo, us, gt, at, gates = _proj(x_ctx, x_lat, mod, g_mix_r, w_in_l, w_qkconv, b_qkconv_r, l,
                                                     nctx_tiles)
        hf, hb = _mlstm(q_act, kt_act, vo, gates, b_gate_l, l, ctx_len // CHUNK)
        yb_ctx, yb_lat = _attn(at, cos_t, sin_t, g_qn_r, w_uq_l, g_kvn_r, w_k_l, w_v_l, l, ctx_len, not last)
        x_mid, h2 = _merge(hf, hb, vo, yb_ctx, yb_lat, us, gt, x_ctx, x_lat, mod, g_mhead_r, g_sgu_r, w_s16, b_s_t,
                           w_branch16, w_out16, g_ffn_r, l, nctx_tiles, last)
        ffn = functools.partial(_ffn, h2, x_mid, mod, w_up16, w_ffconv, b_ffconv_r, w_down16, g_final, l)
        if last:
            return ffn(0, s // ROW_TILE, 2, False, True)
        x_lat = ffn(nctx_tiles, s // ROW_TILE, 2, False, False)
        x_ctx = ffn(0, nctx_tiles, 1, True, False)
```

```python
import functools

import jax
import jax.numpy as jnp
import numpy as np
from jax import lax
from jax.experimental import pallas as pl
from jax.experimental.pallas import tpu as pltpu

EPS = 1e-6
GRID_W = 64
ROPE_BASE = 10000.0
LOG2_E = 1.4426950408889634

HEADS = 4
HEAD_DIM = 128
CHUNK = 128
BRANCH_WIDTH = HEADS * HEAD_DIM
Q_LORA = 384
KV_LORA = 256
ROPE_DIM = 64
N_BRANCH = 3
M_GATES = 4 * HEADS
KV_SRC = KV_LORA + 2 * ROPE_DIM
ATTN_W = Q_LORA + KV_SRC
SCAN_LANES = 2 * HEADS

V7X_LANES = 128
V7X_BF16_SUBLANE_TILE = 16
V7X_MXU_WIDTH = 256
V7X_VMEM_BYTES = 64 * 1024 * 1024
V7X_VMEM_LIMIT = V7X_VMEM_BYTES * 7 // 8

GATE_W = 2 * V7X_LANES
COND_PAD_ROWS = 8
ROW_TILE = V7X_MXU_WIDTH
FFN_HALO = V7X_BF16_SUBLANE_TILE
FFN_CHUNK = V7X_MXU_WIDTH

_BF16 = jnp.bfloat16
_F32 = jnp.float32


def _cparams(n_axes):
    return pltpu.CompilerParams(dimension_semantics=("arbitrary",) * n_axes, vmem_limit_bytes=V7X_VMEM_LIMIT)


def _const_spec(shape):
    return pl.BlockSpec(tuple(shape), lambda *_: (0,) * len(shape))


def _layer_spec(stacked_shape, layer):
    shape = tuple(stacked_shape[1:])
    return pl.BlockSpec((None,) + shape, lambda *_: (layer,) + (0,) * len(shape))


def _sigmoid(x):
    return 1.0 / (1.0 + jnp.exp(-x))


def _silu(x):
    return x * _sigmoid(x)


def _gelu_tanh(x):
    return x * (0.5 * (1.0 + jnp.tanh(0.7978845608028654 * (x + 0.044715 * (x * x * x)))))


def _rms(x):
    return x * lax.rsqrt(jnp.mean(x * x, axis=-1, keepdims=True) + EPS)


def _dot(a, b):
    return jnp.dot(a, b, preferred_element_type=_F32)


def _dot_nt(a, b):
    return lax.dot_general(a, b, (((1,), (1,)), ((), ())), preferred_element_type=_F32)


def _ada_kernel(c_ref, w_ref, b_ref, o_ref):
    cond = _silu(c_ref[...])
    o_ref[...] = jnp.dot(cond, w_ref[...], precision=lax.Precision.HIGHEST, preferred_element_type=_F32) + b_ref[...]


def _ada(cond_rows, w_ada, b_ada):
    rows, d = cond_rows.shape
    depth, _, n = w_ada.shape
    out = pl.pallas_call(
        _ada_kernel,
        grid=(depth, n // d),
        in_specs=[pl.BlockSpec((rows, d), lambda l, j: (0, 0)),
                  pl.BlockSpec((None, d, d), lambda l, j: (l, 0, j)),
                  pl.BlockSpec((None, 1, d), lambda l, j: (l, 0, j))],
        out_specs=pl.BlockSpec((None, rows, d), lambda l, j: (l, 0, j)),
        out_shape=jax.ShapeDtypeStruct((depth, rows, n), _F32),
        compiler_params=_cparams(2),
        name="ada",
    )(cond_rows, w_ada, b_ada.reshape(depth, 1, n))
    return out.reshape(depth, rows, 1, n)


PROJ_TILE_INPUTS = 5
QK_HALO = V7X_BF16_SUBLANE_TILE


def _proj_kernel(*refs, d, nctx_tiles, tiles):
    per_tile = (refs[0:PROJ_TILE_INPUTS], refs[PROJ_TILE_INPUTS:2 * PROJ_TILE_INPUTS])
    g_ref, w_ref, wcv_ref, bcv_ref, q_ref, kt_ref, vo_ref, us_ref, gt_ref, at_ref, gate_ref = refs[2 * PROJ_TILE_INPUTS:]
    bw = BRANCH_WIDTH
    col_at = 6 * bw + N_BRANCH * d
    ext_rows = ROW_TILE + 2 * QK_HALO
    inner = slice(QK_HALO, QK_HALO + ROW_TILE)

    def normalise(h):
        xc_ref, xl_ref, mod_ref, xp_ref, xn_ref = per_tile[h]
        tile = (2 * pl.program_id(0) + h) % tiles
        norm = lambda x: (_rms(x) * g_ref[...] * (1.0 + mod_ref[:, d:2 * d]) + mod_ref[:, 0:d]).astype(_BF16)
        hn = norm(jnp.where(tile < nctx_tiles, xc_ref[...], xl_ref[...]))
        zeros = jnp.zeros((QK_HALO, d), _BF16)
        h_prev = jnp.where(tile > nctx_tiles, norm(xp_ref[...]), zeros)
        h_next = jnp.where(jnp.logical_and(tile >= nctx_tiles, tile < tiles - 1), norm(xn_ref[...]), zeros)
        return hn, jnp.concatenate([h_prev, hn, h_next], axis=0)

    def conv_group(h, h_ext):
        rows = slice(h * ROW_TILE, (h + 1) * ROW_TILE)
        qk = _dot(h_ext, w_ref[:, 0:2 * bw])
        qk_prev = pltpu.roll(qk, 1, axis=0)[inner, :]
        qk_next = pltpu.roll(qk, ext_rows - 1, axis=0)[inner, :]
        a = _silu(bcv_ref[...] + qk_prev * wcv_ref[0:1, :] + qk[inner, :] * wcv_ref[1:2, :] + qk_next * wcv_ref[2:3, :])
        q_ref[rows, :] = a[:, 0:bw].astype(_BF16)
        for head in range(HEADS):
            k = a[:, bw + head * HEAD_DIM:bw + (head + 1) * HEAD_DIM] * (HEAD_DIM ** -0.5)
            kt_ref[h, head] = k.T.astype(_BF16)

    def activated_groups(h, hn):
        rows = slice(h * ROW_TILE, (h + 1) * ROW_TILE)
        us_ref[rows, :] = _gelu_tanh(_dot(hn, w_ref[:, 4 * bw:6 * bw])).astype(_BF16)
        for g in range(N_BRANCH):
            cols = slice(6 * bw + g * d, 6 * bw + (g + 1) * d)
            gt_ref[rows, g * d:(g + 1) * d] = _sigmoid(_dot(hn, w_ref[:, cols])).astype(_BF16)
        vo_ref[rows, bw:2 * bw] = _sigmoid(_dot(hn, w_ref[:, 3 * bw:4 * bw])).astype(_BF16)
        vo_ref[rows, 0:bw] = _dot(hn, w_ref[:, 2 * bw:3 * bw]).astype(_BF16)
        at_ref[rows, :] = _dot(hn, w_ref[:, col_at:col_at + ATTN_W]).astype(_BF16)
        gate_ref[rows, :] = _dot(hn, w_ref[:, col_at + ATTN_W:col_at + ATTN_W + GATE_W])

    tiles_normed = [normalise(0), normalise(1)]
    for h, (hn, h_ext) in enumerate(tiles_normed):
        conv_group(h, h_ext)
        activated_groups(h, hn)


def _pair_tile(h, tiles):
    def index(n):
        k = 2 * n + h
        return k // tiles, k % tiles
    return index


def _two_source_specs(at, off, width, nctx_tiles):
    return [pl.BlockSpec((None, ROW_TILE, width), lambda n: (at(n)[0], jnp.minimum(at(n)[1] + off, nctx_tiles - 1), 0)),
            pl.BlockSpec((None, ROW_TILE, width), lambda n: (at(n)[0], jnp.maximum(at(n)[1] + off - nctx_tiles, 0), 0))]


def _stream_tile_specs(at, off, d, nctx_tiles, mod, layer):
    batch_row = mod.shape[1] - COND_PAD_ROWS
    return _two_source_specs(at, off, d, nctx_tiles) + [
        pl.BlockSpec((None, None, 1, mod.shape[3]),
                     lambda n: (layer, jnp.where(at(n)[1] + off < nctx_tiles, batch_row, at(n)[0]), 0, 0))]


def _proj(x_ctx, x_lat, mod, g_mix, w_in, w_qkconv, b_qkconv, layer, nctx_tiles):
    b, _, d = x_ctx.shape
    lat_rows = x_lat.shape[1]
    tiles = nctx_tiles + lat_rows // ROW_TILE
    bw = BRANCH_WIDTH
    widths = (bw, 2 * bw, 2 * bw, N_BRANCH * d, ATTN_W, GATE_W)
    dtypes = (_BF16, _BF16, _BF16, _BF16, _BF16, _F32)
    assert nctx_tiles == 1 and sum(widths) + bw == w_in.shape[2] and (b * tiles) % 2 == 0
    rows = b * tiles * ROW_TILE
    hb = ROW_TILE // QK_HALO

    def tile_specs(h):
        at = _pair_tile(h, tiles)
        lat_tile = lambda n: jnp.maximum(at(n)[1] - nctx_tiles, 0)
        halo = lambda block_of: pl.BlockSpec((None, QK_HALO, d), lambda n: (at(n)[0], block_of(n), 0))
        return _stream_tile_specs(at, 0, d, nctx_tiles, mod, layer) + [
            halo(lambda n: jnp.maximum(lat_tile(n) * hb - 1, 0)),
            halo(lambda n: jnp.minimum((lat_tile(n) + 1) * hb, lat_rows // QK_HALO - 1))]

    flat = lambda w: pl.BlockSpec((2 * ROW_TILE, w), lambda n: (n, 0))
    tile_args = (x_ctx, x_lat, mod, x_lat, x_lat)
    params = (g_mix, w_in, w_qkconv, b_qkconv)
    outs = pl.pallas_call(
        functools.partial(_proj_kernel, d=d, nctx_tiles=nctx_tiles, tiles=tiles),
        grid=(b * tiles // 2,),
        in_specs=tile_specs(0) + tile_specs(1) + [_layer_spec(a.shape, layer) for a in params],
        out_specs=[flat(widths[0]), pl.BlockSpec((2, HEADS, HEAD_DIM, ROW_TILE), lambda n: (n, 0, 0, 0))]
        + [flat(w) for w in widths[1:]],
        out_shape=[jax.ShapeDtypeStruct((rows, widths[0]), dtypes[0]),
                   jax.ShapeDtypeStruct((b * tiles, HEADS, HEAD_DIM, ROW_TILE), _BF16)]
        + [jax.ShapeDtypeStruct((rows, w), dt) for w, dt in zip(widths[1:], dtypes[1:])],
        compiler_params=_cparams(1),
        name="proj",
    )(*tile_args, *tile_args, *params)
    q, kt = outs[0].reshape(b, tiles * ROW_TILE, bw), outs[1]
    return [q, kt] + [o.reshape(b, tiles * ROW_TILE, o.shape[1]) for o in outs[2:]]


def _mlstm_kernel(q_f, kt_f, v_f, gf_cur, gf_nxt, q_b, kt_b, v_b, gb_cur, gb_nxt, bg_ref, hf_ref, hb_ref,
                  c_scr, m_row, m_col, *pre):
    tc = CHUNK
    low, high = pl.ds(0, tc), pl.ds(tc, tc)

    def scan_step(slot, fwd_rows, bwd_rows):
        views = [(q.at[rows, :], kt.at[:, :, rows], v.at[rows, :], out.at[rows, :])
                 for rows, (q, kt, v, out) in ((fwd_rows, (q_f, kt_f, v_f, hf_ref)), (bwd_rows, (q_b, kt_b, v_b, hb_ref)))]
        _mlstm_scan_step(slot, is_fwd, causal, views[0], views[1], c_scr, m_row, m_col)

    pre_a, pre_b = pre[:len(pre) // 2], pre[len(pre) // 2:]
    lane = lax.broadcasted_iota(jnp.int32, (1, V7X_LANES), 1)
    row = lax.broadcasted_iota(jnp.int32, (tc, 1), 0)
    is_fwd = lane < HEADS
    r_idx = lax.broadcasted_iota(jnp.int32, (tc, tc), 0)
    c_idx = lax.broadcasted_iota(jnp.int32, (tc, tc), 1)
    causal = (c_idx <= r_idx, c_idx >= r_idx)

    def gate_prologue(g_f, g_b, slot):
        bcum_ref, run_ref, rt_ref, dt_ref, rows_ref, blc_ref, dmc_ref = slot
        gi, gf = slice(0, V7X_LANES), slice(V7X_LANES, GATE_W)
        log_i = jnp.where(is_fwd, g_f[:, gi], g_b[:, gi]) + bg_ref[:, gi]
        gates_f = jnp.where(is_fwd, g_f[:, gf], g_b[:, gf]) + bg_ref[:, gf]
        log_f = jnp.minimum(gates_f, 0.0) - jnp.log(1.0 + jnp.exp(-jnp.abs(gates_f)))
        cum_f, step = log_f, 1
        while step < tc:
            cum_f = cum_f + jnp.where(row >= step, pltpu.roll(cum_f, step, axis=0), 0.0)
            step *= 2
        b_last = cum_f[tc - 1:tc, :]
        bcum = jnp.where(is_fwd, cum_f, b_last - cum_f + log_f)
        r = log_i - bcum
        run_f, run_b, step = r, r, 1
        while step < tc:
            run_f = jnp.maximum(run_f, jnp.where(row >= step, pltpu.roll(run_f, step, axis=0), -jnp.inf))
            run_b = jnp.maximum(run_b, jnp.where(row < tc - step, pltpu.roll(run_b, tc - step, axis=0), -jnp.inf))
            step *= 2
        decay = b_last + r
        r_t = r.T[0:SCAN_LANES, :]
        decay_t = decay.T[0:SCAN_LANES, :]
        bcum_ref[...] = bcum
        run_ref[...] = jnp.where(is_fwd, run_f, run_b)
        rt_ref[...] = r_t
        dt_ref[...] = decay_t
        rows_ref[0:1, :] = b_last
        rows_ref[1:2, :] = jnp.max(decay, axis=0, keepdims=True)
        blc_ref[...] = decay_t - r_t
        dmc_ref[...] = jnp.broadcast_to(jnp.max(decay_t, axis=1, keepdims=True), (SCAN_LANES, tc))

    @pl.when(pl.program_id(1) == 0)
    def _():
        c_scr[...] = jnp.zeros_like(c_scr)
        m_row[...] = jnp.zeros_like(m_row)
        m_col[...] = jnp.zeros_like(m_col)
        gate_prologue(gf_cur.at[low, :], gb_cur.at[high, :], pre_a)

    gate_prologue(gf_cur.at[high, :], gb_cur.at[low, :], pre_b)
    scan_step(pre_a, low, high)
    gate_prologue(gf_nxt.at[low, :], gb_nxt.at[high, :], pre_a)
    scan_step(pre_b, high, low)


def _mlstm_scan_step(slot, is_fwd, causal, refs_f, refs_b, c_scr, m_row, m_col):
    tc = CHUNK
    bcum_ref, run_ref, rt_ref, dt_ref, rows_ref, blc_ref, dmc_ref = slot
    bcum = bcum_ref[...]
    b_last, decay_max = rows_ref[0:1, :], rows_ref[1:2, :]
    m_old = m_row[...]
    g = jnp.maximum(m_old, run_ref[...])
    m_new = jnp.maximum(b_last + m_old, decay_max)
    w_carry = jnp.exp(b_last + m_old - m_new)
    m_col_new = jnp.maximum(blc_ref[...] + m_col[...], dmc_ref[...])
    w_src_t = jnp.exp(dt_ref[...] - m_col_new)
    r_t = rt_ref[...]
    ones = jnp.ones((tc, HEAD_DIM), _BF16)

    scans = [(dr * HEADS + h, refs, h, slice(h * HEAD_DIM, (h + 1) * HEAD_DIM), causal[dr])
             for dr, refs in enumerate((refs_f, refs_b)) for h in range(HEADS)]
    v_ext = [jnp.concatenate([refs[2][:, hs], ones], axis=1) for _, refs, _, hs, _ in scans]
    s16, g_b = [], []
    for l, (q_ref, kt_ref, _, _), h, hs, mask in scans:
        g_l = jnp.broadcast_to(g[:, l:l + 1], (tc, tc))
        w_intra = jnp.exp(jnp.where(mask, r_t[l:l + 1, :] - g_l, -jnp.inf))
        s16.append((_dot(q_ref[:, hs], kt_ref[h]) * w_intra).astype(_BF16))
        g_b.append(g_l)
    qc = []
    for l, (q_ref, kt_ref, _, _), h, hs, _ in scans:
        c_old = c_scr[l]
        qc.append(_dot(q_ref[:, hs], c_old.astype(_BF16)))
        kw_t = (kt_ref[h].astype(_F32) * w_src_t[l:l + 1, :]).astype(_BF16)
        c_scr[l] = w_carry[:, l:l + 1] * c_old + _dot(kw_t, v_ext[l])
    for l, (_, _, _, out_ref), h, hs, _ in scans:
        w_inter = jnp.exp(m_old[:, l:l + 1] - g_b[l])
        b_l = jnp.broadcast_to(bcum[:, l:l + 1], (tc, tc))
        sv = _dot(s16[l], v_ext[l])
        num = w_inter * qc[l][:, 0:HEAD_DIM] + sv[:, 0:HEAD_DIM]
        den = w_inter * qc[l][:, HEAD_DIM:] + sv[:, HEAD_DIM:]
        out_ref[:, hs] = num / jnp.maximum(jnp.abs(den), jnp.exp(-(b_l + g_b[l])))
    m_row[...] = m_new
    m_col[...] = m_col_new


def _mlstm(q, kt, vo, gates, b_gate, layer, nctx):
    b, t, _ = q.shape
    blk = 2 * CHUNK
    nblocks, nctx_blocks = t // blk, nctx // 2
    assert t % blk == 0 and nctx % 2 == 0

    def bwd_block(j):
        return jnp.where(j < nctx_blocks, nctx_blocks - 1 - j, nblocks - 1 - (j - nctx_blocks))

    def specs(block_of):
        nxt = lambda j: block_of(jnp.minimum(j + 1, nblocks - 1))
        return [
            pl.BlockSpec((None, blk, BRANCH_WIDTH), lambda bi, j: (bi, block_of(j), 0)),
            pl.BlockSpec((None, HEADS, HEAD_DIM, blk), lambda bi, j: (bi * nblocks + block_of(j), 0, 0, 0)),
            pl.BlockSpec((None, blk, BRANCH_WIDTH), lambda bi, j: (bi, block_of(j), 0)),
            pl.BlockSpec((None, blk, GATE_W), lambda bi, j: (bi, block_of(j), 0)),
            pl.BlockSpec((None, blk, GATE_W), lambda bi, j: (bi, nxt(j), 0)),
        ]

    time_lane, lane_time = pltpu.VMEM((CHUNK, V7X_LANES), _F32), pltpu.VMEM((SCAN_LANES, CHUNK), _F32)
    gate_slot = [time_lane, time_lane, lane_time, lane_time, pltpu.VMEM((8, V7X_LANES), _F32), lane_time, lane_time]
    h_shape = jax.ShapeDtypeStruct((b, t, BRANCH_WIDTH), _F32)
    return pl.pallas_call(
        _mlstm_kernel,
        grid=(b, nblocks),
        in_specs=specs(lambda j: j) + specs(bwd_block) + [_layer_spec(b_gate.shape, layer)],
        out_specs=[pl.BlockSpec((None, blk, BRANCH_WIDTH), lambda bi, j: (bi, j, 0)),
                   pl.BlockSpec((None, blk, BRANCH_WIDTH), lambda bi, j: (bi, bwd_block(j), 0))],
        out_shape=[h_shape, h_shape],
        scratch_shapes=[pltpu.VMEM((SCAN_LANES, HEAD_DIM, 2 * HEAD_DIM), _F32), pltpu.VMEM((1, V7X_LANES), _F32),
                        lane_time] + gate_slot + gate_slot,
        compiler_params=_cparams(2),
        name="mlstm",
    )(q, kt, vo, gates, gates, q, kt, vo, gates, gates, b_gate)


ATTN_SUB = 2
VT_ONES = V7X_BF16_SUBLANE_TILE


def _attn_kernel(cq_ctx_ref, cq_a_ref, cq_b_ref, kv_ref, ca_ref, sa_ref, gq_ref, wq_ref, gkv_ref, wk_ref, wv_ref,
                 o_ctx_ref, o_lat_ref, k_scr, vt_scr, *, ctx_len, need_ctx, scale):
    i = pl.program_id(1)
    t = kv_ref.shape[0]
    head_w = 2 * HEAD_DIM

    def rope(y, cos_t, sin_t):
        return y * cos_t + pltpu.roll(y, ROPE_DIM, axis=1) * sin_t

    def attend(cq_rows, row0, n_keys, o_ref):
        table_rows = pl.ds(pl.multiple_of(row0, ROW_TILE), cq_rows.shape[0])
        cos_q, sin_q = ca_ref[table_rows, :], sa_ref[table_rows, :]
        cq = (_rms(cq_rows.astype(_F32)) * gq_ref[...]).astype(_BF16)
        qa = _dot(cq, wq_ref[...]) * (scale * LOG2_E)

        def scores(h):
            q_nope = qa[:, h * head_w:h * head_w + HEAD_DIM]
            q_rope = rope(qa[:, h * head_w + HEAD_DIM:(h + 1) * head_w], cos_q, sin_q)
            qh = jnp.concatenate([q_nope, q_rope], axis=1).astype(_BF16)
            return _dot_nt(qh, k_scr[0:n_keys, h * head_w:(h + 1) * head_w])

        s_next = scores(0)
        for h in range(HEADS):
            s = s_next
            if h + 1 < HEADS:
                s_next = scores(h + 1)
            e = jnp.exp2(s - jnp.max(s, axis=1, keepdims=True)).astype(_BF16)
            ot = _dot_nt(vt_scr[h, :, 0:n_keys], e)
            ot = ot[0:HEAD_DIM, :] / ot[HEAD_DIM:HEAD_DIM + 1, :]
            o_ref[:, h * HEAD_DIM:(h + 1) * HEAD_DIM] = ot.T.astype(o_ref.dtype)

    @pl.when(i == 0)
    def _():
        for r in range(t // ROW_TILE):
            rows = slice(r * ROW_TILE, (r + 1) * ROW_TILE)
            ckv = (_rms(kv_ref[rows, 0:KV_LORA].astype(_F32)) * gkv_ref[...]).astype(_BF16)
            k_nope = _dot(ckv, wk_ref[...])
            k_rope = rope(kv_ref[rows, KV_LORA:KV_SRC].astype(_F32), ca_ref[rows, :], sa_ref[rows, :]).astype(_BF16)
            for h in range(HEADS):
                k_scr[rows, h * head_w:h * head_w + HEAD_DIM] = k_nope[:, h * HEAD_DIM:(h + 1) * HEAD_DIM].astype(_BF16)
                k_scr[rows, h * head_w + HEAD_DIM:(h + 1) * head_w] = k_rope
            vv = _dot(ckv, wv_ref[...])
            for h in range(HEADS):
                vt_scr[h, 0:HEAD_DIM, rows] = vv[:, h * HEAD_DIM:(h + 1) * HEAD_DIM].T.astype(_BF16)
                vt_scr[h, HEAD_DIM:, rows] = jnp.ones((VT_ONES, ROW_TILE), _BF16)
        if need_ctx:
            attend(cq_ctx_ref[...], 0, ctx_len, o_ctx_ref)
        else:
            o_ctx_ref[...] = jnp.zeros_like(o_ctx_ref)

    @pl.when(i > 0)
    def _():
        cq_rows = jnp.concatenate([cq_a_ref[...], cq_b_ref[...]], axis=0)
        attend(cq_rows, ctx_len + (i - 1) * (ATTN_SUB * ROW_TILE), t, o_lat_ref)


def _attn(at, cos_t, sin_t, g_qn, w_q, g_kvn, w_k, w_v, layer, ctx_len, need_ctx):
    b, t, _ = at.shape
    assert ctx_len == ROW_TILE and (t - ctx_len) % (ATTN_SUB * ROW_TILE) == 0
    lat_steps = (t - ctx_len) // (ATTN_SUB * ROW_TILE)
    scale = (HEAD_DIM + ROPE_DIM) ** -0.5
    lat_tile = lambda s: pl.BlockSpec((None, ROW_TILE, Q_LORA),
                                      lambda bi, i: (bi, 1 + ATTN_SUB * jnp.maximum(i - 1, 0) + s, 0))
    return pl.pallas_call(
        functools.partial(_attn_kernel, ctx_len=ctx_len, need_ctx=need_ctx, scale=scale),
        grid=(b, 1 + lat_steps),
        in_specs=[pl.BlockSpec((None, ROW_TILE, Q_LORA), lambda bi, i: (bi, 0, 0)), lat_tile(0), lat_tile(1),
                  pl.BlockSpec((None, t, KV_SRC), lambda bi, i: (bi, 0, 1)),
                  _const_spec((t, V7X_LANES)), _const_spec((t, V7X_LANES)),
                  _layer_spec(g_qn.shape, layer), _layer_spec(w_q.shape, layer), _layer_spec(g_kvn.shape, layer),
                  _layer_spec(w_k.shape, layer), _layer_spec(w_v.shape, layer)],
        out_specs=[pl.BlockSpec((None, ROW_TILE, BRANCH_WIDTH), lambda bi, i: (bi, 0, 0)),
                   pl.BlockSpec((None, ATTN_SUB * ROW_TILE, BRANCH_WIDTH), lambda bi, i: (bi, jnp.maximum(i - 1, 0), 0))],
        out_shape=[jax.ShapeDtypeStruct((b, ctx_len, BRANCH_WIDTH), _BF16),
                   jax.ShapeDtypeStruct((b, t - ctx_len, BRANCH_WIDTH), _BF16)],
        scratch_shapes=[pltpu.VMEM((t, HEADS * 2 * HEAD_DIM), _BF16),
                        pltpu.VMEM((HEADS, HEAD_DIM + VT_ONES, t), _BF16)],
        compiler_params=_cparams(2),
        name="attn",
    )(at, at, at, at, cos_t, sin_t, g_qn, w_q, g_kvn, w_k, w_v)


def _merge_mix(hf_ref, hb_ref, so_ref, yb, us_ref, gt_ref, gmh_ref, gsgu_ref, ws_ref, bs_ref, wbr_ref, wout_ref, d):
    tm = hf_ref.shape[0]
    half = d // 2
    pb = [_dot(yb, wbr_ref[1, :, c * half:(c + 1) * half]) for c in range(2)]
    hsum = hf_ref[...] + hb_ref[...]
    ya = jnp.concatenate(
        [_rms(hsum[:, h * HEAD_DIM:(h + 1) * HEAD_DIM]) for h in range(HEADS)], axis=1) * gmh_ref[...]
    ya = so_ref[...].astype(_F32) * ya
    cols = []
    for g in range(HEADS):
        gs = slice(BRANCH_WIDTH + g * HEAD_DIM, BRANCH_WIDTH + (g + 1) * HEAD_DIM)
        vn = (_rms(us_ref[:, gs].astype(_F32)) * gsgu_ref[:, g * HEAD_DIM:(g + 1) * HEAD_DIM]).astype(_BF16)
        mixed = [_dot(ws_ref[g], vn[n * CHUNK:(n + 1) * CHUNK, :]) + bs_ref[:, g:g + 1] for n in range(tm // CHUNK)]
        cols.append(jnp.concatenate(mixed, axis=0))
    yc = us_ref[:, 0:BRANCH_WIDTH].astype(_F32) * jnp.concatenate(cols, axis=1)
    ya16, yc16 = ya.astype(_BF16), yc.astype(_BF16)
    pa = [_dot(ya16, wbr_ref[0, :, c * half:(c + 1) * half]) for c in range(2)]
    pc = [_dot(yc16, wbr_ref[2, :, c * half:(c + 1) * half]) for c in range(2)]
    out = None
    for c in range(2):
        gate = lambda g: gt_ref[:, g * d + c * half:g * d + (c + 1) * half].astype(_F32)
        merged = (gate(0) * pa[c] + gate(1) * pb[c] + gate(2) * pc[c]).astype(_BF16)
        part = _dot(merged, wout_ref[c * half:(c + 1) * half, :])
        out = part if out is None else out + part
    return out


MERGE_TILE_INPUTS = 10


def _merge_kernel(*refs, d, nctx_tiles, off, tiles):
    per_tile = (refs[0:MERGE_TILE_INPUTS], refs[MERGE_TILE_INPUTS:2 * MERGE_TILE_INPUTS])
    gmh_ref, gsgu_ref, ws_ref, bs_ref, wbr_ref, wout_ref, gffn_ref, xo_ref, h2_ref = refs[2 * MERGE_TILE_INPUTS:]
    is_ctx = [(2 * pl.program_id(0) + h) % tiles + off < nctx_tiles for h in range(2)]
    outs = []
    for h, (hf_ref, hb_ref, so_ref, us_ref, gt_ref, ybc_ref, ybl_ref) in enumerate(t[0:7] for t in per_tile):
        yb = jnp.where(is_ctx[h], ybc_ref[...], ybl_ref[...])
        outs.append(_merge_mix(hf_ref, hb_ref, so_ref, yb, us_ref, gt_ref, gmh_ref, gsgu_ref, ws_ref, bs_ref, wbr_ref,
                               wout_ref, d))
    for h, tile_refs in enumerate(per_tile):
        xc_ref, xl_ref, mod_ref = tile_refs[7:10]
        rows = slice(h * ROW_TILE, (h + 1) * ROW_TILE)
        x = jnp.where(is_ctx[h], xc_ref[...], xl_ref[...])
        x_new = x + mod_ref[:, 2 * d:3 * d] * outs[h]
        xo_ref[rows, :] = x_new
        h2 = _rms(x_new) * gffn_ref[...]
        h2_ref[rows, :] = (h2 * (1.0 + mod_ref[:, 4 * d:5 * d]) + mod_ref[:, 3 * d:4 * d]).astype(h2_ref.dtype)


def _merge(hf, hb, vo, yb_ctx, yb_lat, us, gt, x_ctx, x_lat, mod, g_mhead, g_sgu, w_s, b_s_t, w_branch, w_out, g_ffn,
           layer, nctx_tiles, skip_ctx):
    b, t, _ = hf.shape
    d = x_ctx.shape[2]
    off = nctx_tiles if skip_ctx else 0
    tiles = t // ROW_TILE - off
    assert (b * tiles) % 2 == 0

    def tile_specs(h):
        at = _pair_tile(h, tiles)
        tile = lambda w, blk: pl.BlockSpec((None, ROW_TILE, w), lambda n: (at(n)[0], at(n)[1] + off, blk))
        return ([tile(BRANCH_WIDTH, 0), tile(BRANCH_WIDTH, 0), tile(BRANCH_WIDTH, 1), tile(2 * BRANCH_WIDTH, 0),
                 tile(N_BRANCH * d, 0)] + _two_source_specs(at, off, BRANCH_WIDTH, nctx_tiles)
                + _stream_tile_specs(at, off, d, nctx_tiles, mod, layer))

    tile_args = (hf, hb, vo, us, gt, yb_ctx, yb_lat, x_ctx, x_lat, mod)
    params = (g_mhead, g_sgu, w_s, b_s_t, w_branch, w_out, g_ffn)
    out_tile = lambda: pl.BlockSpec((2 * ROW_TILE, d), lambda n: (n, 0))
    rows_out = b * tiles * ROW_TILE
    x_mid, h2 = pl.pallas_call(
        functools.partial(_merge_kernel, d=d, nctx_tiles=nctx_tiles, off=off, tiles=tiles),
        grid=(b * tiles // 2,),
        in_specs=tile_specs(0) + tile_specs(1) + [_layer_spec(a.shape, layer) for a in params],
        out_specs=[out_tile(), out_tile()],
        out_shape=[jax.ShapeDtypeStruct((rows_out, d), _F32), jax.ShapeDtypeStruct((rows_out, d), _BF16)],
        compiler_params=_cparams(1),
        name="merge",
    )(*tile_args, *tile_args, *params)
    return x_mid.reshape(b, tiles * ROW_TILE, d), h2.reshape(b, tiles * ROW_TILE, d)


def _ffn_kernel(*refs, d, n_sub, groups, final_norm):
    h_refs, (hp_ref, hn_ref), x_refs = refs[0:n_sub], refs[n_sub:n_sub + 2], refs[n_sub + 2:2 * n_sub + 2]
    mod_ref, wup_ref, wcv_ref, bcv_ref, wdn_ref, gfin_ref, o_ref, ext_scr, act_scr = refs[2 * n_sub + 2:]
    tm = n_sub * ROW_TILE
    ff = wdn_ref.shape[0]
    ck = FFN_CHUNK
    group = pl.program_id(0) % groups
    ext_scr[0:FFN_HALO, :] = jnp.where(group == 0, jnp.zeros_like(hp_ref), hp_ref[...])
    for s, h_ref in enumerate(h_refs):
        ext_scr[FFN_HALO + s * ROW_TILE:FFN_HALO + (s + 1) * ROW_TILE, :] = h_ref[...]
    ext_scr[FFN_HALO + tm:, :] = jnp.where(group == groups - 1, jnp.zeros_like(hn_ref), hn_ref[...])
    ext_rows = tm + 2 * FFN_HALO
    inner = slice(FFN_HALO, FFN_HALO + tm)

    def conv(cols):
        a = _dot(ext_scr[...], wup_ref[:, cols])
        a_prev = pltpu.roll(a, 1, axis=0)[inner, :]
        a_next = pltpu.roll(a, ext_rows - 1, axis=0)[inner, :]
        return (bcv_ref[:, cols] + a_prev * wcv_ref[0:1, cols] + a[inner, :] * wcv_ref[1:2, cols]
                + a_next * wcv_ref[2:3, cols])

    for c in range(ff // ck):
        gate = conv(slice(c * ck, (c + 1) * ck))
        val = conv(slice(ff + c * ck, ff + (c + 1) * ck))
        act_scr[:, c * ck:(c + 1) * ck] = (_silu(gate) * val).astype(_BF16)

    down = _dot(act_scr[...], wdn_ref[...])
    for s, x_ref in enumerate(x_refs):
        rows = slice(s * ROW_TILE, (s + 1) * ROW_TILE)
        x_new = x_ref[...] + mod_ref[:, 5 * d:6 * d] * down[rows, :]
        if final_norm:
            x_new = _rms(x_new) * gfin_ref[...]
        o_ref[rows, :] = x_new


def _ffn(h2, x, mod, w_up, w_cv, b_cv, w_dn, g_final, layer, first_tile, seg_tiles, n_sub, is_ctx, final_norm):
    b, r, d = x.shape
    assert seg_tiles % n_sub == 0
    groups = seg_tiles // n_sub
    hb = ROW_TILE // FFN_HALO
    batch_row = mod.shape[1] - COND_PAD_ROWS
    tile0 = lambda n: first_tile + (n % groups) * n_sub
    sub_specs = [pl.BlockSpec((None, ROW_TILE, d), lambda n, s=s: (n // groups, tile0(n) + s, 0)) for s in range(n_sub)]
    return pl.pallas_call(
        functools.partial(_ffn_kernel, d=d, n_sub=n_sub, groups=groups, final_norm=final_norm),
        grid=(b * groups,),
        in_specs=sub_specs
        + [pl.BlockSpec((None, FFN_HALO, d), lambda n: (n // groups, jnp.maximum(tile0(n) * hb - 1, 0), 0)),
           pl.BlockSpec((None, FFN_HALO, d),
                        lambda n: (n // groups, jnp.minimum((tile0(n) + n_sub) * hb, r // FFN_HALO - 1), 0))]
        + sub_specs
        + [pl.BlockSpec((None, None, 1, mod.shape[3]), lambda n: (layer, batch_row if is_ctx else n // groups, 0, 0))]
        + [_layer_spec(a.shape, layer) for a in (w_up, w_cv, b_cv, w_dn)] + [_const_spec((1, d))],
        out_specs=pl.BlockSpec((None, n_sub * ROW_TILE, d), lambda n: (n // groups, n % groups, 0)),
        out_shape=jax.ShapeDtypeStruct((b, seg_tiles * ROW_TILE, d), _F32),
        scratch_shapes=[pltpu.VMEM((n_sub * ROW_TILE + 2 * FFN_HALO, d), _BF16),
                        pltpu.VMEM((n_sub * ROW_TILE, w_dn.shape[1]), _BF16)],
        compiler_params=_cparams(1),
        name="ffn",
    )(*([h2] * (n_sub + 2)), *([x] * n_sub), mod, w_up, w_cv, b_cv, w_dn, g_final.reshape(1, d))


def _deinterleave(w):
    return jnp.concatenate([w[..., 0::2], w[..., 1::2]], axis=-1)


def _rotated(w):
    return jnp.concatenate([-w[..., 1::2], w[..., 0::2]], axis=-1)


def _layout_w_in(w):
    depth, d, _ = w.shape
    w = w.astype(_BF16)
    sizes = (BRANCH_WIDTH,) * 4 + (M_GATES, Q_LORA, KV_LORA, ROPE_DIM, BRANCH_WIDTH, BRANCH_WIDTH, N_BRANCH * d)
    splits = tuple(int(s) for s in np.cumsum(sizes)[:-1])
    q, k, v, o, mg, cq, ckv, kr, u, s, gt = jnp.split(w, splits, axis=2)
    mg = mg.reshape(depth, d, 2, 2, HEADS)
    pad = jnp.zeros((depth, d, V7X_LANES - SCAN_LANES), w.dtype)
    cols = [q, k, v, o, u, s, gt, cq, ckv, _deinterleave(kr), _rotated(kr),
            mg[:, :, :, 0, :].reshape(depth, d, 2 * HEADS), pad, mg[:, :, :, 1, :].reshape(depth, d, 2 * HEADS), pad]
    return jnp.concatenate(cols, axis=2)


def _layout_gate_bias(bg):
    depth = bg.shape[0]
    bg = bg.reshape(depth, 2, 2, HEADS)
    pad = jnp.zeros((depth, V7X_LANES - SCAN_LANES), bg.dtype)
    return jnp.concatenate([bg[:, :, 0, :].reshape(depth, -1), pad, bg[:, :, 1, :].reshape(depth, -1), pad],
                           axis=1).reshape(depth, 1, GATE_W)


def _layout_w_uq(w):
    depth = w.shape[0]
    w = w.reshape(depth, Q_LORA, HEADS, HEAD_DIM + ROPE_DIM)
    nope, rope = w[..., :HEAD_DIM], w[..., HEAD_DIM:]
    return jnp.concatenate([nope, _deinterleave(rope), _rotated(rope)], axis=-1).reshape(depth, Q_LORA, -1).astype(_BF16)


def _layout_w_ukv(w):
    depth = w.shape[0]
    w = w.reshape(depth, KV_LORA, HEADS, 2 * HEAD_DIM).astype(_BF16)
    return w[..., :HEAD_DIM].reshape(depth, KV_LORA, -1), w[..., HEAD_DIM:].reshape(depth, KV_LORA, -1)


def _rope_tables(ctx_len, n_latent):
    rows = n_latent // GRID_W
    row = jnp.repeat(jnp.arange(rows), GRID_W)
    col = jnp.tile(jnp.arange(GRID_W), rows)
    n_freq = ROPE_DIM // 4
    inv = ROPE_BASE ** (-jnp.arange(n_freq, dtype=_F32) / n_freq)
    ang = jnp.concatenate([row[:, None] * inv, col[:, None] * inv], axis=-1)
    zeros = jnp.zeros((n_latent, V7X_LANES - ROPE_DIM), _F32)
    cos_l = jnp.concatenate([jnp.cos(ang), jnp.cos(ang), zeros], axis=1)
    sin_l = jnp.concatenate([jnp.sin(ang), jnp.sin(ang), zeros], axis=1)
    cos_c = jnp.concatenate([jnp.ones((ctx_len, ROPE_DIM), _F32), jnp.zeros((ctx_len, V7X_LANES - ROPE_DIM), _F32)], axis=1)
    return jnp.concatenate([cos_c, cos_l], axis=0), jnp.concatenate([jnp.zeros_like(cos_c), sin_l], axis=0)


def kernel(x, c, ctx, c_ctx, w_ada, b_ada, g_mix, w_in, w_qkconv, b_qkconv, b_mgate, g_mhead, g_qnorm, w_uq,
           g_kvnorm, w_ukv, g_sgu, w_s, b_s, w_branch, w_out, g_ffn, w_up, w_ffconv, b_ffconv, w_down, g_final):
    b, s, d = x.shape
    ctx_len = ctx.shape[1]
    depth = w_in.shape[0]
    assert ctx_len % ROW_TILE == 0 and s % ROW_TILE == 0 and s % GRID_W == 0
    nctx_tiles = ctx_len // ROW_TILE
    row_param = lambda a: a.reshape(depth, 1, a.shape[-1])

    cos_t, sin_t = _rope_tables(ctx_len, s)
    cond_rows = jnp.concatenate([c, c_ctx[None, :], jnp.zeros((COND_PAD_ROWS - 1, d), c.dtype)], axis=0)
    mod = _ada(cond_rows, w_ada, b_ada)
    w_in_l = _layout_w_in(w_in)
    b_gate_l = _layout_gate_bias(b_mgate)
    w_uq_l = _layout_w_uq(w_uq)
    w_k_l, w_v_l = _layout_w_ukv(w_ukv)
    b_s_t = jnp.pad(jnp.swapaxes(b_s, 1, 2), ((0, 0), (0, 0), (0, V7X_LANES - HEADS)))
    w_s16, w_branch16, w_out16 = w_s.astype(_BF16), w_branch.astype(_BF16), w_out.astype(_BF16)
    w_up16, w_down16 = w_up.astype(_BF16), w_down.astype(_BF16)
    g_mix_r, g_mhead_r, g_sgu_r, g_ffn_r = row_param(g_mix), row_param(g_mhead), row_param(g_sgu), row_param(g_ffn)
    g_qn_r, g_kvn_r, b_qkconv_r, b_ffconv_r = row_param(g_qnorm), row_param(g_kvnorm), row_param(b_qkconv), row_param(b_ffconv)

    x_ctx, x_lat = ctx, x
    for l in range(depth):
        last = l == depth - 1
        q_act, kt_act, vo, us, gt, at, gates = _proj(x_ctx, x_lat, mod, g_mix_r, w_in_l, w_qkconv, b_qkconv_r, l,
                                                     nctx_tiles)
        hf, hb = _mlstm(q_act, kt_act, vo, gates, b_gate_l, l, ctx_len // CHUNK)
        yb_ctx, yb_lat = _attn(at, cos_t, sin_t, g_qn_r, w_uq_l, g_kvn_r, w_k_l, w_v_l, l, ctx_len, not last)
        x_mid, h2 = _merge(hf, hb, vo, yb_ctx, yb_lat, us, gt, x_ctx, x_lat, mod, g_mhead_r, g_sgu_r, w_s16, b_s_t,
                           w_branch16, w_out16, g_ffn_r, l, nctx_tiles, last)
        ffn = functools.partial(_ffn, h2, x_mid, mod, w_up16, w_ffconv, b_ffconv_r, w_down16, g_final, l)
        if last:
            return ffn(0, s // ROW_TILE, 2, False, True)
        x_lat = ffn(nctx_tiles, s // ROW_TILE, 2, False, False)
        x_ctx = ffn(0, nctx_tiles, 1, True, False)
```

```python
import functools

import jax
import jax.numpy as jnp
import numpy as np
from jax import lax
from jax.experimental import pallas as pl
from jax.experimental.pallas import tpu as pltpu

EPS = 1e-6
GRID_W = 64
ROPE_BASE = 10000.0
LOG2_E = 1.4426950408889634

HEADS = 4
HEAD_DIM = 128
CHUNK = 128
BRANCH_WIDTH = HEADS * HEAD_DIM
Q_LORA = 384
KV_LORA = 256
ROPE_DIM = 64
N_BRANCH = 3
M_GATES = 4 * HEADS
KV_SRC = KV_LORA + 2 * ROPE_DIM
ATTN_W = Q_LORA + KV_SRC
SCAN_LANES = 2 * HEADS

V7X_LANES = 128
V7X_BF16_SUBLANE_TILE = 16
V7X_MXU_WIDTH = 256
V7X_VMEM_BYTES = 64 * 1024 * 1024
V7X_VMEM_LIMIT = V7X_VMEM_BYTES * 7 // 8

GATE_W = 2 * V7X_LANES
COND_PAD_ROWS = 8
ROW_TILE = V7X_MXU_WIDTH
FFN_HALO = V7X_BF16_SUBLANE_TILE
FFN_CHUNK = V7X_MXU_WIDTH

_BF16 = jnp.bfloat16
_F32 = jnp.float32


def _cparams(n_axes):
    return pltpu.CompilerParams(dimension_semantics=("arbitrary",) * n_axes, vmem_limit_bytes=V7X_VMEM_LIMIT)


def _const_spec(shape):
    return pl.BlockSpec(tuple(shape), lambda *_: (0,) * len(shape))


def _layer_spec(stacked_shape, layer):
    shape = tuple(stacked_shape[1:])
    return pl.BlockSpec((None,) + shape, lambda *_: (layer,) + (0,) * len(shape))


def _sigmoid(x):
    return 1.0 / (1.0 + jnp.exp(-x))


def _silu(x):
    return x * _sigmoid(x)


def _gelu_tanh(x):
    return x * (0.5 * (1.0 + jnp.tanh(0.7978845608028654 * (x + 0.044715 * (x * x * x)))))


def _rms(x):
    return x * lax.rsqrt(jnp.mean(x * x, axis=-1, keepdims=True) + EPS)


def _dot(a, b):
    return jnp.dot(a, b, preferred_element_type=_F32)


def _dot_nt(a, b):
    return lax.dot_general(a, b, (((1,), (1,)), ((), ())), preferred_element_type=_F32)


def _ada_kernel(c_ref, w_ref, b_ref, o_ref):
    cond = _silu(c_ref[...])
    o_ref[...] = jnp.dot(cond, w_ref[...], precision=lax.Precision.HIGHEST, preferred_element_type=_F32) + b_ref[...]


def _ada(cond_rows, w_ada, b_ada):
    rows, d = cond_rows.shape
    depth, _, n = w_ada.shape
    out = pl.pallas_call(
        _ada_kernel,
        grid=(depth, n // d),
        in_specs=[pl.BlockSpec((rows, d), lambda l, j: (0, 0)),
                  pl.BlockSpec((None, d, d), lambda l, j: (l, 0, j)),
                  pl.BlockSpec((None, 1, d), lambda l, j: (l, 0, j))],
        out_specs=pl.BlockSpec((None, rows, d), lambda l, j: (l, 0, j)),
        out_shape=jax.ShapeDtypeStruct((depth, rows, n), _F32),
        compiler_params=_cparams(2),
        name="ada",
    )(cond_rows, w_ada, b_ada.reshape(depth, 1, n))
    return out.reshape(depth, rows, 1, n)


PROJ_TILE_INPUTS = 5
QK_HALO = V7X_BF16_SUBLANE_TILE


def _proj_kernel(*refs, d, nctx_tiles, tiles):
    per_tile = (refs[0:PROJ_TILE_INPUTS], refs[PROJ_TILE_INPUTS:2 * PROJ_TILE_INPUTS])
    g_ref, w_ref, wcv_ref, bcv_ref, q_ref, kt_ref, vo_ref, us_ref, gt_ref, at_ref, gate_ref = refs[2 * PROJ_TILE_INPUTS:]
    bw = BRANCH_WIDTH
    col_at = 6 * bw + N_BRANCH * d
    ext_rows = ROW_TILE + 2 * QK_HALO
    inner = slice(QK_HALO, QK_HALO + ROW_TILE)

    def normalise(h):
        xc_ref, xl_ref, mod_ref, xp_ref, xn_ref = per_tile[h]
        tile = (2 * pl.program_id(0) + h) % tiles
        norm = lambda x: (_rms(x) * g_ref[...] * (1.0 + mod_ref[:, d:2 * d]) + mod_ref[:, 0:d]).astype(_BF16)
        hn = norm(jnp.where(tile < nctx_tiles, xc_ref[...], xl_ref[...]))
        zeros = jnp.zeros((QK_HALO, d), _BF16)
        h_prev = jnp.where(tile > nctx_tiles, norm(xp_ref[...]), zeros)
        h_next = jnp.where(jnp.logical_and(tile >= nctx_tiles, tile < tiles - 1), norm(xn_ref[...]), zeros)
        return hn, jnp.concatenate([h_prev, hn, h_next], axis=0)

    cw = V7X_MXU_WIDTH

    def conv_columns(h, h_ext, c):
        rows = slice(h * ROW_TILE, (h + 1) * ROW_TILE)
        cols = slice(c * cw, (c + 1) * cw)
        qk = _dot(h_ext, w_ref[:, cols])
        qk_prev = pltpu.roll(qk, 1, axis=0)[inner, :]
        qk_next = pltpu.roll(qk, ext_rows - 1, axis=0)[inner, :]
        a = _silu(bcv_ref[:, cols] + qk_prev * wcv_ref[0:1, cols] + qk[inner, :] * wcv_ref[1:2, cols]
                  + qk_next * wcv_ref[2:3, cols])
        if c < bw // cw:
            q_ref[rows, cols] = a.astype(_BF16)
        else:
            for sub in range(cw // HEAD_DIM):
                head = (c * cw - bw) // HEAD_DIM + sub
                k = a[:, sub * HEAD_DIM:(sub + 1) * HEAD_DIM] * (HEAD_DIM ** -0.5)
                kt_ref[h, head] = k.T.astype(_BF16)

    def column_groups(h, hn, h_ext):
        rows = slice(h * ROW_TILE, (h + 1) * ROW_TILE)
        conv_columns(h, h_ext, 0)
        us_ref[rows, :] = _gelu_tanh(_dot(hn, w_ref[:, 4 * bw:6 * bw])).astype(_BF16)
        for g in range(N_BRANCH):
            conv_columns(h, h_ext, g + 1)
            cols = slice(6 * bw + g * d, 6 * bw + (g + 1) * d)
            gt_ref[rows, g * d:(g + 1) * d] = _sigmoid(_dot(hn, w_ref[:, cols])).astype(_BF16)
        vo_ref[rows, bw:2 * bw] = _sigmoid(_dot(hn, w_ref[:, 3 * bw:4 * bw])).astype(_BF16)
        vo_ref[rows, 0:bw] = _dot(hn, w_ref[:, 2 * bw:3 * bw]).astype(_BF16)
        at_ref[rows, :] = _dot(hn, w_ref[:, col_at:col_at + ATTN_W]).astype(_BF16)
        gate_ref[rows, :] = _dot(hn, w_ref[:, col_at + ATTN_W:col_at + ATTN_W + GATE_W])

    assert 2 * bw // cw == N_BRANCH + 1
    tiles_normed = [normalise(0), normalise(1)]
    for h, (hn, h_ext) in enumerate(tiles_normed):
        column_groups(h, hn, h_ext)


def _pair_tile(h, tiles):
    def index(n):
        k = 2 * n + h
        return k // tiles, k % tiles
    return index


def _two_source_specs(at, off, width, nctx_tiles):
    return [pl.BlockSpec((None, ROW_TILE, width), lambda n: (at(n)[0], jnp.minimum(at(n)[1] + off, nctx_tiles - 1), 0)),
            pl.BlockSpec((None, ROW_TILE, width), lambda n: (at(n)[0], jnp.maximum(at(n)[1] + off - nctx_tiles, 0), 0))]


def _stream_tile_specs(at, off, d, nctx_tiles, mod, layer):
    batch_row = mod.shape[1] - COND_PAD_ROWS
    return _two_source_specs(at, off, d, nctx_tiles) + [
        pl.BlockSpec((None, None, 1, mod.shape[3]),
                     lambda n: (layer, jnp.where(at(n)[1] + off < nctx_tiles, batch_row, at(n)[0]), 0, 0))]


def _proj(x_ctx, x_lat, mod, g_mix, w_in, w_qkconv, b_qkconv, layer, nctx_tiles):
    b, _, d = x_ctx.shape
    lat_rows = x_lat.shape[1]
    tiles = nctx_tiles + lat_rows // ROW_TILE
    bw = BRANCH_WIDTH
    widths = (bw, 2 * bw, 2 * bw, N_BRANCH * d, ATTN_W, GATE_W)
    dtypes = (_BF16, _BF16, _BF16, _BF16, _BF16, _F32)
    assert nctx_tiles == 1 and sum(widths) + bw == w_in.shape[2] and (b * tiles) % 2 == 0
    rows = b * tiles * ROW_TILE
    hb = ROW_TILE // QK_HALO

    def tile_specs(h):
        at = _pair_tile(h, tiles)
        lat_tile = lambda n: jnp.maximum(at(n)[1] - nctx_tiles, 0)
        halo = lambda block_of: pl.BlockSpec((None, QK_HALO, d), lambda n: (at(n)[0], block_of(n), 0))
        return _stream_tile_specs(at, 0, d, nctx_tiles, mod, layer) + [
            halo(lambda n: jnp.maximum(lat_tile(n) * hb - 1, 0)),
            halo(lambda n: jnp.minimum((lat_tile(n) + 1) * hb, lat_rows // QK_HALO - 1))]

    flat = lambda w: pl.BlockSpec((2 * ROW_TILE, w), lambda n: (n, 0))
    tile_args = (x_ctx, x_lat, mod, x_lat, x_lat)
    params = (g_mix, w_in, w_qkconv, b_qkconv)
    outs = pl.pallas_call(
        functools.partial(_proj_kernel, d=d, nctx_tiles=nctx_tiles, tiles=tiles),
        grid=(b * tiles // 2,),
        in_specs=tile_specs(0) + tile_specs(1) + [_layer_spec(a.shape, layer) for a in params],
        out_specs=[flat(widths[0]), pl.BlockSpec((2, HEADS, HEAD_DIM, ROW_TILE), lambda n: (n, 0, 0, 0))]
        + [flat(w) for w in widths[1:]],
        out_shape=[jax.ShapeDtypeStruct((rows, widths[0]), dtypes[0]),
                   jax.ShapeDtypeStruct((b * tiles, HEADS, HEAD_DIM, ROW_TILE), _BF16)]
        + [jax.ShapeDtypeStruct((rows, w), dt) for w, dt in zip(widths[1:], dtypes[1:])],
        compiler_params=_cparams(1),
        name="proj",
    )(*tile_args, *tile_args, *params)
    q, kt = outs[0].reshape(b, tiles * ROW_TILE, bw), outs[1]
    return [q, kt] + [o.reshape(b, tiles * ROW_TILE, o.shape[1]) for o in outs[2:]]


def _mlstm_kernel(q_f, kt_f, v_f, gf_cur, gf_nxt, q_b, kt_b, v_b, gb_cur, gb_nxt, bg_ref, hf_ref, hb_ref,
                  c_scr, m_row, m_col, *pre):
    tc = CHUNK
    low, high = pl.ds(0, tc), pl.ds(tc, tc)

    def scan_step(slot, fwd_rows, bwd_rows):
        views = [(q.at[rows, :], kt.at[:, :, rows], v.at[rows, :], out.at[rows, :])
                 for rows, (q, kt, v, out) in ((fwd_rows, (q_f, kt_f, v_f, hf_ref)), (bwd_rows, (q_b, kt_b, v_b, hb_ref)))]
        _mlstm_scan_step(slot, is_fwd, causal, views[0], views[1], c_scr, m_row, m_col)

    pre_a, pre_b = pre[:len(pre) // 2], pre[len(pre) // 2:]
    lane = lax.broadcasted_iota(jnp.int32, (1, V7X_LANES), 1)
    row = lax.broadcasted_iota(jnp.int32, (tc, 1), 0)
    is_fwd = lane < HEADS
    r_idx = lax.broadcasted_iota(jnp.int32, (tc, tc), 0)
    c_idx = lax.broadcasted_iota(jnp.int32, (tc, tc), 1)
    causal = (c_idx <= r_idx, c_idx >= r_idx)

    def gate_prologue(g_f, g_b, slot):
        bcum_ref, run_ref, rt_ref, dt_ref, rows_ref, blc_ref, dmc_ref = slot
        gi, gf = slice(0, V7X_LANES), slice(V7X_LANES, GATE_W)
        log_i = jnp.where(is_fwd, g_f[:, gi], g_b[:, gi]) + bg_ref[:, gi]
        gates_f = jnp.where(is_fwd, g_f[:, gf], g_b[:, gf]) + bg_ref[:, gf]
        log_f = jnp.minimum(gates_f, 0.0) - jnp.log(1.0 + jnp.exp(-jnp.abs(gates_f)))
        cum_f, step = log_f, 1
        while step < tc:
            cum_f = cum_f + jnp.where(row >= step, pltpu.roll(cum_f, step, axis=0), 0.0)
            step *= 2
        b_last = cum_f[tc - 1:tc, :]
        bcum = jnp.where(is_fwd, cum_f, b_last - cum_f + log_f)
        r = log_i - bcum
        run_f, run_b, step = r, r, 1
        while step < tc:
            run_f = jnp.maximum(run_f, jnp.where(row >= step, pltpu.roll(run_f, step, axis=0), -jnp.inf))
            run_b = jnp.maximum(run_b, jnp.where(row < tc - step, pltpu.roll(run_b, tc - step, axis=0), -jnp.inf))
            step *= 2
        decay = b_last + r
        r_t = r.T[0:SCAN_LANES, :]
        decay_t = decay.T[0:SCAN_LANES, :]
        bcum_ref[...] = bcum
        run_ref[...] = jnp.where(is_fwd, run_f, run_b)
        rt_ref[...] = r_t
        dt_ref[...] = decay_t
        rows_ref[0:1, :] = b_last
        rows_ref[1:2, :] = jnp.max(decay, axis=0, keepdims=True)
        blc_ref[...] = decay_t - r_t
        dmc_ref[...] = jnp.broadcast_to(jnp.max(decay_t, axis=1, keepdims=True), (SCAN_LANES, tc))

    @pl.when(pl.program_id(1) == 0)
    def _():
        c_scr[...] = jnp.zeros_like(c_scr)
        m_row[...] = jnp.zeros_like(m_row)
        m_col[...] = jnp.zeros_like(m_col)
        gate_prologue(gf_cur.at[low, :], gb_cur.at[high, :], pre_a)

    gate_prologue(gf_cur.at[high, :], gb_cur.at[low, :], pre_b)
    scan_step(pre_a, low, high)
    gate_prologue(gf_nxt.at[low, :], gb_nxt.at[high, :], pre_a)
    scan_step(pre_b, high, low)


def _mlstm_scan_step(slot, is_fwd, causal, refs_f, refs_b, c_scr, m_row, m_col):
    tc = CHUNK
    bcum_ref, run_ref, rt_ref, dt_ref, rows_ref, blc_ref, dmc_ref = slot
    bcum = bcum_ref[...]
    b_last, decay_max = rows_ref[0:1, :], rows_ref[1:2, :]
    m_old = m_row[...]
    g = jnp.maximum(m_old, run_ref[...])
    m_new = jnp.maximum(b_last + m_old, decay_max)
    w_carry = jnp.exp(b_last + m_old - m_new)
    m_col_new = jnp.maximum(blc_ref[...] + m_col[...], dmc_ref[...])
    w_src_t = jnp.exp(dt_ref[...] - m_col_new)
    r_t = rt_ref[...]
    ones = jnp.ones((tc, HEAD_DIM), _BF16)

    scans = [(dr * HEADS + h, refs, h, slice(h * HEAD_DIM, (h + 1) * HEAD_DIM), causal[dr])
             for dr, refs in enumerate((refs_f, refs_b)) for h in range(HEADS)]
    v_ext = [jnp.concatenate([refs[2][:, hs], ones], axis=1) for _, refs, _, hs, _ in scans]
    s16, g_b = [], []
    for l, (q_ref, kt_ref, _, _), h, hs, mask in scans:
        g_l = jnp.broadcast_to(g[:, l:l + 1], (tc, tc))
        w_intra = jnp.exp(jnp.where(mask, r_t[l:l + 1, :] - g_l, -jnp.inf))
        s16.append((_dot(q_ref[:, hs], kt_ref[h]) * w_intra).astype(_BF16))
        g_b.append(g_l)
    qc = []
    for l, (q_ref, kt_ref, _, _), h, hs, _ in scans:
        c_old = c_scr[l]
        qc.append(_dot(q_ref[:, hs], c_old.astype(_BF16)))
        kw_t = (kt_ref[h].astype(_F32) * w_src_t[l:l + 1, :]).astype(_BF16)
        c_scr[l] = w_carry[:, l:l + 1] * c_old + _dot(kw_t, v_ext[l])
    for l, (_, _, _, out_ref), h, hs, _ in scans:
        w_inter = jnp.exp(m_old[:, l:l + 1] - g_b[l])
        b_l = jnp.broadcast_to(bcum[:, l:l + 1], (tc, tc))
        sv = _dot(s16[l], v_ext[l])
        num = w_inter * qc[l][:, 0:HEAD_DIM] + sv[:, 0:HEAD_DIM]
        den = w_inter * qc[l][:, HEAD_DIM:] + sv[:, HEAD_DIM:]
        out_ref[:, hs] = num / jnp.maximum(jnp.abs(den), jnp.exp(-(b_l + g_b[l])))
    m_row[...] = m_new
    m_col[...] = m_col_new


def _mlstm(q, kt, vo, gates, b_gate, layer, nctx):
    b, t, _ = q.shape
    blk = 2 * CHUNK
    nblocks, nctx_blocks = t // blk, nctx // 2
    assert t % blk == 0 and nctx % 2 == 0

    def bwd_block(j):
        return jnp.where(j < nctx_blocks, nctx_blocks - 1 - j, nblocks - 1 - (j - nctx_blocks))

    def specs(block_of):
        nxt = lambda j: block_of(jnp.minimum(j + 1, nblocks - 1))
        return [
            pl.BlockSpec((None, blk, BRANCH_WIDTH), lambda bi, j: (bi, block_of(j), 0)),
            pl.BlockSpec((None, HEADS, HEAD_DIM, blk), lambda bi, j: (bi * nblocks + block_of(j), 0, 0, 0)),
            pl.BlockSpec((None, blk, BRANCH_WIDTH), lambda bi, j: (bi, block_of(j), 0)),
            pl.BlockSpec((None, blk, GATE_W), lambda bi, j: (bi, block_of(j), 0)),
            pl.BlockSpec((None, blk, GATE_W), lambda bi, j: (bi, nxt(j), 0)),
        ]

    time_lane, lane_time = pltpu.VMEM((CHUNK, V7X_LANES), _F32), pltpu.VMEM((SCAN_LANES, CHUNK), _F32)
    gate_slot = [time_lane, time_lane, lane_time, lane_time, pltpu.VMEM((8, V7X_LANES), _F32), lane_time, lane_time]
    h_shape = jax.ShapeDtypeStruct((b, t, BRANCH_WIDTH), _F32)
    return pl.pallas_call(
        _mlstm_kernel,
        grid=(b, nblocks),
        in_specs=specs(lambda j: j) + specs(bwd_block) + [_layer_spec(b_gate.shape, layer)],
        out_specs=[pl.BlockSpec((None, blk, BRANCH_WIDTH), lambda bi, j: (bi, j, 0)),
                   pl.BlockSpec((None, blk, BRANCH_WIDTH), lambda bi, j: (bi, bwd_block(j), 0))],
        out_shape=[h_shape, h_shape],
        scratch_shapes=[pltpu.VMEM((SCAN_LANES, HEAD_DIM, 2 * HEAD_DIM), _F32), pltpu.VMEM((1, V7X_LANES), _F32),
                        lane_time] + gate_slot + gate_slot,
        compiler_params=_cparams(2),
        name="mlstm",
    )(q, kt, vo, gates, gates, q, kt, vo, gates, gates, b_gate)


ATTN_SUB = 2
VT_ONES = V7X_BF16_SUBLANE_TILE


def _attn_kernel(cq_ctx_ref, cq_a_ref, cq_b_ref, kv_ref, ca_ref, sa_ref, gq_ref, wq_ref, gkv_ref, wk_ref, wv_ref,
                 o_ctx_ref, o_lat_ref, k_scr, vt_scr, *, ctx_len, need_ctx, scale):
    i = pl.program_id(1)
    t = kv_ref.shape[0]
    head_w = 2 * HEAD_DIM

    def rope(y, cos_t, sin_t):
        return y * cos_t + pltpu.roll(y, ROPE_DIM, axis=1) * sin_t

    def attend(cq_rows, row0, n_keys, o_ref):
        table_rows = pl.ds(pl.multiple_of(row0, ROW_TILE), cq_rows.shape[0])
        cos_q, sin_q = ca_ref[table_rows, :], sa_ref[table_rows, :]
        cq = (_rms(cq_rows.astype(_F32)) * gq_ref[...]).astype(_BF16)
        qa = _dot(cq, wq_ref[...]) * (scale * LOG2_E)

        def scores(h):
            q_nope = qa[:, h * head_w:h * head_w + HEAD_DIM]
            q_rope = rope(qa[:, h * head_w + HEAD_DIM:(h + 1) * head_w], cos_q, sin_q)
            qh = jnp.concatenate([q_nope, q_rope], axis=1).astype(_BF16)
            return _dot_nt(qh, k_scr[0:n_keys, h * head_w:(h + 1) * head_w])

        s_next = scores(0)
        for h in range(HEADS):
            s = s_next
            if h + 1 < HEADS:
                s_next = scores(h + 1)
            e = jnp.exp2(s - jnp.max(s, axis=1, keepdims=True)).astype(_BF16)
            ot = _dot_nt(vt_scr[h, :, 0:n_keys], e)
            ot = ot[0:HEAD_DIM, :] / ot[HEAD_DIM:HEAD_DIM + 1, :]
            o_ref[:, h * HEAD_DIM:(h + 1) * HEAD_DIM] = ot.T.astype(o_ref.dtype)

    @pl.when(i == 0)
    def _():
        for r in range(t // ROW_TILE):
            rows = slice(r * ROW_TILE, (r + 1) * ROW_TILE)
            ckv = (_rms(kv_ref[rows, 0:KV_LORA].astype(_F32)) * gkv_ref[...]).astype(_BF16)
            k_nope = _dot(ckv, wk_ref[...])
            k_rope = rope(kv_ref[rows, KV_LORA:KV_SRC].astype(_F32), ca_ref[rows, :], sa_ref[rows, :]).astype(_BF16)
            for h in range(HEADS):
                k_scr[rows, h * head_w:h * head_w + HEAD_DIM] = k_nope[:, h * HEAD_DIM:(h + 1) * HEAD_DIM].astype(_BF16)
                k_scr[rows, h * head_w + HEAD_DIM:(h + 1) * head_w] = k_rope
            vv = _dot(ckv, wv_ref[...])
            for h in range(HEADS):
                vt_scr[h, 0:HEAD_DIM, rows] = vv[:, h * HEAD_DIM:(h + 1) * HEAD_DIM].T.astype(_BF16)
                vt_scr[h, HEAD_DIM:, rows] = jnp.ones((VT_ONES, ROW_TILE), _BF16)
        if need_ctx:
            attend(cq_ctx_ref[...], 0, ctx_len, o_ctx_ref)
        else:
            o_ctx_ref[...] = jnp.zeros_like(o_ctx_ref)

    @pl.when(i > 0)
    def _():
        cq_rows = jnp.concatenate([cq_a_ref[...], cq_b_ref[...]], axis=0)
        attend(cq_rows, ctx_len + (i - 1) * (ATTN_SUB * ROW_TILE), t, o_lat_ref)


def _attn(at, cos_t, sin_t, g_qn, w_q, g_kvn, w_k, w_v, layer, ctx_len, need_ctx):
    b, t, _ = at.shape
    assert ctx_len == ROW_TILE and (t - ctx_len) % (ATTN_SUB * ROW_TILE) == 0
    lat_steps = (t - ctx_len) // (ATTN_SUB * ROW_TILE)
    scale = (HEAD_DIM + ROPE_DIM) ** -0.5
    lat_tile = lambda s: pl.BlockSpec((None, ROW_TILE, Q_LORA),
                                      lambda bi, i: (bi, 1 + ATTN_SUB * jnp.maximum(i - 1, 0) + s, 0))
    return pl.pallas_call(
        functools.partial(_attn_kernel, ctx_len=ctx_len, need_ctx=need_ctx, scale=scale),
        grid=(b, 1 + lat_steps),
        in_specs=[pl.BlockSpec((None, ROW_TILE, Q_LORA), lambda bi, i: (bi, 0, 0)), lat_tile(0), lat_tile(1),
                  pl.BlockSpec((None, t, KV_SRC), lambda bi, i: (bi, 0, 1)),
                  _const_spec((t, V7X_LANES)), _const_spec((t, V7X_LANES)),
                  _layer_spec(g_qn.shape, layer), _layer_spec(w_q.shape, layer), _layer_spec(g_kvn.shape, layer),
                  _layer_spec(w_k.shape, layer), _layer_spec(w_v.shape, layer)],
        out_specs=[pl.BlockSpec((None, ROW_TILE, BRANCH_WIDTH), lambda bi, i: (bi, 0, 0)),
                   pl.BlockSpec((None, ATTN_SUB * ROW_TILE, BRANCH_WIDTH), lambda bi, i: (bi, jnp.maximum(i - 1, 0), 0))],
        out_shape=[jax.ShapeDtypeStruct((b, ctx_len, BRANCH_WIDTH), _BF16),
                   jax.ShapeDtypeStruct((b, t - ctx_len, BRANCH_WIDTH), _BF16)],
        scratch_shapes=[pltpu.VMEM((t, HEADS * 2 * HEAD_DIM), _BF16),
                        pltpu.VMEM((HEADS, HEAD_DIM + VT_ONES, t), _BF16)],
        compiler_params=_cparams(2),
        name="attn",
    )(at, at, at, at, cos_t, sin_t, g_qn, w_q, g_kvn, w_k, w_v)


def _merge_mix(hf_ref, hb_ref, so_ref, yb, us_ref, gt_ref, gmh_ref, gsgu_ref, ws_ref, bs_ref, wbr_ref, wout_ref, d):
    tm = hf_ref.shape[0]
    half = d // 2
    pb = [_dot(yb, wbr_ref[1, :, c * half:(c + 1) * half]) for c in range(2)]
    hsum = hf_ref[...] + hb_ref[...]
    ya = jnp.concatenate(
        [_rms(hsum[:, h * HEAD_DIM:(h + 1) * HEAD_DIM]) for h in range(HEADS)], axis=1) * gmh_ref[...]
    ya = so_ref[...].astype(_F32) * ya
    cols = []
    for g in range(HEADS):
        gs = slice(BRANCH_WIDTH + g * HEAD_DIM, BRANCH_WIDTH + (g + 1) * HEAD_DIM)
        vn = (_rms(us_ref[:, gs].astype(_F32)) * gsgu_ref[:, g * HEAD_DIM:(g + 1) * HEAD_DIM]).astype(_BF16)
        mixed = [_dot(ws_ref[g], vn[n * CHUNK:(n + 1) * CHUNK, :]) + bs_ref[:, g:g + 1] for n in range(tm // CHUNK)]
        cols.append(jnp.concatenate(mixed, axis=0))
    yc = us_ref[:, 0:BRANCH_WIDTH].astype(_F32) * jnp.concatenate(cols, axis=1)
    ya16, yc16 = ya.astype(_BF16), yc.astype(_BF16)
    pa = [_dot(ya16, wbr_ref[0, :, c * half:(c + 1) * half]) for c in range(2)]
    pc = [_dot(yc16, wbr_ref[2, :, c * half:(c + 1) * half]) for c in range(2)]
    out = None
    for c in range(2):
        gate = lambda g: gt_ref[:, g * d + c * half:g * d + (c + 1) * half].astype(_F32)
        merged = (gate(0) * pa[c] + gate(1) * pb[c] + gate(2) * pc[c]).astype(_BF16)
        part = _dot(merged, wout_ref[c * half:(c + 1) * half, :])
        out = part if out is None else out + part
    return out


MERGE_TILE_INPUTS = 10


def _merge_kernel(*refs, d, nctx_tiles, off, tiles):
    per_tile = (refs[0:MERGE_TILE_INPUTS], refs[MERGE_TILE_INPUTS:2 * MERGE_TILE_INPUTS])
    gmh_ref, gsgu_ref, ws_ref, bs_ref, wbr_ref, wout_ref, gffn_ref, xo_ref, h2_ref = refs[2 * MERGE_TILE_INPUTS:]
    is_ctx = [(2 * pl.program_id(0) + h) % tiles + off < nctx_tiles for h in range(2)]
    outs = []
    for h, (hf_ref, hb_ref, so_ref, us_ref, gt_ref, ybc_ref, ybl_ref) in enumerate(t[0:7] for t in per_tile):
        yb = jnp.where(is_ctx[h], ybc_ref[...], ybl_ref[...])
        outs.append(_merge_mix(hf_ref, hb_ref, so_ref, yb, us_ref, gt_ref, gmh_ref, gsgu_ref, ws_ref, bs_ref, wbr_ref,
                               wout_ref, d))
    for h, tile_refs in enumerate(per_tile):
        xc_ref, xl_ref, mod_ref = tile_refs[7:10]
        rows = slice(h * ROW_TILE, (h + 1) * ROW_TILE)
        x = jnp.where(is_ctx[h], xc_ref[...], xl_ref[...])
        x_new = x + mod_ref[:, 2 * d:3 * d] * outs[h]
        xo_ref[rows, :] = x_new
        h2 = _rms(x_new) * gffn_ref[...]
        h2_ref[rows, :] = (h2 * (1.0 + mod_ref[:, 4 * d:5 * d]) + mod_ref[:, 3 * d:4 * d]).astype(h2_ref.dtype)


def _merge(hf, hb, vo, yb_ctx, yb_lat, us, gt, x_ctx, x_lat, mod, g_mhead, g_sgu, w_s, b_s_t, w_branch, w_out, g_ffn,
           layer, nctx_tiles, skip_ctx):
    b, t, _ = hf.shape
    d = x_ctx.shape[2]
    off = nctx_tiles if skip_ctx else 0
    tiles = t // ROW_TILE - off
    assert (b * tiles) % 2 == 0

    def tile_specs(h):
        at = _pair_tile(h, tiles)
        tile = lambda w, blk: pl.BlockSpec((None, ROW_TILE, w), lambda n: (at(n)[0], at(n)[1] + off, blk))
        return ([tile(BRANCH_WIDTH, 0), tile(BRANCH_WIDTH, 0), tile(BRANCH_WIDTH, 1), tile(2 * BRANCH_WIDTH, 0),
                 tile(N_BRANCH * d, 0)] + _two_source_specs(at, off, BRANCH_WIDTH, nctx_tiles)
                + _stream_tile_specs(at, off, d, nctx_tiles, mod, layer))

    tile_args = (hf, hb, vo, us, gt, yb_ctx, yb_lat, x_ctx, x_lat, mod)
    params = (g_mhead, g_sgu, w_s, b_s_t, w_branch, w_out, g_ffn)
    out_tile = lambda: pl.BlockSpec((2 * ROW_TILE, d), lambda n: (n, 0))
    rows_out = b * tiles * ROW_TILE
    x_mid, h2 = pl.pallas_call(
        functools.partial(_merge_kernel, d=d, nctx_tiles=nctx_tiles, off=off, tiles=tiles),
        grid=(b * tiles // 2,),
        in_specs=tile_specs(0) + tile_specs(1) + [_layer_spec(a.shape, layer) for a in params],
        out_specs=[out_tile(), out_tile()],
        out_shape=[jax.ShapeDtypeStruct((rows_out, d), _F32), jax.ShapeDtypeStruct((rows_out, d), _BF16)],
        compiler_params=_cparams(1),
        name="merge",
    )(*tile_args, *tile_args, *params)
    return x_mid.reshape(b, tiles * ROW_TILE, d), h2.reshape(b, tiles * ROW_TILE, d)


def _ffn_kernel(*refs, d, n_sub, groups, final_norm):
    h_refs, (hp_ref, hn_ref), x_refs = refs[0:n_sub], refs[n_sub:n_sub + 2], refs[n_sub + 2:2 * n_sub + 2]
    mod_ref, wup_ref, wcv_ref, bcv_ref, wdn_ref, gfin_ref, o_ref, ext_scr, act_scr = refs[2 * n_sub + 2:]
    tm = n_sub * ROW_TILE
    ff = wdn_ref.shape[0]
    ck = FFN_CHUNK
    group = pl.program_id(0) % groups
    ext_scr[0:FFN_HALO, :] = jnp.where(group == 0, jnp.zeros_like(hp_ref), hp_ref[...])
    for s, h_ref in enumerate(h_refs):
        ext_scr[FFN_HALO + s * ROW_TILE:FFN_HALO + (s + 1) * ROW_TILE, :] = h_ref[...]
    ext_scr[FFN_HALO + tm:, :] = jnp.where(group == groups - 1, jnp.zeros_like(hn_ref), hn_ref[...])
    ext_rows = tm + 2 * FFN_HALO
    inner = slice(FFN_HALO, FFN_HALO + tm)

    def conv(cols):
        a = _dot(ext_scr[...], wup_ref[:, cols])
        a_prev = pltpu.roll(a, 1, axis=0)[inner, :]
        a_next = pltpu.roll(a, ext_rows - 1, axis=0)[inner, :]
        return (bcv_ref[:, cols] + a_prev * wcv_ref[0:1, cols] + a[inner, :] * wcv_ref[1:2, cols]
                + a_next * wcv_ref[2:3, cols])

    for c in range(ff // ck):
        gate = conv(slice(c * ck, (c + 1) * ck))
        val = conv(slice(ff + c * ck, ff + (c + 1) * ck))
        act_scr[:, c * ck:(c + 1) * ck] = (_silu(gate) * val).astype(_BF16)

    down = _dot(act_scr[...], wdn_ref[...])
    for s, x_ref in enumerate(x_refs):
        rows = slice(s * ROW_TILE, (s + 1) * ROW_TILE)
        x_new = x_ref[...] + mod_ref[:, 5 * d:6 * d] * down[rows, :]
        if final_norm:
            x_new = _rms(x_new) * gfin_ref[...]
        o_ref[rows, :] = x_new


def _ffn(h2, x, mod, w_up, w_cv, b_cv, w_dn, g_final, layer, first_tile, seg_tiles, n_sub, is_ctx, final_norm):
    b, r, d = x.shape
    assert seg_tiles % n_sub == 0
    groups = seg_tiles // n_sub
    hb = ROW_TILE // FFN_HALO
    batch_row = mod.shape[1] - COND_PAD_ROWS
    tile0 = lambda n: first_tile + (n % groups) * n_sub
    sub_specs = [pl.BlockSpec((None, ROW_TILE, d), lambda n, s=s: (n // groups, tile0(n) + s, 0)) for s in range(n_sub)]
    return pl.pallas_call(
        functools.partial(_ffn_kernel, d=d, n_sub=n_sub, groups=groups, final_norm=final_norm),
        grid=(b * groups,),
        in_specs=sub_specs
        + [pl.BlockSpec((None, FFN_HALO, d), lambda n: (n // groups, jnp.maximum(tile0(n) * hb - 1, 0), 0)),
           pl.BlockSpec((None, FFN_HALO, d),
                        lambda n: (n // groups, jnp.minimum((tile0(n) + n_sub) * hb, r // FFN_HALO - 1), 0))]
        + sub_specs
        + [pl.BlockSpec((None, None, 1, mod.shape[3]), lambda n: (layer, batch_row if is_ctx else n // groups, 0, 0))]
        + [_layer_spec(a.shape, layer) for a in (w_up, w_cv, b_cv, w_dn)] + [_const_spec((1, d))],
        out_specs=pl.BlockSpec((None, n_sub * ROW_TILE, d), lambda n: (n // groups, n % groups, 0)),
        out_shape=jax.ShapeDtypeStruct((b, seg_tiles * ROW_TILE, d), _F32),
        scratch_shapes=[pltpu.VMEM((n_sub * ROW_TILE + 2 * FFN_HALO, d), _BF16),
                        pltpu.VMEM((n_sub * ROW_TILE, w_dn.shape[1]), _BF16)],
        compiler_params=_cparams(1),
        name="ffn",
    )(*([h2] * (n_sub + 2)), *([x] * n_sub), mod, w_up, w_cv, b_cv, w_dn, g_final.reshape(1, d))


def _deinterleave(w):
    return jnp.concatenate([w[..., 0::2], w[..., 1::2]], axis=-1)


def _rotated(w):
    return jnp.concatenate([-w[..., 1::2], w[..., 0::2]], axis=-1)


def _w_in_layout_kernel(w_ref, tail_ref, o_ref, *, moves):
    for src, dst, width in moves:
        o_ref[:, dst:dst + width] = w_ref[:, src:src + width].astype(_BF16)
    o_ref[:, o_ref.shape[1] - tail_ref.shape[1]:] = tail_ref[...]


def _layout_w_in(w):
    depth, d, n_in = w.shape
    sizes = (BRANCH_WIDTH,) * 4 + (M_GATES, Q_LORA, KV_LORA, ROPE_DIM, BRANCH_WIDTH, BRANCH_WIDTH, N_BRANCH * d)
    start = dict(zip("q k v o mg cq ckv kr u s gt".split(), (int(x) for x in np.cumsum((0,) + sizes[:-1]))))
    kr = lax.slice_in_dim(w, start["kr"], start["kr"] + ROPE_DIM, axis=2)
    mg = lax.slice_in_dim(w, start["mg"], start["mg"] + M_GATES, axis=2).reshape(depth, d, 2, 2, HEADS)
    pad = jnp.zeros((depth, d, V7X_LANES - SCAN_LANES), w.dtype)
    tail = jnp.concatenate([_deinterleave(kr), _rotated(kr), mg[:, :, :, 0, :].reshape(depth, d, SCAN_LANES), pad,
                            mg[:, :, :, 1, :].reshape(depth, d, SCAN_LANES), pad], axis=2).astype(_BF16)
    bw = BRANCH_WIDTH
    moves = ((start["q"], 0, 4 * bw), (start["u"], 4 * bw, 2 * bw), (start["gt"], 6 * bw, N_BRANCH * d),
             (start["cq"], 6 * bw + N_BRANCH * d, Q_LORA + KV_LORA))
    n_out = 6 * bw + N_BRANCH * d + Q_LORA + KV_LORA + tail.shape[2]
    return pl.pallas_call(
        functools.partial(_w_in_layout_kernel, moves=moves),
        grid=(depth, d // ROW_TILE),
        in_specs=[pl.BlockSpec((None, ROW_TILE, n_in), lambda l, i: (l, i, 0)),
                  pl.BlockSpec((None, ROW_TILE, tail.shape[2]), lambda l, i: (l, i, 0))],
        out_specs=pl.BlockSpec((None, ROW_TILE, n_out), lambda l, i: (l, i, 0)),
        out_shape=jax.ShapeDtypeStruct((depth, d, n_out), _BF16),
        compiler_params=_cparams(2),
        name="w_in_layout",
    )(w, tail)


def _layout_gate_bias(bg):
    depth = bg.shape[0]
    bg = bg.reshape(depth, 2, 2, HEADS)
    pad = jnp.zeros((depth, V7X_LANES - SCAN_LANES), bg.dtype)
    return jnp.concatenate([bg[:, :, 0, :].reshape(depth, -1), pad, bg[:, :, 1, :].reshape(depth, -1), pad],
                           axis=1).reshape(depth, 1, GATE_W)


def _layout_w_uq(w):
    depth = w.shape[0]
    w = w.reshape(depth, Q_LORA, HEADS, HEAD_DIM + ROPE_DIM)
    nope, rope = w[..., :HEAD_DIM], w[..., HEAD_DIM:]
    return jnp.concatenate([nope, _deinterleave(rope), _rotated(rope)], axis=-1).reshape(depth, Q_LORA, -1).astype(_BF16)


def _layout_w_ukv(w):
    depth = w.shape[0]
    w = w.reshape(depth, KV_LORA, HEADS, 2 * HEAD_DIM).astype(_BF16)
    return w[..., :HEAD_DIM].reshape(depth, KV_LORA, -1), w[..., HEAD_DIM:].reshape(depth, KV_LORA, -1)


def _rope_tables(ctx_len, n_latent):
    rows = n_latent // GRID_W
    row = jnp.repeat(jnp.arange(rows), GRID_W)
    col = jnp.tile(jnp.arange(GRID_W), rows)
    n_freq = ROPE_DIM // 4
    inv = ROPE_BASE ** (-jnp.arange(n_freq, dtype=_F32) / n_freq)
    ang = jnp.concatenate([row[:, None] * inv, col[:, None] * inv], axis=-1)
    zeros = jnp.zeros((n_latent, V7X_LANES - ROPE_DIM), _F32)
    cos_l = jnp.concatenate([jnp.cos(ang), jnp.cos(ang), zeros], axis=1)
    sin_l = jnp.concatenate([jnp.sin(ang), jnp.sin(ang), zeros], axis=1)
    cos_c = jnp.concatenate([jnp.ones((ctx_len, ROPE_DIM), _F32), jnp.zeros((ctx_len, V7X_LANES - ROPE_DIM), _F32)], axis=1)
    return jnp.concatenate([cos_c, cos_l], axis=0), jnp.concatenate([jnp.zeros_like(cos_c), sin_l], axis=0)


def kernel(x, c, ctx, c_ctx, w_ada, b_ada, g_mix, w_in, w_qkconv, b_qkconv, b_mgate, g_mhead, g_qnorm, w_uq,
           g_kvnorm, w_ukv, g_sgu, w_s, b_s, w_branch, w_out, g_ffn, w_up, w_ffconv, b_ffconv, w_down, g_final):
    b, s, d = x.shape
    ctx_len = ctx.shape[1]
    depth = w_in.shape[0]
    assert ctx_len % ROW_TILE == 0 and s % ROW_TILE == 0 and s % GRID_W == 0
    nctx_tiles = ctx_len // ROW_TILE
    row_param = lambda a: a.reshape(depth, 1, a.shape[-1])

    cos_t, sin_t = _rope_tables(ctx_len, s)
    cond_rows = jnp.concatenate([c, c_ctx[None, :], jnp.zeros((COND_PAD_ROWS - 1, d), c.dtype)], axis=0)
    mod = _ada(cond_rows, w_ada, b_ada)
    w_in_l = _layout_w_in(w_in)
    b_gate_l = _layout_gate_bias(b_mgate)
    w_uq_l = _layout_w_uq(w_uq)
    w_k_l, w_v_l = _layout_w_ukv(w_ukv)
    b_s_t = jnp.pad(jnp.swapaxes(b_s, 1, 2), ((0, 0), (0, 0), (0, V7X_LANES - HEADS)))
    w_s16, w_branch16, w_out16 = w_s.astype(_BF16), w_branch.astype(_BF16), w_out.astype(_BF16)
    w_up16, w_down16 = w_up.astype(_BF16), w_down.astype(_BF16)
    g_mix_r, g_mhead_r, g_sgu_r, g_ffn_r = row_param(g_mix), row_param(g_mhead), row_param(g_sgu), row_param(g_ffn)
    g_qn_r, g_kvn_r, b_qkconv_r, b_ffconv_r = row_param(g_qnorm), row_param(g_kvnorm), row_param(b_qkconv), row_param(b_ffconv)

    x_ctx, x_lat = ctx, x
    for l in range(depth):
        last = l == depth - 1
        q_act, kt_act, vo, us, gt, at, gates = _proj(x_ctx, x_lat, mod, g_mix_r, w_in_l, w_qkconv, b_qkconv_r, l,
                                                     nctx_tiles)
        hf, hb = _mlstm(q_act, kt_act, vo, gates, b_gate_l, l, ctx_len // CHUNK)
        yb_ctx, yb_lat = _attn(at, cos_t, sin_t, g_qn_r, w_uq_l, g_kvn_r, w_k_l, w_v_l, l, ctx_len, not last)
        x_mid, h2 = _merge(hf, hb, vo, yb_ctx, yb_lat, us, gt, x_ctx, x_lat, mod, g_mhead_r, g_sgu_r, w_s16, b_s_t,
                           w_branch16, w_out16, g_ffn_r, l, nctx_tiles, last)
        ffn = functools.partial(_ffn, h2, x_mid, mod, w_up16, w_ffconv, b_ffconv_r, w_down16, g_final, l)
        if last:
            return ffn(0, s // ROW_TILE, 2, False, True)
        x_lat = ffn(nctx_tiles, s // ROW_TILE, 2, False, False)
        x_ctx = ffn(0, nctx_tiles, 1, True, False)
```

```python
import functools

import jax
import jax.numpy as jnp
import numpy as np
from jax import lax
from jax.experimental import pallas as pl
from jax.experimental.pallas import tpu as pltpu

EPS = 1e-6
GRID_W = 64
ROPE_BASE = 10000.0
LOG2_E = 1.4426950408889634

HEADS = 4
HEAD_DIM = 128
CHUNK = 128
BRANCH_WIDTH = HEADS * HEAD_DIM
Q_LORA = 384
KV_LORA = 256
ROPE_DIM = 64
N_BRANCH = 3
M_GATES = 4 * HEADS
KV_SRC = KV_LORA + 2 * ROPE_DIM
ATTN_W = Q_LORA + KV_SRC
SCAN_LANES = 2 * HEADS

V7X_LANES = 128
V7X_BF16_SUBLANE_TILE = 16
V7X_MXU_WIDTH = 256
V7X_VMEM_BYTES = 64 * 1024 * 1024
V7X_VMEM_LIMIT = V7X_VMEM_BYTES * 7 // 8

GATE_W = 2 * V7X_LANES
COND_PAD_ROWS = 8
ROW_TILE = V7X_MXU_WIDTH
FFN_HALO = V7X_BF16_SUBLANE_TILE
FFN_CHUNK = V7X_MXU_WIDTH

_BF16 = jnp.bfloat16
_F32 = jnp.float32


def _cparams(n_axes):
    return pltpu.CompilerParams(dimension_semantics=("arbitrary",) * n_axes, vmem_limit_bytes=V7X_VMEM_LIMIT)


def _const_spec(shape):
    return pl.BlockSpec(tuple(shape), lambda *_: (0,) * len(shape))


def _layer_spec(stacked_shape, layer):
    shape = tuple(stacked_shape[1:])
    return pl.BlockSpec((None,) + shape, lambda *_: (layer,) + (0,) * len(shape))


def _sigmoid(x):
    return 1.0 / (1.0 + jnp.exp(-x))


def _silu(x):
    return x * _sigmoid(x)


def _gelu_tanh(x):
    return x * (0.5 * (1.0 + jnp.tanh(0.7978845608028654 * (x + 0.044715 * (x * x * x)))))


def _rms(x):
    return x * lax.rsqrt(jnp.mean(x * x, axis=-1, keepdims=True) + EPS)


def _dot(a, b):
    return jnp.dot(a, b, preferred_element_type=_F32)


def _dot_nt(a, b):
    return lax.dot_general(a, b, (((1,), (1,)), ((), ())), preferred_element_type=_F32)


def _ada_kernel(c_ref, w_ref, b_ref, o_ref):
    cond = _silu(c_ref[...])
    o_ref[...] = jnp.dot(cond, w_ref[...], precision=lax.Precision.HIGHEST, preferred_element_type=_F32) + b_ref[...]


def _ada(cond_rows, w_ada, b_ada):
    rows, d = cond_rows.shape
    depth, _, n = w_ada.shape
    out = pl.pallas_call(
        _ada_kernel,
        grid=(depth, n // d),
        in_specs=[pl.BlockSpec((rows, d), lambda l, j: (0, 0)),
                  pl.BlockSpec((None, d, d), lambda l, j: (l, 0, j)),
                  pl.BlockSpec((None, 1, d), lambda l, j: (l, 0, j))],
        out_specs=pl.BlockSpec((None, rows, d), lambda l, j: (l, 0, j)),
        out_shape=jax.ShapeDtypeStruct((depth, rows, n), _F32),
        compiler_params=_cparams(2),
        name="ada",
    )(cond_rows, w_ada, b_ada.reshape(depth, 1, n))
    return out.reshape(depth, rows, 1, n)


PROJ_TILE_INPUTS = 5
QK_HALO = V7X_BF16_SUBLANE_TILE


def _proj_kernel(*refs, d, nctx_tiles, tiles):
    per_tile = (refs[0:PROJ_TILE_INPUTS], refs[PROJ_TILE_INPUTS:2 * PROJ_TILE_INPUTS])
    g_ref, w_ref, wcv_ref, bcv_ref, q_ref, kt_ref, vo_ref, us_ref, gt_ref, at_ref, gate_ref = refs[2 * PROJ_TILE_INPUTS:]
    bw = BRANCH_WIDTH
    col_at = 6 * bw + N_BRANCH * d
    ext_rows = ROW_TILE + 2 * QK_HALO
    inner = slice(QK_HALO, QK_HALO + ROW_TILE)

    def normalise(h):
        xc_ref, xl_ref, mod_ref, xp_ref, xn_ref = per_tile[h]
        tile = (2 * pl.program_id(0) + h) % tiles
        norm = lambda x: (_rms(x) * g_ref[...] * (1.0 + mod_ref[:, d:2 * d]) + mod_ref[:, 0:d]).astype(_BF16)
        hn = norm(jnp.where(tile < nctx_tiles, xc_ref[...], xl_ref[...]))
        zeros = jnp.zeros((QK_HALO, d), _BF16)
        h_prev = jnp.where(tile > nctx_tiles, norm(xp_ref[...]), zeros)
        h_next = jnp.where(jnp.logical_and(tile >= nctx_tiles, tile < tiles - 1), norm(xn_ref[...]), zeros)
        return hn, jnp.concatenate([h_prev, hn, h_next], axis=0)

    cw = V7X_MXU_WIDTH

    def conv_columns(h, h_ext, c):
        rows = slice(h * ROW_TILE, (h + 1) * ROW_TILE)
        cols = slice(c * cw, (c + 1) * cw)
        qk = _dot(h_ext, w_ref[:, cols])
        qk_prev = pltpu.roll(qk, 1, axis=0)[inner, :]
        qk_next = pltpu.roll(qk, ext_rows - 1, axis=0)[inner, :]
        a = _silu(bcv_ref[:, cols] + qk_prev * wcv_ref[0:1, cols] + qk[inner, :] * wcv_ref[1:2, cols]
                  + qk_next * wcv_ref[2:3, cols])
        if c < bw // cw:
            q_ref[rows, cols] = a.astype(_BF16)
        else:
            for sub in range(cw // HEAD_DIM):
                head = (c * cw - bw) // HEAD_DIM + sub
                k = a[:, sub * HEAD_DIM:(sub + 1) * HEAD_DIM] * (HEAD_DIM ** -0.5)
                kt_ref[h, head] = k.T.astype(_BF16)

    def column_groups(h, hn, h_ext):
        rows = slice(h * ROW_TILE, (h + 1) * ROW_TILE)
        conv_columns(h, h_ext, 0)
        us_ref[rows, :] = _gelu_tanh(_dot(hn, w_ref[:, 4 * bw:6 * bw])).astype(_BF16)
        for g in range(N_BRANCH):
            conv_columns(h, h_ext, g + 1)
            cols = slice(6 * bw + g * d, 6 * bw + (g + 1) * d)
            gt_ref[rows, g * d:(g + 1) * d] = _sigmoid(_dot(hn, w_ref[:, cols])).astype(_BF16)
        vo_ref[rows, bw:2 * bw] = _sigmoid(_dot(hn, w_ref[:, 3 * bw:4 * bw])).astype(_BF16)
        vo_ref[rows, 0:bw] = _dot(hn, w_ref[:, 2 * bw:3 * bw]).astype(_BF16)
        at_ref[rows, :] = _dot(hn, w_ref[:, col_at:col_at + ATTN_W]).astype(_BF16)
        gate_ref[rows, :] = _dot(hn, w_ref[:, col_at + ATTN_W:col_at + ATTN_W + GATE_W])

    assert 2 * bw // cw == N_BRANCH + 1
    tiles_normed = [normalise(0), normalise(1)]
    for h, (hn, h_ext) in enumerate(tiles_normed):
        column_groups(h, hn, h_ext)


def _pair_tile(h, tiles):
    def index(n):
        k = 2 * n + h
        return k // tiles, k % tiles
    return index


def _two_source_specs(at, off, width, nctx_tiles):
    return [pl.BlockSpec((None, ROW_TILE, width), lambda n: (at(n)[0], jnp.minimum(at(n)[1] + off, nctx_tiles - 1), 0)),
            pl.BlockSpec((None, ROW_TILE, width), lambda n: (at(n)[0], jnp.maximum(at(n)[1] + off - nctx_tiles, 0), 0))]


def _stream_tile_specs(at, off, d, nctx_tiles, mod, layer):
    batch_row = mod.shape[1] - COND_PAD_ROWS
    return _two_source_specs(at, off, d, nctx_tiles) + [
        pl.BlockSpec((None, None, 1, mod.shape[3]),
                     lambda n: (layer, jnp.where(at(n)[1] + off < nctx_tiles, batch_row, at(n)[0]), 0, 0))]


def _proj(x_ctx, x_lat, mod, g_mix, w_in, w_qkconv, b_qkconv, layer, nctx_tiles):
    b, _, d = x_ctx.shape
    lat_rows = x_lat.shape[1]
    tiles = nctx_tiles + lat_rows // ROW_TILE
    bw = BRANCH_WIDTH
    widths = (bw, 2 * bw, 2 * bw, N_BRANCH * d, ATTN_W, GATE_W)
    dtypes = (_BF16, _BF16, _BF16, _BF16, _BF16, _F32)
    assert nctx_tiles == 1 and sum(widths) + bw == w_in.shape[2] and (b * tiles) % 2 == 0
    rows = b * tiles * ROW_TILE
    hb = ROW_TILE // QK_HALO

    def tile_specs(h):
        at = _pair_tile(h, tiles)
        lat_tile = lambda n: jnp.maximum(at(n)[1] - nctx_tiles, 0)
        halo = lambda block_of: pl.BlockSpec((None, QK_HALO, d), lambda n: (at(n)[0], block_of(n), 0))
        return _stream_tile_specs(at, 0, d, nctx_tiles, mod, layer) + [
            halo(lambda n: jnp.maximum(lat_tile(n) * hb - 1, 0)),
            halo(lambda n: jnp.minimum((lat_tile(n) + 1) * hb, lat_rows // QK_HALO - 1))]

    flat = lambda w: pl.BlockSpec((2 * ROW_TILE, w), lambda n: (n, 0))
    tile_args = (x_ctx, x_lat, mod, x_lat, x_lat)
    params = (g_mix, w_in, w_qkconv, b_qkconv)
    outs = pl.pallas_call(
        functools.partial(_proj_kernel, d=d, nctx_tiles=nctx_tiles, tiles=tiles),
        grid=(b * tiles // 2,),
        in_specs=tile_specs(0) + tile_specs(1) + [_layer_spec(a.shape, layer) for a in params],
        out_specs=[flat(widths[0]), pl.BlockSpec((2, HEADS, HEAD_DIM, ROW_TILE), lambda n: (n, 0, 0, 0))]
        + [flat(w) for w in widths[1:]],
        out_shape=[jax.ShapeDtypeStruct((rows, widths[0]), dtypes[0]),
                   jax.ShapeDtypeStruct((b * tiles, HEADS, HEAD_DIM, ROW_TILE), _BF16)]
        + [jax.ShapeDtypeStruct((rows, w), dt) for w, dt in zip(widths[1:], dtypes[1:])],
        compiler_params=_cparams(1),
        name="proj",
    )(*tile_args, *tile_args, *params)
    q, kt = outs[0].reshape(b, tiles * ROW_TILE, bw), outs[1]
    return [q, kt] + [o.reshape(b, tiles * ROW_TILE, o.shape[1]) for o in outs[2:]]


MLSTM_BATCH = 2
MLSTM_ELEMENT_INPUTS = 10
MLSTM_ELEMENT_SCRATCH = 17


def _mlstm_kernel(*refs):
    tc = CHUNK
    low, high = pl.ds(0, tc), pl.ds(tc, tc)
    n_in = MLSTM_BATCH * MLSTM_ELEMENT_INPUTS
    bg_ref, hf_ref, hb_ref = refs[n_in:n_in + 3]
    inputs = [refs[e * MLSTM_ELEMENT_INPUTS:(e + 1) * MLSTM_ELEMENT_INPUTS] for e in range(MLSTM_BATCH)]
    scratch = [refs[n_in + 3 + e * MLSTM_ELEMENT_SCRATCH:n_in + 3 + (e + 1) * MLSTM_ELEMENT_SCRATCH]
               for e in range(MLSTM_BATCH)]

    def scan_step(slot_of, fwd_rows, bwd_rows):
        elements = []
        for e, ((q_f, kt_f, v_f, _, _, q_b, kt_b, v_b, _, _), scr) in enumerate(zip(inputs, scratch)):
            views = [(q.at[rows, :], kt.at[:, :, rows], v.at[rows, :], out.at[e, rows, :])
                     for rows, (q, kt, v, out) in ((fwd_rows, (q_f, kt_f, v_f, hf_ref)), (bwd_rows, (q_b, kt_b, v_b, hb_ref)))]
            elements.append((slot_of(scr), views[0], views[1], scr[0], scr[1], scr[2]))
        _mlstm_scan_step(elements, is_fwd, causal)

    slot_a, slot_b = (lambda scr: scr[3:10]), (lambda scr: scr[10:17])
    lane = lax.broadcasted_iota(jnp.int32, (1, V7X_LANES), 1)
    row = lax.broadcasted_iota(jnp.int32, (tc, 1), 0)
    is_fwd = lane < HEADS
    r_idx = lax.broadcasted_iota(jnp.int32, (tc, tc), 0)
    c_idx = lax.broadcasted_iota(jnp.int32, (tc, tc), 1)
    causal = (c_idx <= r_idx, c_idx >= r_idx)

    def gate_prologue(g_f, g_b, slot):
        bcum_ref, run_ref, rt_ref, dt_ref, rows_ref, blc_ref, dmc_ref = slot
        gi, gf = slice(0, V7X_LANES), slice(V7X_LANES, GATE_W)
        log_i = jnp.where(is_fwd, g_f[:, gi], g_b[:, gi]) + bg_ref[:, gi]
        gates_f = jnp.where(is_fwd, g_f[:, gf], g_b[:, gf]) + bg_ref[:, gf]
        log_f = jnp.minimum(gates_f, 0.0) - jnp.log(1.0 + jnp.exp(-jnp.abs(gates_f)))
        cum_f, step = log_f, 1
        while step < tc:
            cum_f = cum_f + jnp.where(row >= step, pltpu.roll(cum_f, step, axis=0), 0.0)
            step *= 2
        b_last = cum_f[tc - 1:tc, :]
        bcum = jnp.where(is_fwd, cum_f, b_last - cum_f + log_f)
        r = log_i - bcum
        run_f, run_b, step = r, r, 1
        while step < tc:
            run_f = jnp.maximum(run_f, jnp.where(row >= step, pltpu.roll(run_f, step, axis=0), -jnp.inf))
            run_b = jnp.maximum(run_b, jnp.where(row < tc - step, pltpu.roll(run_b, tc - step, axis=0), -jnp.inf))
            step *= 2
        decay = b_last + r
        r_t = r.T[0:SCAN_LANES, :]
        decay_t = decay.T[0:SCAN_LANES, :]
        bcum_ref[...] = bcum
        run_ref[...] = jnp.where(is_fwd, run_f, run_b)
        rt_ref[...] = r_t
        dt_ref[...] = decay_t
        rows_ref[0:1, :] = b_last
        rows_ref[1:2, :] = jnp.max(decay, axis=0, keepdims=True)
        blc_ref[...] = decay_t - r_t
        dmc_ref[...] = jnp.broadcast_to(jnp.max(decay_t, axis=1, keepdims=True), (SCAN_LANES, tc))

    def prologues(which, fwd_rows, bwd_rows, slot_of):
        for ins, scr in zip(inputs, scratch):
            gf, gb = (ins[3], ins[8]) if which == "now" else (ins[4], ins[9])
            gate_prologue(gf.at[fwd_rows, :], gb.at[bwd_rows, :], slot_of(scr))

    @pl.when(pl.program_id(1) == 0)
    def _():
        for scr in scratch:
            for state in scr[0:3]:
                state[...] = jnp.zeros_like(state)
        prologues("now", low, high, slot_a)

    prologues("now", high, low, slot_b)
    scan_step(slot_a, low, high)
    prologues("next", low, high, slot_a)
    scan_step(slot_b, high, low)


def _mlstm_scan_step(elements, is_fwd, causal):
    tc = CHUNK
    ones = jnp.ones((tc, HEAD_DIM), _BF16)
    scans = []
    for slot, refs_f, refs_b, c_scr, m_row, m_col in elements:
        bcum_ref, run_ref, rt_ref, dt_ref, rows_ref, blc_ref, dmc_ref = slot
        b_last, decay_max = rows_ref[0:1, :], rows_ref[1:2, :]
        m_old = m_row[...]
        m_new = jnp.maximum(b_last + m_old, decay_max)
        m_col_new = jnp.maximum(blc_ref[...] + m_col[...], dmc_ref[...])
        shared = dict(bcum=bcum_ref[...], g=jnp.maximum(m_old, run_ref[...]), m_old=m_old, r_t=rt_ref[...],
                      w_carry=jnp.exp(b_last + m_old - m_new), w_src_t=jnp.exp(dt_ref[...] - m_col_new), c_scr=c_scr)
        m_row[...] = m_new
        m_col[...] = m_col_new
        for dr, (q_ref, kt_ref, v_ref, out_ref) in enumerate((refs_f, refs_b)):
            for h in range(HEADS):
                hs = slice(h * HEAD_DIM, (h + 1) * HEAD_DIM)
                scans.append(dict(shared, l=dr * HEADS + h, q=q_ref[:, hs], kt=kt_ref[h], mask=causal[dr], hs=hs,
                                  out_ref=out_ref, v_ext=jnp.concatenate([v_ref[:, hs], ones], axis=1)))
    for s in scans:
        l = s["l"]
        s["g_l"] = jnp.broadcast_to(s["g"][:, l:l + 1], (tc, tc))
        w_intra = jnp.exp(jnp.where(s["mask"], s["r_t"][l:l + 1, :] - s["g_l"], -jnp.inf))
        s["s16"] = (_dot(s["q"], s["kt"]) * w_intra).astype(_BF16)
    for s in scans:
        l = s["l"]
        c_old = s["c_scr"][l]
        s["qc"] = _dot(s["q"], c_old.astype(_BF16))
        kw_t = (s["kt"].astype(_F32) * s["w_src_t"][l:l + 1, :]).astype(_BF16)
        s["c_scr"][l] = s["w_carry"][:, l:l + 1] * c_old + _dot(kw_t, s["v_ext"])
    for s in scans:
        l = s["l"]
        w_inter = jnp.exp(s["m_old"][:, l:l + 1] - s["g_l"])
        b_l = jnp.broadcast_to(s["bcum"][:, l:l + 1], (tc, tc))
        sv = _dot(s["s16"], s["v_ext"])
        num = w_inter * s["qc"][:, 0:HEAD_DIM] + sv[:, 0:HEAD_DIM]
        den = w_inter * s["qc"][:, HEAD_DIM:] + sv[:, HEAD_DIM:]
        s["out_ref"][:, s["hs"]] = num / jnp.maximum(jnp.abs(den), jnp.exp(-(b_l + s["g_l"])))


def _mlstm(q, kt, vo, gates, b_gate, layer, nctx):
    b, t, _ = q.shape
    blk = 2 * CHUNK
    nblocks, nctx_blocks = t // blk, nctx // 2
    assert t % blk == 0 and nctx % 2 == 0 and b % MLSTM_BATCH == 0

    def bwd_block(j):
        return jnp.where(j < nctx_blocks, nctx_blocks - 1 - j, nblocks - 1 - (j - nctx_blocks))

    def specs(e, block_of):
        nxt = lambda j: block_of(jnp.minimum(j + 1, nblocks - 1))
        elem = lambda bi: MLSTM_BATCH * bi + e
        return [
            pl.BlockSpec((None, blk, BRANCH_WIDTH), lambda bi, j: (elem(bi), block_of(j), 0)),
            pl.BlockSpec((None, HEADS, HEAD_DIM, blk), lambda bi, j: (elem(bi) * nblocks + block_of(j), 0, 0, 0)),
            pl.BlockSpec((None, blk, BRANCH_WIDTH), lambda bi, j: (elem(bi), block_of(j), 0)),
            pl.BlockSpec((None, blk, GATE_W), lambda bi, j: (elem(bi), block_of(j), 0)),
            pl.BlockSpec((None, blk, GATE_W), lambda bi, j: (elem(bi), nxt(j), 0)),
        ]

    time_lane, lane_time = pltpu.VMEM((CHUNK, V7X_LANES), _F32), pltpu.VMEM((SCAN_LANES, CHUNK), _F32)
    gate_slot = [time_lane, time_lane, lane_time, lane_time, pltpu.VMEM((8, V7X_LANES), _F32), lane_time, lane_time]
    element_scratch = [pltpu.VMEM((SCAN_LANES, HEAD_DIM, 2 * HEAD_DIM), _F32), pltpu.VMEM((1, V7X_LANES), _F32),
                       lane_time] + gate_slot + gate_slot
    assert len(element_scratch) == MLSTM_ELEMENT_SCRATCH
    h_shape = jax.ShapeDtypeStruct((b, t, BRANCH_WIDTH), _F32)
    in_specs = []
    for e in range(MLSTM_BATCH):
        in_specs += specs(e, lambda j: j) + specs(e, bwd_block)
    return pl.pallas_call(
        _mlstm_kernel,
        grid=(b // MLSTM_BATCH, nblocks),
        in_specs=in_specs + [_layer_spec(b_gate.shape, layer)],
        out_specs=[pl.BlockSpec((MLSTM_BATCH, blk, BRANCH_WIDTH), lambda bi, j: (bi, j, 0)),
                   pl.BlockSpec((MLSTM_BATCH, blk, BRANCH_WIDTH), lambda bi, j: (bi, bwd_block(j), 0))],
        out_shape=[h_shape, h_shape],
        scratch_shapes=element_scratch * MLSTM_BATCH,
        compiler_params=_cparams(2),
        name="mlstm",
    )(*((q, kt, vo, gates, gates) * (2 * MLSTM_BATCH)), b_gate)


ATTN_SUB = 2
VT_ONES = V7X_BF16_SUBLANE_TILE


def _attn_kernel(cq_ctx_ref, cq_a_ref, cq_b_ref, kv_ref, ca_ref, sa_ref, gq_ref, wq_ref, gkv_ref, wk_ref, wv_ref,
                 o_ctx_ref, o_lat_ref, k_scr, vt_scr, *, ctx_len, need_ctx, scale):
    i = pl.program_id(1)
    t = kv_ref.shape[0]
    head_w = 2 * HEAD_DIM

    def rope(y, cos_t, sin_t):
        return y * cos_t + pltpu.roll(y, ROPE_DIM, axis=1) * sin_t

    def attend(cq_rows, row0, n_keys, o_ref):
        table_rows = pl.ds(pl.multiple_of(row0, ROW_TILE), cq_rows.shape[0])
        cos_q, sin_q = ca_ref[table_rows, :], sa_ref[table_rows, :]
        cq = (_rms(cq_rows.astype(_F32)) * gq_ref[...]).astype(_BF16)
        qa = _dot(cq, wq_ref[...]) * (scale * LOG2_E)

        def scores(h):
            q_nope = qa[:, h * head_w:h * head_w + HEAD_DIM]
            q_rope = rope(qa[:, h * head_w + HEAD_DIM:(h + 1) * head_w], cos_q, sin_q)
            qh = jnp.concatenate([q_nope, q_rope], axis=1).astype(_BF16)
            return _dot_nt(qh, k_scr[0:n_keys, h * head_w:(h + 1) * head_w])

        s_next = scores(0)
        for h in range(HEADS):
            s = s_next
            if h + 1 < HEADS:
                s_next = scores(h + 1)
            e = jnp.exp2(s - jnp.max(s, axis=1, keepdims=True)).astype(_BF16)
            ot = _dot_nt(vt_scr[h, :, 0:n_keys], e)
            ot = ot[0:HEAD_DIM, :] / ot[HEAD_DIM:HEAD_DIM + 1, :]
            o_ref[:, h * HEAD_DIM:(h + 1) * HEAD_DIM] = ot.T.astype(o_ref.dtype)

    @pl.when(i == 0)
    def _():
        for r in range(t // ROW_TILE):
            rows = slice(r * ROW_TILE, (r + 1) * ROW_TILE)
            ckv = (_rms(kv_ref[rows, 0:KV_LORA].astype(_F32)) * gkv_ref[...]).astype(_BF16)
            k_nope = _dot(ckv, wk_ref[...])
            k_rope = rope(kv_ref[rows, KV_LORA:KV_SRC].astype(_F32), ca_ref[rows, :], sa_ref[rows, :]).astype(_BF16)
            for h in range(HEADS):
                k_scr[rows, h * head_w:h * head_w + HEAD_DIM] = k_nope[:, h * HEAD_DIM:(h + 1) * HEAD_DIM].astype(_BF16)
                k_scr[rows, h * head_w + HEAD_DIM:(h + 1) * head_w] = k_rope
            vv = _dot(ckv, wv_ref[...])
            for h in range(HEADS):
                vt_scr[h, 0:HEAD_DIM, rows] = vv[:, h * HEAD_DIM:(h + 1) * HEAD_DIM].T.astype(_BF16)
                vt_scr[h, HEAD_DIM:, rows] = jnp.ones((VT_ONES, ROW_TILE), _BF16)
        if need_ctx:
            attend(cq_ctx_ref[...], 0, ctx_len, o_ctx_ref)
        else:
            o_ctx_ref[...] = jnp.zeros_like(o_ctx_ref)

    @pl.when(i > 0)
    def _():
        cq_rows = jnp.concatenate([cq_a_ref[...], cq_b_ref[...]], axis=0)
        attend(cq_rows, ctx_len + (i - 1) * (ATTN_SUB * ROW_TILE), t, o_lat_ref)


def _attn(at, cos_t, sin_t, g_qn, w_q, g_kvn, w_k, w_v, layer, ctx_len, need_ctx):
    b, t, _ = at.shape
    assert ctx_len == ROW_TILE and (t - ctx_len) % (ATTN_SUB * ROW_TILE) == 0
    lat_steps = (t - ctx_len) // (ATTN_SUB * ROW_TILE)
    scale = (HEAD_DIM + ROPE_DIM) ** -0.5
    lat_tile = lambda s: pl.BlockSpec((None, ROW_TILE, Q_LORA),
                                      lambda bi, i: (bi, 1 + ATTN_SUB * jnp.maximum(i - 1, 0) + s, 0))
    return pl.pallas_call(
        functools.partial(_attn_kernel, ctx_len=ctx_len, need_ctx=need_ctx, scale=scale),
        grid=(b, 1 + lat_steps),
        in_specs=[pl.BlockSpec((None, ROW_TILE, Q_LORA), lambda bi, i: (bi, 0, 0)), lat_tile(0), lat_tile(1),
                  pl.BlockSpec((None, t, KV_SRC), lambda bi, i: (bi, 0, 1)),
                  _const_spec((t, V7X_LANES)), _const_spec((t, V7X_LANES)),
                  _layer_spec(g_qn.shape, layer), _layer_spec(w_q.shape, layer), _layer_spec(g_kvn.shape, layer),
                  _layer_spec(w_k.shape, layer), _layer_spec(w_v.shape, layer)],
        out_specs=[pl.BlockSpec((None, ROW_TILE, BRANCH_WIDTH), lambda bi, i: (bi, 0, 0)),
                   pl.BlockSpec((None, ATTN_SUB * ROW_TILE, BRANCH_WIDTH), lambda bi, i: (bi, jnp.maximum(i - 1, 0), 0))],
        out_shape=[jax.ShapeDtypeStruct((b, ctx_len, BRANCH_WIDTH), _BF16),
                   jax.ShapeDtypeStruct((b, t - ctx_len, BRANCH_WIDTH), _BF16)],
        scratch_shapes=[pltpu.VMEM((t, HEADS * 2 * HEAD_DIM), _BF16),
                        pltpu.VMEM((HEADS, HEAD_DIM + VT_ONES, t), _BF16)],
        compiler_params=_cparams(2),
        name="attn",
    )(at, at, at, at, cos_t, sin_t, g_qn, w_q, g_kvn, w_k, w_v)


def _merge_mix(hf_ref, hb_ref, so_ref, yb, us_ref, gt_ref, gmh_ref, gsgu_ref, ws_ref, bs_ref, wbr_ref, wout_ref, d):
    tm = hf_ref.shape[0]
    half = d // 2
    pb = [_dot(yb, wbr_ref[1, :, c * half:(c + 1) * half]) for c in range(2)]
    hsum = hf_ref[...] + hb_ref[...]
    ya = jnp.concatenate(
        [_rms(hsum[:, h * HEAD_DIM:(h + 1) * HEAD_DIM]) for h in range(HEADS)], axis=1) * gmh_ref[...]
    ya = so_ref[...].astype(_F32) * ya
    cols = []
    for g in range(HEADS):
        gs = slice(BRANCH_WIDTH + g * HEAD_DIM, BRANCH_WIDTH + (g + 1) * HEAD_DIM)
        vn = (_rms(us_ref[:, gs].astype(_F32)) * gsgu_ref[:, g * HEAD_DIM:(g + 1) * HEAD_DIM]).astype(_BF16)
        mixed = [_dot(ws_ref[g], vn[n * CHUNK:(n + 1) * CHUNK, :]) + bs_ref[:, g:g + 1] for n in range(tm // CHUNK)]
        cols.append(jnp.concatenate(mixed, axis=0))
    yc = us_ref[:, 0:BRANCH_WIDTH].astype(_F32) * jnp.concatenate(cols, axis=1)
    ya16, yc16 = ya.astype(_BF16), yc.astype(_BF16)
    pa = [_dot(ya16, wbr_ref[0, :, c * half:(c + 1) * half]) for c in range(2)]
    pc = [_dot(yc16, wbr_ref[2, :, c * half:(c + 1) * half]) for c in range(2)]
    out = None
    for c in range(2):
        gate = lambda g: gt_ref[:, g * d + c * half:g * d + (c + 1) * half].astype(_F32)
        merged = (gate(0) * pa[c] + gate(1) * pb[c] + gate(2) * pc[c]).astype(_BF16)
        part = _dot(merged, wout_ref[c * half:(c + 1) * half, :])
        out = part if out is None else out + part
    return out


MERGE_TILE_INPUTS = 10


def _merge_kernel(*refs, d, nctx_tiles, off, tiles):
    per_tile = (refs[0:MERGE_TILE_INPUTS], refs[MERGE_TILE_INPUTS:2 * MERGE_TILE_INPUTS])
    gmh_ref, gsgu_ref, ws_ref, bs_ref, wbr_ref, wout_ref, gffn_ref, xo_ref, h2_ref = refs[2 * MERGE_TILE_INPUTS:]
    is_ctx = [(2 * pl.program_id(0) + h) % tiles + off < nctx_tiles for h in range(2)]
    outs = []
    for h, (hf_ref, hb_ref, so_ref, us_ref, gt_ref, ybc_ref, ybl_ref) in enumerate(t[0:7] for t in per_tile):
        yb = jnp.where(is_ctx[h], ybc_ref[...], ybl_ref[...])
        outs.append(_merge_mix(hf_ref, hb_ref, so_ref, yb, us_ref, gt_ref, gmh_ref, gsgu_ref, ws_ref, bs_ref, wbr_ref,
                               wout_ref, d))
    for h, tile_refs in enumerate(per_tile):
        xc_ref, xl_ref, mod_ref = tile_refs[7:10]
        rows = slice(h * ROW_TILE, (h + 1) * ROW_TILE)
        x = jnp.where(is_ctx[h], xc_ref[...], xl_ref[...])
        x_new = x + mod_ref[:, 2 * d:3 * d] * outs[h]
        xo_ref[rows, :] = x_new
        h2 = _rms(x_new) * gffn_ref[...]
        h2_ref[rows, :] = (h2 * (1.0 + mod_ref[:, 4 * d:5 * d]) + mod_ref[:, 3 * d:4 * d]).astype(h2_ref.dtype)


def _merge(hf, hb, vo, yb_ctx, yb_lat, us, gt, x_ctx, x_lat, mod, g_mhead, g_sgu, w_s, b_s_t, w_branch, w_out, g_ffn,
           layer, nctx_tiles, skip_ctx):
    b, t, _ = hf.shape
    d = x_ctx.shape[2]
    off = nctx_tiles if skip_ctx else 0
    tiles = t // ROW_TILE - off
    assert (b * tiles) % 2 == 0

    def tile_specs(h):
        at = _pair_tile(h, tiles)
        tile = lambda w, blk: pl.BlockSpec((None, ROW_TILE, w), lambda n: (at(n)[0], at(n)[1] + off, blk))
        return ([tile(BRANCH_WIDTH, 0), tile(BRANCH_WIDTH, 0), tile(BRANCH_WIDTH, 1), tile(2 * BRANCH_WIDTH, 0),
                 tile(N_BRANCH * d, 0)] + _two_source_specs(at, off, BRANCH_WIDTH, nctx_tiles)
                + _stream_tile_specs(at, off, d, nctx_tiles, mod, layer))

    tile_args = (hf, hb, vo, us, gt, yb_ctx, yb_lat, x_ctx, x_lat, mod)
    params = (g_mhead, g_sgu, w_s, b_s_t, w_branch, w_out, g_ffn)
    out_tile = lambda: pl.BlockSpec((2 * ROW_TILE, d), lambda n: (n, 0))
    rows_out = b * tiles * ROW_TILE
    x_mid, h2 = pl.pallas_call(
        functools.partial(_merge_kernel, d=d, nctx_tiles=nctx_tiles, off=off, tiles=tiles),
        grid=(b * tiles // 2,),
        in_specs=tile_specs(0) + tile_specs(1) + [_layer_spec(a.shape, layer) for a in params],
        out_specs=[out_tile(), out_tile()],
        out_shape=[jax.ShapeDtypeStruct((rows_out, d), _F32), jax.ShapeDtypeStruct((rows_out, d), _BF16)],
        compiler_params=_cparams(1),
        name="merge",
    )(*tile_args, *tile_args, *params)
    return x_mid.reshape(b, tiles * ROW_TILE, d), h2.reshape(b, tiles * ROW_TILE, d)


def _ffn_kernel(*refs, d, n_sub, groups, final_norm):
    h_refs, (hp_ref, hn_ref), x_refs = refs[0:n_sub], refs[n_sub:n_sub + 2], refs[n_sub + 2:2 * n_sub + 2]
    mod_ref, wup_ref, wcv_ref, bcv_ref, wdn_ref, gfin_ref, o_ref, ext_scr, act_scr = refs[2 * n_sub + 2:]
    tm = n_sub * ROW_TILE
    ff = wdn_ref.shape[0]
    ck = FFN_CHUNK
    group = pl.program_id(0) % groups
    ext_scr[0:FFN_HALO, :] = jnp.where(group == 0, jnp.zeros_like(hp_ref), hp_ref[...])
    for s, h_ref in enumerate(h_refs):
        ext_scr[FFN_HALO + s * ROW_TILE:FFN_HALO + (s + 1) * ROW_TILE, :] = h_ref[...]
    ext_scr[FFN_HALO + tm:, :] = jnp.where(group == groups - 1, jnp.zeros_like(hn_ref), hn_ref[...])
    ext_rows = tm + 2 * FFN_HALO
    inner = slice(FFN_HALO, FFN_HALO + tm)

    def conv(cols):
        a = _dot(ext_scr[...], wup_ref[:, cols])
        a_prev = pltpu.roll(a, 1, axis=0)[inner, :]
        a_next = pltpu.roll(a, ext_rows - 1, axis=0)[inner, :]
        return (bcv_ref[:, cols] + a_prev * wcv_ref[0:1, cols] + a[inner, :] * wcv_ref[1:2, cols]
                + a_next * wcv_ref[2:3, cols])

    for c in range(ff // ck):
        gate = conv(slice(c * ck, (c + 1) * ck))
        val = conv(slice(ff + c * ck, ff + (c + 1) * ck))
        act_scr[:, c * ck:(c + 1) * ck] = (_silu(gate) * val).astype(_BF16)

    down = _dot(act_scr[...], wdn_ref[...])
    for s, x_ref in enumerate(x_refs):
        rows = slice(s * ROW_TILE, (s + 1) * ROW_TILE)
        x_new = x_ref[...] + mod_ref[:, 5 * d:6 * d] * down[rows, :]
        if final_norm:
            x_new = _rms(x_new) * gfin_ref[...]
        o_ref[rows, :] = x_new


def _ffn(h2, x, mod, w_up, w_cv, b_cv, w_dn, g_final, layer, first_tile, seg_tiles, n_sub, is_ctx, final_norm):
    b, r, d = x.shape
    assert seg_tiles % n_sub == 0
    groups = seg_tiles // n_sub
    hb = ROW_TILE // FFN_HALO
    batch_row = mod.shape[1] - COND_PAD_ROWS
    tile0 = lambda n: first_tile + (n % groups) * n_sub
    sub_specs = [pl.BlockSpec((None, ROW_TILE, d), lambda n, s=s: (n // groups, tile0(n) + s, 0)) for s in range(n_sub)]
    return pl.pallas_call(
        functools.partial(_ffn_kernel, d=d, n_sub=n_sub, groups=groups, final_norm=final_norm),
        grid=(b * groups,),
        in_specs=sub_specs
        + [pl.BlockSpec((None, FFN_HALO, d), lambda n: (n // groups, jnp.maximum(tile0(n) * hb - 1, 0), 0)),
           pl.BlockSpec((None, FFN_HALO, d),
                        lambda n: (n // groups, jnp.minimum((tile0(n) + n_sub) * hb, r // FFN_HALO - 1), 0))]
        + sub_specs
        + [pl.BlockSpec((None, None, 1, mod.shape[3]), lambda n: (layer, batch_row if is_ctx else n // groups, 0, 0))]
        + [_layer_spec(a.shape, layer) for a in (w_up, w_cv, b_cv, w_dn)] + [_const_spec((1, d))],
        out_specs=pl.BlockSpec((None, n_sub * ROW_TILE, d), lambda n: (n // groups, n % groups, 0)),
        out_shape=jax.ShapeDtypeStruct((b, seg_tiles * ROW_TILE, d), _F32),
        scratch_shapes=[pltpu.VMEM((n_sub * ROW_TILE + 2 * FFN_HALO, d), _BF16),
                        pltpu.VMEM((n_sub * ROW_TILE, w_dn.shape[1]), _BF16)],
        compiler_params=_cparams(1),
        name="ffn",
    )(*([h2] * (n_sub + 2)), *([x] * n_sub), mod, w_up, w_cv, b_cv, w_dn, g_final.reshape(1, d))


def _deinterleave(w):
    return jnp.concatenate([w[..., 0::2], w[..., 1::2]], axis=-1)


def _rotated(w):
    return jnp.concatenate([-w[..., 1::2], w[..., 0::2]], axis=-1)


def _layout_w_in(w):
    depth, d, _ = w.shape
    w = w.astype(_BF16)
    sizes = (BRANCH_WIDTH,) * 4 + (M_GATES, Q_LORA, KV_LORA, ROPE_DIM, BRANCH_WIDTH, BRANCH_WIDTH, N_BRANCH * d)
    splits = tuple(int(s) for s in np.cumsum(sizes)[:-1])
    q, k, v, o, mg, cq, ckv, kr, u, s, gt = jnp.split(w, splits, axis=2)
    mg = mg.reshape(depth, d, 2, 2, HEADS)
    pad = jnp.zeros((depth, d, V7X_LANES - SCAN_LANES), w.dtype)
    cols = [q, k, v, o, u, s, gt, cq, ckv, _deinterleave(kr), _rotated(kr),
            mg[:, :, :, 0, :].reshape(depth, d, SCAN_LANES), pad, mg[:, :, :, 1, :].reshape(depth, d, SCAN_LANES), pad]
    return jnp.concatenate(cols, axis=2)


def _layout_gate_bias(bg):
    depth = bg.shape[0]
    bg = bg.reshape(depth, 2, 2, HEADS)
    pad = jnp.zeros((depth, V7X_LANES - SCAN_LANES), bg.dtype)
    return jnp.concatenate([bg[:, :, 0, :].reshape(depth, -1), pad, bg[:, :, 1, :].reshape(depth, -1), pad],
                           axis=1).reshape(depth, 1, GATE_W)


def _layout_w_uq(w):
    depth = w.shape[0]
    w = w.reshape(depth, Q_LORA, HEADS, HEAD_DIM + ROPE_DIM)
    nope, rope = w[..., :HEAD_DIM], w[..., HEAD_DIM:]
    return jnp.concatenate([nope, _deinterleave(rope), _rotated(rope)], axis=-1).reshape(depth, Q_LORA, -1).astype(_BF16)


def _layout_w_ukv(w):
    depth = w.shape[0]
    w = w.reshape(depth, KV_LORA, HEADS, 2 * HEAD_DIM).astype(_BF16)
    return w[..., :HEAD_DIM].reshape(depth, KV_LORA, -1), w[..., HEAD_DIM:].reshape(depth, KV_LORA, -1)


def _rope_tables(ctx_len, n_latent):
    rows = n_latent // GRID_W
    row = jnp.repeat(jnp.arange(rows), GRID_W)
    col = jnp.tile(jnp.arange(GRID_W), rows)
    n_freq = ROPE_DIM // 4
    inv = ROPE_BASE ** (-jnp.arange(n_freq, dtype=_F32) / n_freq)
    ang = jnp.concatenate([row[:, None] * inv, col[:, None] * inv], axis=-1)
    zeros = jnp.zeros((n_latent, V7X_LANES - ROPE_DIM), _F32)
    cos_l = jnp.concatenate([jnp.cos(ang), jnp.cos(ang), zeros], axis=1)
    sin_l = jnp.concatenate([jnp.sin(ang), jnp.sin(ang), zeros], axis=1)
    cos_c = jnp.concatenate([jnp.ones((ctx_len, ROPE_DIM), _F32), jnp.zeros((ctx_len, V7X_LANES - ROPE_DIM), _F32)], axis=1)
    return jnp.concatenate([cos_c, cos_l], axis=0), jnp.concatenate([jnp.zeros_like(cos_c), sin_l], axis=0)


def kernel(x, c, ctx, c_ctx, w_ada, b_ada, g_mix, w_in, w_qkconv, b_qkconv, b_mgate, g_mhead, g_qnorm, w_uq,
           g_kvnorm, w_ukv, g_sgu, w_s, b_s, w_branch, w_out, g_ffn, w_up, w_ffconv, b_ffconv, w_down, g_final):
    b, s, d = x.shape
    ctx_len = ctx.shape[1]
    depth = w_in.shape[0]
    assert ctx_len % ROW_TILE == 0 and s % ROW_TILE == 0 and s % GRID_W == 0
    nctx_tiles = ctx_len // ROW_TILE
    row_param = lambda a: a.reshape(depth, 1, a.shape[-1])

    cos_t, sin_t = _rope_tables(ctx_len, s)
    cond_rows = jnp.concatenate([c, c_ctx[None, :], jnp.zeros((COND_PAD_ROWS - 1, d), c.dtype)], axis=0)
    mod = _ada(cond_rows, w_ada, b_ada)
    w_in_l = _layout_w_in(w_in)
    b_gate_l = _layout_gate_bias(b_mgate)
    w_uq_l = _layout_w_uq(w_uq)
    w_k_l, w_v_l = _layout_w_ukv(w_ukv)
    b_s_t = jnp.pad(jnp.swapaxes(b_s, 1, 2), ((0, 0), (0, 0), (0, V7X_LANES - HEADS)))
    w_s16, w_branch16, w_out16 = w_s.astype(_BF16), w_branch.astype(_BF16), w_out.astype(_BF16)
    w_up16, w_down16 = w_up.astype(_BF16), w_down.astype(_BF16)
    g_mix_r, g_mhead_r, g_sgu_r, g_ffn_r = row_param(g_mix), row_param(g_mhead), row_param(g_sgu), row_param(g_ffn)
    g_qn_r, g_kvn_r, b_qkconv_r, b_ffconv_r = row_param(g_qnorm), row_param(g_kvnorm), row_param(b_qkconv), row_param(b_ffconv)

    x_ctx, x_lat = ctx, x
    for l in range(depth):
        last = l == depth - 1
        q_act, kt_act, vo, us, gt, at, gates = _proj(x_ctx, x_lat, mod, g_mix_r, w_in_l, w_qkconv, b_qkconv_r, l,
                                                     nctx_tiles)
        hf, hb = _mlstm(q_act, kt_act, vo, gates, b_gate_l, l, ctx_len // CHUNK)
        yb_ctx, yb_lat = _attn(at, cos_t, sin_t, g_qn_r, w_uq_l, g_kvn_r, w_k_l, w_v_l, l, ctx_len, not last)
        x_mid, h2 = _merge(hf, hb, vo, yb_ctx, yb_lat, us, gt, x_ctx, x_lat, mod, g_mhead_r, g_sgu_r, w_s16, b_s_t,
                           w_branch16, w_out16, g_ffn_r, l, nctx_tiles, last)
        ffn = functools.partial(_ffn, h2, x_mid, mod, w_up16, w_ffconv, b_ffconv_r, w_down16, g_final, l)
        if last:
            return ffn(0, s // ROW_TILE, 2, False, True)
        x_lat = ffn(nctx_tiles, s // ROW_TILE, 2, False, False)
        x_ctx = ffn(0, nctx_tiles, 1, True, False)
```

```python
import functools

import jax
import jax.numpy as jnp
import numpy as np
from jax import lax
from jax.experimental import pallas as pl
from jax.experimental.pallas import tpu as pltpu

EPS = 1e-6
GRID_W = 64
ROPE_BASE = 10000.0
LOG2_E = 1.4426950408889634

HEADS = 4
HEAD_DIM = 128
CHUNK = 128
BRANCH_WIDTH = HEADS * HEAD_DIM
Q_LORA = 384
KV_LORA = 256
ROPE_DIM = 64
N_BRANCH = 3
M_GATES = 4 * HEADS
KV_SRC = KV_LORA + 2 * ROPE_DIM
ATTN_W = Q_LORA + KV_SRC
SCAN_LANES = 2 * HEADS

V7X_LANES = 128
V7X_BF16_SUBLANE_TILE = 16
V7X_MXU_WIDTH = 256
V7X_VMEM_BYTES = 64 * 1024 * 1024
V7X_VMEM_LIMIT = V7X_VMEM_BYTES * 7 // 8

GATE_W = 2 * V7X_LANES
COND_PAD_ROWS = 8
ROW_TILE = V7X_MXU_WIDTH
FFN_HALO = V7X_BF16_SUBLANE_TILE
FFN_CHUNK = V7X_MXU_WIDTH
FFN_SUB = 4

_BF16 = jnp.bfloat16
_F32 = jnp.float32


def _cparams(n_axes):
    return pltpu.CompilerParams(dimension_semantics=("arbitrary",) * n_axes, vmem_limit_bytes=V7X_VMEM_LIMIT)


def _const_spec(shape):
    return pl.BlockSpec(tuple(shape), lambda *_: (0,) * len(shape))


def _layer_spec(stacked_shape, layer):
    shape = tuple(stacked_shape[1:])
    return pl.BlockSpec((None,) + shape, lambda *_: (layer,) + (0,) * len(shape), pipeline_mode=pl.Buffered(1))


def _sigmoid(x):
    return 1.0 / (1.0 + jnp.exp(-x))


def _silu(x):
    return x * _sigmoid(x)


def _gelu_tanh(x):
    return x * (0.5 * (1.0 + jnp.tanh(0.7978845608028654 * (x + 0.044715 * (x * x * x)))))


def _rms(x):
    return x * lax.rsqrt(jnp.mean(x * x, axis=-1, keepdims=True) + EPS)


def _dot(a, b):
    return jnp.dot(a, b, preferred_element_type=_F32)


def _dot_nt(a, b):
    return lax.dot_general(a, b, (((1,), (1,)), ((), ())), preferred_element_type=_F32)


def _ada_kernel(c_ref, w_ref, b_ref, o_ref):
    cond = _silu(c_ref[...])
    o_ref[...] = jnp.dot(cond, w_ref[...], precision=lax.Precision.HIGHEST, preferred_element_type=_F32) + b_ref[...]


def _ada(cond_rows, w_ada, b_ada):
    rows, d = cond_rows.shape
    depth, _, n = w_ada.shape
    out = pl.pallas_call(
        _ada_kernel,
        grid=(depth, n // d),
        in_specs=[pl.BlockSpec((rows, d), lambda l, j: (0, 0)),
                  pl.BlockSpec((None, d, d), lambda l, j: (l, 0, j)),
                  pl.BlockSpec((None, 1, d), lambda l, j: (l, 0, j))],
        out_specs=pl.BlockSpec((None, rows, d), lambda l, j: (l, 0, j)),
        out_shape=jax.ShapeDtypeStruct((depth, rows, n), _F32),
        compiler_params=_cparams(2),
        name="ada",
    )(cond_rows, w_ada, b_ada.reshape(depth, 1, n))
    return out.reshape(depth, rows, 1, n)


PROJ_TILE_INPUTS = 5
QK_HALO = V7X_BF16_SUBLANE_TILE


def _proj_kernel(*refs, d, nctx_tiles, tiles):
    per_tile = (refs[0:PROJ_TILE_INPUTS], refs[PROJ_TILE_INPUTS:2 * PROJ_TILE_INPUTS])
    g_ref, w_ref, wcv_ref, bcv_ref, q_ref, kt_ref, vo_ref, us_ref, gt_ref, at_ref, gate_ref = refs[2 * PROJ_TILE_INPUTS:]
    bw = BRANCH_WIDTH
    col_at = 6 * bw + N_BRANCH * d
    ext_rows = ROW_TILE + 2 * QK_HALO
    inner = slice(QK_HALO, QK_HALO + ROW_TILE)

    def normalise(h):
        xc_ref, xl_ref, mod_ref, xp_ref, xn_ref = per_tile[h]
        tile = (2 * pl.program_id(0) + h) % tiles
        norm = lambda x: (_rms(x) * g_ref[...] * (1.0 + mod_ref[:, d:2 * d]) + mod_ref[:, 0:d]).astype(_BF16)
        hn = norm(jnp.where(tile < nctx_tiles, xc_ref[...], xl_ref[...]))
        zeros = jnp.zeros((QK_HALO, d), _BF16)
        h_prev = jnp.where(tile > nctx_tiles, norm(xp_ref[...]), zeros)
        h_next = jnp.where(jnp.logical_and(tile >= nctx_tiles, tile < tiles - 1), norm(xn_ref[...]), zeros)
        return hn, jnp.concatenate([h_prev, hn, h_next], axis=0)

    cw = V7X_MXU_WIDTH

    def conv_columns(h, h_ext, c):
        rows = slice(h * ROW_TILE, (h + 1) * ROW_TILE)
        cols = slice(c * cw, (c + 1) * cw)
        qk = _dot(h_ext, w_ref[:, cols])
        qk_prev = pltpu.roll(qk, 1, axis=0)[inner, :]
        qk_next = pltpu.roll(qk, ext_rows - 1, axis=0)[inner, :]
        a = _silu(bcv_ref[:, cols] + qk_prev * wcv_ref[0:1, cols] + qk[inner, :] * wcv_ref[1:2, cols]
                  + qk_next * wcv_ref[2:3, cols])
        if c < bw // cw:
            q_ref[rows, cols] = a.astype(_BF16)
        else:
            for sub in range(cw // HEAD_DIM):
                head = (c * cw - bw) // HEAD_DIM + sub
                k = a[:, sub * HEAD_DIM:(sub + 1) * HEAD_DIM] * (HEAD_DIM ** -0.5)
                kt_ref[h, head] = k.T.astype(_BF16)

    def column_groups(h, hn, h_ext):
        rows = slice(h * ROW_TILE, (h + 1) * ROW_TILE)
        conv_columns(h, h_ext, 0)
        us_ref[rows, :] = _gelu_tanh(_dot(hn, w_ref[:, 4 * bw:6 * bw])).astype(_BF16)
        for g in range(N_BRANCH):
            conv_columns(h, h_ext, g + 1)
            cols = slice(6 * bw + g * d, 6 * bw + (g + 1) * d)
            gt_ref[rows, g * d:(g + 1) * d] = _sigmoid(_dot(hn, w_ref[:, cols])).astype(_BF16)
        vo_ref[rows, bw:2 * bw] = _sigmoid(_dot(hn, w_ref[:, 3 * bw:4 * bw])).astype(_BF16)
        vo_ref[rows, 0:bw] = _dot(hn, w_ref[:, 2 * bw:3 * bw]).astype(_BF16)
        at_ref[rows, :] = _dot(hn, w_ref[:, col_at:col_at + ATTN_W]).astype(_BF16)
        gate_ref[rows, :] = _dot(hn, w_ref[:, col_at + ATTN_W:col_at + ATTN_W + GATE_W])

    assert 2 * bw // cw == N_BRANCH + 1
    tiles_normed = [normalise(0), normalise(1)]
    for h, (hn, h_ext) in enumerate(tiles_normed):
        column_groups(h, hn, h_ext)


def _pair_tile(h, tiles):
    def index(n):
        k = 2 * n + h
        return k // tiles, k % tiles
    return index


def _two_source_specs(at, off, width, nctx_tiles):
    return [pl.BlockSpec((None, ROW_TILE, width), lambda n: (at(n)[0], jnp.minimum(at(n)[1] + off, nctx_tiles - 1), 0)),
            pl.BlockSpec((None, ROW_TILE, width), lambda n: (at(n)[0], jnp.maximum(at(n)[1] + off - nctx_tiles, 0), 0))]


def _stream_tile_specs(at, off, d, nctx_tiles, mod, layer):
    batch_row = mod.shape[1] - COND_PAD_ROWS
    return _two_source_specs(at, off, d, nctx_tiles) + [
        pl.BlockSpec((None, None, 1, mod.shape[3]),
                     lambda n: (layer, jnp.where(at(n)[1] + off < nctx_tiles, batch_row, at(n)[0]), 0, 0))]


def _proj(x_ctx, x_lat, mod, g_mix, w_in, w_qkconv, b_qkconv, layer, nctx_tiles):
    b, _, d = x_ctx.shape
    lat_rows = x_lat.shape[1]
    tiles = nctx_tiles + lat_rows // ROW_TILE
    bw = BRANCH_WIDTH
    widths = (bw, 2 * bw, 2 * bw, N_BRANCH * d, ATTN_W, GATE_W)
    dtypes = (_BF16, _BF16, _BF16, _BF16, _BF16, _F32)
    assert nctx_tiles == 1 and sum(widths) + bw == w_in.shape[2] and (b * tiles) % 2 == 0
    rows = b * tiles * ROW_TILE
    hb = ROW_TILE // QK_HALO

    def tile_specs(h):
        at = _pair_tile(h, tiles)
        lat_tile = lambda n: jnp.maximum(at(n)[1] - nctx_tiles, 0)
        halo = lambda block_of: pl.BlockSpec((None, QK_HALO, d), lambda n: (at(n)[0], block_of(n), 0))
        return _stream_tile_specs(at, 0, d, nctx_tiles, mod, layer) + [
            halo(lambda n: jnp.maximum(lat_tile(n) * hb - 1, 0)),
            halo(lambda n: jnp.minimum((lat_tile(n) + 1) * hb, lat_rows // QK_HALO - 1))]

    flat = lambda w: pl.BlockSpec((2 * ROW_TILE, w), lambda n: (n, 0))
    tile_args = (x_ctx, x_lat, mod, x_lat, x_lat)
    params = (g_mix, w_in, w_qkconv, b_qkconv)
    outs = pl.pallas_call(
        functools.partial(_proj_kernel, d=d, nctx_tiles=nctx_tiles, tiles=tiles),
        grid=(b * tiles // 2,),
        in_specs=tile_specs(0) + tile_specs(1) + [_layer_spec(a.shape, layer) for a in params],
        out_specs=[flat(widths[0]), pl.BlockSpec((2, HEADS, HEAD_DIM, ROW_TILE), lambda n: (n, 0, 0, 0))]
        + [flat(w) for w in widths[1:]],
        out_shape=[jax.ShapeDtypeStruct((rows, widths[0]), dtypes[0]),
                   jax.ShapeDtypeStruct((b * tiles, HEADS, HEAD_DIM, ROW_TILE), _BF16)]
        + [jax.ShapeDtypeStruct((rows, w), dt) for w, dt in zip(widths[1:], dtypes[1:])],
        compiler_params=_cparams(1),
        name="proj",
    )(*tile_args, *tile_args, *params)
    q, kt = outs[0].reshape(b, tiles * ROW_TILE, bw), outs[1]
    return [q, kt] + [o.reshape(b, tiles * ROW_TILE, o.shape[1]) for o in outs[2:]]


def _mlstm_kernel(q_f, kt_f, v_f, gf_cur, gf_nxt, q_b, kt_b, v_b, gb_cur, gb_nxt, bg_ref, hf_ref, hb_ref,
                  c_scr, m_row, m_col, *pre):
    tc = CHUNK
    low, high = pl.ds(0, tc), pl.ds(tc, tc)

    def scan_step(slot, fwd_rows, bwd_rows):
        views = [(q.at[rows, :], kt.at[:, :, rows], v.at[rows, :], out.at[rows, :])
                 for rows, (q, kt, v, out) in ((fwd_rows, (q_f, kt_f, v_f, hf_ref)), (bwd_rows, (q_b, kt_b, v_b, hb_ref)))]
        _mlstm_scan_step(slot, is_fwd, causal, views[0], views[1], c_scr, m_row, m_col)

    pre_a, pre_b = pre[:len(pre) // 2], pre[len(pre) // 2:]
    lane = lax.broadcasted_iota(jnp.int32, (1, V7X_LANES), 1)
    row = lax.broadcasted_iota(jnp.int32, (tc, 1), 0)
    is_fwd = lane < HEADS
    r_idx = lax.broadcasted_iota(jnp.int32, (tc, tc), 0)
    c_idx = lax.broadcasted_iota(jnp.int32, (tc, tc), 1)
    causal = (c_idx <= r_idx, c_idx >= r_idx)

    def gate_prologue(g_f, g_b, slot):
        bcum_ref, run_ref, rt_ref, dt_ref, rows_ref, blc_ref, dmc_ref = slot
        gi, gf = slice(0, V7X_LANES), slice(V7X_LANES, GATE_W)
        log_i = jnp.where(is_fwd, g_f[:, gi], g_b[:, gi]) + bg_ref[:, gi]
        gates_f = jnp.where(is_fwd, g_f[:, gf], g_b[:, gf]) + bg_ref[:, gf]
        log_f = jnp.minimum(gates_f, 0.0) - jnp.log(1.0 + jnp.exp(-jnp.abs(gates_f)))
        cum_f, step = log_f, 1
        while step < tc:
            cum_f = cum_f + jnp.where(row >= step, pltpu.roll(cum_f, step, axis=0), 0.0)
            step *= 2
        b_last = cum_f[tc - 1:tc, :]
        bcum = jnp.where(is_fwd, cum_f, b_last - cum_f + log_f)
        r = log_i - bcum
        run_f, run_b, step = r, r, 1
        while step < tc:
            run_f = jnp.maximum(run_f, jnp.where(row >= step, pltpu.roll(run_f, step, axis=0), -jnp.inf))
            run_b = jnp.maximum(run_b, jnp.where(row < tc - step, pltpu.roll(run_b, tc - step, axis=0), -jnp.inf))
            step *= 2
        decay = b_last + r
        r_t = r.T[0:SCAN_LANES, :]
        decay_t = decay.T[0:SCAN_LANES, :]
        bcum_ref[...] = bcum
        run_ref[...] = jnp.where(is_fwd, run_f, run_b)
        rt_ref[...] = r_t
        dt_ref[...] = decay_t
        rows_ref[0:1, :] = b_last
        rows_ref[1:2, :] = jnp.max(decay, axis=0, keepdims=True)
        blc_ref[...] = decay_t - r_t
        dmc_ref[...] = jnp.broadcast_to(jnp.max(decay_t, axis=1, keepdims=True), (SCAN_LANES, tc))

    @pl.when(pl.program_id(1) == 0)
    def _():
        c_scr[...] = jnp.zeros_like(c_scr)
        m_row[...] = jnp.zeros_like(m_row)
        m_col[...] = jnp.zeros_like(m_col)
        gate_prologue(gf_cur.at[low, :], gb_cur.at[high, :], pre_a)

    gate_prologue(gf_cur.at[high, :], gb_cur.at[low, :], pre_b)
    scan_step(pre_a, low, high)
    gate_prologue(gf_nxt.at[low, :], gb_nxt.at[high, :], pre_a)
    scan_step(pre_b, high, low)


def _mlstm_scan_step(slot, is_fwd, causal, refs_f, refs_b, c_scr, m_row, m_col):
    tc = CHUNK
    bcum_ref, run_ref, rt_ref, dt_ref, rows_ref, blc_ref, dmc_ref = slot
    bcum = bcum_ref[...]
    b_last, decay_max = rows_ref[0:1, :], rows_ref[1:2, :]
    m_old = m_row[...]
    g = jnp.maximum(m_old, run_ref[...])
    m_new = jnp.maximum(b_last + m_old, decay_max)
    w_carry = jnp.exp(b_last + m_old - m_new)
    m_col_new = jnp.maximum(blc_ref[...] + m_col[...], dmc_ref[...])
    w_src_t = jnp.exp(dt_ref[...] - m_col_new)
    r_t = rt_ref[...]
    ones = jnp.ones((tc, HEAD_DIM), _BF16)

    scans = [(dr * HEADS + h, refs, h, slice(h * HEAD_DIM, (h + 1) * HEAD_DIM), causal[dr])
             for dr, refs in enumerate((refs_f, refs_b)) for h in range(HEADS)]
    v_ext = [jnp.concatenate([refs[2][:, hs], ones], axis=1) for _, refs, _, hs, _ in scans]
    s16, g_b = [], []
    for l, (q_ref, kt_ref, _, _), h, hs, mask in scans:
        g_l = jnp.broadcast_to(g[:, l:l + 1], (tc, tc))
        w_intra = jnp.exp(jnp.where(mask, r_t[l:l + 1, :] - g_l, -jnp.inf))
        s16.append((_dot(q_ref[:, hs], kt_ref[h]) * w_intra).astype(_BF16))
        g_b.append(g_l)
    qc = []
    for l, (q_ref, kt_ref, _, _), h, hs, _ in scans:
        c_old = c_scr[l]
        qc.append(_dot(q_ref[:, hs], c_old.astype(_BF16)))
        kw_t = (kt_ref[h].astype(_F32) * w_src_t[l:l + 1, :]).astype(_BF16)
        c_scr[l] = w_carry[:, l:l + 1] * c_old + _dot(kw_t, v_ext[l])
    for l, (_, _, _, out_ref), h, hs, _ in scans:
        w_inter = jnp.exp(m_old[:, l:l + 1] - g_b[l])
        b_l = jnp.broadcast_to(bcum[:, l:l + 1], (tc, tc))
        sv = _dot(s16[l], v_ext[l])
        num = w_inter * qc[l][:, 0:HEAD_DIM] + sv[:, 0:HEAD_DIM]
        den = w_inter * qc[l][:, HEAD_DIM:] + sv[:, HEAD_DIM:]
        out_ref[:, hs] = num / jnp.maximum(jnp.abs(den), jnp.exp(-(b_l + g_b[l])))
    m_row[...] = m_new
    m_col[...] = m_col_new


def _mlstm(q, kt, vo, gates, b_gate, layer, nctx):
    b, t, _ = q.shape
    blk = 2 * CHUNK
    nblocks, nctx_blocks = t // blk, nctx // 2
    assert t % blk == 0 and nctx % 2 == 0

    def bwd_block(j):
        return jnp.where(j < nctx_blocks, nctx_blocks - 1 - j, nblocks - 1 - (j - nctx_blocks))

    def specs(block_of):
        nxt = lambda j: block_of(jnp.minimum(j + 1, nblocks - 1))
        return [
            pl.BlockSpec((None, blk, BRANCH_WIDTH), lambda bi, j: (bi, block_of(j), 0)),
            pl.BlockSpec((None, HEADS, HEAD_DIM, blk), lambda bi, j: (bi * nblocks + block_of(j), 0, 0, 0)),
            pl.BlockSpec((None, blk, BRANCH_WIDTH), lambda bi, j: (bi, block_of(j), 0)),
            pl.BlockSpec((None, blk, GATE_W), lambda bi, j: (bi, block_of(j), 0)),
            pl.BlockSpec((None, blk, GATE_W), lambda bi, j: (bi, nxt(j), 0)),
        ]

    time_lane, lane_time = pltpu.VMEM((CHUNK, V7X_LANES), _F32), pltpu.VMEM((SCAN_LANES, CHUNK), _F32)
    gate_slot = [time_lane, time_lane, lane_time, lane_time, pltpu.VMEM((8, V7X_LANES), _F32), lane_time, lane_time]
    h_shape = jax.ShapeDtypeStruct((b, t, BRANCH_WIDTH), _F32)
    return pl.pallas_call(
        _mlstm_kernel,
        grid=(b, nblocks),
        in_specs=specs(lambda j: j) + specs(bwd_block) + [_layer_spec(b_gate.shape, layer)],
        out_specs=[pl.BlockSpec((None, blk, BRANCH_WIDTH), lambda bi, j: (bi, j, 0)),
                   pl.BlockSpec((None, blk, BRANCH_WIDTH), lambda bi, j: (bi, bwd_block(j), 0))],
        out_shape=[h_shape, h_shape],
        scratch_shapes=[pltpu.VMEM((SCAN_LANES, HEAD_DIM, 2 * HEAD_DIM), _F32), pltpu.VMEM((1, V7X_LANES), _F32),
                        lane_time] + gate_slot + gate_slot,
        compiler_params=_cparams(2),
        name="mlstm",
    )(q, kt, vo, gates, gates, q, kt, vo, gates, gates, b_gate)


ATTN_SUB = 2
VT_ONES = V7X_BF16_SUBLANE_TILE


def _attn_kernel(cq_ctx_ref, cq_a_ref, cq_b_ref, kv_ref, ca_ref, sa_ref, gq_ref, wq_ref, gkv_ref, wk_ref, wv_ref,
                 o_ctx_ref, o_lat_ref, k_scr, vt_scr, *, ctx_len, need_ctx, scale):
    i = pl.program_id(1)
    t = kv_ref.shape[0]
    head_w = 2 * HEAD_DIM

    def rope(y, cos_t, sin_t):
        return y * cos_t + pltpu.roll(y, ROPE_DIM, axis=1) * sin_t

    def attend(cq_rows, row0, n_keys, o_ref):
        table_rows = pl.ds(pl.multiple_of(row0, ROW_TILE), cq_rows.shape[0])
        cos_q, sin_q = ca_ref[table_rows, :], sa_ref[table_rows, :]
        cq = (_rms(cq_rows.astype(_F32)) * gq_ref[...]).astype(_BF16)
        qa = _dot(cq, wq_ref[...]) * (scale * LOG2_E)

        def scores(h):
            q_nope = qa[:, h * head_w:h * head_w + HEAD_DIM]
            q_rope = rope(qa[:, h * head_w + HEAD_DIM:(h + 1) * head_w], cos_q, sin_q)
            qh = jnp.concatenate([q_nope, q_rope], axis=1).astype(_BF16)
            return _dot_nt(qh, k_scr[0:n_keys, h * head_w:(h + 1) * head_w])

        s_next = scores(0)
        for h in range(HEADS):
            s = s_next
            if h + 1 < HEADS:
                s_next = scores(h + 1)
            e = jnp.exp2(s - jnp.max(s, axis=1, keepdims=True)).astype(_BF16)
            ot = _dot_nt(vt_scr[h, :, 0:n_keys], e)
            ot = ot[0:HEAD_DIM, :] / ot[HEAD_DIM:HEAD_DIM + 1, :]
            o_ref[:, h * HEAD_DIM:(h + 1) * HEAD_DIM] = ot.T.astype(o_ref.dtype)

    @pl.when(i == 0)
    def _():
        for r in range(t // ROW_TILE):
            rows = slice(r * ROW_TILE, (r + 1) * ROW_TILE)
            ckv = (_rms(kv_ref[rows, 0:KV_LORA].astype(_F32)) * gkv_ref[...]).astype(_BF16)
            k_nope = _dot(ckv, wk_ref[...])
            k_rope = rope(kv_ref[rows, KV_LORA:KV_SRC].astype(_F32), ca_ref[rows, :], sa_ref[rows, :]).astype(_BF16)
            for h in range(HEADS):
                k_scr[rows, h * head_w:h * head_w + HEAD_DIM] = k_nope[:, h * HEAD_DIM:(h + 1) * HEAD_DIM].astype(_BF16)
                k_scr[rows, h * head_w + HEAD_DIM:(h + 1) * head_w] = k_rope
            vv = _dot(ckv, wv_ref[...])
            for h in range(HEADS):
                vt_scr[h, 0:HEAD_DIM, rows] = vv[:, h * HEAD_DIM:(h + 1) * HEAD_DIM].T.astype(_BF16)
                vt_scr[h, HEAD_DIM:, rows] = jnp.ones((VT_ONES, ROW_TILE), _BF16)
        if need_ctx:
            attend(cq_ctx_ref[...], 0, ctx_len, o_ctx_ref)
        else:
            o_ctx_ref[...] = jnp.zeros_like(o_ctx_ref)

    @pl.when(i > 0)
    def _():
        cq_rows = jnp.concatenate([cq_a_ref[...], cq_b_ref[...]], axis=0)
        attend(cq_rows, ctx_len + (i - 1) * (ATTN_SUB * ROW_TILE), t, o_lat_ref)


def _attn(at, cos_t, sin_t, g_qn, w_q, g_kvn, w_k, w_v, layer, ctx_len, need_ctx):
    b, t, _ = at.shape
    assert ctx_len == ROW_TILE and (t - ctx_len) % (ATTN_SUB * ROW_TILE) == 0
    lat_steps = (t - ctx_len) // (ATTN_SUB * ROW_TILE)
    scale = (HEAD_DIM + ROPE_DIM) ** -0.5
    lat_tile = lambda s: pl.BlockSpec((None, ROW_TILE, Q_LORA),
                                      lambda bi, i: (bi, 1 + ATTN_SUB * jnp.maximum(i - 1, 0) + s, 0))
    return pl.pallas_call(
        functools.partial(_attn_kernel, ctx_len=ctx_len, need_ctx=need_ctx, scale=scale),
        grid=(b, 1 + lat_steps),
        in_specs=[pl.BlockSpec((None, ROW_TILE, Q_LORA), lambda bi, i: (bi, 0, 0)), lat_tile(0), lat_tile(1),
                  pl.BlockSpec((None, t, KV_SRC), lambda bi, i: (bi, 0, 1)),
                  _const_spec((t, V7X_LANES)), _const_spec((t, V7X_LANES)),
                  _layer_spec(g_qn.shape, layer), _layer_spec(w_q.shape, layer), _layer_spec(g_kvn.shape, layer),
                  _layer_spec(w_k.shape, layer), _layer_spec(w_v.shape, layer)],
        out_specs=[pl.BlockSpec((None, ROW_TILE, BRANCH_WIDTH), lambda bi, i: (bi, 0, 0)),
                   pl.BlockSpec((None, ATTN_SUB * ROW_TILE, BRANCH_WIDTH), lambda bi, i: (bi, jnp.maximum(i - 1, 0), 0))],
        out_shape=[jax.ShapeDtypeStruct((b, ctx_len, BRANCH_WIDTH), _BF16),
                   jax.ShapeDtypeStruct((b, t - ctx_len, BRANCH_WIDTH), _BF16)],
        scratch_shapes=[pltpu.VMEM((t, HEADS * 2 * HEAD_DIM), _BF16),
                        pltpu.VMEM((HEADS, HEAD_DIM + VT_ONES, t), _BF16)],
        compiler_params=_cparams(2),
        name="attn",
    )(at, at, at, at, cos_t, sin_t, g_qn, w_q, g_kvn, w_k, w_v)


def _merge_mix(hf_ref, hb_ref, so_ref, yb, us_ref, gt_ref, gmh_ref, gsgu_ref, ws_ref, bs_ref, wbr_ref, wout_ref, d):
    tm = hf_ref.shape[0]
    half = d // 2
    pb = [_dot(yb, wbr_ref[1, :, c * half:(c + 1) * half]) for c in range(2)]
    hsum = hf_ref[...] + hb_ref[...]
    ya = jnp.concatenate(
        [_rms(hsum[:, h * HEAD_DIM:(h + 1) * HEAD_DIM]) for h in range(HEADS)], axis=1) * gmh_ref[...]
    ya = so_ref[...].astype(_F32) * ya
    cols = []
    for g in range(HEADS):
        gs = slice(BRANCH_WIDTH + g * HEAD_DIM, BRANCH_WIDTH + (g + 1) * HEAD_DIM)
        vn = (_rms(us_ref[:, gs].astype(_F32)) * gsgu_ref[:, g * HEAD_DIM:(g + 1) * HEAD_DIM]).astype(_BF16)
        mixed = [_dot(ws_ref[g], vn[n * CHUNK:(n + 1) * CHUNK, :]) + bs_ref[:, g:g + 1] for n in range(tm // CHUNK)]
        cols.append(jnp.concatenate(mixed, axis=0))
    yc = us_ref[:, 0:BRANCH_WIDTH].astype(_F32) * jnp.concatenate(cols, axis=1)
    ya16, yc16 = ya.astype(_BF16), yc.astype(_BF16)
    pa = [_dot(ya16, wbr_ref[0, :, c * half:(c + 1) * half]) for c in range(2)]
    pc = [_dot(yc16, wbr_ref[2, :, c * half:(c + 1) * half]) for c in range(2)]
    out = None
    for c in range(2):
        gate = lambda g: gt_ref[:, g * d + c * half:g * d + (c + 1) * half].astype(_F32)
        merged = (gate(0) * pa[c] + gate(1) * pb[c] + gate(2) * pc[c]).astype(_BF16)
        part = _dot(merged, wout_ref[c * half:(c + 1) * half, :])
        out = part if out is None else out + part
    return out


MERGE_TILE_INPUTS = 10


def _merge_kernel(*refs, d, nctx_tiles, off, tiles):
    per_tile = (refs[0:MERGE_TILE_INPUTS], refs[MERGE_TILE_INPUTS:2 * MERGE_TILE_INPUTS])
    gmh_ref, gsgu_ref, ws_ref, bs_ref, wbr_ref, wout_ref, gffn_ref, xo_ref, h2_ref = refs[2 * MERGE_TILE_INPUTS:]
    is_ctx = [(2 * pl.program_id(0) + h) % tiles + off < nctx_tiles for h in range(2)]
    outs = []
    for h, (hf_ref, hb_ref, so_ref, us_ref, gt_ref, ybc_ref, ybl_ref) in enumerate(t[0:7] for t in per_tile):
        yb = jnp.where(is_ctx[h], ybc_ref[...], ybl_ref[...])
        outs.append(_merge_mix(hf_ref, hb_ref, so_ref, yb, us_ref, gt_ref, gmh_ref, gsgu_ref, ws_ref, bs_ref, wbr_ref,
                               wout_ref, d))
    for h, tile_refs in enumerate(per_tile):
        xc_ref, xl_ref, mod_ref = tile_refs[7:10]
        rows = slice(h * ROW_TILE, (h + 1) * ROW_TILE)
        x = jnp.where(is_ctx[h], xc_ref[...], xl_ref[...])
        x_new = x + mod_ref[:, 2 * d:3 * d] * outs[h]
        xo_ref[rows, :] = x_new
        h2 = _rms(x_new) * gffn_ref[...]
        h2_ref[rows, :] = (h2 * (1.0 + mod_ref[:, 4 * d:5 * d]) + mod_ref[:, 3 * d:4 * d]).astype(h2_ref.dtype)


def _merge(hf, hb, vo, yb_ctx, yb_lat, us, gt, x_ctx, x_lat, mod, g_mhead, g_sgu, w_s, b_s_t, w_branch, w_out, g_ffn,
           layer, nctx_tiles, skip_ctx):
    b, t, _ = hf.shape
    d = x_ctx.shape[2]
    off = nctx_tiles if skip_ctx else 0
    tiles = t // ROW_TILE - off
    assert (b * tiles) % 2 == 0

    def tile_specs(h):
        at = _pair_tile(h, tiles)
        tile = lambda w, blk: pl.BlockSpec((None, ROW_TILE, w), lambda n: (at(n)[0], at(n)[1] + off, blk))
        return ([tile(BRANCH_WIDTH, 0), tile(BRANCH_WIDTH, 0), tile(BRANCH_WIDTH, 1), tile(2 * BRANCH_WIDTH, 0),
                 tile(N_BRANCH * d, 0)] + _two_source_specs(at, off, BRANCH_WIDTH, nctx_tiles)
                + _stream_tile_specs(at, off, d, nctx_tiles, mod, layer))

    tile_args = (hf, hb, vo, us, gt, yb_ctx, yb_lat, x_ctx, x_lat, mod)
    params = (g_mhead, g_sgu, w_s, b_s_t, w_branch, w_out, g_ffn)
    out_tile = lambda: pl.BlockSpec((2 * ROW_TILE, d), lambda n: (n, 0))
    rows_out = b * tiles * ROW_TILE
    x_mid, h2 = pl.pallas_call(
        functools.partial(_merge_kernel, d=d, nctx_tiles=nctx_tiles, off=off, tiles=tiles),
        grid=(b * tiles // 2,),
        in_specs=tile_specs(0) + tile_specs(1) + [_layer_spec(a.shape, layer) for a in params],
        out_specs=[out_tile(), out_tile()],
        out_shape=[jax.ShapeDtypeStruct((rows_out, d), _F32), jax.ShapeDtypeStruct((rows_out, d), _BF16)],
        compiler_params=_cparams(1),
        name="merge",
    )(*tile_args, *tile_args, *params)
    return x_mid.reshape(b, tiles * ROW_TILE, d), h2.reshape(b, tiles * ROW_TILE, d)


def _ffn_kernel(*refs, d, n_sub, groups, final_norm):
    h_refs, (hp_ref, hn_ref), x_refs = refs[0:n_sub], refs[n_sub:n_sub + 2], refs[n_sub + 2:2 * n_sub + 2]
    mod_ref, wup_ref, wcv_ref, bcv_ref, wdn_ref, gfin_ref, o_ref, ext_scr, act_scr = refs[2 * n_sub + 2:]
    tm = n_sub * ROW_TILE
    ff = wdn_ref.shape[0]
    ck = FFN_CHUNK
    group = pl.program_id(0) % groups
    ext_scr[0:FFN_HALO, :] = jnp.where(group == 0, jnp.zeros_like(hp_ref), hp_ref[...])
    for s, h_ref in enumerate(h_refs):
        ext_scr[FFN_HALO + s * ROW_TILE:FFN_HALO + (s + 1) * ROW_TILE, :] = h_ref[...]
    ext_scr[FFN_HALO + tm:, :] = jnp.where(group == groups - 1, jnp.zeros_like(hn_ref), hn_ref[...])
    ext_rows = tm + 2 * FFN_HALO
    inner = slice(FFN_HALO, FFN_HALO + tm)

    def conv(cols):
        a = _dot(ext_scr[...], wup_ref[:, cols])
        a_prev = pltpu.roll(a, 1, axis=0)[inner, :]
        a_next = pltpu.roll(a, ext_rows - 1, axis=0)[inner, :]
        return (bcv_ref[:, cols] + a_prev * wcv_ref[0:1, cols] + a[inner, :] * wcv_ref[1:2, cols]
                + a_next * wcv_ref[2:3, cols])

    for c in range(ff // ck):
        gate = conv(slice(c * ck, (c + 1) * ck))
        val = conv(slice(ff + c * ck, ff + (c + 1) * ck))
        act_scr[:, c * ck:(c + 1) * ck] = (_silu(gate) * val).astype(_BF16)

    down = _dot(act_scr[...], wdn_ref[...])
    for s, x_ref in enumerate(x_refs):
        rows = slice(s * ROW_TILE, (s + 1) * ROW_TILE)
        x_new = x_ref[...] + mod_ref[:, 5 * d:6 * d] * down[rows, :]
        if final_norm:
            x_new = _rms(x_new) * gfin_ref[...]
        o_ref[rows, :] = x_new


def _ffn(h2, x, mod, w_up, w_cv, b_cv, w_dn, g_final, layer, first_tile, seg_tiles, n_sub, is_ctx, final_norm):
    b, r, d = x.shape
    assert seg_tiles % n_sub == 0
    groups = seg_tiles // n_sub
    hb = ROW_TILE // FFN_HALO
    batch_row = mod.shape[1] - COND_PAD_ROWS
    tile0 = lambda n: first_tile + (n % groups) * n_sub
    sub_specs = [pl.BlockSpec((None, ROW_TILE, d), lambda n, s=s: (n // groups, tile0(n) + s, 0)) for s in range(n_sub)]
    return pl.pallas_call(
        functools.partial(_ffn_kernel, d=d, n_sub=n_sub, groups=groups, final_norm=final_norm),
        grid=(b * groups,),
        in_specs=sub_specs
        + [pl.BlockSpec((None, FFN_HALO, d), lambda n: (n // groups, jnp.maximum(tile0(n) * hb - 1, 0), 0)),
           pl.BlockSpec((None, FFN_HALO, d),
                        lambda n: (n // groups, jnp.minimum((tile0(n) + n_sub) * hb, r // FFN_HALO - 1), 0))]
        + sub_specs
        + [pl.BlockSpec((None, None, 1, mod.shape[3]), lambda n: (layer, batch_row if is_ctx else n // groups, 0, 0))]
        + [_layer_spec(a.shape, layer) for a in (w_up, w_cv, b_cv, w_dn)] + [_const_spec((1, d))],
        out_specs=pl.BlockSpec((None, n_sub * ROW_TILE, d), lambda n: (n // groups, n % groups, 0)),
        out_shape=jax.ShapeDtypeStruct((b, seg_tiles * ROW_TILE, d), _F32),
        scratch_shapes=[pltpu.VMEM((n_sub * ROW_TILE + 2 * FFN_HALO, d), _BF16),
                        pltpu.VMEM((n_sub * ROW_TILE, w_dn.shape[1]), _BF16)],
        compiler_params=_cparams(1),
        name="ffn",
    )(*([h2] * (n_sub + 2)), *([x] * n_sub), mod, w_up, w_cv, b_cv, w_dn, g_final.reshape(1, d))


def _deinterleave(w):
    return jnp.concatenate([w[..., 0::2], w[..., 1::2]], axis=-1)


def _rotated(w):
    return jnp.concatenate([-w[..., 1::2], w[..., 0::2]], axis=-1)


def _w_in_layout_kernel(w_ref, tail_ref, o_ref, *, moves):
    for src, dst, width in moves:
        o_ref[:, dst:dst + width] = w_ref[:, src:src + width].astype(_BF16)
    o_ref[:, o_ref.shape[1] - tail_ref.shape[1]:] = tail_ref[...]


def _layout_w_in(w):
    depth, d, n_in = w.shape
    sizes = (BRANCH_WIDTH,) * 4 + (M_GATES, Q_LORA, KV_LORA, ROPE_DIM, BRANCH_WIDTH, BRANCH_WIDTH, N_BRANCH * d)
    start = dict(zip("q k v o mg cq ckv kr u s gt".split(), (int(x) for x in np.cumsum((0,) + sizes[:-1]))))
    kr = lax.slice_in_dim(w, start["kr"], start["kr"] + ROPE_DIM, axis=2)
    mg = lax.slice_in_dim(w, start["mg"], start["mg"] + M_GATES, axis=2).reshape(depth, d, 2, 2, HEADS)
    pad = jnp.zeros((depth, d, V7X_LANES - SCAN_LANES), w.dtype)
    tail = jnp.concatenate([_deinterleave(kr), _rotated(kr), mg[:, :, :, 0, :].reshape(depth, d, SCAN_LANES), pad,
                            mg[:, :, :, 1, :].reshape(depth, d, SCAN_LANES), pad], axis=2).astype(_BF16)
    bw = BRANCH_WIDTH
    moves = ((start["q"], 0, 4 * bw), (start["u"], 4 * bw, 2 * bw), (start["gt"], 6 * bw, N_BRANCH * d),
             (start["cq"], 6 * bw + N_BRANCH * d, Q_LORA + KV_LORA))
    n_out = 6 * bw + N_BRANCH * d + Q_LORA + KV_LORA + tail.shape[2]
    return pl.pallas_call(
        functools.partial(_w_in_layout_kernel, moves=moves),
        grid=(depth, d // ROW_TILE),
        in_specs=[pl.BlockSpec((None, ROW_TILE, n_in), lambda l, i: (l, i, 0)),
                  pl.BlockSpec((None, ROW_TILE, tail.shape[2]), lambda l, i: (l, i, 0))],
        out_specs=pl.BlockSpec((None, ROW_TILE, n_out), lambda l, i: (l, i, 0)),
        out_shape=jax.ShapeDtypeStruct((depth, d, n_out), _BF16),
        compiler_params=_cparams(2),
        name="w_in_layout",
    )(w, tail)


def _layout_gate_bias(bg):
    depth = bg.shape[0]
    bg = bg.reshape(depth, 2, 2, HEADS)
    pad = jnp.zeros((depth, V7X_LANES - SCAN_LANES), bg.dtype)
    return jnp.concatenate([bg[:, :, 0, :].reshape(depth, -1), pad, bg[:, :, 1, :].reshape(depth, -1), pad],
                           axis=1).reshape(depth, 1, GATE_W)


def _layout_w_uq(w):
    depth = w.shape[0]
    w = w.reshape(depth, Q_LORA, HEADS, HEAD_DIM + ROPE_DIM)
    nope, rope = w[..., :HEAD_DIM], w[..., HEAD_DIM:]
    return jnp.concatenate([nope, _deinterleave(rope), _rotated(rope)], axis=-1).reshape(depth, Q_LORA, -1).astype(_BF16)


def _layout_w_ukv(w):
    depth = w.shape[0]
    w = w.reshape(depth, KV_LORA, HEADS, 2 * HEAD_DIM).astype(_BF16)
    return w[..., :HEAD_DIM].reshape(depth, KV_LORA, -1), w[..., HEAD_DIM:].reshape(depth, KV_LORA, -1)


def _rope_tables(ctx_len, n_latent):
    rows = n_latent // GRID_W
    row = jnp.repeat(jnp.arange(rows), GRID_W)
    col = jnp.tile(jnp.arange(GRID_W), rows)
    n_freq = ROPE_DIM // 4
    inv = ROPE_BASE ** (-jnp.arange(n_freq, dtype=_F32) / n_freq)
    ang = jnp.concatenate([row[:, None] * inv, col[:, None] * inv], axis=-1)
    zeros = jnp.zeros((n_latent, V7X_LANES - ROPE_DIM), _F32)
    cos_l = jnp.concatenate([jnp.cos(ang), jnp.cos(ang), zeros], axis=1)
    sin_l = jnp.concatenate([jnp.sin(ang), jnp.sin(ang), zeros], axis=1)
    cos_c = jnp.concatenate([jnp.ones((ctx_len, ROPE_DIM), _F32), jnp.zeros((ctx_len, V7X_LANES - ROPE_DIM), _F32)], axis=1)
    return jnp.concatenate([cos_c, cos_l], axis=0), jnp.concatenate([jnp.zeros_like(cos_c), sin_l], axis=0)


def kernel(x, c, ctx, c_ctx, w_ada, b_ada, g_mix, w_in, w_qkconv, b_qkconv, b_mgate, g_mhead, g_qnorm, w_uq,
           g_kvnorm, w_ukv, g_sgu, w_s, b_s, w_branch, w_out, g_ffn, w_up, w_ffconv, b_ffconv, w_down, g_final):
    b, s, d = x.shape
    ctx_len = ctx.shape[1]
    depth = w_in.shape[0]
    assert ctx_len % ROW_TILE == 0 and s % ROW_TILE == 0 and s % GRID_W == 0
    nctx_tiles = ctx_len // ROW_TILE
    row_param = lambda a: a.reshape(depth, 1, a.shape[-1])

    cos_t, sin_t = _rope_tables(ctx_len, s)
    cond_rows = jnp.concatenate([c, c_ctx[None, :], jnp.zeros((COND_PAD_ROWS - 1, d), c.dtype)], axis=0)
    mod = _ada(cond_rows, w_ada, b_ada)
    w_in_l = _layout_w_in(w_in)
    b_gate_l = _layout_gate_bias(b_mgate)
    w_uq_l = _layout_w_uq(w_uq)
    w_k_l, w_v_l = _layout_w_ukv(w_ukv)
    b_s_t = jnp.pad(jnp.swapaxes(b_s, 1, 2), ((0, 0), (0, 0), (0, V7X_LANES - HEADS)))
    w_s16, w_branch16, w_out16 = w_s.astype(_BF16), w_branch.astype(_BF16), w_out.astype(_BF16)
    w_up16, w_down16 = w_up.astype(_BF16), w_down.astype(_BF16)
    g_mix_r, g_mhead_r, g_sgu_r, g_ffn_r = row_param(g_mix), row_param(g_mhead), row_param(g_sgu), row_param(g_ffn)
    g_qn_r, g_kvn_r, b_qkconv_r, b_ffconv_r = row_param(g_qnorm), row_param(g_kvnorm), row_param(b_qkconv), row_param(b_ffconv)

    x_ctx, x_lat = ctx, x
    for l in range(depth):
        last = l == depth - 1
        q_act, kt_act, vo, us, gt, at, gates = _proj(x_ctx, x_lat, mod, g_mix_r, w_in_l, w_qkconv, b_qkconv_r, l,
                                                     nctx_tiles)
        hf, hb = _mlstm(q_act, kt_act, vo, gates, b_gate_l, l, ctx_len // CHUNK)
        yb_ctx, yb_lat = _attn(at, cos_t, sin_t, g_qn_r, w_uq_l, g_kvn_r, w_k_l, w_v_l, l, ctx_len, not last)
        x_mid, h2 = _merge(hf, hb, vo, yb_ctx, yb_lat, us, gt, x_ctx, x_lat, mod, g_mhead_r, g_sgu_r, w_s16, b_s_t,
                           w_branch16, w_out16, g_ffn_r, l, nctx_tiles, last)
        ffn = functools.partial(_ffn, h2, x_mid, mod, w_up16, w_ffconv, b_ffconv_r, w_down16, g_final, l)
        if last:
            return ffn(0, s // ROW_TILE, FFN_SUB, False, True)
        x_lat = ffn(nctx_tiles, s // ROW_TILE, FFN_SUB, False, False)
        x_ctx = ffn(0, nctx_tiles, 1, True, False)
```

```python
import functools

import jax
import jax.numpy as jnp
import numpy as np
from jax import lax
from jax.experimental import pallas as pl
from jax.experimental.pallas import tpu as pltpu

EPS = 1e-6
GRID_W = 64
ROPE_BASE = 10000.0
LOG2_E = 1.4426950408889634

HEADS = 4
HEAD_DIM = 128
CHUNK = 128
BRANCH_WIDTH = HEADS * HEAD_DIM
Q_LORA = 384
KV_LORA = 256
ROPE_DIM = 64
N_BRANCH = 3
M_GATES = 4 * HEADS
KV_SRC = KV_LORA + 2 * ROPE_DIM
ATTN_W = Q_LORA + KV_SRC
SCAN_LANES = 2 * HEADS

V7X_LANES = 128
V7X_BF16_SUBLANE_TILE = 16
V7X_MXU_WIDTH = 256
V7X_VMEM_BYTES = 64 * 1024 * 1024
V7X_VMEM_LIMIT = V7X_VMEM_BYTES * 7 // 8

GATE_W = 2 * V7X_LANES
COND_PAD_ROWS = 8
ROW_TILE = V7X_MXU_WIDTH
FFN_HALO = V7X_BF16_SUBLANE_TILE
FFN_CHUNK = V7X_MXU_WIDTH
FFN_SUB = 4

_BF16 = jnp.bfloat16
_F32 = jnp.float32


def _cparams(n_axes):
    return pltpu.CompilerParams(dimension_semantics=("arbitrary",) * n_axes, vmem_limit_bytes=V7X_VMEM_LIMIT)


def _const_spec(shape):
    return pl.BlockSpec(tuple(shape), lambda *_: (0,) * len(shape))


def _layer_spec(stacked_shape, layer):
    shape = tuple(stacked_shape[1:])
    return pl.BlockSpec((None,) + shape, lambda *_: (layer,) + (0,) * len(shape), pipeline_mode=pl.Buffered(1))


def _sigmoid(x):
    return 1.0 / (1.0 + jnp.exp(-x))


def _silu(x):
    return x * _sigmoid(x)


def _gelu_tanh(x):
    return x * (0.5 * (1.0 + jnp.tanh(0.7978845608028654 * (x + 0.044715 * (x * x * x)))))


def _rms(x):
    return x * lax.rsqrt(jnp.mean(x * x, axis=-1, keepdims=True) + EPS)


def _dot(a, b):
    return jnp.dot(a, b, preferred_element_type=_F32)


def _dot_nt(a, b):
    return lax.dot_general(a, b, (((1,), (1,)), ((), ())), preferred_element_type=_F32)


def _ada_kernel(c_ref, w_ref, b_ref, o_ref):
    cond = _silu(c_ref[...])
    o_ref[...] = jnp.dot(cond, w_ref[...], precision=lax.Precision.HIGHEST, preferred_element_type=_F32) + b_ref[...]


def _ada(cond_rows, w_ada, b_ada):
    rows, d = cond_rows.shape
    depth, _, n = w_ada.shape
    out = pl.pallas_call(
        _ada_kernel,
        grid=(depth, n // d),
        in_specs=[pl.BlockSpec((rows, d), lambda l, j: (0, 0)),
                  pl.BlockSpec((None, d, d), lambda l, j: (l, 0, j)),
                  pl.BlockSpec((None, 1, d), lambda l, j: (l, 0, j))],
        out_specs=pl.BlockSpec((None, rows, d), lambda l, j: (l, 0, j)),
        out_shape=jax.ShapeDtypeStruct((depth, rows, n), _F32),
        compiler_params=_cparams(2),
        name="ada",
    )(cond_rows, w_ada, b_ada.reshape(depth, 1, n))
    return out.reshape(depth, rows, 1, n)


PROJ_TILE_INPUTS = 5
QK_HALO = V7X_BF16_SUBLANE_TILE


def _proj_kernel(*refs, d, nctx_tiles, tiles):
    per_tile = (refs[0:PROJ_TILE_INPUTS], refs[PROJ_TILE_INPUTS:2 * PROJ_TILE_INPUTS])
    (g_ref, w_ref, wcv_ref, bcv_ref, bg_ref,
     q_ref, kt_ref, vo_ref, us_ref, gt_ref, at_ref, gate_ref) = refs[2 * PROJ_TILE_INPUTS:]
    bw = BRANCH_WIDTH
    col_at = 6 * bw + N_BRANCH * d
    ext_rows = ROW_TILE + 2 * QK_HALO
    inner = slice(QK_HALO, QK_HALO + ROW_TILE)

    def normalise(h):
        xc_ref, xl_ref, mod_ref, xp_ref, xn_ref = per_tile[h]
        tile = (2 * pl.program_id(0) + h) % tiles
        norm = lambda x: (_rms(x) * g_ref[...] * (1.0 + mod_ref[:, d:2 * d]) + mod_ref[:, 0:d]).astype(_BF16)
        hn = norm(jnp.where(tile < nctx_tiles, xc_ref[...], xl_ref[...]))
        zeros = jnp.zeros((QK_HALO, d), _BF16)
        h_prev = jnp.where(tile > nctx_tiles, norm(xp_ref[...]), zeros)
        h_next = jnp.where(jnp.logical_and(tile >= nctx_tiles, tile < tiles - 1), norm(xn_ref[...]), zeros)
        return hn, jnp.concatenate([h_prev, hn, h_next], axis=0)

    cw = V7X_MXU_WIDTH

    def conv_columns(h, h_ext, c):
        rows = slice(h * ROW_TILE, (h + 1) * ROW_TILE)
        cols = slice(c * cw, (c + 1) * cw)
        qk = _dot(h_ext, w_ref[:, cols])
        qk_prev = pltpu.roll(qk, 1, axis=0)[inner, :]
        qk_next = pltpu.roll(qk, ext_rows - 1, axis=0)[inner, :]
        a = _silu(bcv_ref[:, cols] + qk_prev * wcv_ref[0:1, cols] + qk[inner, :] * wcv_ref[1:2, cols]
                  + qk_next * wcv_ref[2:3, cols])
        if c < bw // cw:
            q_ref[rows, cols] = a.astype(_BF16)
        else:
            for sub in range(cw // HEAD_DIM):
                head = (c * cw - bw) // HEAD_DIM + sub
                k = a[:, sub * HEAD_DIM:(sub + 1) * HEAD_DIM] * (HEAD_DIM ** -0.5)
                kt_ref[h, head] = k.T.astype(_BF16)

    def column_groups(h, hn, h_ext):
        rows = slice(h * ROW_TILE, (h + 1) * ROW_TILE)
        conv_columns(h, h_ext, 0)
        us_ref[rows, :] = _gelu_tanh(_dot(hn, w_ref[:, 4 * bw:6 * bw])).astype(_BF16)
        for g in range(N_BRANCH):
            conv_columns(h, h_ext, g + 1)
            cols = slice(6 * bw + g * d, 6 * bw + (g + 1) * d)
            gt_ref[rows, g * d:(g + 1) * d] = _sigmoid(_dot(hn, w_ref[:, cols])).astype(_BF16)
        vo_ref[rows, bw:2 * bw] = _sigmoid(_dot(hn, w_ref[:, 3 * bw:4 * bw])).astype(_BF16)
        vo_ref[rows, 0:bw] = _dot(hn, w_ref[:, 2 * bw:3 * bw]).astype(_BF16)
        at_ref[rows, :] = _dot(hn, w_ref[:, col_at:col_at + ATTN_W]).astype(_BF16)
        gates = _dot(hn, w_ref[:, col_at + ATTN_W:col_at + ATTN_W + GATE_W]) + bg_ref[...]
        gate_f = gates[:, V7X_LANES:GATE_W]
        gate_ref[rows, 0:V7X_LANES] = gates[:, 0:V7X_LANES]
        gate_ref[rows, V7X_LANES:GATE_W] = jnp.minimum(gate_f, 0.0) - jnp.log(1.0 + jnp.exp(-jnp.abs(gate_f)))

    assert 2 * bw // cw == N_BRANCH + 1
    tiles_normed = [normalise(0), normalise(1)]
    for h, (hn, h_ext) in enumerate(tiles_normed):
        column_groups(h, hn, h_ext)


def _pair_tile(h, tiles):
    def index(n):
        k = 2 * n + h
        return k // tiles, k % tiles
    return index


def _two_source_specs(at, off, width, nctx_tiles):
    return [pl.BlockSpec((None, ROW_TILE, width), lambda n: (at(n)[0], jnp.minimum(at(n)[1] + off, nctx_tiles - 1), 0)),
            pl.BlockSpec((None, ROW_TILE, width), lambda n: (at(n)[0], jnp.maximum(at(n)[1] + off - nctx_tiles, 0), 0))]


def _stream_tile_specs(at, off, d, nctx_tiles, mod, layer):
    batch_row = mod.shape[1] - COND_PAD_ROWS
    return _two_source_specs(at, off, d, nctx_tiles) + [
        pl.BlockSpec((None, None, 1, mod.shape[3]),
                     lambda n: (layer, jnp.where(at(n)[1] + off < nctx_tiles, batch_row, at(n)[0]), 0, 0))]


def _proj(x_ctx, x_lat, mod, g_mix, w_in, w_qkconv, b_qkconv, b_gate, layer, nctx_tiles):
    b, _, d = x_ctx.shape
    lat_rows = x_lat.shape[1]
    tiles = nctx_tiles + lat_rows // ROW_TILE
    bw = BRANCH_WIDTH
    widths = (bw, 2 * bw, 2 * bw, N_BRANCH * d, ATTN_W, GATE_W)
    dtypes = (_BF16, _BF16, _BF16, _BF16, _BF16, _F32)
    assert nctx_tiles == 1 and sum(widths) + bw == w_in.shape[2] and (b * tiles) % 2 == 0
    rows = b * tiles * ROW_TILE
    hb = ROW_TILE // QK_HALO

    def tile_specs(h):
        at = _pair_tile(h, tiles)
        lat_tile = lambda n: jnp.maximum(at(n)[1] - nctx_tiles, 0)
        halo = lambda block_of: pl.BlockSpec((None, QK_HALO, d), lambda n: (at(n)[0], block_of(n), 0))
        return _stream_tile_specs(at, 0, d, nctx_tiles, mod, layer) + [
            halo(lambda n: jnp.maximum(lat_tile(n) * hb - 1, 0)),
            halo(lambda n: jnp.minimum((lat_tile(n) + 1) * hb, lat_rows // QK_HALO - 1))]

    flat = lambda w: pl.BlockSpec((2 * ROW_TILE, w), lambda n: (n, 0))
    tile_args = (x_ctx, x_lat, mod, x_lat, x_lat)
    params = (g_mix, w_in, w_qkconv, b_qkconv, b_gate)
    outs = pl.pallas_call(
        functools.partial(_proj_kernel, d=d, nctx_tiles=nctx_tiles, tiles=tiles),
        grid=(b * tiles // 2,),
        in_specs=tile_specs(0) + tile_specs(1) + [_layer_spec(a.shape, layer) for a in params],
        out_specs=[flat(widths[0]), pl.BlockSpec((2, HEADS, HEAD_DIM, ROW_TILE), lambda n: (n, 0, 0, 0))]
        + [flat(w) for w in widths[1:]],
        out_shape=[jax.ShapeDtypeStruct((rows, widths[0]), dtypes[0]),
                   jax.ShapeDtypeStruct((b * tiles, HEADS, HEAD_DIM, ROW_TILE), _BF16)]
        + [jax.ShapeDtypeStruct((rows, w), dt) for w, dt in zip(widths[1:], dtypes[1:])],
        compiler_params=_cparams(1),
        name="proj",
    )(*tile_args, *tile_args, *params)
    q, kt = outs[0].reshape(b, tiles * ROW_TILE, bw), outs[1]
    return [q, kt] + [o.reshape(b, tiles * ROW_TILE, o.shape[1]) for o in outs[2:]]


def _mlstm_kernel(q_f, kt_f, v_f, gf_cur, gf_nxt, q_b, kt_b, v_b, gb_cur, gb_nxt, hf_ref, hb_ref,
                  c_scr, m_row, m_col, *pre):
    tc = CHUNK
    low, high = pl.ds(0, tc), pl.ds(tc, tc)

    def scan_step(slot, fwd_rows, bwd_rows):
        views = [(q.at[rows, :], kt.at[:, :, rows], v.at[rows, :], out.at[rows, :])
                 for rows, (q, kt, v, out) in ((fwd_rows, (q_f, kt_f, v_f, hf_ref)), (bwd_rows, (q_b, kt_b, v_b, hb_ref)))]
        _mlstm_scan_step(slot, is_fwd, causal, views[0], views[1], c_scr, m_row, m_col)

    pre_a, pre_b = pre[:len(pre) // 2], pre[len(pre) // 2:]
    lane = lax.broadcasted_iota(jnp.int32, (1, V7X_LANES), 1)
    row = lax.broadcasted_iota(jnp.int32, (tc, 1), 0)
    is_fwd = lane < HEADS
    r_idx = lax.broadcasted_iota(jnp.int32, (tc, tc), 0)
    c_idx = lax.broadcasted_iota(jnp.int32, (tc, tc), 1)
    causal = (c_idx <= r_idx, c_idx >= r_idx)

    def gate_prologue(g_f, g_b, slot):
        bcum_ref, run_ref, rt_ref, dt_ref, rows_ref, blc_ref, dmc_ref = slot
        gi, gf = slice(0, V7X_LANES), slice(V7X_LANES, GATE_W)
        log_i = jnp.where(is_fwd, g_f[:, gi], g_b[:, gi])
        log_f = jnp.where(is_fwd, g_f[:, gf], g_b[:, gf])
        cum_f, step = log_f, 1
        while step < tc:
            cum_f = cum_f + jnp.where(row >= step, pltpu.roll(cum_f, step, axis=0), 0.0)
            step *= 2
        b_last = cum_f[tc - 1:tc, :]
        bcum = jnp.where(is_fwd, cum_f, b_last - cum_f + log_f)
        r = log_i - bcum
        run_f, run_b, step = r, r, 1
        while step < tc:
            run_f = jnp.maximum(run_f, jnp.where(row >= step, pltpu.roll(run_f, step, axis=0), -jnp.inf))
            run_b = jnp.maximum(run_b, jnp.where(row < tc - step, pltpu.roll(run_b, tc - step, axis=0), -jnp.inf))
            step *= 2
        decay = b_last + r
        r_t = r.T[0:SCAN_LANES, :]
        decay_t = decay.T[0:SCAN_LANES, :]
        bcum_ref[...] = bcum
        run_ref[...] = jnp.where(is_fwd, run_f, run_b)
        rt_ref[...] = r_t
        dt_ref[...] = decay_t
        rows_ref[0:1, :] = b_last
        rows_ref[1:2, :] = jnp.max(decay, axis=0, keepdims=True)
        blc_ref[...] = decay_t - r_t
        dmc_ref[...] = jnp.broadcast_to(jnp.max(decay_t, axis=1, keepdims=True), (SCAN_LANES, tc))

    @pl.when(pl.program_id(1) == 0)
    def _():
        c_scr[...] = jnp.zeros_like(c_scr)
        m_row[...] = jnp.zeros_like(m_row)
        m_col[...] = jnp.zeros_like(m_col)
        gate_prologue(gf_cur.at[low, :], gb_cur.at[high, :], pre_a)

    gate_prologue(gf_cur.at[high, :], gb_cur.at[low, :], pre_b)
    scan_step(pre_a, low, high)
    gate_prologue(gf_nxt.at[low, :], gb_nxt.at[high, :], pre_a)
    scan_step(pre_b, high, low)


def _mlstm_scan_step(slot, is_fwd, causal, refs_f, refs_b, c_scr, m_row, m_col):
    tc = CHUNK
    bcum_ref, run_ref, rt_ref, dt_ref, rows_ref, blc_ref, dmc_ref = slot
    bcum = bcum_ref[...]
    b_last, decay_max = rows_ref[0:1, :], rows_ref[1:2, :]
    m_old = m_row[...]
    g = jnp.maximum(m_old, run_ref[...])
    m_new = jnp.maximum(b_last + m_old, decay_max)
    w_carry = jnp.exp(b_last + m_old - m_new)
    m_col_new = jnp.maximum(blc_ref[...] + m_col[...], dmc_ref[...])
    w_src_t = jnp.exp(dt_ref[...] - m_col_new)
    r_t = rt_ref[...]
    ones = jnp.ones((tc, HEAD_DIM), _BF16)

    scans = [(dr * HEADS + h, refs, h, slice(h * HEAD_DIM, (h + 1) * HEAD_DIM), causal[dr])
             for dr, refs in enumerate((refs_f, refs_b)) for h in range(HEADS)]
    v_ext = [jnp.concatenate([refs[2][:, hs], ones], axis=1) for _, refs, _, hs, _ in scans]
    s16, g_b = [], []
    for l, (q_ref, kt_ref, _, _), h, hs, mask in scans:
        g_l = jnp.broadcast_to(g[:, l:l + 1], (tc, tc))
        w_intra = jnp.exp(jnp.where(mask, r_t[l:l + 1, :] - g_l, -jnp.inf))
        s16.append((_dot(q_ref[:, hs], kt_ref[h]) * w_intra).astype(_BF16))
        g_b.append(g_l)
    qc = []
    for l, (q_ref, kt_ref, _, _), h, hs, _ in scans:
        c_old = c_scr[l]
        qc.append(_dot(q_ref[:, hs], c_old.astype(_BF16)))
        kw_t = (kt_ref[h].astype(_F32) * w_src_t[l:l + 1, :]).astype(_BF16)
        c_scr[l] = w_carry[:, l:l + 1] * c_old + _dot(kw_t, v_ext[l])
    for l, (_, _, _, out_ref), h, hs, _ in scans:
        w_inter = jnp.exp(m_old[:, l:l + 1] - g_b[l])
        b_l = jnp.broadcast_to(bcum[:, l:l + 1], (tc, tc))
        sv = _dot(s16[l], v_ext[l])
        num = w_inter * qc[l][:, 0:HEAD_DIM] + sv[:, 0:HEAD_DIM]
        den = w_inter * qc[l][:, HEAD_DIM:] + sv[:, HEAD_DIM:]
        out_ref[:, hs] = num / jnp.maximum(jnp.abs(den), jnp.exp(-(b_l + g_b[l])))
    m_row[...] = m_new
    m_col[...] = m_col_new


def _mlstm(q, kt, vo, gates, nctx):
    b, t, _ = q.shape
    blk = 2 * CHUNK
    nblocks, nctx_blocks = t // blk, nctx // 2
    assert t % blk == 0 and nctx % 2 == 0

    def bwd_block(j):
        return jnp.where(j < nctx_blocks, nctx_blocks - 1 - j, nblocks - 1 - (j - nctx_blocks))

    def specs(block_of):
        nxt = lambda j: block_of(jnp.minimum(j + 1, nblocks - 1))
        return [
            pl.BlockSpec((None, blk, BRANCH_WIDTH), lambda bi, j: (bi, block_of(j), 0)),
            pl.BlockSpec((None, HEADS, HEAD_DIM, blk), lambda bi, j: (bi * nblocks + block_of(j), 0, 0, 0)),
            pl.BlockSpec((None, blk, BRANCH_WIDTH), lambda bi, j: (bi, block_of(j), 0)),
            pl.BlockSpec((None, blk, GATE_W), lambda bi, j: (bi, block_of(j), 0)),
            pl.BlockSpec((None, blk, GATE_W), lambda bi, j: (bi, nxt(j), 0)),
        ]

    time_lane, lane_time = pltpu.VMEM((CHUNK, V7X_LANES), _F32), pltpu.VMEM((SCAN_LANES, CHUNK), _F32)
    gate_slot = [time_lane, time_lane, lane_time, lane_time, pltpu.VMEM((8, V7X_LANES), _F32), lane_time, lane_time]
    h_shape = jax.ShapeDtypeStruct((b, t, BRANCH_WIDTH), _F32)
    return pl.pallas_call(
        _mlstm_kernel,
        grid=(b, nblocks),
        in_specs=specs(lambda j: j) + specs(bwd_block),
        out_specs=[pl.BlockSpec((None, blk, BRANCH_WIDTH), lambda bi, j: (bi, j, 0)),
                   pl.BlockSpec((None, blk, BRANCH_WIDTH), lambda bi, j: (bi, bwd_block(j), 0))],
        out_shape=[h_shape, h_shape],
        scratch_shapes=[pltpu.VMEM((SCAN_LANES, HEAD_DIM, 2 * HEAD_DIM), _F32), pltpu.VMEM((1, V7X_LANES), _F32),
                        lane_time] + gate_slot + gate_slot,
        compiler_params=_cparams(2),
        name="mlstm",
    )(q, kt, vo, gates, gates, q, kt, vo, gates, gates)


ATTN_SUB = 2
VT_ONES = V7X_BF16_SUBLANE_TILE


def _attn_kernel(cq_ctx_ref, cq_a_ref, cq_b_ref, kv_ref, ca_ref, sa_ref, gq_ref, wq_ref, gkv_ref, wk_ref, wv_ref,
                 o_ctx_ref, o_lat_ref, k_scr, vt_scr, *, ctx_len, need_ctx, scale):
    i = pl.program_id(1)
    t = kv_ref.shape[0]
    head_w = 2 * HEAD_DIM

    def rope(y, cos_t, sin_t):
        return y * cos_t + pltpu.roll(y, ROPE_DIM, axis=1) * sin_t

    def attend(cq_rows, row0, n_keys, o_ref):
        table_rows = pl.ds(pl.multiple_of(row0, ROW_TILE), cq_rows.shape[0])
        cos_q, sin_q = ca_ref[table_rows, :], sa_ref[table_rows, :]
        cq = (_rms(cq_rows.astype(_F32)) * gq_ref[...]).astype(_BF16)
        qa = _dot(cq, wq_ref[...]) * (scale * LOG2_E)

        def scores(h):
            q_nope = qa[:, h * head_w:h * head_w + HEAD_DIM]
            q_rope = rope(qa[:, h * head_w + HEAD_DIM:(h + 1) * head_w], cos_q, sin_q)
            qh = jnp.concatenate([q_nope, q_rope], axis=1).astype(_BF16)
            return _dot_nt(qh, k_scr[0:n_keys, h * head_w:(h + 1) * head_w])

        s_next = scores(0)
        for h in range(HEADS):
            s = s_next
            if h + 1 < HEADS:
                s_next = scores(h + 1)
            e = jnp.exp2(s - jnp.max(s, axis=1, keepdims=True)).astype(_BF16)
            ot = _dot_nt(vt_scr[h, :, 0:n_keys], e)
            ot = ot[0:HEAD_DIM, :] / ot[HEAD_DIM:HEAD_DIM + 1, :]
            o_ref[:, h * HEAD_DIM:(h + 1) * HEAD_DIM] = ot.T.astype(o_ref.dtype)

    @pl.when(i == 0)
    def _():
        for r in range(t // ROW_TILE):
            rows = slice(r * ROW_TILE, (r + 1) * ROW_TILE)
            ckv = (_rms(kv_ref[rows, 0:KV_LORA].astype(_F32)) * gkv_ref[...]).astype(_BF16)
            k_nope = _dot(ckv, wk_ref[...])
            k_rope = rope(kv_ref[rows, KV_LORA:KV_SRC].astype(_F32), ca_ref[rows, :], sa_ref[rows, :]).astype(_BF16)
            for h in range(HEADS):
                k_scr[rows, h * head_w:h * head_w + HEAD_DIM] = k_nope[:, h * HEAD_DIM:(h + 1) * HEAD_DIM].astype(_BF16)
                k_scr[rows, h * head_w + HEAD_DIM:(h + 1) * head_w] = k_rope
            vv = _dot(ckv, wv_ref[...])
            for h in range(HEADS):
                vt_scr[h, 0:HEAD_DIM, rows] = vv[:, h * HEAD_DIM:(h + 1) * HEAD_DIM].T.astype(_BF16)
                vt_scr[h, HEAD_DIM:, rows] = jnp.ones((VT_ONES, ROW_TILE), _BF16)
        if need_ctx:
            attend(cq_ctx_ref[...], 0, ctx_len, o_ctx_ref)
        else:
            o_ctx_ref[...] = jnp.zeros_like(o_ctx_ref)

    @pl.when(i > 0)
    def _():
        cq_rows = jnp.concatenate([cq_a_ref[...], cq_b_ref[...]], axis=0)
        attend(cq_rows, ctx_len + (i - 1) * (ATTN_SUB * ROW_TILE), t, o_lat_ref)


def _attn(at, cos_t, sin_t, g_qn, w_q, g_kvn, w_k, w_v, layer, ctx_len, need_ctx):
    b, t, _ = at.shape
    assert ctx_len == ROW_TILE and (t - ctx_len) % (ATTN_SUB * ROW_TILE) == 0
    lat_steps = (t - ctx_len) // (ATTN_SUB * ROW_TILE)
    scale = (HEAD_DIM + ROPE_DIM) ** -0.5
    lat_tile = lambda s: pl.BlockSpec((None, ROW_TILE, Q_LORA),
                                      lambda bi, i: (bi, 1 + ATTN_SUB * jnp.maximum(i - 1, 0) + s, 0))
    return pl.pallas_call(
        functools.partial(_attn_kernel, ctx_len=ctx_len, need_ctx=need_ctx, scale=scale),
        grid=(b, 1 + lat_steps),
        in_specs=[pl.BlockSpec((None, ROW_TILE, Q_LORA), lambda bi, i: (bi, 0, 0)), lat_tile(0), lat_tile(1),
                  pl.BlockSpec((None, t, KV_SRC), lambda bi, i: (bi, 0, 1)),
                  _const_spec((t, V7X_LANES)), _const_spec((t, V7X_LANES)),
                  _layer_spec(g_qn.shape, layer), _layer_spec(w_q.shape, layer), _layer_spec(g_kvn.shape, layer),
                  _layer_spec(w_k.shape, layer), _layer_spec(w_v.shape, layer)],
        out_specs=[pl.BlockSpec((None, ROW_TILE, BRANCH_WIDTH), lambda bi, i: (bi, 0, 0)),
                   pl.BlockSpec((None, ATTN_SUB * ROW_TILE, BRANCH_WIDTH), lambda bi, i: (bi, jnp.maximum(i - 1, 0), 0))],
        out_shape=[jax.ShapeDtypeStruct((b, ctx_len, BRANCH_WIDTH), _BF16),
                   jax.ShapeDtypeStruct((b, t - ctx_len, BRANCH_WIDTH), _BF16)],
        scratch_shapes=[pltpu.VMEM((t, HEADS * 2 * HEAD_DIM), _BF16),
                        pltpu.VMEM((HEADS, HEAD_DIM + VT_ONES, t), _BF16)],
        compiler_params=_cparams(2),
        name="attn",
    )(at, at, at, at, cos_t, sin_t, g_qn, w_q, g_kvn, w_k, w_v)


def _merge_mix(hf_ref, hb_ref, so_ref, yb, us_ref, gt_ref, gmh_ref, gsgu_ref, ws_ref, bs_ref, wbr_ref, wout_ref, d):
    tm = hf_ref.shape[0]
    half = d // 2
    pb = [_dot(yb, wbr_ref[1, :, c * half:(c + 1) * half]) for c in range(2)]
    hsum = hf_ref[...] + hb_ref[...]
    ya = jnp.concatenate(
        [_rms(hsum[:, h * HEAD_DIM:(h + 1) * HEAD_DIM]) for h in range(HEADS)], axis=1) * gmh_ref[...]
    ya = so_ref[...].astype(_F32) * ya
    cols = []
    for g in range(HEADS):
        gs = slice(BRANCH_WIDTH + g * HEAD_DIM, BRANCH_WIDTH + (g + 1) * HEAD_DIM)
        vn = (_rms(us_ref[:, gs].astype(_F32)) * gsgu_ref[:, g * HEAD_DIM:(g + 1) * HEAD_DIM]).astype(_BF16)
        mixed = [_dot(ws_ref[g], vn[n * CHUNK:(n + 1) * CHUNK, :]) + bs_ref[:, g:g + 1] for n in range(tm // CHUNK)]
        cols.append(jnp.concatenate(mixed, axis=0))
    yc = us_ref[:, 0:BRANCH_WIDTH].astype(_F32) * jnp.concatenate(cols, axis=1)
    ya16, yc16 = ya.astype(_BF16), yc.astype(_BF16)
    pa = [_dot(ya16, wbr_ref[0, :, c * half:(c + 1) * half]) for c in range(2)]
    pc = [_dot(yc16, wbr_ref[2, :, c * half:(c + 1) * half]) for c in range(2)]
    out = None
    for c in range(2):
        gate = lambda g: gt_ref[:, g * d + c * half:g * d + (c + 1) * half].astype(_F32)
        merged = (gate(0) * pa[c] + gate(1) * pb[c] + gate(2) * pc[c]).astype(_BF16)
        part = _dot(merged, wout_ref[c * half:(c + 1) * half, :])
        out = part if out is None else out + part
    return out


MERGE_TILE_INPUTS = 10


def _merge_kernel(*refs, d, nctx_tiles, off, tiles):
    per_tile = (refs[0:MERGE_TILE_INPUTS], refs[MERGE_TILE_INPUTS:2 * MERGE_TILE_INPUTS])
    gmh_ref, gsgu_ref, ws_ref, bs_ref, wbr_ref, wout_ref, gffn_ref, xo_ref, h2_ref = refs[2 * MERGE_TILE_INPUTS:]
    is_ctx = [(2 * pl.program_id(0) + h) % tiles + off < nctx_tiles for h in range(2)]
    outs = []
    for h, (hf_ref, hb_ref, so_ref, us_ref, gt_ref, ybc_ref, ybl_ref) in enumerate(t[0:7] for t in per_tile):
        yb = jnp.where(is_ctx[h], ybc_ref[...], ybl_ref[...])
        outs.append(_merge_mix(hf_ref, hb_ref, so_ref, yb, us_ref, gt_ref, gmh_ref, gsgu_ref, ws_ref, bs_ref, wbr_ref,
                               wout_ref, d))
    for h, tile_refs in enumerate(per_tile):
        xc_ref, xl_ref, mod_ref = tile_refs[7:10]
        rows = slice(h * ROW_TILE, (h + 1) * ROW_TILE)
        x = jnp.where(is_ctx[h], xc_ref[...], xl_ref[...])
        x_new = x + mod_ref[:, 2 * d:3 * d] * outs[h]
        xo_ref[rows, :] = x_new
        h2 = _rms(x_new) * gffn_ref[...]
        h2_ref[rows, :] = (h2 * (1.0 + mod_ref[:, 4 * d:5 * d]) + mod_ref[:, 3 * d:4 * d]).astype(h2_ref.dtype)


def _merge(hf, hb, vo, yb_ctx, yb_lat, us, gt, x_ctx, x_lat, mod, g_mhead, g_sgu, w_s, b_s_t, w_branch, w_out, g_ffn,
           layer, nctx_tiles, skip_ctx):
    b, t, _ = hf.shape
    d = x_ctx.shape[2]
    off = nctx_tiles if skip_ctx else 0
    tiles = t // ROW_TILE - off
    assert (b * tiles) % 2 == 0

    def tile_specs(h):
        at = _pair_tile(h, tiles)
        tile = lambda w, blk: pl.BlockSpec((None, ROW_TILE, w), lambda n: (at(n)[0], at(n)[1] + off, blk))
        return ([tile(BRANCH_WIDTH, 0), tile(BRANCH_WIDTH, 0), tile(BRANCH_WIDTH, 1), tile(2 * BRANCH_WIDTH, 0),
                 tile(N_BRANCH * d, 0)] + _two_source_specs(at, off, BRANCH_WIDTH, nctx_tiles)
                + _stream_tile_specs(at, off, d, nctx_tiles, mod, layer))

    tile_args = (hf, hb, vo, us, gt, yb_ctx, yb_lat, x_ctx, x_lat, mod)
    params = (g_mhead, g_sgu, w_s, b_s_t, w_branch, w_out, g_ffn)
    out_tile = lambda: pl.BlockSpec((2 * ROW_TILE, d), lambda n: (n, 0))
    rows_out = b * tiles * ROW_TILE
    x_mid, h2 = pl.pallas_call(
        functools.partial(_merge_kernel, d=d, nctx_tiles=nctx_tiles, off=off, tiles=tiles),
        grid=(b * tiles // 2,),
        in_specs=tile_specs(0) + tile_specs(1) + [_layer_spec(a.shape, layer) for a in params],
        out_specs=[out_tile(), out_tile()],
        out_shape=[jax.ShapeDtypeStruct((rows_out, d), _F32), jax.ShapeDtypeStruct((rows_out, d), _BF16)],
        compiler_params=_cparams(1),
        name="merge",
    )(*tile_args, *tile_args, *params)
    return x_mid.reshape(b, tiles * ROW_TILE, d), h2.reshape(b, tiles * ROW_TILE, d)


def _ffn_kernel(*refs, d, n_sub, groups, final_norm):
    h_refs, (hp_ref, hn_ref), x_refs = refs[0:n_sub], refs[n_sub:n_sub + 2], refs[n_sub + 2:2 * n_sub + 2]
    mod_ref, wup_ref, wcv_ref, bcv_ref, wdn_ref, gfin_ref, o_ref, ext_scr, act_scr = refs[2 * n_sub + 2:]
    tm = n_sub * ROW_TILE
    ff = wdn_ref.shape[0]
    ck = FFN_CHUNK
    group = pl.program_id(0) % groups
    ext_scr[0:FFN_HALO, :] = jnp.where(group == 0, jnp.zeros_like(hp_ref), hp_ref[...])
    for s, h_ref in enumerate(h_refs):
        ext_scr[FFN_HALO + s * ROW_TILE:FFN_HALO + (s + 1) * ROW_TILE, :] = h_ref[...]
    ext_scr[FFN_HALO + tm:, :] = jnp.where(group == groups - 1, jnp.zeros_like(hn_ref), hn_ref[...])
    ext_rows = tm + 2 * FFN_HALO
    inner = slice(FFN_HALO, FFN_HALO + tm)

    def conv(cols):
        a = _dot(ext_scr[...], wup_ref[:, cols])
        a_prev = pltpu.roll(a, 1, axis=0)[inner, :]
        a_next = pltpu.roll(a, ext_rows - 1, axis=0)[inner, :]
        return (bcv_ref[:, cols] + a_prev * wcv_ref[0:1, cols] + a[inner, :] * wcv_ref[1:2, cols]
                + a_next * wcv_ref[2:3, cols])

    for c in range(ff // ck):
        gate = conv(slice(c * ck, (c + 1) * ck))
        val = conv(slice(ff + c * ck, ff + (c + 1) * ck))
        act_scr[:, c * ck:(c + 1) * ck] = (_silu(gate) * val).astype(_BF16)

    down = _dot(act_scr[...], wdn_ref[...])
    for s, x_ref in enumerate(x_refs):
        rows = slice(s * ROW_TILE, (s + 1) * ROW_TILE)
        x_new = x_ref[...] + mod_ref[:, 5 * d:6 * d] * down[rows, :]
        if final_norm:
            x_new = _rms(x_new) * gfin_ref[...]
        o_ref[rows, :] = x_new


def _ffn(h2, x, mod, w_up, w_cv, b_cv, w_dn, g_final, layer, first_tile, seg_tiles, n_sub, is_ctx, final_norm):
    b, r, d = x.shape
    assert seg_tiles % n_sub == 0
    groups = seg_tiles // n_sub
    hb = ROW_TILE // FFN_HALO
    batch_row = mod.shape[1] - COND_PAD_ROWS
    tile0 = lambda n: first_tile + (n % groups) * n_sub
    sub_specs = [pl.BlockSpec((None, ROW_TILE, d), lambda n, s=s: (n // groups, tile0(n) + s, 0)) for s in range(n_sub)]
    return pl.pallas_call(
        functools.partial(_ffn_kernel, d=d, n_sub=n_sub, groups=groups, final_norm=final_norm),
        grid=(b * groups,),
        in_specs=sub_specs
        + [pl.BlockSpec((None, FFN_HALO, d), lambda n: (n // groups, jnp.maximum(tile0(n) * hb - 1, 0), 0)),
           pl.BlockSpec((None, FFN_HALO, d),
                        lambda n: (n // groups, jnp.minimum((tile0(n) + n_sub) * hb, r // FFN_HALO - 1), 0))]
        + sub_specs
        + [pl.BlockSpec((None, None, 1, mod.shape[3]), lambda n: (layer, batch_row if is_ctx else n // groups, 0, 0))]
        + [_layer_spec(a.shape, layer) for a in (w_up, w_cv, b_cv, w_dn)] + [_const_spec((1, d))],
        out_specs=pl.BlockSpec((None, n_sub * ROW_TILE, d), lambda n: (n // groups, n % groups, 0)),
        out_shape=jax.ShapeDtypeStruct((b, seg_tiles * ROW_TILE, d), _F32),
        scratch_shapes=[pltpu.VMEM((n_sub * ROW_TILE + 2 * FFN_HALO, d), _BF16),
                        pltpu.VMEM((n_sub * ROW_TILE, w_dn.shape[1]), _BF16)],
        compiler_params=_cparams(1),
        name="ffn",
    )(*([h2] * (n_sub + 2)), *([x] * n_sub), mod, w_up, w_cv, b_cv, w_dn, g_final.reshape(1, d))


def _deinterleave(w):
    return jnp.concatenate([w[..., 0::2], w[..., 1::2]], axis=-1)


def _rotated(w):
    return jnp.concatenate([-w[..., 1::2], w[..., 0::2]], axis=-1)


def _w_in_layout_kernel(w_ref, tail_ref, o_ref, *, moves):
    for src, dst, width in moves:
        o_ref[:, dst:dst + width] = w_ref[:, src:src + width].astype(_BF16)
    o_ref[:, o_ref.shape[1] - tail_ref.shape[1]:] = tail_ref[...]


def _layout_w_in(w):
    depth, d, n_in = w.shape
    sizes = (BRANCH_WIDTH,) * 4 + (M_GATES, Q_LORA, KV_LORA, ROPE_DIM, BRANCH_WIDTH, BRANCH_WIDTH, N_BRANCH * d)
    start = dict(zip("q k v o mg cq ckv kr u s gt".split(), (int(x) for x in np.cumsum((0,) + sizes[:-1]))))
    kr = lax.slice_in_dim(w, start["kr"], start["kr"] + ROPE_DIM, axis=2)
    mg = lax.slice_in_dim(w, start["mg"], start["mg"] + M_GATES, axis=2).reshape(depth, d, 2, 2, HEADS)
    pad = jnp.zeros((depth, d, V7X_LANES - SCAN_LANES), w.dtype)
    tail = jnp.concatenate([_deinterleave(kr), _rotated(kr), mg[:, :, :, 0, :].reshape(depth, d, SCAN_LANES), pad,
                            mg[:, :, :, 1, :].reshape(depth, d, SCAN_LANES), pad], axis=2).astype(_BF16)
    bw = BRANCH_WIDTH
    moves = ((start["q"], 0, 4 * bw), (start["u"], 4 * bw, 2 * bw), (start["gt"], 6 * bw, N_BRANCH * d),
             (start["cq"], 6 * bw + N_BRANCH * d, Q_LORA + KV_LORA))
    n_out = 6 * bw + N_BRANCH * d + Q_LORA + KV_LORA + tail.shape[2]
    return pl.pallas_call(
        functools.partial(_w_in_layout_kernel, moves=moves),
        grid=(depth, d // ROW_TILE),
        in_specs=[pl.BlockSpec((None, ROW_TILE, n_in), lambda l, i: (l, i, 0)),
                  pl.BlockSpec((None, ROW_TILE, tail.shape[2]), lambda l, i: (l, i, 0))],
        out_specs=pl.BlockSpec((None, ROW_TILE, n_out), lambda l, i: (l, i, 0)),
        out_shape=jax.ShapeDtypeStruct((depth, d, n_out), _BF16),
        compiler_params=_cparams(2),
        name="w_in_layout",
    )(w, tail)


def _layout_gate_bias(bg):
    depth = bg.shape[0]
    bg = bg.reshape(depth, 2, 2, HEADS)
    pad = jnp.zeros((depth, V7X_LANES - SCAN_LANES), bg.dtype)
    return jnp.concatenate([bg[:, :, 0, :].reshape(depth, -1), pad, bg[:, :, 1, :].reshape(depth, -1), pad],
                           axis=1).reshape(depth, 1, GATE_W)


def _layout_w_uq(w):
    depth = w.shape[0]
    w = w.reshape(depth, Q_LORA, HEADS, HEAD_DIM + ROPE_DIM)
    nope, rope = w[..., :HEAD_DIM], w[..., HEAD_DIM:]
    return jnp.concatenate([nope, _deinterleave(rope), _rotated(rope)], axis=-1).reshape(depth, Q_LORA, -1).astype(_BF16)


def _layout_w_ukv(w):
    depth = w.shape[0]
    w = w.reshape(depth, KV_LORA, HEADS, 2 * HEAD_DIM).astype(_BF16)
    return w[..., :HEAD_DIM].reshape(depth, KV_LORA, -1), w[..., HEAD_DIM:].reshape(depth, KV_LORA, -1)


def _rope_tables(ctx_len, n_latent):
    rows = n_latent // GRID_W
    row = jnp.repeat(jnp.arange(rows), GRID_W)
    col = jnp.tile(jnp.arange(GRID_W), rows)
    n_freq = ROPE_DIM // 4
    inv = ROPE_BASE ** (-jnp.arange(n_freq, dtype=_F32) / n_freq)
    ang = jnp.concatenate([row[:, None] * inv, col[:, None] * inv], axis=-1)
    zeros = jnp.zeros((n_latent, V7X_LANES - ROPE_DIM), _F32)
    cos_l = jnp.concatenate([jnp.cos(ang), jnp.cos(ang), zeros], axis=1)
    sin_l = jnp.concatenate([jnp.sin(ang), jnp.sin(ang), zeros], axis=1)
    cos_c = jnp.concatenate([jnp.ones((ctx_len, ROPE_DIM), _F32), jnp.zeros((ctx_len, V7X_LANES - ROPE_DIM), _F32)], axis=1)
    return jnp.concatenate([cos_c, cos_l], axis=0), jnp.concatenate([jnp.zeros_like(cos_c), sin_l], axis=0)


def kernel(x, c, ctx, c_ctx, w_ada, b_ada, g_mix, w_in, w_qkconv, b_qkconv, b_mgate, g_mhead, g_qnorm, w_uq,
           g_kvnorm, w_ukv, g_sgu, w_s, b_s, w_branch, w_out, g_ffn, w_up, w_ffconv, b_ffconv, w_down, g_final):
    b, s, d = x.shape
    ctx_len = ctx.shape[1]
    depth = w_in.shape[0]
    assert ctx_len % ROW_TILE == 0 and s % ROW_TILE == 0 and s % GRID_W == 0
    nctx_tiles = ctx_len // ROW_TILE
    row_param = lambda a: a.reshape(depth, 1, a.shape[-1])

    cos_t, sin_t = _rope_tables(ctx_len, s)
    cond_rows = jnp.concatenate([c, c_ctx[None, :], jnp.zeros((COND_PAD_ROWS - 1, d), c.dtype)], axis=0)
    mod = _ada(cond_rows, w_ada, b_ada)
    w_in_l = _layout_w_in(w_in)
    b_gate_l = _layout_gate_bias(b_mgate)
    w_uq_l = _layout_w_uq(w_uq)
    w_k_l, w_v_l = _layout_w_ukv(w_ukv)
    b_s_t = jnp.pad(jnp.swapaxes(b_s, 1, 2), ((0, 0), (0, 0), (0, V7X_LANES - HEADS)))
    w_s16, w_branch16, w_out16 = w_s.astype(_BF16), w_branch.astype(_BF16), w_out.astype(_BF16)
    w_up16, w_down16 = w_up.astype(_BF16), w_down.astype(_BF16)
    g_mix_r, g_mhead_r, g_sgu_r, g_ffn_r = row_param(g_mix), row_param(g_mhead), row_param(g_sgu), row_param(g_ffn)
    g_qn_r, g_kvn_r, b_qkconv_r, b_ffconv_r = row_param(g_qnorm), row_param(g_kvnorm), row_param(b_qkconv), row_param(b_ffconv)

    x_ctx, x_lat = ctx, x
    for l in range(depth):
        last = l == depth - 1
        q_act, kt_act, vo, us, gt, at, gates = _proj(x_ctx, x_lat, mod, g_mix_r, w_in_l, w_qkconv, b_qkconv_r, b_gate_l,
                                                     l, nctx_tiles)
        hf, hb = _mlstm(q_act, kt_act, vo, gates, ctx_len // CHUNK)
        yb_ctx, yb_lat = _attn(at, cos_t, sin_t, g_qn_r, w_uq_l, g_kvn_r, w_k_l, w_v_l, l, ctx_len, not last)
        x_mid, h2 = _merge(hf, hb, vo, yb_ctx, yb_lat, us, gt, x_ctx, x_lat, mod, g_mhead_r, g_sgu_r, w_s16, b_s_t,
                           w_branch16, w_out16, g_ffn_r, l, nctx_tiles, last)
        ffn = functools.partial(_ffn, h2, x_mid, mod, w_up16, w_ffconv, b_ffconv_r, w_down16, g_final, l)
        if last:
            return ffn(0, s // ROW_TILE, FFN_SUB, False, True)
        x_lat = ffn(nctx_tiles, s // ROW_TILE, FFN_SUB, False, False)
        x_ctx = ffn(0, nctx_tiles, 1, True, False)
```

```python
import functools

import jax
import jax.numpy as jnp
import numpy as np
from jax import lax
from jax.experimental import pallas as pl
from jax.experimental.pallas import tpu as pltpu

EPS = 1e-6
GRID_W = 64
ROPE_BASE = 10000.0
LOG2_E = 1.4426950408889634

HEADS = 4
HEAD_DIM = 128
CHUNK = 128
BRANCH_WIDTH = HEADS * HEAD_DIM
Q_LORA = 384
KV_LORA = 256
ROPE_DIM = 64
N_BRANCH = 3
M_GATES = 4 * HEADS
KV_SRC = KV_LORA + 2 * ROPE_DIM
ATTN_W = Q_LORA + KV_SRC
SCAN_LANES = 2 * HEADS

V7X_LANES = 128
V7X_BF16_SUBLANE_TILE = 16
V7X_MXU_WIDTH = 256
V7X_VMEM_BYTES = 64 * 1024 * 1024
V7X_VMEM_LIMIT = V7X_VMEM_BYTES * 7 // 8

GATE_W = 2 * V7X_LANES
GATE_OUT_W = 3 * V7X_LANES
COND_PAD_ROWS = 8
ROW_TILE = V7X_MXU_WIDTH
FFN_HALO = V7X_BF16_SUBLANE_TILE
FFN_CHUNK = V7X_MXU_WIDTH
FFN_SUB = 4

_BF16 = jnp.bfloat16
_F32 = jnp.float32


def _cparams(n_axes):
    return pltpu.CompilerParams(dimension_semantics=("arbitrary",) * n_axes, vmem_limit_bytes=V7X_VMEM_LIMIT)


def _const_spec(shape):
    return pl.BlockSpec(tuple(shape), lambda *_: (0,) * len(shape))


def _layer_spec(stacked_shape, layer):
    shape = tuple(stacked_shape[1:])
    return pl.BlockSpec((None,) + shape, lambda *_: (layer,) + (0,) * len(shape), pipeline_mode=pl.Buffered(1))


def _sigmoid(x):
    return 1.0 / (1.0 + jnp.exp(-x))


def _silu(x):
    return x * _sigmoid(x)


def _gelu_tanh(x):
    return x * (0.5 * (1.0 + jnp.tanh(0.7978845608028654 * (x + 0.044715 * (x * x * x)))))


def _rms(x):
    return x * lax.rsqrt(jnp.mean(x * x, axis=-1, keepdims=True) + EPS)


def _dot(a, b):
    return jnp.dot(a, b, preferred_element_type=_F32)


def _dot_nt(a, b):
    return lax.dot_general(a, b, (((1,), (1,)), ((), ())), preferred_element_type=_F32)


def _ada_kernel(c_ref, w_ref, b_ref, o_ref):
    cond = _silu(c_ref[...])
    o_ref[...] = jnp.dot(cond, w_ref[...], precision=lax.Precision.HIGHEST, preferred_element_type=_F32) + b_ref[...]


def _ada(cond_rows, w_ada, b_ada):
    rows, d = cond_rows.shape
    depth, _, n = w_ada.shape
    out = pl.pallas_call(
        _ada_kernel,
        grid=(depth, n // d),
        in_specs=[pl.BlockSpec((rows, d), lambda l, j: (0, 0)),
                  pl.BlockSpec((None, d, d), lambda l, j: (l, 0, j)),
                  pl.BlockSpec((None, 1, d), lambda l, j: (l, 0, j))],
        out_specs=pl.BlockSpec((None, rows, d), lambda l, j: (l, 0, j)),
        out_shape=jax.ShapeDtypeStruct((depth, rows, n), _F32),
        compiler_params=_cparams(2),
        name="ada",
    )(cond_rows, w_ada, b_ada.reshape(depth, 1, n))
    return out.reshape(depth, rows, 1, n)


PROJ_TILE_INPUTS = 5
QK_HALO = V7X_BF16_SUBLANE_TILE


def _proj_kernel(*refs, d, nctx_tiles, tiles):
    per_tile = (refs[0:PROJ_TILE_INPUTS], refs[PROJ_TILE_INPUTS:2 * PROJ_TILE_INPUTS])
    (g_ref, w_ref, wcv_ref, bcv_ref, bg_ref,
     q_ref, kt_ref, vo_ref, us_ref, gt_ref, at_ref, gate_ref) = refs[2 * PROJ_TILE_INPUTS:]
    bw = BRANCH_WIDTH
    col_at = 6 * bw + N_BRANCH * d
    ext_rows = ROW_TILE + 2 * QK_HALO
    inner = slice(QK_HALO, QK_HALO + ROW_TILE)
    chunk_row = lax.broadcasted_iota(jnp.int32, (ROW_TILE, 1), 0) % CHUNK
    fwd_lane = lax.broadcasted_iota(jnp.int32, (1, V7X_LANES), 1) < HEADS

    def normalise(h):
        xc_ref, xl_ref, mod_ref, xp_ref, xn_ref = per_tile[h]
        tile = (2 * pl.program_id(0) + h) % tiles
        norm = lambda x: (_rms(x) * g_ref[...] * (1.0 + mod_ref[:, d:2 * d]) + mod_ref[:, 0:d]).astype(_BF16)
        hn = norm(jnp.where(tile < nctx_tiles, xc_ref[...], xl_ref[...]))
        zeros = jnp.zeros((QK_HALO, d), _BF16)
        h_prev = jnp.where(tile > nctx_tiles, norm(xp_ref[...]), zeros)
        h_next = jnp.where(jnp.logical_and(tile >= nctx_tiles, tile < tiles - 1), norm(xn_ref[...]), zeros)
        return hn, jnp.concatenate([h_prev, hn, h_next], axis=0)

    cw = V7X_MXU_WIDTH

    def conv_columns(h, h_ext, c):
        rows = slice(h * ROW_TILE, (h + 1) * ROW_TILE)
        cols = slice(c * cw, (c + 1) * cw)
        qk = _dot(h_ext, w_ref[:, cols])
        qk_prev = pltpu.roll(qk, 1, axis=0)[inner, :]
        qk_next = pltpu.roll(qk, ext_rows - 1, axis=0)[inner, :]
        a = _silu(bcv_ref[:, cols] + qk_prev * wcv_ref[0:1, cols] + qk[inner, :] * wcv_ref[1:2, cols]
                  + qk_next * wcv_ref[2:3, cols])
        if c < bw // cw:
            q_ref[rows, cols] = a.astype(_BF16)
        else:
            for sub in range(cw // HEAD_DIM):
                head = (c * cw - bw) // HEAD_DIM + sub
                k = a[:, sub * HEAD_DIM:(sub + 1) * HEAD_DIM] * (HEAD_DIM ** -0.5)
                kt_ref[h, head] = k.T.astype(_BF16)

    def column_groups(h, hn, h_ext):
        rows = slice(h * ROW_TILE, (h + 1) * ROW_TILE)
        conv_columns(h, h_ext, 0)
        us_ref[rows, :] = _gelu_tanh(_dot(hn, w_ref[:, 4 * bw:6 * bw])).astype(_BF16)
        for g in range(N_BRANCH):
            conv_columns(h, h_ext, g + 1)
            cols = slice(6 * bw + g * d, 6 * bw + (g + 1) * d)
            gt_ref[rows, g * d:(g + 1) * d] = _sigmoid(_dot(hn, w_ref[:, cols])).astype(_BF16)
        vo_ref[rows, bw:2 * bw] = _sigmoid(_dot(hn, w_ref[:, 3 * bw:4 * bw])).astype(_BF16)
        vo_ref[rows, 0:bw] = _dot(hn, w_ref[:, 2 * bw:3 * bw]).astype(_BF16)
        at_ref[rows, :] = _dot(hn, w_ref[:, col_at:col_at + ATTN_W]).astype(_BF16)
        gates = _dot(hn, w_ref[:, col_at + ATTN_W:col_at + ATTN_W + GATE_W]) + bg_ref[...]
        gate_f = gates[:, V7X_LANES:GATE_W]
        log_f = jnp.minimum(gate_f, 0.0) - jnp.log(1.0 + jnp.exp(-jnp.abs(gate_f)))
        cum, step = log_f, 1
        while step < CHUNK:
            cum = cum + jnp.where(chunk_row >= step, pltpu.roll(cum, step, axis=0), 0.0)
            step *= 2
        total = jnp.concatenate([jnp.broadcast_to(cum[(n + 1) * CHUNK - 1:(n + 1) * CHUNK, :], (CHUNK, V7X_LANES))
                                 for n in range(ROW_TILE // CHUNK)], axis=0)
        bcum = jnp.where(fwd_lane, cum, total - cum + log_f)
        r = gates[:, 0:V7X_LANES] - bcum
        run_f, run_b, step = r, r, 1
        while step < CHUNK:
            run_f = jnp.maximum(run_f, jnp.where(chunk_row >= step, pltpu.roll(run_f, step, axis=0), -jnp.inf))
            run_b = jnp.maximum(run_b, jnp.where(chunk_row < CHUNK - step,
                                                 pltpu.roll(run_b, ROW_TILE - step, axis=0), -jnp.inf))
            step *= 2
        gate_ref[rows, 0:V7X_LANES] = r
        gate_ref[rows, V7X_LANES:2 * V7X_LANES] = bcum
        gate_ref[rows, 2 * V7X_LANES:GATE_OUT_W] = jnp.where(fwd_lane, run_f, run_b)

    assert 2 * bw // cw == N_BRANCH + 1
    tiles_normed = [normalise(0), normalise(1)]
    for h, (hn, h_ext) in enumerate(tiles_normed):
        column_groups(h, hn, h_ext)


def _pair_tile(h, tiles):
    def index(n):
        k = 2 * n + h
        return k // tiles, k % tiles
    return index


def _two_source_specs(at, off, width, nctx_tiles):
    return [pl.BlockSpec((None, ROW_TILE, width), lambda n: (at(n)[0], jnp.minimum(at(n)[1] + off, nctx_tiles - 1), 0)),
            pl.BlockSpec((None, ROW_TILE, width), lambda n: (at(n)[0], jnp.maximum(at(n)[1] + off - nctx_tiles, 0), 0))]


def _stream_tile_specs(at, off, d, nctx_tiles, mod, layer):
    batch_row = mod.shape[1] - COND_PAD_ROWS
    return _two_source_specs(at, off, d, nctx_tiles) + [
        pl.BlockSpec((None, None, 1, mod.shape[3]),
                     lambda n: (layer, jnp.where(at(n)[1] + off < nctx_tiles, batch_row, at(n)[0]), 0, 0))]


def _proj(x_ctx, x_lat, mod, g_mix, w_in, w_qkconv, b_qkconv, b_gate, layer, nctx_tiles):
    b, _, d = x_ctx.shape
    lat_rows = x_lat.shape[1]
    tiles = nctx_tiles + lat_rows // ROW_TILE
    bw = BRANCH_WIDTH
    widths = (bw, 2 * bw, 2 * bw, N_BRANCH * d, ATTN_W, GATE_OUT_W)
    dtypes = (_BF16, _BF16, _BF16, _BF16, _BF16, _F32)
    assert nctx_tiles == 1 and sum(widths[:-1]) + bw + GATE_W == w_in.shape[2] and (b * tiles) % 2 == 0
    rows = b * tiles * ROW_TILE
    hb = ROW_TILE // QK_HALO

    def tile_specs(h):
        at = _pair_tile(h, tiles)
        lat_tile = lambda n: jnp.maximum(at(n)[1] - nctx_tiles, 0)
        halo = lambda block_of: pl.BlockSpec((None, QK_HALO, d), lambda n: (at(n)[0], block_of(n), 0))
        return _stream_tile_specs(at, 0, d, nctx_tiles, mod, layer) + [
            halo(lambda n: jnp.maximum(lat_tile(n) * hb - 1, 0)),
            halo(lambda n: jnp.minimum((lat_tile(n) + 1) * hb, lat_rows // QK_HALO - 1))]

    flat = lambda w: pl.BlockSpec((2 * ROW_TILE, w), lambda n: (n, 0))
    tile_args = (x_ctx, x_lat, mod, x_lat, x_lat)
    params = (g_mix, w_in, w_qkconv, b_qkconv, b_gate)
    outs = pl.pallas_call(
        functools.partial(_proj_kernel, d=d, nctx_tiles=nctx_tiles, tiles=tiles),
        grid=(b * tiles // 2,),
        in_specs=tile_specs(0) + tile_specs(1) + [_layer_spec(a.shape, layer) for a in params],
        out_specs=[flat(widths[0]), pl.BlockSpec((2, HEADS, HEAD_DIM, ROW_TILE), lambda n: (n, 0, 0, 0))]
        + [flat(w) for w in widths[1:]],
        out_shape=[jax.ShapeDtypeStruct((rows, widths[0]), dtypes[0]),
                   jax.ShapeDtypeStruct((b * tiles, HEADS, HEAD_DIM, ROW_TILE), _BF16)]
        + [jax.ShapeDtypeStruct((rows, w), dt) for w, dt in zip(widths[1:], dtypes[1:])],
        compiler_params=_cparams(1),
        name="proj",
    )(*tile_args, *tile_args, *params)
    q, kt = outs[0].reshape(b, tiles * ROW_TILE, bw), outs[1]
    return [q, kt] + [o.reshape(b, tiles * ROW_TILE, o.shape[1]) for o in outs[2:]]


def _mlstm_kernel(q_f, kt_f, v_f, gf_cur, gf_nxt, q_b, kt_b, v_b, gb_cur, gb_nxt, hf_ref, hb_ref,
                  c_scr, m_row, m_col, *pre):
    tc = CHUNK
    low, high = pl.ds(0, tc), pl.ds(tc, tc)

    def scan_step(slot, fwd_rows, bwd_rows):
        views = [(q.at[rows, :], kt.at[:, :, rows], v.at[rows, :], out.at[rows, :])
                 for rows, (q, kt, v, out) in ((fwd_rows, (q_f, kt_f, v_f, hf_ref)), (bwd_rows, (q_b, kt_b, v_b, hb_ref)))]
        _mlstm_scan_step(slot, is_fwd, causal, views[0], views[1], c_scr, m_row, m_col)

    pre_a, pre_b = pre[:len(pre) // 2], pre[len(pre) // 2:]
    lane = lax.broadcasted_iota(jnp.int32, (1, V7X_LANES), 1)
    row = lax.broadcasted_iota(jnp.int32, (tc, 1), 0)
    is_fwd = lane < HEADS
    r_idx = lax.broadcasted_iota(jnp.int32, (tc, tc), 0)
    c_idx = lax.broadcasted_iota(jnp.int32, (tc, tc), 1)
    causal = (c_idx <= r_idx, c_idx >= r_idx)

    def gate_prologue(g_f, g_b, slot):
        bcum_ref, run_ref, rt_ref, dt_ref, rows_ref, blc_ref, dmc_ref = slot
        pick = lambda part: jnp.where(is_fwd, g_f[:, part * V7X_LANES:(part + 1) * V7X_LANES],
                                      g_b[:, part * V7X_LANES:(part + 1) * V7X_LANES])
        r, bcum, run = pick(0), pick(1), pick(2)
        b_last = jnp.where(is_fwd, bcum[tc - 1:tc, :], bcum[0:1, :])
        decay = b_last + r
        r_t = r.T[0:SCAN_LANES, :]
        decay_t = decay.T[0:SCAN_LANES, :]
        bcum_ref[...] = bcum
        run_ref[...] = run
        rt_ref[...] = r_t
        dt_ref[...] = decay_t
        rows_ref[0:1, :] = b_last
        rows_ref[1:2, :] = jnp.max(decay, axis=0, keepdims=True)
        blc_ref[...] = decay_t - r_t
        dmc_ref[...] = jnp.broadcast_to(jnp.max(decay_t, axis=1, keepdims=True), (SCAN_LANES, tc))

    @pl.when(pl.program_id(1) == 0)
    def _():
        c_scr[...] = jnp.zeros_like(c_scr)
        m_row[...] = jnp.zeros_like(m_row)
        m_col[...] = jnp.zeros_like(m_col)
        gate_prologue(gf_cur.at[low, :], gb_cur.at[high, :], pre_a)

    gate_prologue(gf_cur.at[high, :], gb_cur.at[low, :], pre_b)
    scan_step(pre_a, low, high)
    gate_prologue(gf_nxt.at[low, :], gb_nxt.at[high, :], pre_a)
    scan_step(pre_b, high, low)


def _mlstm_scan_step(slot, is_fwd, causal, refs_f, refs_b, c_scr, m_row, m_col):
    tc = CHUNK
    bcum_ref, run_ref, rt_ref, dt_ref, rows_ref, blc_ref, dmc_ref = slot
    bcum = bcum_ref[...]
    b_last, decay_max = rows_ref[0:1, :], rows_ref[1:2, :]
    m_old = m_row[...]
    g = jnp.maximum(m_old, run_ref[...])
    m_new = jnp.maximum(b_last + m_old, decay_max)
    w_carry = jnp.exp(b_last + m_old - m_new)
    m_col_new = jnp.maximum(blc_ref[...] + m_col[...], dmc_ref[...])
    w_src_t = jnp.exp(dt_ref[...] - m_col_new)
    r_t = rt_ref[...]
    ones = jnp.ones((tc, HEAD_DIM), _BF16)

    scans = [(dr * HEADS + h, refs, h, slice(h * HEAD_DIM, (h + 1) * HEAD_DIM), causal[dr])
             for dr, refs in enumerate((refs_f, refs_b)) for h in range(HEADS)]
    v_ext = [jnp.concatenate([refs[2][:, hs], ones], axis=1) for _, refs, _, hs, _ in scans]
    s16, g_b = [], []
    for l, (q_ref, kt_ref, _, _), h, hs, mask in scans:
        g_l = jnp.broadcast_to(g[:, l:l + 1], (tc, tc))
        w_intra = jnp.exp(jnp.where(mask, r_t[l:l + 1, :] - g_l, -jnp.inf))
        s16.append((_dot(q_ref[:, hs], kt_ref[h]) * w_intra).astype(_BF16))
        g_b.append(g_l)
    qc = []
    for l, (q_ref, kt_ref, _, _), h, hs, _ in scans:
        c_old = c_scr[l]
        qc.append(_dot(q_ref[:, hs], c_old.astype(_BF16)))
        kw_t = (kt_ref[h].astype(_F32) * w_src_t[l:l + 1, :]).astype(_BF16)
        c_scr[l] = w_carry[:, l:l + 1] * c_old + _dot(kw_t, v_ext[l])
    for l, (_, _, _, out_ref), h, hs, _ in scans:
        w_inter = jnp.exp(m_old[:, l:l + 1] - g_b[l])
        b_l = jnp.broadcast_to(bcum[:, l:l + 1], (tc, tc))
        sv = _dot(s16[l], v_ext[l])
        num = w_inter * qc[l][:, 0:HEAD_DIM] + sv[:, 0:HEAD_DIM]
        den = w_inter * qc[l][:, HEAD_DIM:] + sv[:, HEAD_DIM:]
        out_ref[:, hs] = num / jnp.maximum(jnp.abs(den), jnp.exp(-(b_l + g_b[l])))
    m_row[...] = m_new
    m_col[...] = m_col_new


def _mlstm(q, kt, vo, gates, nctx):
    b, t, _ = q.shape
    blk = 2 * CHUNK
    nblocks, nctx_blocks = t // blk, nctx // 2
    assert t % blk == 0 and nctx % 2 == 0

    def bwd_block(j):
        return jnp.where(j < nctx_blocks, nctx_blocks - 1 - j, nblocks - 1 - (j - nctx_blocks))

    def specs(block_of):
        nxt = lambda j: block_of(jnp.minimum(j + 1, nblocks - 1))
        return [
            pl.BlockSpec((None, blk, BRANCH_WIDTH), lambda bi, j: (bi, block_of(j), 0)),
            pl.BlockSpec((None, HEADS, HEAD_DIM, blk), lambda bi, j: (bi * nblocks + block_of(j), 0, 0, 0)),
            pl.BlockSpec((None, blk, BRANCH_WIDTH), lambda bi, j: (bi, block_of(j), 0)),
            pl.BlockSpec((None, blk, GATE_OUT_W), lambda bi, j: (bi, block_of(j), 0)),
            pl.BlockSpec((None, blk, GATE_OUT_W), lambda bi, j: (bi, nxt(j), 0)),
        ]

    time_lane, lane_time = pltpu.VMEM((CHUNK, V7X_LANES), _F32), pltpu.VMEM((SCAN_LANES, CHUNK), _F32)
    gate_slot = [time_lane, time_lane, lane_time, lane_time, pltpu.VMEM((8, V7X_LANES), _F32), lane_time, lane_time]
    h_shape = jax.ShapeDtypeStruct((b, t, BRANCH_WIDTH), _F32)
    return pl.pallas_call(
        _mlstm_kernel,
        grid=(b, nblocks),
        in_specs=specs(lambda j: j) + specs(bwd_block),
        out_specs=[pl.BlockSpec((None, blk, BRANCH_WIDTH), lambda bi, j: (bi, j, 0)),
                   pl.BlockSpec((None, blk, BRANCH_WIDTH), lambda bi, j: (bi, bwd_block(j), 0))],
        out_shape=[h_shape, h_shape],
        scratch_shapes=[pltpu.VMEM((SCAN_LANES, HEAD_DIM, 2 * HEAD_DIM), _F32), pltpu.VMEM((1, V7X_LANES), _F32),
                        lane_time] + gate_slot + gate_slot,
        compiler_params=_cparams(2),
        name="mlstm",
    )(q, kt, vo, gates, gates, q, kt, vo, gates, gates)


ATTN_SUB = 2
VT_ONES = V7X_BF16_SUBLANE_TILE


def _attn_kernel(cq_ctx_ref, cq_a_ref, cq_b_ref, kv_ref, ca_ref, sa_ref, gq_ref, wq_ref, gkv_ref, wk_ref, wv_ref,
                 o_ctx_ref, o_lat_ref, k_scr, vt_scr, *, ctx_len, need_ctx, scale):
    i = pl.program_id(1)
    t = kv_ref.shape[0]
    head_w = 2 * HEAD_DIM

    def rope(y, cos_t, sin_t):
        return y * cos_t + pltpu.roll(y, ROPE_DIM, axis=1) * sin_t

    def attend(cq_rows, row0, n_keys, o_ref):
        table_rows = pl.ds(pl.multiple_of(row0, ROW_TILE), cq_rows.shape[0])
        cos_q, sin_q = ca_ref[table_rows, :], sa_ref[table_rows, :]
        cq = (_rms(cq_rows.astype(_F32)) * gq_ref[...]).astype(_BF16)
        qa = _dot(cq, wq_ref[...]) * (scale * LOG2_E)

        def scores(h):
            q_nope = qa[:, h * head_w:h * head_w + HEAD_DIM]
            q_rope = rope(qa[:, h * head_w + HEAD_DIM:(h + 1) * head_w], cos_q, sin_q)
            qh = jnp.concatenate([q_nope, q_rope], axis=1).astype(_BF16)
            return _dot_nt(qh, k_scr[0:n_keys, h * head_w:(h + 1) * head_w])

        s_next = scores(0)
        for h in range(HEADS):
            s = s_next
            if h + 1 < HEADS:
                s_next = scores(h + 1)
            e = jnp.exp2(s - jnp.max(s, axis=1, keepdims=True)).astype(_BF16)
            ot = _dot_nt(vt_scr[h, :, 0:n_keys], e)
            ot = ot[0:HEAD_DIM, :] / ot[HEAD_DIM:HEAD_DIM + 1, :]
            o_ref[:, h * HEAD_DIM:(h + 1) * HEAD_DIM] = ot.T.astype(o_ref.dtype)

    @pl.when(i == 0)
    def _():
        for r in range(t // ROW_TILE):
            rows = slice(r * ROW_TILE, (r + 1) * ROW_TILE)
            ckv = (_rms(kv_ref[rows, 0:KV_LORA].astype(_F32)) * gkv_ref[...]).astype(_BF16)
            k_nope = _dot(ckv, wk_ref[...])
            k_rope = rope(kv_ref[rows, KV_LORA:KV_SRC].astype(_F32), ca_ref[rows, :], sa_ref[rows, :]).astype(_BF16)
            for h in range(HEADS):
                k_scr[rows, h * head_w:h * head_w + HEAD_DIM] = k_nope[:, h * HEAD_DIM:(h + 1) * HEAD_DIM].astype(_BF16)
                k_scr[rows, h * head_w + HEAD_DIM:(h + 1) * head_w] = k_rope
            vv = _dot(ckv, wv_ref[...])
            for h in range(HEADS):
                vt_scr[h, 0:HEAD_DIM, rows] = vv[:, h * HEAD_DIM:(h + 1) * HEAD_DIM].T.astype(_BF16)
                vt_scr[h, HEAD_DIM:, rows] = jnp.ones((VT_ONES, ROW_TILE), _BF16)
        if need_ctx:
            attend(cq_ctx_ref[...], 0, ctx_len, o_ctx_ref)
        else:
            o_ctx_ref[...] = jnp.zeros_like(o_ctx_ref)

    @pl.when(i > 0)
    def _():
        cq_rows = jnp.concatenate([cq_a_ref[...], cq_b_ref[...]], axis=0)
        attend(cq_rows, ctx_len + (i - 1) * (ATTN_SUB * ROW_TILE), t, o_lat_ref)


def _attn(at, cos_t, sin_t, g_qn, w_q, g_kvn, w_k, w_v, layer, ctx_len, need_ctx):
    b, t, _ = at.shape
    assert ctx_len == ROW_TILE and (t - ctx_len) % (ATTN_SUB * ROW_TILE) == 0
    lat_steps = (t - ctx_len) // (ATTN_SUB * ROW_TILE)
    scale = (HEAD_DIM + ROPE_DIM) ** -0.5
    lat_tile = lambda s: pl.BlockSpec((None, ROW_TILE, Q_LORA),
                                      lambda bi, i: (bi, 1 + ATTN_SUB * jnp.maximum(i - 1, 0) + s, 0))
    return pl.pallas_call(
        functools.partial(_attn_kernel, ctx_len=ctx_len, need_ctx=need_ctx, scale=scale),
        grid=(b, 1 + lat_steps),
        in_specs=[pl.BlockSpec((None, ROW_TILE, Q_LORA), lambda bi, i: (bi, 0, 0)), lat_tile(0), lat_tile(1),
                  pl.BlockSpec((None, t, KV_SRC), lambda bi, i: (bi, 0, 1)),
                  _const_spec((t, V7X_LANES)), _const_spec((t, V7X_LANES)),
                  _layer_spec(g_qn.shape, layer), _layer_spec(w_q.shape, layer), _layer_spec(g_kvn.shape, layer),
                  _layer_spec(w_k.shape, layer), _layer_spec(w_v.shape, layer)],
        out_specs=[pl.BlockSpec((None, ROW_TILE, BRANCH_WIDTH), lambda bi, i: (bi, 0, 0)),
                   pl.BlockSpec((None, ATTN_SUB * ROW_TILE, BRANCH_WIDTH), lambda bi, i: (bi, jnp.maximum(i - 1, 0), 0))],
        out_shape=[jax.ShapeDtypeStruct((b, ctx_len, BRANCH_WIDTH), _BF16),
                   jax.ShapeDtypeStruct((b, t - ctx_len, BRANCH_WIDTH), _BF16)],
        scratch_shapes=[pltpu.VMEM((t, HEADS * 2 * HEAD_DIM), _BF16),
                        pltpu.VMEM((HEADS, HEAD_DIM + VT_ONES, t), _BF16)],
        compiler_params=_cparams(2),
        name="attn",
    )(at, at, at, at, cos_t, sin_t, g_qn, w_q, g_kvn, w_k, w_v)


def _merge_mix(hf_ref, hb_ref, so_ref, yb, us_ref, gt_ref, gmh_ref, gsgu_ref, ws_ref, bs_ref, wbr_ref, wout_ref, d):
    tm = hf_ref.shape[0]
    half = d // 2
    pb = [_dot(yb, wbr_ref[1, :, c * half:(c + 1) * half]) for c in range(2)]
    hsum = hf_ref[...] + hb_ref[...]
    ya = jnp.concatenate(
        [_rms(hsum[:, h * HEAD_DIM:(h + 1) * HEAD_DIM]) for h in range(HEADS)], axis=1) * gmh_ref[...]
    ya = so_ref[...].astype(_F32) * ya
    cols = []
    for g in range(HEADS):
        gs = slice(BRANCH_WIDTH + g * HEAD_DIM, BRANCH_WIDTH + (g + 1) * HEAD_DIM)
        vn = (_rms(us_ref[:, gs].astype(_F32)) * gsgu_ref[:, g * HEAD_DIM:(g + 1) * HEAD_DIM]).astype(_BF16)
        mixed = [_dot(ws_ref[g], vn[n * CHUNK:(n + 1) * CHUNK, :]) + bs_ref[:, g:g + 1] for n in range(tm // CHUNK)]
        cols.append(jnp.concatenate(mixed, axis=0))
    yc = us_ref[:, 0:BRANCH_WIDTH].astype(_F32) * jnp.concatenate(cols, axis=1)
    ya16, yc16 = ya.astype(_BF16), yc.astype(_BF16)
    pa = [_dot(ya16, wbr_ref[0, :, c * half:(c + 1) * half]) for c in range(2)]
    pc = [_dot(yc16, wbr_ref[2, :, c * half:(c + 1) * half]) for c in range(2)]
    out = None
    for c in range(2):
        gate = lambda g: gt_ref[:, g * d + c * half:g * d + (c + 1) * half].astype(_F32)
        merged = (gate(0) * pa[c] + gate(1) * pb[c] + gate(2) * pc[c]).astype(_BF16)
        part = _dot(merged, wout_ref[c * half:(c + 1) * half, :])
        out = part if out is None else out + part
    return out


MERGE_TILE_INPUTS = 10


def _merge_kernel(*refs, d, nctx_tiles, off, tiles):
    per_tile = (refs[0:MERGE_TILE_INPUTS], refs[MERGE_TILE_INPUTS:2 * MERGE_TILE_INPUTS])
    gmh_ref, gsgu_ref, ws_ref, bs_ref, wbr_ref, wout_ref, gffn_ref, xo_ref, h2_ref = refs[2 * MERGE_TILE_INPUTS:]
    is_ctx = [(2 * pl.program_id(0) + h) % tiles + off < nctx_tiles for h in range(2)]
    outs = []
    for h, (hf_ref, hb_ref, so_ref, us_ref, gt_ref, ybc_ref, ybl_ref) in enumerate(t[0:7] for t in per_tile):
        yb = jnp.where(is_ctx[h], ybc_ref[...], ybl_ref[...])
        outs.append(_merge_mix(hf_ref, hb_ref, so_ref, yb, us_ref, gt_ref, gmh_ref, gsgu_ref, ws_ref, bs_ref, wbr_ref,
                               wout_ref, d))
    for h, tile_refs in enumerate(per_tile):
        xc_ref, xl_ref, mod_ref = tile_refs[7:10]
        rows = slice(h * ROW_TILE, (h + 1) * ROW_TILE)
        x = jnp.where(is_ctx[h], xc_ref[...], xl_ref[...])
        x_new = x + mod_ref[:, 2 * d:3 * d] * outs[h]
        xo_ref[rows, :] = x_new
        h2 = _rms(x_new) * gffn_ref[...]
        h2_ref[rows, :] = (h2 * (1.0 + mod_ref[:, 4 * d:5 * d]) + mod_ref[:, 3 * d:4 * d]).astype(h2_ref.dtype)


def _merge(hf, hb, vo, yb_ctx, yb_lat, us, gt, x_ctx, x_lat, mod, g_mhead, g_sgu, w_s, b_s_t, w_branch, w_out, g_ffn,
           layer, nctx_tiles, skip_ctx):
    b, t, _ = hf.shape
    d = x_ctx.shape[2]
    off = nctx_tiles if skip_ctx else 0
    tiles = t // ROW_TILE - off
    assert (b * tiles) % 2 == 0

    def tile_specs(h):
        at = _pair_tile(h, tiles)
        tile = lambda w, blk: pl.BlockSpec((None, ROW_TILE, w), lambda n: (at(n)[0], at(n)[1] + off, blk))
        return ([tile(BRANCH_WIDTH, 0), tile(BRANCH_WIDTH, 0), tile(BRANCH_WIDTH, 1), tile(2 * BRANCH_WIDTH, 0),
                 tile(N_BRANCH * d, 0)] + _two_source_specs(at, off, BRANCH_WIDTH, nctx_tiles)
                + _stream_tile_specs(at, off, d, nctx_tiles, mod, layer))

    tile_args = (hf, hb, vo, us, gt, yb_ctx, yb_lat, x_ctx, x_lat, mod)
    params = (g_mhead, g_sgu, w_s, b_s_t, w_branch, w_out, g_ffn)
    out_tile = lambda: pl.BlockSpec((2 * ROW_TILE, d), lambda n: (n, 0))
    rows_out = b * tiles * ROW_TILE
    x_mid, h2 = pl.pallas_call(
        functools.partial(_merge_kernel, d=d, nctx_tiles=nctx_tiles, off=off, tiles=tiles),
        grid=(b * tiles // 2,),
        in_specs=tile_specs(0) + tile_specs(1) + [_layer_spec(a.shape, layer) for a in params],
        out_specs=[out_tile(), out_tile()],
        out_shape=[jax.ShapeDtypeStruct((rows_out, d), _F32), jax.ShapeDtypeStruct((rows_out, d), _BF16)],
        compiler_params=_cparams(1),
        name="merge",
    )(*tile_args, *tile_args, *params)
    return x_mid.reshape(b, tiles * ROW_TILE, d), h2.reshape(b, tiles * ROW_TILE, d)


def _ffn_kernel(*refs, d, n_sub, groups, final_norm):
    h_refs, (hp_ref, hn_ref), x_refs = refs[0:n_sub], refs[n_sub:n_sub + 2], refs[n_sub + 2:2 * n_sub + 2]
    mod_ref, wup_ref, wcv_ref, bcv_ref, wdn_ref, gfin_ref, o_ref, ext_scr, act_scr = refs[2 * n_sub + 2:]
    tm = n_sub * ROW_TILE
    ff = wdn_ref.shape[0]
    ck = FFN_CHUNK
    group = pl.program_id(0) % groups
    ext_scr[0:FFN_HALO, :] = jnp.where(group == 0, jnp.zeros_like(hp_ref), hp_ref[...])
    for s, h_ref in enumerate(h_refs):
        ext_scr[FFN_HALO + s * ROW_TILE:FFN_HALO + (s + 1) * ROW_TILE, :] = h_ref[...]
    ext_scr[FFN_HALO + tm:, :] = jnp.where(group == groups - 1, jnp.zeros_like(hn_ref), hn_ref[...])
    ext_rows = tm + 2 * FFN_HALO
    inner = slice(FFN_HALO, FFN_HALO + tm)

    def conv(cols):
        a = _dot(ext_scr[...], wup_ref[:, cols])
        a_prev = pltpu.roll(a, 1, axis=0)[inner, :]
        a_next = pltpu.roll(a, ext_rows - 1, axis=0)[inner, :]
        return (bcv_ref[:, cols] + a_prev * wcv_ref[0:1, cols] + a[inner, :] * wcv_ref[1:2, cols]
                + a_next * wcv_ref[2:3, cols])

    for c in range(ff // ck):
        gate = conv(slice(c * ck, (c + 1) * ck))
        val = conv(slice(ff + c * ck, ff + (c + 1) * ck))
        act_scr[:, c * ck:(c + 1) * ck] = (_silu(gate) * val).astype(_BF16)

    down = _dot(act_scr[...], wdn_ref[...])
    for s, x_ref in enumerate(x_refs):
        rows = slice(s * ROW_TILE, (s + 1) * ROW_TILE)
        x_new = x_ref[...] + mod_ref[:, 5 * d:6 * d] * down[rows, :]
        if final_norm:
            x_new = _rms(x_new) * gfin_ref[...]
        o_ref[rows, :] = x_new


def _ffn(h2, x, mod, w_up, w_cv, b_cv, w_dn, g_final, layer, first_tile, seg_tiles, n_sub, is_ctx, final_norm):
    b, r, d = x.shape
    assert seg_tiles % n_sub == 0
    groups = seg_tiles // n_sub
    hb = ROW_TILE // FFN_HALO
    batch_row = mod.shape[1] - COND_PAD_ROWS
    tile0 = lambda n: first_tile + (n % groups) * n_sub
    sub_specs = [pl.BlockSpec((None, ROW_TILE, d), lambda n, s=s: (n // groups, tile0(n) + s, 0)) for s in range(n_sub)]
    return pl.pallas_call(
        functools.partial(_ffn_kernel, d=d, n_sub=n_sub, groups=groups, final_norm=final_norm),
        grid=(b * groups,),
        in_specs=sub_specs
        + [pl.BlockSpec((None, FFN_HALO, d), lambda n: (n // groups, jnp.maximum(tile0(n) * hb - 1, 0), 0)),
           pl.BlockSpec((None, FFN_HALO, d),
                        lambda n: (n // groups, jnp.minimum((tile0(n) + n_sub) * hb, r // FFN_HALO - 1), 0))]
        + sub_specs
        + [pl.BlockSpec((None, None, 1, mod.shape[3]), lambda n: (layer, batch_row if is_ctx else n // groups, 0, 0))]
        + [_layer_spec(a.shape, layer) for a in (w_up, w_cv, b_cv, w_dn)] + [_const_spec((1, d))],
        out_specs=pl.BlockSpec((None, n_sub * ROW_TILE, d), lambda n: (n // groups, n % groups, 0)),
        out_shape=jax.ShapeDtypeStruct((b, seg_tiles * ROW_TILE, d), _F32),
        scratch_shapes=[pltpu.VMEM((n_sub * ROW_TILE + 2 * FFN_HALO, d), _BF16),
                        pltpu.VMEM((n_sub * ROW_TILE, w_dn.shape[1]), _BF16)],
        compiler_params=_cparams(1),
        name="ffn",
    )(*([h2] * (n_sub + 2)), *([x] * n_sub), mod, w_up, w_cv, b_cv, w_dn, g_final.reshape(1, d))


def _deinterleave(w):
    return jnp.concatenate([w[..., 0::2], w[..., 1::2]], axis=-1)


def _rotated(w):
    return jnp.concatenate([-w[..., 1::2], w[..., 0::2]], axis=-1)


def _w_in_layout_kernel(w_ref, tail_ref, o_ref, *, moves):
    for src, dst, width in moves:
        o_ref[:, dst:dst + width] = w_ref[:, src:src + width].astype(_BF16)
    o_ref[:, o_ref.shape[1] - tail_ref.shape[1]:] = tail_ref[...]


def _layout_w_in(w):
    depth, d, n_in = w.shape
    sizes = (BRANCH_WIDTH,) * 4 + (M_GATES, Q_LORA, KV_LORA, ROPE_DIM, BRANCH_WIDTH, BRANCH_WIDTH, N_BRANCH * d)
    start = dict(zip("q k v o mg cq ckv kr u s gt".split(), (int(x) for x in np.cumsum((0,) + sizes[:-1]))))
    kr = lax.slice_in_dim(w, start["kr"], start["kr"] + ROPE_DIM, axis=2)
    mg = lax.slice_in_dim(w, start["mg"], start["mg"] + M_GATES, axis=2).reshape(depth, d, 2, 2, HEADS)
    pad = jnp.zeros((depth, d, V7X_LANES - SCAN_LANES), w.dtype)
    tail = jnp.concatenate([_deinterleave(kr), _rotated(kr), mg[:, :, :, 0, :].reshape(depth, d, SCAN_LANES), pad,
                            mg[:, :, :, 1, :].reshape(depth, d, SCAN_LANES), pad], axis=2).astype(_BF16)
    bw = BRANCH_WIDTH
    moves = ((start["q"], 0, 4 * bw), (start["u"], 4 * bw, 2 * bw), (start["gt"], 6 * bw, N_BRANCH * d),
             (start["cq"], 6 * bw + N_BRANCH * d, Q_LORA + KV_LORA))
    n_out = 6 * bw + N_BRANCH * d + Q_LORA + KV_LORA + tail.shape[2]
    return pl.pallas_call(
        functools.partial(_w_in_layout_kernel, moves=moves),
        grid=(depth, d // ROW_TILE),
        in_specs=[pl.BlockSpec((None, ROW_TILE, n_in), lambda l, i: (l, i, 0)),
                  pl.BlockSpec((None, ROW_TILE, tail.shape[2]), lambda l, i: (l, i, 0))],
        out_specs=pl.BlockSpec((None, ROW_TILE, n_out), lambda l, i: (l, i, 0)),
        out_shape=jax.ShapeDtypeStruct((depth, d, n_out), _BF16),
        compiler_params=_cparams(2),
        name="w_in_layout",
    )(w, tail)


def _layout_gate_bias(bg):
    depth = bg.shape[0]
    bg = bg.reshape(depth, 2, 2, HEADS)
    pad = jnp.zeros((depth, V7X_LANES - SCAN_LANES), bg.dtype)
    return jnp.concatenate([bg[:, :, 0, :].reshape(depth, -1), pad, bg[:, :, 1, :].reshape(depth, -1), pad],
                           axis=1).reshape(depth, 1, GATE_W)


def _layout_w_uq(w):
    depth = w.shape[0]
    w = w.reshape(depth, Q_LORA, HEADS, HEAD_DIM + ROPE_DIM)
    nope, rope = w[..., :HEAD_DIM], w[..., HEAD_DIM:]
    return jnp.concatenate([nope, _deinterleave(rope), _rotated(rope)], axis=-1).reshape(depth, Q_LORA, -1).astype(_BF16)


def _layout_w_ukv(w):
    depth = w.shape[0]
    w = w.reshape(depth, KV_LORA, HEADS, 2 * HEAD_DIM).astype(_BF16)
    return w[..., :HEAD_DIM].reshape(depth, KV_LORA, -1), w[..., HEAD_DIM:].reshape(depth, KV_LORA, -1)


def _rope_tables(ctx_len, n_latent):
    rows = n_latent // GRID_W
    row = jnp.repeat(jnp.arange(rows), GRID_W)
    col = jnp.tile(jnp.arange(GRID_W), rows)
    n_freq = ROPE_DIM // 4
    inv = ROPE_BASE ** (-jnp.arange(n_freq, dtype=_F32) / n_freq)
    ang = jnp.concatenate([row[:, None] * inv, col[:, None] * inv], axis=-1)
    zeros = jnp.zeros((n_latent, V7X_LANES - ROPE_DIM), _F32)
    cos_l = jnp.concatenate([jnp.cos(ang), jnp.cos(ang), zeros], axis=1)
    sin_l = jnp.concatenate([jnp.sin(ang), jnp.sin(ang), zeros], axis=1)
    cos_c = jnp.concatenate([jnp.ones((ctx_len, ROPE_DIM), _F32), jnp.zeros((ctx_len, V7X_LANES - ROPE_DIM), _F32)], axis=1)
    return jnp.concatenate([cos_c, cos_l], axis=0), jnp.concatenate([jnp.zeros_like(cos_c), sin_l], axis=0)


def kernel(x, c, ctx, c_ctx, w_ada, b_ada, g_mix, w_in, w_qkconv, b_qkconv, b_mgate, g_mhead, g_qnorm, w_uq,
           g_kvnorm, w_ukv, g_sgu, w_s, b_s, w_branch, w_out, g_ffn, w_up, w_ffconv, b_ffconv, w_down, g_final):
    b, s, d = x.shape
    ctx_len = ctx.shape[1]
    depth = w_in.shape[0]
    assert ctx_len % ROW_TILE == 0 and s % ROW_TILE == 0 and s % GRID_W == 0
    nctx_tiles = ctx_len // ROW_TILE
    row_param = lambda a: a.reshape(depth, 1, a.shape[-1])

    cos_t, sin_t = _rope_tables(ctx_len, s)
    cond_rows = jnp.concatenate([c, c_ctx[None, :], jnp.zeros((COND_PAD_ROWS - 1, d), c.dtype)], axis=0)
    mod = _ada(cond_rows, w_ada, b_ada)
    w_in_l = _layout_w_in(w_in)
    b_gate_l = _layout_gate_bias(b_mgate)
    w_uq_l = _layout_w_uq(w_uq)
    w_k_l, w_v_l = _layout_w_ukv(w_ukv)
    b_s_t = jnp.pad(jnp.swapaxes(b_s, 1, 2), ((0, 0), (0, 0), (0, V7X_LANES - HEADS)))
    w_s16, w_branch16, w_out16 = w_s.astype(_BF16), w_branch.astype(_BF16), w_out.astype(_BF16)
    w_up16, w_down16 = w_up.astype(_BF16), w_down.astype(_BF16)
    g_mix_r, g_mhead_r, g_sgu_r, g_ffn_r = row_param(g_mix), row_param(g_mhead), row_param(g_sgu), row_param(g_ffn)
    g_qn_r, g_kvn_r, b_qkconv_r, b_ffconv_r = row_param(g_qnorm), row_param(g_kvnorm), row_param(b_qkconv), row_param(b_ffconv)

    x_ctx, x_lat = ctx, x
    for l in range(depth):
        last = l == depth - 1
        q_act, kt_act, vo, us, gt, at, gates = _proj(x_ctx, x_lat, mod, g_mix_r, w_in_l, w_qkconv, b_qkconv_r, b_gate_l,
                                                     l, nctx_tiles)
        hf, hb = _mlstm(q_act, kt_act, vo, gates, ctx_len // CHUNK)
        yb_ctx, yb_lat = _attn(at, cos_t, sin_t, g_qn_r, w_uq_l, g_kvn_r, w_k_l, w_v_l, l, ctx_len, not last)
        x_mid, h2 = _merge(hf, hb, vo, yb_ctx, yb_lat, us, gt, x_ctx, x_lat, mod, g_mhead_r, g_sgu_r, w_s16, b_s_t,
                           w_branch16, w_out16, g_ffn_r, l, nctx_tiles, last)
        ffn = functools.partial(_ffn, h2, x_mid, mod, w_up16, w_ffconv, b_ffconv_r, w_down16, g_final, l)
        if last:
            return ffn(0, s // ROW_TILE, FFN_SUB, False, True)
        x_lat = ffn(nctx_tiles, s // ROW_TILE, FFN_SUB, False, False)
        x_ctx = ffn(0, nctx_tiles, 1, True, False)
```

```python
import functools

import jax
import jax.numpy as jnp
import numpy as np
from jax import lax
from jax.experimental import pallas as pl
from jax.experimental.pallas import tpu as pltpu

EPS = 1e-6
GRID_W = 64
ROPE_BASE = 10000.0
LOG2_E = 1.4426950408889634

HEADS = 4
HEAD_DIM = 128
CHUNK = 128
BRANCH_WIDTH = HEADS * HEAD_DIM
Q_LORA = 384
KV_LORA = 256
ROPE_DIM = 64
N_BRANCH = 3
M_GATES = 4 * HEADS
KV_SRC = KV_LORA + 2 * ROPE_DIM
ATTN_W = Q_LORA + KV_SRC
SCAN_LANES = 2 * HEADS

V7X_LANES = 128
V7X_BF16_SUBLANE_TILE = 16
V7X_MXU_WIDTH = 256
V7X_VMEM_BYTES = 64 * 1024 * 1024
V7X_VMEM_LIMIT = V7X_VMEM_BYTES * 7 // 8

GATE_W = 2 * V7X_LANES
COND_PAD_ROWS = 8
ROW_TILE = V7X_MXU_WIDTH
FFN_HALO = V7X_BF16_SUBLANE_TILE
FFN_CHUNK = V7X_MXU_WIDTH
FFN_SUB = 4

_BF16 = jnp.bfloat16
_F32 = jnp.float32


def _cparams(n_axes):
    return pltpu.CompilerParams(dimension_semantics=("arbitrary",) * n_axes, vmem_limit_bytes=V7X_VMEM_LIMIT)


def _const_spec(shape):
    return pl.BlockSpec(tuple(shape), lambda *_: (0,) * len(shape))


def _layer_spec(stacked_shape, layer):
    shape = tuple(stacked_shape[1:])
    return pl.BlockSpec((None,) + shape, lambda *_: (layer,) + (0,) * len(shape), pipeline_mode=pl.Buffered(1))


def _sigmoid(x):
    return 1.0 / (1.0 + jnp.exp(-x))


def _silu(x):
    return x * _sigmoid(x)


def _gelu_tanh(x):
    return x * (0.5 * (1.0 + jnp.tanh(0.7978845608028654 * (x + 0.044715 * (x * x * x)))))


def _rms(x):
    return x * lax.rsqrt(jnp.mean(x * x, axis=-1, keepdims=True) + EPS)


def _dot(a, b):
    return jnp.dot(a, b, preferred_element_type=_F32)


def _dot_nt(a, b):
    return lax.dot_general(a, b, (((1,), (1,)), ((), ())), preferred_element_type=_F32)


def _ada_kernel(c_ref, w_ref, b_ref, o_ref):
    cond = _silu(c_ref[...])
    o_ref[...] = jnp.dot(cond, w_ref[...], precision=lax.Precision.HIGHEST, preferred_element_type=_F32) + b_ref[...]


def _ada(cond_rows, w_ada, b_ada):
    rows, d = cond_rows.shape
    depth, _, n = w_ada.shape
    out = pl.pallas_call(
        _ada_kernel,
        grid=(depth, n // d),
        in_specs=[pl.BlockSpec((rows, d), lambda l, j: (0, 0)),
                  pl.BlockSpec((None, d, d), lambda l, j: (l, 0, j)),
                  pl.BlockSpec((None, 1, d), lambda l, j: (l, 0, j))],
        out_specs=pl.BlockSpec((None, rows, d), lambda l, j: (l, 0, j)),
        out_shape=jax.ShapeDtypeStruct((depth, rows, n), _F32),
        compiler_params=_cparams(2),
        name="ada",
    )(cond_rows, w_ada, b_ada.reshape(depth, 1, n))
    return out.reshape(depth, rows, 1, n)


PROJ_TILE_INPUTS = 5
QK_HALO = V7X_BF16_SUBLANE_TILE


def _proj_kernel(*refs, d, nctx_tiles, tiles):
    per_tile = (refs[0:PROJ_TILE_INPUTS], refs[PROJ_TILE_INPUTS:2 * PROJ_TILE_INPUTS])
    (g_ref, w_ref, wcv_ref, bcv_ref, bg_ref,
     q_ref, kt_ref, vo_ref, us_ref, gt_ref, at_ref, gate_ref) = refs[2 * PROJ_TILE_INPUTS:]
    bw = BRANCH_WIDTH
    col_at = 6 * bw + N_BRANCH * d
    ext_rows = ROW_TILE + 2 * QK_HALO
    inner = slice(QK_HALO, QK_HALO + ROW_TILE)

    def normalise(h):
        xc_ref, xl_ref, mod_ref, xp_ref, xn_ref = per_tile[h]
        tile = (2 * pl.program_id(0) + h) % tiles
        norm = lambda x: (_rms(x) * g_ref[...] * (1.0 + mod_ref[:, d:2 * d]) + mod_ref[:, 0:d]).astype(_BF16)
        hn = norm(jnp.where(tile < nctx_tiles, xc_ref[...], xl_ref[...]))
        zeros = jnp.zeros((QK_HALO, d), _BF16)
        h_prev = jnp.where(tile > nctx_tiles, norm(xp_ref[...]), zeros)
        h_next = jnp.where(jnp.logical_and(tile >= nctx_tiles, tile < tiles - 1), norm(xn_ref[...]), zeros)
        return hn, jnp.concatenate([h_prev, hn, h_next], axis=0)

    cw = V7X_MXU_WIDTH

    def conv_columns(h, h_ext, c):
        rows = slice(h * ROW_TILE, (h + 1) * ROW_TILE)
        cols = slice(c * cw, (c + 1) * cw)
        qk = _dot(h_ext, w_ref[:, cols])
        qk_prev = pltpu.roll(qk, 1, axis=0)[inner, :]
        qk_next = pltpu.roll(qk, ext_rows - 1, axis=0)[inner, :]
        a = _silu(bcv_ref[:, cols] + qk_prev * wcv_ref[0:1, cols] + qk[inner, :] * wcv_ref[1:2, cols]
                  + qk_next * wcv_ref[2:3, cols])
        if c < bw // cw:
            q_ref[rows, cols] = a.astype(_BF16)
        else:
            for sub in range(cw // HEAD_DIM):
                head = (c * cw - bw) // HEAD_DIM + sub
                k = a[:, sub * HEAD_DIM:(sub + 1) * HEAD_DIM] * (HEAD_DIM ** -0.5)
                kt_ref[h, head] = k.T.astype(_BF16)

    def column_groups(h, hn, h_ext):
        rows = slice(h * ROW_TILE, (h + 1) * ROW_TILE)
        conv_columns(h, h_ext, 0)
        us_ref[rows, :] = _gelu_tanh(_dot(hn, w_ref[:, 4 * bw:6 * bw])).astype(_BF16)
        for g in range(N_BRANCH):
            conv_columns(h, h_ext, g + 1)
            cols = slice(6 * bw + g * d, 6 * bw + (g + 1) * d)
            gt_ref[rows, g * d:(g + 1) * d] = _sigmoid(_dot(hn, w_ref[:, cols])).astype(_BF16)
        vo_ref[rows, bw:2 * bw] = _sigmoid(_dot(hn, w_ref[:, 3 * bw:4 * bw])).astype(_BF16)
        vo_ref[rows, 0:bw] = _dot(hn, w_ref[:, 2 * bw:3 * bw]).astype(_BF16)
        at_ref[rows, :] = _dot(hn, w_ref[:, col_at:col_at + ATTN_W]).astype(_BF16)
        gates = _dot(hn, w_ref[:, col_at + ATTN_W:col_at + ATTN_W + GATE_W]) + bg_ref[...]
        gate_f = gates[:, V7X_LANES:GATE_W]
        gate_ref[rows, 0:V7X_LANES] = gates[:, 0:V7X_LANES]
        gate_ref[rows, V7X_LANES:GATE_W] = jnp.minimum(gate_f, 0.0) - jnp.log(1.0 + jnp.exp(-jnp.abs(gate_f)))

    assert 2 * bw // cw == N_BRANCH + 1
    tiles_normed = [normalise(0), normalise(1)]
    for h, (hn, h_ext) in enumerate(tiles_normed):
        column_groups(h, hn, h_ext)


def _pair_tile(h, tiles):
    def index(n):
        k = 2 * n + h
        return k // tiles, k % tiles
    return index


def _two_source_specs(at, off, width, nctx_tiles):
    return [pl.BlockSpec((None, ROW_TILE, width), lambda n: (at(n)[0], jnp.minimum(at(n)[1] + off, nctx_tiles - 1), 0)),
            pl.BlockSpec((None, ROW_TILE, width), lambda n: (at(n)[0], jnp.maximum(at(n)[1] + off - nctx_tiles, 0), 0))]


def _stream_tile_specs(at, off, d, nctx_tiles, mod, layer):
    batch_row = mod.shape[1] - COND_PAD_ROWS
    return _two_source_specs(at, off, d, nctx_tiles) + [
        pl.BlockSpec((None, None, 1, mod.shape[3]),
                     lambda n: (layer, jnp.where(at(n)[1] + off < nctx_tiles, batch_row, at(n)[0]), 0, 0))]


def _proj(x_ctx, x_lat, mod, g_mix, w_in, w_qkconv, b_qkconv, b_gate, layer, nctx_tiles):
    b, _, d = x_ctx.shape
    lat_rows = x_lat.shape[1]
    tiles = nctx_tiles + lat_rows // ROW_TILE
    bw = BRANCH_WIDTH
    widths = (bw, 2 * bw, 2 * bw, N_BRANCH * d, ATTN_W, GATE_W)
    dtypes = (_BF16, _BF16, _BF16, _BF16, _BF16, _F32)
    assert nctx_tiles == 1 and sum(widths) + bw == w_in.shape[2] and (b * tiles) % 2 == 0
    rows = b * tiles * ROW_TILE
    hb = ROW_TILE // QK_HALO

    def tile_specs(h):
        at = _pair_tile(h, tiles)
        lat_tile = lambda n: jnp.maximum(at(n)[1] - nctx_tiles, 0)
        halo = lambda block_of: pl.BlockSpec((None, QK_HALO, d), lambda n: (at(n)[0], block_of(n), 0))
        return _stream_tile_specs(at, 0, d, nctx_tiles, mod, layer) + [
            halo(lambda n: jnp.maximum(lat_tile(n) * hb - 1, 0)),
            halo(lambda n: jnp.minimum((lat_tile(n) + 1) * hb, lat_rows // QK_HALO - 1))]

    flat = lambda w: pl.BlockSpec((2 * ROW_TILE, w), lambda n: (n, 0))
    tile_args = (x_ctx, x_lat, mod, x_lat, x_lat)
    params = (g_mix, w_in, w_qkconv, b_qkconv, b_gate)
    outs = pl.pallas_call(
        functools.partial(_proj_kernel, d=d, nctx_tiles=nctx_tiles, tiles=tiles),
        grid=(b * tiles // 2,),
        in_specs=tile_specs(0) + tile_specs(1) + [_layer_spec(a.shape, layer) for a in params],
        out_specs=[flat(widths[0]), pl.BlockSpec((2, HEADS, HEAD_DIM, ROW_TILE), lambda n: (n, 0, 0, 0))]
        + [flat(w) for w in widths[1:]],
        out_shape=[jax.ShapeDtypeStruct((rows, widths[0]), dtypes[0]),
                   jax.ShapeDtypeStruct((b * tiles, HEADS, HEAD_DIM, ROW_TILE), _BF16)]
        + [jax.ShapeDtypeStruct((rows, w), dt) for w, dt in zip(widths[1:], dtypes[1:])],
        compiler_params=_cparams(1),
        name="proj",
    )(*tile_args, *tile_args, *params)
    q, kt = outs[0].reshape(b, tiles * ROW_TILE, bw), outs[1]
    return [q, kt] + [o.reshape(b, tiles * ROW_TILE, o.shape[1]) for o in outs[2:]]


def _mlstm_kernel(q_f, kt_f, v_f, gf_cur, gf_nxt, q_b, kt_b, v_b, gb_cur, gb_nxt, hf_ref, hb_ref,
                  c_scr, m_row, m_col, *pre):
    tc = CHUNK
    low, high = pl.ds(0, tc), pl.ds(tc, tc)

    def scan_step(slot, fwd_rows, bwd_rows):
        views = [(q.at[rows, :], kt.at[:, :, rows], v.at[rows, :], out.at[rows, :])
                 for rows, (q, kt, v, out) in ((fwd_rows, (q_f, kt_f, v_f, hf_ref)), (bwd_rows, (q_b, kt_b, v_b, hb_ref)))]
        _mlstm_scan_step(slot, is_fwd, causal, views[0], views[1], c_scr, m_row, m_col)

    pre_a, pre_b = pre[:len(pre) // 2], pre[len(pre) // 2:]
    lane = lax.broadcasted_iota(jnp.int32, (1, V7X_LANES), 1)
    row = lax.broadcasted_iota(jnp.int32, (tc, 1), 0)
    is_fwd = lane < HEADS
    r_idx = lax.broadcasted_iota(jnp.int32, (tc, tc), 0)
    c_idx = lax.broadcasted_iota(jnp.int32, (tc, tc), 1)
    causal = (c_idx <= r_idx, c_idx >= r_idx)

    def gate_prologue(g_f, g_b, slot):
        bcum_ref, run_ref, rt_ref, dt_ref, rows_ref, blc_ref, dmc_ref = slot
        gi, gf = slice(0, V7X_LANES), slice(V7X_LANES, GATE_W)
        log_i = jnp.where(is_fwd, g_f[:, gi], g_b[:, gi])
        log_f = jnp.where(is_fwd, g_f[:, gf], g_b[:, gf])
        cum_f, step = log_f, 1
        while step < tc:
            cum_f = cum_f + jnp.where(row >= step, pltpu.roll(cum_f, step, axis=0), 0.0)
            step *= 2
        b_last = cum_f[tc - 1:tc, :]
        bcum = jnp.where(is_fwd, cum_f, b_last - cum_f + log_f)
        r = log_i - bcum
        run_f, run_b, step = r, r, 1
        while step < tc:
            run_f = jnp.maximum(run_f, jnp.where(row >= step, pltpu.roll(run_f, step, axis=0), -jnp.inf))
            run_b = jnp.maximum(run_b, jnp.where(row < tc - step, pltpu.roll(run_b, tc - step, axis=0), -jnp.inf))
            step *= 2
        decay = b_last + r
        r_t = r.T[0:SCAN_LANES, :]
        decay_t = decay.T[0:SCAN_LANES, :]
        bcum_ref[...] = bcum
        run_ref[...] = jnp.where(is_fwd, run_f, run_b)
        rt_ref[...] = r_t
        dt_ref[...] = decay_t
        rows_ref[0:1, :] = b_last
        rows_ref[1:2, :] = jnp.max(decay, axis=0, keepdims=True)
        blc_ref[...] = decay_t - r_t
        dmc_ref[...] = jnp.broadcast_to(jnp.max(decay_t, axis=1, keepdims=True), (SCAN_LANES, tc))

    @pl.when(pl.program_id(1) == 0)
    def _():
        c_scr[...] = jnp.zeros_like(c_scr)
        m_row[...] = jnp.zeros_like(m_row)
        m_col[...] = jnp.zeros_like(m_col)
        gate_prologue(gf_cur.at[low, :], gb_cur.at[high, :], pre_a)

    gate_prologue(gf_cur.at[high, :], gb_cur.at[low, :], pre_b)
    scan_step(pre_a, low, high)
    gate_prologue(gf_nxt.at[low, :], gb_nxt.at[high, :], pre_a)
    scan_step(pre_b, high, low)


def _mlstm_scan_step(slot, is_fwd, causal, refs_f, refs_b, c_scr, m_row, m_col):
    tc = CHUNK
    bcum_ref, run_ref, rt_ref, dt_ref, rows_ref, blc_ref, dmc_ref = slot
    bcum = bcum_ref[...]
    b_last, decay_max = rows_ref[0:1, :], rows_ref[1:2, :]
    m_old = m_row[...]
    g = jnp.maximum(m_old, run_ref[...])
    m_new = jnp.maximum(b_last + m_old, decay_max)
    w_carry = jnp.exp(b_last + m_old - m_new)
    m_col_new = jnp.maximum(blc_ref[...] + m_col[...], dmc_ref[...])
    w_src_t = jnp.exp(dt_ref[...] - m_col_new)
    r_t = rt_ref[...]
    ones = jnp.ones((tc, HEAD_DIM), _BF16)

    scans = [(dr * HEADS + h, refs, h, slice(h * HEAD_DIM, (h + 1) * HEAD_DIM), causal[dr])
             for dr, refs in enumerate((refs_f, refs_b)) for h in range(HEADS)]
    v_ext = [jnp.concatenate([refs[2][:, hs], ones], axis=1) for _, refs, _, hs, _ in scans]
    s16, g_b = [], []
    for l, (q_ref, kt_ref, _, _), h, hs, mask in scans:
        g_l = jnp.broadcast_to(g[:, l:l + 1], (tc, tc))
        w_intra = jnp.exp(jnp.where(mask, r_t[l:l + 1, :] - g_l, -jnp.inf))
        s16.append((_dot(q_ref[:, hs], kt_ref[h]) * w_intra).astype(_BF16))
        g_b.append(g_l)
    qc = []
    for l, (q_ref, kt_ref, _, _), h, hs, _ in scans:
        c_old = c_scr[l]
        qc.append(_dot(q_ref[:, hs], c_old.astype(_BF16)))
        kw_t = (kt_ref[h].astype(_F32) * w_src_t[l:l + 1, :]).astype(_BF16)
        c_scr[l] = w_carry[:, l:l + 1] * c_old + _dot(kw_t, v_ext[l])
    for l, (_, _, _, out_ref), h, hs, _ in scans:
        w_inter = jnp.exp(m_old[:, l:l + 1] - g_b[l])
        b_l = jnp.broadcast_to(bcum[:, l:l + 1], (tc, tc))
        sv = _dot(s16[l], v_ext[l])
        num = w_inter * qc[l][:, 0:HEAD_DIM] + sv[:, 0:HEAD_DIM]
        den = w_inter * qc[l][:, HEAD_DIM:] + sv[:, HEAD_DIM:]
        out_ref[:, hs] = num / jnp.maximum(jnp.abs(den), jnp.exp(-(b_l + g_b[l])))
    m_row[...] = m_new
    m_col[...] = m_col_new


def _mlstm(q, kt, vo, gates, nctx):
    b, t, _ = q.shape
    blk = 2 * CHUNK
    nblocks, nctx_blocks = t // blk, nctx // 2
    assert t % blk == 0 and nctx % 2 == 0

    def bwd_block(j):
        return jnp.where(j < nctx_blocks, nctx_blocks - 1 - j, nblocks - 1 - (j - nctx_blocks))

    def specs(block_of):
        nxt = lambda j: block_of(jnp.minimum(j + 1, nblocks - 1))
        return [
            pl.BlockSpec((None, blk, BRANCH_WIDTH), lambda bi, j: (bi, block_of(j), 0)),
            pl.BlockSpec((None, HEADS, HEAD_DIM, blk), lambda bi, j: (bi * nblocks + block_of(j), 0, 0, 0)),
            pl.BlockSpec((None, blk, BRANCH_WIDTH), lambda bi, j: (bi, block_of(j), 0)),
            pl.BlockSpec((None, blk, GATE_W), lambda bi, j: (bi, block_of(j), 0)),
            pl.BlockSpec((None, blk, GATE_W), lambda bi, j: (bi, nxt(j), 0)),
        ]

    time_lane, lane_time = pltpu.VMEM((CHUNK, V7X_LANES), _F32), pltpu.VMEM((SCAN_LANES, CHUNK), _F32)
    gate_slot = [time_lane, time_lane, lane_time, lane_time, pltpu.VMEM((8, V7X_LANES), _F32), lane_time, lane_time]
    h_shape = jax.ShapeDtypeStruct((b, t, BRANCH_WIDTH), _F32)
    return pl.pallas_call(
        _mlstm_kernel,
        grid=(b, nblocks),
        in_specs=specs(lambda j: j) + specs(bwd_block),
        out_specs=[pl.BlockSpec((None, blk, BRANCH_WIDTH), lambda bi, j: (bi, j, 0)),
                   pl.BlockSpec((None, blk, BRANCH_WIDTH), lambda bi, j: (bi, bwd_block(j), 0))],
        out_shape=[h_shape, h_shape],
        scratch_shapes=[pltpu.VMEM((SCAN_LANES, HEAD_DIM, 2 * HEAD_DIM), _F32), pltpu.VMEM((1, V7X_LANES), _F32),
                        lane_time] + gate_slot + gate_slot,
        compiler_params=_cparams(2),
        name="mlstm",
    )(q, kt, vo, gates, gates, q, kt, vo, gates, gates)


ATTN_SUB = 2
VT_ONES = V7X_BF16_SUBLANE_TILE


def _attn_kernel(cq_ctx_ref, cq_a_ref, cq_b_ref, kv_ref, ca_ref, sa_ref, gq_ref, wq_ref, gkv_ref, wk_ref, wv_ref,
                 o_ctx_ref, o_lat_ref, kt_scr, vt_scr, *, ctx_len, need_ctx, scale):
    i = pl.program_id(1)
    t = kv_ref.shape[0]
    head_w = 2 * HEAD_DIM

    def rope(y, cos_t, sin_t):
        return y * cos_t + pltpu.roll(y, ROPE_DIM, axis=1) * sin_t

    def attend(cq_rows, row0, n_keys, o_ref):
        table_rows = pl.ds(pl.multiple_of(row0, ROW_TILE), cq_rows.shape[0])
        cos_q, sin_q = ca_ref[table_rows, :], sa_ref[table_rows, :]
        cq = (_rms(cq_rows.astype(_F32)) * gq_ref[...]).astype(_BF16)
        qa = _dot(cq, wq_ref[...]) * (scale * LOG2_E)

        def scores(h):
            q_nope = qa[:, h * head_w:h * head_w + HEAD_DIM]
            q_rope = rope(qa[:, h * head_w + HEAD_DIM:(h + 1) * head_w], cos_q, sin_q)
            qh = jnp.concatenate([q_nope, q_rope], axis=1).astype(_BF16)
            return _dot(qh, kt_scr[h, :, 0:n_keys])

        s_next = scores(0)
        for h in range(HEADS):
            s = s_next
            if h + 1 < HEADS:
                s_next = scores(h + 1)
            e = jnp.exp2(s - jnp.max(s, axis=1, keepdims=True)).astype(_BF16)
            ot = _dot_nt(vt_scr[h, :, 0:n_keys], e)
            ot = ot[0:HEAD_DIM, :] / ot[HEAD_DIM:HEAD_DIM + 1, :]
            o_ref[:, h * HEAD_DIM:(h + 1) * HEAD_DIM] = ot.T.astype(o_ref.dtype)

    @pl.when(i == 0)
    def _():
        for r in range(t // ROW_TILE):
            rows = slice(r * ROW_TILE, (r + 1) * ROW_TILE)
            ckv = (_rms(kv_ref[rows, 0:KV_LORA].astype(_F32)) * gkv_ref[...]).astype(_BF16)
            k_nope = _dot(ckv, wk_ref[...])
            k_rope = rope(kv_ref[rows, KV_LORA:KV_SRC].astype(_F32), ca_ref[rows, :], sa_ref[rows, :])
            for h in range(HEADS):
                k_h = jnp.concatenate([k_nope[:, h * HEAD_DIM:(h + 1) * HEAD_DIM], k_rope], axis=1)
                kt_scr[h, :, rows] = k_h.T.astype(_BF16)
            vv = _dot(ckv, wv_ref[...])
            for h in range(HEADS):
                vt_scr[h, 0:HEAD_DIM, rows] = vv[:, h * HEAD_DIM:(h + 1) * HEAD_DIM].T.astype(_BF16)
                vt_scr[h, HEAD_DIM:, rows] = jnp.ones((VT_ONES, ROW_TILE), _BF16)
        if need_ctx:
            attend(cq_ctx_ref[...], 0, ctx_len, o_ctx_ref)
        else:
            o_ctx_ref[...] = jnp.zeros_like(o_ctx_ref)

    @pl.when(i > 0)
    def _():
        cq_rows = jnp.concatenate([cq_a_ref[...], cq_b_ref[...]], axis=0)
        attend(cq_rows, ctx_len + (i - 1) * (ATTN_SUB * ROW_TILE), t, o_lat_ref)


def _attn(at, cos_t, sin_t, g_qn, w_q, g_kvn, w_k, w_v, layer, ctx_len, need_ctx):
    b, t, _ = at.shape
    assert ctx_len == ROW_TILE and (t - ctx_len) % (ATTN_SUB * ROW_TILE) == 0
    lat_steps = (t - ctx_len) // (ATTN_SUB * ROW_TILE)
    scale = (HEAD_DIM + ROPE_DIM) ** -0.5
    lat_tile = lambda s: pl.BlockSpec((None, ROW_TILE, Q_LORA),
                                      lambda bi, i: (bi, 1 + ATTN_SUB * jnp.maximum(i - 1, 0) + s, 0))
    return pl.pallas_call(
        functools.partial(_attn_kernel, ctx_len=ctx_len, need_ctx=need_ctx, scale=scale),
        grid=(b, 1 + lat_steps),
        in_specs=[pl.BlockSpec((None, ROW_TILE, Q_LORA), lambda bi, i: (bi, 0, 0)), lat_tile(0), lat_tile(1),
                  pl.BlockSpec((None, t, KV_SRC), lambda bi, i: (bi, 0, 1)),
                  _const_spec((t, V7X_LANES)), _const_spec((t, V7X_LANES)),
                  _layer_spec(g_qn.shape, layer), _layer_spec(w_q.shape, layer), _layer_spec(g_kvn.shape, layer),
                  _layer_spec(w_k.shape, layer), _layer_spec(w_v.shape, layer)],
        out_specs=[pl.BlockSpec((None, ROW_TILE, BRANCH_WIDTH), lambda bi, i: (bi, 0, 0)),
                   pl.BlockSpec((None, ATTN_SUB * ROW_TILE, BRANCH_WIDTH), lambda bi, i: (bi, jnp.maximum(i - 1, 0), 0))],
        out_shape=[jax.ShapeDtypeStruct((b, ctx_len, BRANCH_WIDTH), _BF16),
                   jax.ShapeDtypeStruct((b, t - ctx_len, BRANCH_WIDTH), _BF16)],
        scratch_shapes=[pltpu.VMEM((HEADS, 2 * HEAD_DIM, t), _BF16),
                        pltpu.VMEM((HEADS, HEAD_DIM + VT_ONES, t), _BF16)],
        compiler_params=_cparams(2),
        name="attn",
    )(at, at, at, at, cos_t, sin_t, g_qn, w_q, g_kvn, w_k, w_v)


def _merge_mix(hf_ref, hb_ref, so_ref, yb, us_ref, gt_ref, gmh_ref, gsgu_ref, ws_ref, bs_ref, wbr_ref, wout_ref, d):
    tm = hf_ref.shape[0]
    half = d // 2
    pb = [_dot(yb, wbr_ref[1, :, c * half:(c + 1) * half]) for c in range(2)]
    hsum = hf_ref[...] + hb_ref[...]
    ya = jnp.concatenate(
        [_rms(hsum[:, h * HEAD_DIM:(h + 1) * HEAD_DIM]) for h in range(HEADS)], axis=1) * gmh_ref[...]
    ya = so_ref[...].astype(_F32) * ya
    cols = []
    for g in range(HEADS):
        gs = slice(BRANCH_WIDTH + g * HEAD_DIM, BRANCH_WIDTH + (g + 1) * HEAD_DIM)
        vn = (_rms(us_ref[:, gs].astype(_F32)) * gsgu_ref[:, g * HEAD_DIM:(g + 1) * HEAD_DIM]).astype(_BF16)
        mixed = [_dot(ws_ref[g], vn[n * CHUNK:(n + 1) * CHUNK, :]) + bs_ref[:, g:g + 1] for n in range(tm // CHUNK)]
        cols.append(jnp.concatenate(mixed, axis=0))
    yc = us_ref[:, 0:BRANCH_WIDTH].astype(_F32) * jnp.concatenate(cols, axis=1)
    ya16, yc16 = ya.astype(_BF16), yc.astype(_BF16)
    pa = [_dot(ya16, wbr_ref[0, :, c * half:(c + 1) * half]) for c in range(2)]
    pc = [_dot(yc16, wbr_ref[2, :, c * half:(c + 1) * half]) for c in range(2)]
    out = None
    for c in range(2):
        gate = lambda g: gt_ref[:, g * d + c * half:g * d + (c + 1) * half].astype(_F32)
        merged = (gate(0) * pa[c] + gate(1) * pb[c] + gate(2) * pc[c]).astype(_BF16)
        part = _dot(merged, wout_ref[c * half:(c + 1) * half, :])
        out = part if out is None else out + part
    return out


MERGE_TILE_INPUTS = 10


def _merge_kernel(*refs, d, nctx_tiles, off, tiles):
    per_tile = (refs[0:MERGE_TILE_INPUTS], refs[MERGE_TILE_INPUTS:2 * MERGE_TILE_INPUTS])
    gmh_ref, gsgu_ref, ws_ref, bs_ref, wbr_ref, wout_ref, gffn_ref, xo_ref, h2_ref = refs[2 * MERGE_TILE_INPUTS:]
    is_ctx = [(2 * pl.program_id(0) + h) % tiles + off < nctx_tiles for h in range(2)]
    outs = []
    for h, (hf_ref, hb_ref, so_ref, us_ref, gt_ref, ybc_ref, ybl_ref) in enumerate(t[0:7] for t in per_tile):
        yb = jnp.where(is_ctx[h], ybc_ref[...], ybl_ref[...])
        outs.append(_merge_mix(hf_ref, hb_ref, so_ref, yb, us_ref, gt_ref, gmh_ref, gsgu_ref, ws_ref, bs_ref, wbr_ref,
                               wout_ref, d))
    for h, tile_refs in enumerate(per_tile):
        xc_ref, xl_ref, mod_ref = tile_refs[7:10]
        rows = slice(h * ROW_TILE, (h + 1) * ROW_TILE)
        x = jnp.where(is_ctx[h], xc_ref[...], xl_ref[...])
        x_new = x + mod_ref[:, 2 * d:3 * d] * outs[h]
        xo_ref[rows, :] = x_new
        h2 = _rms(x_new) * gffn_ref[...]
        h2_ref[rows, :] = (h2 * (1.0 + mod_ref[:, 4 * d:5 * d]) + mod_ref[:, 3 * d:4 * d]).astype(h2_ref.dtype)


def _merge(hf, hb, vo, yb_ctx, yb_lat, us, gt, x_ctx, x_lat, mod, g_mhead, g_sgu, w_s, b_s_t, w_branch, w_out, g_ffn,
           layer, nctx_tiles, skip_ctx):
    b, t, _ = hf.shape
    d = x_ctx.shape[2]
    off = nctx_tiles if skip_ctx else 0
    tiles = t // ROW_TILE - off
    assert (b * tiles) % 2 == 0

    def tile_specs(h):
        at = _pair_tile(h, tiles)
        tile = lambda w, blk: pl.BlockSpec((None, ROW_TILE, w), lambda n: (at(n)[0], at(n)[1] + off, blk))
        return ([tile(BRANCH_WIDTH, 0), tile(BRANCH_WIDTH, 0), tile(BRANCH_WIDTH, 1), tile(2 * BRANCH_WIDTH, 0),
                 tile(N_BRANCH * d, 0)] + _two_source_specs(at, off, BRANCH_WIDTH, nctx_tiles)
                + _stream_tile_specs(at, off, d, nctx_tiles, mod, layer))

    tile_args = (hf, hb, vo, us, gt, yb_ctx, yb_lat, x_ctx, x_lat, mod)
    params = (g_mhead, g_sgu, w_s, b_s_t, w_branch, w_out, g_ffn)
    out_tile = lambda: pl.BlockSpec((2 * ROW_TILE, d), lambda n: (n, 0))
    rows_out = b * tiles * ROW_TILE
    x_mid, h2 = pl.pallas_call(
        functools.partial(_merge_kernel, d=d, nctx_tiles=nctx_tiles, off=off, tiles=tiles),
        grid=(b * tiles // 2,),
        in_specs=tile_specs(0) + tile_specs(1) + [_layer_spec(a.shape, layer) for a in params],
        out_specs=[out_tile(), out_tile()],
        out_shape=[jax.ShapeDtypeStruct((rows_out, d), _F32), jax.ShapeDtypeStruct((rows_out, d), _BF16)],
        compiler_params=_cparams(1),
        name="merge",
    )(*tile_args, *tile_args, *params)
    return x_mid.reshape(b, tiles * ROW_TILE, d), h2.reshape(b, tiles * ROW_TILE, d)


def _ffn_kernel(*refs, d, n_sub, groups, final_norm):
    h_refs, (hp_ref, hn_ref), x_refs = refs[0:n_sub], refs[n_sub:n_sub + 2], refs[n_sub + 2:2 * n_sub + 2]
    mod_ref, wup_ref, wcv_ref, bcv_ref, wdn_ref, gfin_ref, o_ref, ext_scr, act_scr = refs[2 * n_sub + 2:]
    tm = n_sub * ROW_TILE
    ff = wdn_ref.shape[0]
    ck = FFN_CHUNK
    group = pl.program_id(0) % groups
    ext_scr[0:FFN_HALO, :] = jnp.where(group == 0, jnp.zeros_like(hp_ref), hp_ref[...])
    for s, h_ref in enumerate(h_refs):
        ext_scr[FFN_HALO + s * ROW_TILE:FFN_HALO + (s + 1) * ROW_TILE, :] = h_ref[...]
    ext_scr[FFN_HALO + tm:, :] = jnp.where(group == groups - 1, jnp.zeros_like(hn_ref), hn_ref[...])
    ext_rows = tm + 2 * FFN_HALO
    inner = slice(FFN_HALO, FFN_HALO + tm)

    def conv(cols):
        a = _dot(ext_scr[...], wup_ref[:, cols])
        a_prev = pltpu.roll(a, 1, axis=0)[inner, :]
        a_next = pltpu.roll(a, ext_rows - 1, axis=0)[inner, :]
        return (bcv_ref[:, cols] + a_prev * wcv_ref[0:1, cols] + a[inner, :] * wcv_ref[1:2, cols]
                + a_next * wcv_ref[2:3, cols])

    for c in range(ff // ck):
        gate = conv(slice(c * ck, (c + 1) * ck))
        val = conv(slice(ff + c * ck, ff + (c + 1) * ck))
        act_scr[:, c * ck:(c + 1) * ck] = (_silu(gate) * val).astype(_BF16)

    down = _dot(act_scr[...], wdn_ref[...])
    for s, x_ref in enumerate(x_refs):
        rows = slice(s * ROW_TILE, (s + 1) * ROW_TILE)
        x_new = x_ref[...] + mod_ref[:, 5 * d:6 * d] * down[rows, :]
        if final_norm:
            x_new = _rms(x_new) * gfin_ref[...]
        o_ref[rows, :] = x_new


def _ffn(h2, x, mod, w_up, w_cv, b_cv, w_dn, g_final, layer, first_tile, seg_tiles, n_sub, is_ctx, final_norm):
    b, r, d = x.shape
    assert seg_tiles % n_sub == 0
    groups = seg_tiles // n_sub
    hb = ROW_TILE // FFN_HALO
    batch_row = mod.shape[1] - COND_PAD_ROWS
    tile0 = lambda n: first_tile + (n % groups) * n_sub
    sub_specs = [pl.BlockSpec((None, ROW_TILE, d), lambda n, s=s: (n // groups, tile0(n) + s, 0)) for s in range(n_sub)]
    return pl.pallas_call(
        functools.partial(_ffn_kernel, d=d, n_sub=n_sub, groups=groups, final_norm=final_norm),
        grid=(b * groups,),
        in_specs=sub_specs
        + [pl.BlockSpec((None, FFN_HALO, d), lambda n: (n // groups, jnp.maximum(tile0(n) * hb - 1, 0), 0)),
           pl.BlockSpec((None, FFN_HALO, d),
                        lambda n: (n // groups, jnp.minimum((tile0(n) + n_sub) * hb, r // FFN_HALO - 1), 0))]
        + sub_specs
        + [pl.BlockSpec((None, None, 1, mod.shape[3]), lambda n: (layer, batch_row if is_ctx else n // groups, 0, 0))]
        + [_layer_spec(a.shape, layer) for a in (w_up, w_cv, b_cv, w_dn)] + [_const_spec((1, d))],
        out_specs=pl.BlockSpec((None, n_sub * ROW_TILE, d), lambda n: (n // groups, n % groups, 0)),
        out_shape=jax.ShapeDtypeStruct((b, seg_tiles * ROW_TILE, d), _F32),
        scratch_shapes=[pltpu.VMEM((n_sub * ROW_TILE + 2 * FFN_HALO, d), _BF16),
                        pltpu.VMEM((n_sub * ROW_TILE, w_dn.shape[1]), _BF16)],
        compiler_params=_cparams(1),
        name="ffn",
    )(*([h2] * (n_sub + 2)), *([x] * n_sub), mod, w_up, w_cv, b_cv, w_dn, g_final.reshape(1, d))


def _deinterleave(w):
    return jnp.concatenate([w[..., 0::2], w[..., 1::2]], axis=-1)


def _rotated(w):
    return jnp.concatenate([-w[..., 1::2], w[..., 0::2]], axis=-1)


def _w_in_layout_kernel(w_ref, tail_ref, o_ref, *, moves):
    for src, dst, width in moves:
        o_ref[:, dst:dst + width] = w_ref[:, src:src + width].astype(_BF16)
    o_ref[:, o_ref.shape[1] - tail_ref.shape[1]:] = tail_ref[...]


def _layout_w_in(w):
    depth, d, n_in = w.shape
    sizes = (BRANCH_WIDTH,) * 4 + (M_GATES, Q_LORA, KV_LORA, ROPE_DIM, BRANCH_WIDTH, BRANCH_WIDTH, N_BRANCH * d)
    start = dict(zip("q k v o mg cq ckv kr u s gt".split(), (int(x) for x in np.cumsum((0,) + sizes[:-1]))))
    kr = lax.slice_in_dim(w, start["kr"], start["kr"] + ROPE_DIM, axis=2)
    mg = lax.slice_in_dim(w, start["mg"], start["mg"] + M_GATES, axis=2).reshape(depth, d, 2, 2, HEADS)
    pad = jnp.zeros((depth, d, V7X_LANES - SCAN_LANES), w.dtype)
    tail = jnp.concatenate([_deinterleave(kr), _rotated(kr), mg[:, :, :, 0, :].reshape(depth, d, SCAN_LANES), pad,
                            mg[:, :, :, 1, :].reshape(depth, d, SCAN_LANES), pad], axis=2).astype(_BF16)
    bw = BRANCH_WIDTH
    moves = ((start["q"], 0, 4 * bw), (start["u"], 4 * bw, 2 * bw), (start["gt"], 6 * bw, N_BRANCH * d),
             (start["cq"], 6 * bw + N_BRANCH * d, Q_LORA + KV_LORA))
    n_out = 6 * bw + N_BRANCH * d + Q_LORA + KV_LORA + tail.shape[2]
    return pl.pallas_call(
        functools.partial(_w_in_layout_kernel, moves=moves),
        grid=(depth, d // ROW_TILE),
        in_specs=[pl.BlockSpec((None, ROW_TILE, n_in), lambda l, i: (l, i, 0)),
                  pl.BlockSpec((None, ROW_TILE, tail.shape[2]), lambda l, i: (l, i, 0))],
        out_specs=pl.BlockSpec((None, ROW_TILE, n_out), lambda l, i: (l, i, 0)),
        out_shape=jax.ShapeDtypeStruct((depth, d, n_out), _BF16),
        compiler_params=_cparams(2),
        name="w_in_layout",
    )(w, tail)


def _layout_gate_bias(bg):
    depth = bg.shape[0]
    bg = bg.reshape(depth, 2, 2, HEADS)
    pad = jnp.zeros((depth, V7X_LANES - SCAN_LANES), bg.dtype)
    return jnp.concatenate([bg[:, :, 0, :].reshape(depth, -1), pad, bg[:, :, 1, :].reshape(depth, -1), pad],
                           axis=1).reshape(depth, 1, GATE_W)


def _layout_w_uq(w):
    depth = w.shape[0]
    w = w.reshape(depth, Q_LORA, HEADS, HEAD_DIM + ROPE_DIM)
    nope, rope = w[..., :HEAD_DIM], w[..., HEAD_DIM:]
    return jnp.concatenate([nope, _deinterleave(rope), _rotated(rope)], axis=-1).reshape(depth, Q_LORA, -1).astype(_BF16)


def _layout_w_ukv(w):
    depth = w.shape[0]
    w = w.reshape(depth, KV_LORA, HEADS, 2 * HEAD_DIM).astype(_BF16)
    return w[..., :HEAD_DIM].reshape(depth, KV_LORA, -1), w[..., HEAD_DIM:].reshape(depth, KV_LORA, -1)


def _rope_tables(ctx_len, n_latent):
    rows = n_latent // GRID_W
    row = jnp.repeat(jnp.arange(rows), GRID_W)
    col = jnp.tile(jnp.arange(GRID_W), rows)
    n_freq = ROPE_DIM // 4
    inv = ROPE_BASE ** (-jnp.arange(n_freq, dtype=_F32) / n_freq)
    ang = jnp.concatenate([row[:, None] * inv, col[:, None] * inv], axis=-1)
    zeros = jnp.zeros((n_latent, V7X_LANES - ROPE_DIM), _F32)
    cos_l = jnp.concatenate([jnp.cos(ang), jnp.cos(ang), zeros], axis=1)
    sin_l = jnp.concatenate([jnp.sin(ang), jnp.sin(ang), zeros], axis=1)
    cos_c = jnp.concatenate([jnp.ones((ctx_len, ROPE_DIM), _F32), jnp.zeros((ctx_len, V7X_LANES - ROPE_DIM), _F32)], axis=1)
    return jnp.concatenate([cos_c, cos_l], axis=0), jnp.concatenate([jnp.zeros_like(cos_c), sin_l], axis=0)


def kernel(x, c, ctx, c_ctx, w_ada, b_ada, g_mix, w_in, w_qkconv, b_qkconv, b_mgate, g_mhead, g_qnorm, w_uq,
           g_kvnorm, w_ukv, g_sgu, w_s, b_s, w_branch, w_out, g_ffn, w_up, w_ffconv, b_ffconv, w_down, g_final):
    b, s, d = x.shape
    ctx_len = ctx.shape[1]
    depth = w_in.shape[0]
    assert ctx_len % ROW_TILE == 0 and s % ROW_TILE == 0 and s % GRID_W == 0
    nctx_tiles = ctx_len // ROW_TILE
    row_param = lambda a: a.reshape(depth, 1, a.shape[-1])

    cos_t, sin_t = _rope_tables(ctx_len, s)
    cond_rows = jnp.concatenate([c, c_ctx[None, :], jnp.zeros((COND_PAD_ROWS - 1, d), c.dtype)], axis=0)
    mod = _ada(cond_rows, w_ada, b_ada)
    w_in_l = _layout_w_in(w_in)
    b_gate_l = _layout_gate_bias(b_mgate)
    w_uq_l = _layout_w_uq(w_uq)
    w_k_l, w_v_l = _layout_w_ukv(w_ukv)
    b_s_t = jnp.pad(jnp.swapaxes(b_s, 1, 2), ((0, 0), (0, 0), (0, V7X_LANES - HEADS)))
    w_s16, w_branch16, w_out16 = w_s.astype(_BF16), w_branch.astype(_BF16), w_out.astype(_BF16)
    w_up16, w_down16 = w_up.astype(_BF16), w_down.astype(_BF16)
    g_mix_r, g_mhead_r, g_sgu_r, g_ffn_r = row_param(g_mix), row_param(g_mhead), row_param(g_sgu), row_param(g_ffn)
    g_qn_r, g_kvn_r, b_qkconv_r, b_ffconv_r = row_param(g_qnorm), row_param(g_kvnorm), row_param(b_qkconv), row_param(b_ffconv)

    x_ctx, x_lat = ctx, x
    for l in range(depth):
        last = l == depth - 1
        q_act, kt_act, vo, us, gt, at, gates = _proj(x_ctx, x_lat, mod, g_mix_r, w_in_l, w_qkconv, b_qkconv_r, b_gate_l,
                                                     l, nctx_tiles)
        hf, hb = _mlstm(q_act, kt_act, vo, gates, ctx_len // CHUNK)
        yb_ctx, yb_lat = _attn(at, cos_t, sin_t, g_qn_r, w_uq_l, g_kvn_r, w_k_l, w_v_l, l, ctx_len, not last)
        x_mid, h2 = _merge(hf, hb, vo, yb_ctx, yb_lat, us, gt, x_ctx, x_lat, mod, g_mhead_r, g_sgu_r, w_s16, b_s_t,
                           w_branch16, w_out16, g_ffn_r, l, nctx_tiles, last)
        ffn = functools.partial(_ffn, h2, x_mid, mod, w_up16, w_ffconv, b_ffconv_r, w_down16, g_final, l)
        if last:
            return ffn(0, s // ROW_TILE, FFN_SUB, False, True)
        x_lat = ffn(nctx_tiles, s // ROW_TILE, FFN_SUB, False, False)
        x_ctx = ffn(0, nctx_tiles, 1, True, False)
```
